```python
import math
import jax, jax.numpy as jnp
from jax import lax
import numpy as np


D_MODEL = 2048
BATCH = 1
SEQ = 8192
DEPTH = 1

CTX_LEN = 256
GRID_W = 64
HEAD_DIM = 128
ROPE_PAIRS = HEAD_DIM // 4
ROPE_THETA = 10000.0
A_HEADS = 8
A_KV_HEADS = 2
A_GROUP = A_HEADS // A_KV_HEADS
B_HEADS = 4
B_V_DIM = 2 * HEAD_DIM
N_EXPERTS = 64
TOP_K = 8
N_GROUPS = 8
TOPK_GROUPS = 4
EXPERT_DIM = 512
SHARED_DIM = 512
ROUTED_SCALE = 2.5
EXPERT_BLOCK = 128
Q_BLOCK = 128
N_MOD = 6
EPS = 1e-6

A_Q_W = A_HEADS * HEAD_DIM
A_KV_W = A_KV_HEADS * HEAD_DIM
B_QK_W = B_HEADS * 2 * HEAD_DIM
B_V_W = B_HEADS * B_V_DIM
IN_SIZES = (A_Q_W, A_KV_W, A_KV_W, B_QK_W, B_QK_W, B_V_W, D_MODEL, D_MODEL)
IN_W = sum(IN_SIZES)
IN_OFFSETS = tuple(sum(IN_SIZES[:j + 1]) for j in range(len(IN_SIZES) - 1))

kernel_name = 'hybrid_gated_gqa_diffattn_moe_dit'


def rms_norm(x, g):
    xf = x.astype(jnp.float32)
    y = xf * lax.rsqrt(jnp.mean(xf * xf, axis=-1, keepdims=True) + EPS)
    return (y * g.astype(jnp.float32)).astype(x.dtype)


def adaln(cvec, w, b):
    m = jax.nn.silu(cvec) @ w + b
    return m.reshape(cvec.shape[0], N_MOD, 1, D_MODEL)


def modulate(h, shift, scale):
    return h * (1 + scale) + shift


def rope_tables(rows_n):
    rows = jnp.repeat(jnp.arange(rows_n, dtype=jnp.float32), GRID_W)
    cols = jnp.tile(jnp.arange(GRID_W, dtype=jnp.float32), rows_n)
    inv = ROPE_THETA ** (-jnp.arange(ROPE_PAIRS, dtype=jnp.float32) / ROPE_PAIRS)
    ang_r = rows[:, None] * inv
    ang_c = cols[:, None] * inv
    return (jnp.cos(ang_r)[:, None, :], jnp.sin(ang_r)[:, None, :],
            jnp.cos(ang_c)[:, None, :], jnp.sin(ang_c)[:, None, :])


def _rotate(u, cos, sin):
    u1, u2 = jnp.split(u, 2, axis=-1)
    return jnp.concatenate([u1 * cos - u2 * sin, u1 * sin + u2 * cos], axis=-1)


def apply_rope(x, tables):
    cr, sr, cc, sc = (t.astype(x.dtype) for t in tables)
    half = HEAD_DIM // 2
    return jnp.concatenate([_rotate(x[..., :half], cr, sr),
                            _rotate(x[..., half:], cc, sc)], axis=-1)


def project_heads(h, w_in, qn_a, kn_a, qn_b, kn_b, tables):
    b_, t_, _ = h.shape
    aq, ak, av, bq, bk, bv, ga, gb = jnp.split(h @ w_in, IN_OFFSETS, axis=-1)
    aq = rms_norm(aq.reshape(b_, t_, A_HEADS, HEAD_DIM), qn_a)
    ak = rms_norm(ak.reshape(b_, t_, A_KV_HEADS, HEAD_DIM), kn_a)
    bq = rms_norm(bq.reshape(b_, t_, 2 * B_HEADS, HEAD_DIM), qn_b)
    bk = rms_norm(bk.reshape(b_, t_, 2 * B_HEADS, HEAD_DIM), kn_b)
    if tables is not None:
        aq = apply_rope(aq, tables)
        ak = apply_rope(ak, tables)
        bq = apply_rope(bq, tables)
        bk = apply_rope(bk, tables)
    aq = aq.reshape(b_, t_, A_KV_HEADS, A_GROUP, HEAD_DIM)
    av = av.reshape(b_, t_, A_KV_HEADS, HEAD_DIM)
    bq = bq.reshape(b_, t_, B_HEADS, 2, HEAD_DIM)
    bk = bk.reshape(b_, t_, B_HEADS, 2, HEAD_DIM)
    bv = bv.reshape(b_, t_, B_HEADS, B_V_DIM)
    return aq, ak, av, bq, bk, bv, ga, gb


def gqa_core(q, k, v):
    s = jnp.einsum('bqkgd,bskd->bkgqs', q, k, preferred_element_type=jnp.float32)
    p = jax.nn.softmax(s * (HEAD_DIM ** -0.5), axis=-1)
    return jnp.einsum('bkgqs,bskd->bqkgd', p.astype(v.dtype), v)


def diff_core(q, k, v, lam):
    s = jnp.einsum('bqhmd,bshmd->bhmqs', q, k, preferred_element_type=jnp.float32)
    p = jax.nn.softmax(s * (HEAD_DIM ** -0.5), axis=-1)
    a = p[:, :, 0] - lam * p[:, :, 1]
    return jnp.einsum('bhqs,bshe->bqhe', a.astype(v.dtype), v)


def _to_blocks(a):
    b_, t_ = a.shape[:2]
    a = a.reshape((b_, t_ // Q_BLOCK, Q_BLOCK) + a.shape[2:])
    return jnp.moveaxis(a, 1, 0)


def _from_blocks(a):
    a = jnp.moveaxis(a, 0, 1)
    return a.reshape((a.shape[0], a.shape[1] * a.shape[2]) + a.shape[3:])


def sweep_latent(core, q):
    return _from_blocks(lax.map(core, _to_blocks(q)))


def merge_branches(oa, ob, ga, gb, subln_g, lam_init, wa, wb, wo):
    b_, t_ = oa.shape[:2]
    oa = oa.reshape(b_, t_, A_Q_W)
    ob = (rms_norm(ob, subln_g) * (1.0 - lam_init)).reshape(b_, t_, B_V_W)
    y = jax.nn.sigmoid(ga) * (oa @ wa) + jax.nn.sigmoid(gb) * (ob @ wb)
    return y @ wo


def swiglu(x, wg, wu, wd):
    return (jax.nn.silu(x @ wg) * (x @ wu)) @ wd


def routed_experts(hf, idx, wsel, weg, weu, wed):
    n = hf.shape[0]
    n_assign = n * TOP_K
    flat_e = idx.reshape(-1)
    order = jnp.argsort(flat_e, stable=True)
    sorted_e = flat_e[order]
    sorted_tok = (order // TOP_K).astype(jnp.int32)
    sorted_w = wsel.reshape(-1)[order]
    counts = jnp.bincount(flat_e, length=N_EXPERTS)
    padded = (counts + EXPERT_BLOCK - 1) // EXPERT_BLOCK * EXPERT_BLOCK
    pad_end = jnp.cumsum(padded)
    pad_start = pad_end - padded
    start = jnp.cumsum(counts) - counts
    dest = pad_start[sorted_e] + (jnp.arange(n_assign) - start[sorted_e])
    n_blocks = -(-n_assign // EXPERT_BLOCK) + N_EXPERTS
    total = n_blocks * EXPERT_BLOCK
    tok_buf = jnp.zeros((total,), jnp.int32).at[dest].set(sorted_tok)
    w_buf = jnp.zeros((total,), hf.dtype).at[dest].set(sorted_w)
    block_e = jnp.minimum(
        jnp.searchsorted(pad_end, jnp.arange(n_blocks) * EXPERT_BLOCK, side='right'),
        N_EXPERTS - 1)

    def step(out, blk):
        tok, wt, e = blk
        y = swiglu(hf[tok], weg[e], weu[e], wed[e]) * wt[:, None]
        return out.at[tok].add(y), None

    out, _ = lax.scan(step, jnp.zeros_like(hf),
                      (tok_buf.reshape(n_blocks, EXPERT_BLOCK),
                       w_buf.reshape(n_blocks, EXPERT_BLOCK), block_e))
    return out


def moe(h, w_router, router_bias, weg, weu, wed, wsg, wsu, wsd):
    shape = h.shape
    hf = h.reshape(-1, D_MODEL)
    n = hf.shape[0]
    scores = jax.nn.sigmoid((hf @ w_router).astype(jnp.float32))
    biased = scores + router_bias.astype(jnp.float32)
    per_group = N_EXPERTS // N_GROUPS
    grp_score = lax.top_k(biased.reshape(n, N_GROUPS, per_group), 2)[0].sum(-1)
    _, top_groups = lax.top_k(grp_score, TOPK_GROUPS)
    group_mask = jax.nn.one_hot(top_groups, N_GROUPS, dtype=jnp.float32).sum(1) > 0
    expert_mask = jnp.repeat(group_mask, per_group, axis=1)
    _, idx = lax.top_k(jnp.where(expert_mask, biased, -jnp.inf), TOP_K)
    wsel = jnp.take_along_axis(scores, idx, axis=1)
    wsel = wsel / jnp.sum(wsel, axis=-1, keepdims=True) * ROUTED_SCALE
    routed = routed_experts(hf, idx, wsel.astype(hf.dtype), weg, weu, wed)
    return (routed + swiglu(hf, wsg, wsu, wsd)).reshape(shape)


def setup_inputs(seed: int = 0) -> dict:
    key = jax.random.key(seed)
    ks = jax.random.split(key, 29)
    f32 = jnp.float32

    def nrm(k, shape, scale):
        return jax.random.normal(k, shape, f32) * scale

    def gain(k, shape):
        return 1.0 + 0.02 * jax.random.normal(k, shape, f32)

    D, E, F, FS = D_MODEL, N_EXPERTS, EXPERT_DIM, SHARED_DIM
    return {
        'x': nrm(ks[0], (BATCH, SEQ, D), 1.0),
        'c': nrm(ks[1], (BATCH, D), 1.0),
        'ctx': nrm(ks[2], (BATCH, CTX_LEN, D), 1.0),
        'c_ctx': nrm(ks[3], (D,), 1.0),
        'w_ada': nrm(ks[4], (DEPTH, D, N_MOD * D), 0.5 * D ** -0.5),
        'b_ada': nrm(ks[5], (DEPTH, N_MOD * D), 0.02),
        'norm_mix': gain(ks[6], (DEPTH, D)),
        'norm_ffn': gain(ks[7], (DEPTH, D)),
        'w_in': nrm(ks[8], (DEPTH, D, IN_W), D ** -0.5),
        'q_norm_a': gain(ks[9], (DEPTH, HEAD_DIM)),
        'k_norm_a': gain(ks[10], (DEPTH, HEAD_DIM)),
        'q_norm_b': gain(ks[11], (DEPTH, HEAD_DIM)),
        'k_norm_b': gain(ks[12], (DEPTH, HEAD_DIM)),
        'lambda_q1': nrm(ks[13], (DEPTH, HEAD_DIM), 0.1),
        'lambda_k1': nrm(ks[14], (DEPTH, HEAD_DIM), 0.1),
        'lambda_q2': nrm(ks[15], (DEPTH, HEAD_DIM), 0.1),
        'lambda_k2': nrm(ks[16], (DEPTH, HEAD_DIM), 0.1),
        'subln_b': gain(ks[17], (DEPTH, B_V_DIM)),
        'w_branch_a': nrm(ks[18], (DEPTH, A_Q_W, D), A_Q_W ** -0.5),
        'w_branch_b': nrm(ks[19], (DEPTH, B_V_W, D), B_V_W ** -0.5),
        'w_out': nrm(ks[20], (DEPTH, D, D), D ** -0.5),
        'w_router': nrm(ks[21], (DEPTH, D, E), D ** -0.5),
        'router_bias': nrm(ks[22], (DEPTH, E), 0.01),
        'w_exp_gate': nrm(ks[23], (DEPTH, E, D, F), D ** -0.5),
        'w_exp_up': nrm(ks[24], (DEPTH, E, D, F), D ** -0.5),
        'w_exp_down': nrm(ks[25], (DEPTH, E, F, D), F ** -0.5),
        'w_sh_gate': nrm(ks[26], (DEPTH, D, FS), D ** -0.5),
        'w_sh_up': nrm(ks[27], (DEPTH, D, FS), D ** -0.5),
        'w_sh_down': nrm(ks[28], (DEPTH, FS, D), FS ** -0.5),
    }


def reference(x, c, ctx, c_ctx, w_ada, b_ada, norm_mix, norm_ffn, w_in,
              q_norm_a, k_norm_a, q_norm_b, k_norm_b,
              lambda_q1, lambda_k1, lambda_q2, lambda_k2, subln_b,
              w_branch_a, w_branch_b, w_out, w_router, router_bias,
              w_exp_gate, w_exp_up, w_exp_down, w_sh_gate, w_sh_up, w_sh_down):
    ROWS = x.shape[1] // GRID_W
    tables = rope_tables(ROWS)
    f32 = jnp.float32
    for i in range(DEPTH):
        last = i == DEPTH - 1
        lam_init = 0.8 - 0.6 * math.exp(-0.3 * i)
        lam = (jnp.exp(jnp.sum(lambda_q1[i].astype(f32) * lambda_k1[i].astype(f32)))
               - jnp.exp(jnp.sum(lambda_q2[i].astype(f32) * lambda_k2[i].astype(f32)))
               + lam_init)
        mod = adaln(c, w_ada[i], b_ada[i])
        mod_c = adaln(c_ctx[None, :], w_ada[i], b_ada[i])

        h = modulate(rms_norm(x, norm_mix[i]), mod[:, 0], mod[:, 1])
        hc = modulate(rms_norm(ctx, norm_mix[i]), mod_c[:, 0], mod_c[:, 1])
        aq, ak, av, bq, bk, bv, ga, gb = project_heads(
            h, w_in[i], q_norm_a[i], k_norm_a[i], q_norm_b[i], k_norm_b[i], tables)
        caq, cak, cav, cbq, cbk, cbv, cga, cgb = project_heads(
            hc, w_in[i], q_norm_a[i], k_norm_a[i], q_norm_b[i], k_norm_b[i], None)
        ak_all = jnp.concatenate([cak, ak], axis=1)
        av_all = jnp.concatenate([cav, av], axis=1)
        bk_all = jnp.concatenate([cbk, bk], axis=1)
        bv_all = jnp.concatenate([cbv, bv], axis=1)
        oa = sweep_latent(lambda qb: gqa_core(qb, ak_all, av_all), aq)
        ob = sweep_latent(lambda qb: diff_core(qb, bk_all, bv_all, lam), bq)
        y = merge_branches(oa, ob, ga, gb, subln_b[i], lam_init,
                           w_branch_a[i], w_branch_b[i], w_out[i])
        x = x + mod[:, 2] * y

        if not last:
            oac = gqa_core(caq, cak, cav)
            obc = diff_core(cbq, cbk, cbv, lam)
            yc = merge_branches(oac, obc, cga, cgb, subln_b[i], lam_init,
                                w_branch_a[i], w_branch_b[i], w_out[i])
            ctx = ctx + mod_c[:, 2] * yc
            hc2 = modulate(rms_norm(ctx, norm_ffn[i]), mod_c[:, 3], mod_c[:, 4])
            ctx = ctx + mod_c[:, 5] * moe(hc2, w_router[i], router_bias[i],
                                          w_exp_gate[i], w_exp_up[i], w_exp_down[i],
                                          w_sh_gate[i], w_sh_up[i], w_sh_down[i])

        h2 = modulate(rms_norm(x, norm_ffn[i]), mod[:, 3], mod[:, 4])
        x = x + mod[:, 5] * moe(h2, w_router[i], router_bias[i],
                                w_exp_gate[i], w_exp_up[i], w_exp_down[i],
                                w_sh_gate[i], w_sh_up[i], w_sh_down[i])
    return x
```

```python
import functools
import math

import jax
import jax.numpy as jnp
from jax import lax
from jax.experimental import pallas as pl
from jax.experimental.pallas import tpu as pltpu

F32 = jnp.float32
BF16 = jnp.bfloat16

D_MODEL = 2048
GRID_W = 64
HEAD_DIM = 128
ROPE_PAIRS = HEAD_DIM // 4
ROPE_THETA = 10000.0
A_HEADS = 8
A_KV_HEADS = 2
A_GROUP = A_HEADS // A_KV_HEADS
B_HEADS = 4
B_V_DIM = 2 * HEAD_DIM
N_EXPERTS = 64
TOP_K = 8
N_GROUPS = 8
TOPK_GROUPS = 4
EXPERT_DIM = 512
SHARED_DIM = 512
ROUTED_SCALE = 2.5
N_MOD = 6
EPS = 1e-6

A_Q_W = A_HEADS * HEAD_DIM
A_KV_W = A_KV_HEADS * HEAD_DIM
B_QK_W = B_HEADS * 2 * HEAD_DIM
B_V_W = B_HEADS * B_V_DIM
IN_W = A_Q_W + 2 * A_KV_W + 2 * B_QK_W + B_V_W + 2 * D_MODEL

COL_AQ = 0
COL_AK = A_Q_W // HEAD_DIM
COL_AV = COL_AK + A_KV_HEADS
COL_BQ = COL_AV + A_KV_HEADS
COL_BK = COL_BQ + 2 * B_HEADS
COL_BV = COL_BK + 2 * B_HEADS
COL_GA = COL_BV + B_V_W // HEAD_DIM
COL_GB = COL_GA + D_MODEL // HEAD_DIM

LANES = 128
SUBLANES = 8
VMEM_LIMIT = 56 * 1024 * 1024

PROJ_TN = 512
MOE_BLOCK = 128
ONES_ROWS = 16
LOG2E = 1.4426950408889634


def _cparams(sem, vmem=VMEM_LIMIT):
    return pltpu.CompilerParams(dimension_semantics=sem, vmem_limit_bytes=vmem)


def _adaln_kernel(cb_ref, w_ref, b_ref, o_ref):
    tn = w_ref.shape[1]
    nl = tn // LANES
    rows = 32

    def body(g, accs):
        accs = list(accs)
        r0 = pl.multiple_of(g * rows, rows)
        for u in range(rows // SUBLANES):
            r = r0 + u * SUBLANES
            w = w_ref[pl.ds(r, SUBLANES), :]
            for v in range(2):
                c = cb_ref[v, pl.ds(r, SUBLANES), :]
                s = c * jax.nn.sigmoid(c)
                for j in range(nl):
                    accs[v * nl + j] = accs[v * nl + j] + w[:, j * LANES:(j + 1) * LANES] * s
        return tuple(accs)

    init = tuple(jnp.zeros((SUBLANES, LANES), F32) for _ in range(2 * nl))
    accs = lax.fori_loop(0, w_ref.shape[0] // rows, body, init)
    for v in range(2):
        row = jnp.concatenate(
            [jnp.sum(accs[v * nl + j], axis=0, keepdims=True) for j in range(nl)], axis=1)
        o_ref[v:v + 1, :] = row + b_ref[...]


def _adaln(cvecs, w, b):
    d, n = w.shape
    tn = 1536
    cb = jnp.broadcast_to(cvecs[:, :, None], (2, d, LANES))
    return pl.pallas_call(
        _adaln_kernel,
        grid=(n // tn,),
        in_specs=[pl.BlockSpec((2, d, LANES), lambda j: (0, 0, 0)),
                  pl.BlockSpec((d, tn), lambda j: (0, j)),
                  pl.BlockSpec((1, tn), lambda j: (0, j))],
        out_specs=pl.BlockSpec((2, tn), lambda j: (0, j)),
        out_shape=jax.ShapeDtypeStruct((2, n), F32),
        compiler_params=_cparams(("arbitrary",)),
        name="adaln",
    )(cb, w, b.reshape(1, n))


def _rms_mod(x, g, shift, scale):
    y = x * lax.rsqrt(jnp.mean(x * x, axis=-1, keepdims=True) + EPS) * g
    return y * (1.0 + scale) + shift


def _prenorm_kernel(x_ref, c_ref, g_ref, mod_ref, o_ref, *, n_lat_tiles):
    is_ctx = pl.program_id(0) >= n_lat_tiles
    x = jnp.where(is_ctx, c_ref[...], x_ref[...])
    o_ref[...] = _rms_mod(x, g_ref[...], mod_ref[0, 0:1, :], mod_ref[0, 1:2, :]).astype(o_ref.dtype)


def _prenorm(x, ctx, g, mod):
    s, d = x.shape
    c = ctx.shape[0]
    tm = 256
    nl, nc = s // tm, c // tm
    return pl.pallas_call(
        functools.partial(_prenorm_kernel, n_lat_tiles=nl),
        grid=(nl + nc,),
        in_specs=[pl.BlockSpec((tm, d), lambda i: (jnp.minimum(i, nl - 1), 0)),
                  pl.BlockSpec((tm, d), lambda i: (jnp.maximum(i - nl, 0), 0)),
                  pl.BlockSpec((1, d), lambda i: (0, 0)),
                  pl.BlockSpec((1, N_MOD, d), lambda i: (i // nl, 0, 0))],
        out_specs=pl.BlockSpec((tm, d), lambda i: (i, 0)),
        out_shape=jax.ShapeDtypeStruct((s + c, d), BF16),
        compiler_params=_cparams(("arbitrary",)),
        name="prenorm_mix",
    )(x, ctx, g.reshape(1, d), mod)


def _inproj_kernel(h_ref, w_ref, gain_ref, c_ref, sa_ref, sb_ref, o_ref):
    j = pl.program_id(1)
    acc = jnp.dot(h_ref[...], w_ref[...], preferred_element_type=F32)
    nh = acc.shape[1] // HEAD_DIM

    def norm_rope(a, gain):
        y = a * lax.rsqrt(jnp.mean(a * a, axis=-1, keepdims=True) + EPS) * gain
        return (y * c_ref[...] + pltpu.roll(y, ROPE_PAIRS, 1) * sa_ref[...]
                + pltpu.roll(y, HEAD_DIM - ROPE_PAIRS, 1) * sb_ref[...])

    def store(n_normed):
        for hd in range(nh):
            sl = slice(hd * HEAD_DIM, (hd + 1) * HEAD_DIM)
            a = acc[:, sl]
            if hd < n_normed:
                a = norm_rope(a, gain_ref[0, :, sl])
            o_ref[:, sl] = a.astype(o_ref.dtype)

    all_normed = (j < 2) | ((j >= 3) & (j < 7))
    pl.when(all_normed)(lambda: store(nh))
    pl.when(j == 2)(lambda: store(A_KV_HEADS))
    pl.when(j >= 7)(lambda: store(0))


def _inproj(h, w_bf16, gains, rope_c, rope_sa, rope_sb):
    t, d = h.shape
    n = w_bf16.shape[1]
    tm = t // 8
    tn = PROJ_TN
    return pl.pallas_call(
        _inproj_kernel,
        grid=(t // tm, n // tn),
        in_specs=[pl.BlockSpec((tm, d), lambda i, j: (i, 0)),
                  pl.BlockSpec((d, tn), lambda i, j: (0, j)),
                  pl.BlockSpec((1, 1, tn), lambda i, j: (j, 0, 0)),
                  pl.BlockSpec((tm, HEAD_DIM), lambda i, j: (i, 0)),
                  pl.BlockSpec((tm, HEAD_DIM), lambda i, j: (i, 0)),
                  pl.BlockSpec((tm, HEAD_DIM), lambda i, j: (i, 0))],
        out_specs=pl.BlockSpec((tm, tn), lambda i, j: (i, j)),
        out_shape=jax.ShapeDtypeStruct((t, n), BF16),
        compiler_params=_cparams(("arbitrary", "arbitrary")),
        name="inproj",
    )(h, w_bf16, gains, rope_c, rope_sa, rope_sb)


def _rope_tables(s, c):
    rows_n = s // GRID_W
    rows = jnp.repeat(jnp.arange(rows_n, dtype=F32), GRID_W)
    cols = jnp.tile(jnp.arange(GRID_W, dtype=F32), rows_n)
    inv = ROPE_THETA ** (-jnp.arange(ROPE_PAIRS, dtype=F32) / ROPE_PAIRS)
    ang_r = rows[:, None] * inv
    ang_c = cols[:, None] * inv
    cr, sr, cc, sc = jnp.cos(ang_r), jnp.sin(ang_r), jnp.cos(ang_c), jnp.sin(ang_c)
    z = jnp.zeros_like(sr)
    tc = jnp.concatenate([cr, cr, cc, cc], axis=1)
    tsa = jnp.concatenate([z, sr, z, sc], axis=1)
    tsb = jnp.concatenate([-sr, z, -sc, z], axis=1)
    pad = lambda a, v: jnp.concatenate([a, jnp.full((c, HEAD_DIM), v, F32)], axis=0)
    return pad(tc, 1.0), pad(tsa, 0.0), pad(tsb, 0.0)


def _head_gains(qn_a, kn_a, qn_b, kn_b):
    qs = HEAD_DIM ** -0.5 * LOG2E
    one = jnp.ones((HEAD_DIM,), F32)
    heads = ([qn_a * qs] * A_HEADS + [kn_a] * A_KV_HEADS + [one] * A_KV_HEADS
             + [qn_b * qs] * (2 * B_HEADS) + [kn_b] * (2 * B_HEADS))
    heads = heads + [one] * (IN_W // HEAD_DIM - len(heads))
    return jnp.concatenate(heads).reshape(IN_W // PROJ_TN, 1, PROJ_TN)


def _build_vt(v_ref, vt_ref, tk):
    n_chunks, rows, _ = vt_ref.shape
    dv = rows - ONES_ROWS
    tail = (lax.broadcasted_iota(jnp.int32, (ONES_ROWS, tk), 0) == 0).astype(vt_ref.dtype)
    for c in range(n_chunks):
        vt_ref[c, 0:dv, :] = v_ref[c * tk:(c + 1) * tk, :].astype(F32).T.astype(vt_ref.dtype)
        vt_ref[c, dv:rows, :] = tail


def _attend(q, k_ref, vt_ref, m_ref, acc_ref):
    n_chunks, _, tk = vt_ref.shape
    m_ref[...] = jnp.full(m_ref.shape, -jnp.inf, F32)
    acc_ref[...] = jnp.zeros(acc_ref.shape, F32)

    def body(c, carry):
        off = pl.multiple_of(c * tk, tk)
        s = lax.dot_general(k_ref[pl.ds(off, tk), :], q, (((1,), (1,)), ((), ())),
                            preferred_element_type=F32)
        m_old = m_ref[...]
        m_new = jnp.maximum(m_old, jnp.max(s, axis=0, keepdims=True))
        p = jnp.exp2(s - m_new).astype(vt_ref.dtype)
        acc_ref[...] = (acc_ref[...] * jnp.exp2(m_old - m_new)
                        + jnp.dot(vt_ref[c], p, preferred_element_type=F32))
        m_ref[...] = m_new
        return carry

    lax.fori_loop(0, n_chunks, body, 0)


def _gqa_kernel(q_ref, k_ref, v_ref, o_ref, vt_ref, m_ref, acc_ref):
    tk = vt_ref.shape[2]
    pl.when((pl.program_id(1) == 0) & (pl.program_id(2) == 0))(lambda: _build_vt(v_ref, vt_ref, tk))
    _attend(q_ref[...], k_ref, vt_ref, m_ref, acc_ref)
    o_t = acc_ref[0:HEAD_DIM, :] / acc_ref[HEAD_DIM:HEAD_DIM + 1, :]
    o_ref[...] = o_t.T.astype(o_ref.dtype)


def _key_chunk(t):
    for tk in (768, 1024, 512, 640, 384, 256, 128):
        if t % tk == 0:
            return tk
    raise ValueError(f"unsupported key count {t}")


def _gqa(proj, s):
    t = proj.shape[0]
    tq = 512
    tk = _key_chunk(t)
    return pl.pallas_call(
        _gqa_kernel,
        grid=(A_KV_HEADS, s // tq, A_GROUP),
        in_specs=[pl.BlockSpec((tq, HEAD_DIM), lambda g, i, hh: (i, COL_AQ + g * A_GROUP + hh)),
                  pl.BlockSpec((t, HEAD_DIM), lambda g, i, hh: (0, COL_AK + g)),
                  pl.BlockSpec((t, HEAD_DIM), lambda g, i, hh: (0, COL_AV + g))],
        out_specs=pl.BlockSpec((tq, HEAD_DIM), lambda g, i, hh: (i, g * A_GROUP + hh)),
        out_shape=jax.ShapeDtypeStruct((s, A_Q_W), BF16),
        scratch_shapes=[pltpu.VMEM((t // tk, HEAD_DIM + ONES_ROWS, tk), BF16),
                        pltpu.VMEM((1, tq), F32),
                        pltpu.VMEM((HEAD_DIM + ONES_ROWS, tq), F32)],
        compiler_params=_cparams(("arbitrary", "arbitrary", "arbitrary")),
        name="gqa_attn",
    )(proj, proj, proj)


def _diff_kernel(lam_ref, q0_ref, q1_ref, k0_ref, k1_ref, v_ref, g_ref, o_ref,
                 vt_ref, m_ref, acc0_ref, acc1_ref, *, lam_init):
    tk = vt_ref.shape[2]
    pl.when(pl.program_id(1) == 0)(lambda: _build_vt(v_ref, vt_ref, tk))
    lv = lam_ref[...]
    lam = (jnp.exp(jnp.sum(lv[0:1, :] * lv[1:2, :], axis=-1, keepdims=True))
           - jnp.exp(jnp.sum(lv[2:3, :] * lv[3:4, :], axis=-1, keepdims=True)) + lam_init)
    _attend(q0_ref[...], k0_ref, vt_ref, m_ref, acc0_ref)
    _attend(q1_ref[...], k1_ref, vt_ref, m_ref, acc1_ref)
    o_t = (acc0_ref[0:B_V_DIM, :] / acc0_ref[B_V_DIM:B_V_DIM + 1, :]
           - lam * (acc1_ref[0:B_V_DIM, :] / acc1_ref[B_V_DIM:B_V_DIM + 1, :]))
    o = o_t.T
    y = o * lax.rsqrt(jnp.mean(o * o, axis=-1, keepdims=True) + EPS) * g_ref[...]
    o_ref[...] = (y * (1.0 - lam_init)).astype(o_ref.dtype)


def _diff(proj, s, lam_vecs, subln_g, lam_init):
    t = proj.shape[0]
    tq = 512
    tk = _key_chunk(t)
    vb = B_V_DIM // HEAD_DIM
    return pl.pallas_call(
        functools.partial(_diff_kernel, lam_init=lam_init),
        grid=(B_HEADS, s // tq),
        in_specs=[pl.BlockSpec((4, HEAD_DIM), lambda h, i: (0, 0)),
                  pl.BlockSpec((tq, HEAD_DIM), lambda h, i: (i, COL_BQ + 2 * h)),
                  pl.BlockSpec((tq, HEAD_DIM), lambda h, i: (i, COL_BQ + 2 * h + 1)),
                  pl.BlockSpec((t, HEAD_DIM), lambda h, i: (0, COL_BK + 2 * h)),
                  pl.BlockSpec((t, HEAD_DIM), lambda h, i: (0, COL_BK + 2 * h + 1)),
                  pl.BlockSpec((t, B_V_DIM), lambda h, i: (0, COL_BV // vb + h)),
                  pl.BlockSpec((1, B_V_DIM), lambda h, i: (0, 0))],
        out_specs=pl.BlockSpec((tq, B_V_DIM), lambda h, i: (i, h)),
        out_shape=jax.ShapeDtypeStruct((s, B_V_W), BF16),
        scratch_shapes=[pltpu.VMEM((t // tk, B_V_DIM + ONES_ROWS, tk), BF16),
                        pltpu.VMEM((1, tq), F32),
                        pltpu.VMEM((B_V_DIM + ONES_ROWS, tq), F32),
                        pltpu.VMEM((B_V_DIM + ONES_ROWS, tq), F32)],
        compiler_params=_cparams(("arbitrary", "arbitrary")),
        name="diff_attn",
    )(lam_vecs, proj, proj, proj, proj, proj, subln_g.reshape(1, B_V_DIM))


def _merge_kernel(oa_ref, ob_ref, ga_ref, gb_ref, wa_ref, wb_ref, wo_ref, x_ref, gate_ref, o_ref, t_ref):
    c = pl.program_id(1)
    nc, _, tn = t_ref.shape
    ya = jnp.dot(oa_ref[...], wa_ref[...], preferred_element_type=F32)
    yb = jnp.dot(ob_ref[...], wb_ref[...], preferred_element_type=F32)
    t = (jax.nn.sigmoid(ga_ref[...].astype(F32)) * ya + jax.nn.sigmoid(gb_ref[...].astype(F32)) * yb)
    t_ref[c] = t.astype(t_ref.dtype)

    @pl.when(c == nc - 1)
    def _():
        y = jnp.dot(t_ref[0], wo_ref[0:tn, :], preferred_element_type=F32)
        for cc in range(1, nc):
            y = y + jnp.dot(t_ref[cc], wo_ref[cc * tn:(cc + 1) * tn, :], preferred_element_type=F32)
        o_ref[...] = x_ref[...] + gate_ref[...] * y


def _merge(oa, ob, proj, wa, wb, wo, x, gate):
    s, d = x.shape
    tm = 512
    tn = PROJ_TN
    nc = d // tn
    ga0 = COL_GA * HEAD_DIM // tn
    gb0 = COL_GB * HEAD_DIM // tn
    return pl.pallas_call(
        _merge_kernel,
        grid=(s // tm, nc),
        in_specs=[pl.BlockSpec((tm, A_Q_W), lambda i, c: (i, 0)),
                  pl.BlockSpec((tm, B_V_W), lambda i, c: (i, 0)),
                  pl.BlockSpec((tm, tn), lambda i, c: (i, ga0 + c)),
                  pl.BlockSpec((tm, tn), lambda i, c: (i, gb0 + c)),
                  pl.BlockSpec((A_Q_W, tn), lambda i, c: (0, c)),
                  pl.BlockSpec((B_V_W, tn), lambda i, c: (0, c)),
                  pl.BlockSpec((d, d), lambda i, c: (0, 0)),
                  pl.BlockSpec((tm, d), lambda i, c: (i, 0)),
                  pl.BlockSpec((1, d), lambda i, c: (0, 0))],
        out_specs=pl.BlockSpec((tm, d), lambda i, c: (i, 0)),
        out_shape=jax.ShapeDtypeStruct((s, d), F32),
        scratch_shapes=[pltpu.VMEM((nc, tm, tn), BF16)],
        compiler_params=_cparams(("arbitrary", "arbitrary")),
        name="merge_out",
    )(oa, ob, proj, proj, wa, wb, wo, x, gate)


def _lane_max(x):
    return jnp.max(x, axis=-1, keepdims=True)


def _lane_min(x):
    return jnp.min(x, axis=-1, keepdims=True)


def _group_allreduce(x, lane, op):
    for sft in (1, 2, 4):
        up = pltpu.roll(x, sft, 1)
        dn = pltpu.roll(x, LANES - sft, 1)
        x = op(x, jnp.where((lane & sft) != 0, up, dn))
    return x


def _router_kernel(x_ref, g_ref, mod_ref, wr_ref, rb_ref,
                   h_ref, hp_ref, idx_ref, wsel_ref, rank_ref, cnt_ref, carry_ref):
    @pl.when(pl.program_id(0) == 0)
    def _():
        carry_ref[...] = jnp.zeros_like(carry_ref)

    h = _rms_mod(x_ref[...], g_ref[...], mod_ref[0, 3:4, :], mod_ref[0, 4:5, :])
    hb = h.astype(BF16)
    h_ref[...] = hb
    bits = lax.bitcast_convert_type(hb.astype(F32), jnp.uint32)
    half = bits.shape[1] // 2
    hp_ref[...] = (bits[:, :half] >> 16) | (bits[:, half:] & jnp.uint32(0xFFFF0000))

    tm = h.shape[0]
    logits = jnp.dot(h, wr_ref[...], preferred_element_type=F32, precision=lax.Precision.HIGHEST)
    scores = jax.nn.sigmoid(logits)
    lane = lax.broadcasted_iota(jnp.int32, (tm, LANES), 1)
    valid = lane < N_EXPERTS
    neg = jnp.float32(-jnp.inf)
    big = jnp.int32(LANES)
    biased = jnp.where(valid, scores + rb_ref[...], neg)

    m1 = _group_allreduce(biased, lane, jnp.maximum)
    a1 = _group_allreduce(jnp.where(biased == m1, lane, big), lane, jnp.minimum)
    m2 = _group_allreduce(jnp.where(lane == a1, neg, biased), lane, jnp.maximum)
    gscore = jnp.where(valid, m1 + m2, neg)
    gid = lane >> 3
    keep = jnp.zeros((tm, LANES), jnp.bool_)
    for _ in range(TOPK_GROUPS):
        best = _lane_max(gscore)
        gsel = _lane_min(jnp.where(gscore == best, gid, big))
        hit = gid == gsel
        keep = keep | hit
        gscore = jnp.where(hit, neg, gscore)
    cand = jnp.where(keep & valid, biased, neg)

    onehot = jnp.zeros((tm, LANES), F32)
    idx_out = jnp.zeros((tm, LANES), jnp.int32)
    w_out = jnp.zeros((tm, LANES), F32)
    sels = []
    for k in range(TOP_K):
        best = _lane_max(cand)
        sel = _lane_min(jnp.where(cand == best, lane, big))
        hit = lane == sel
        wk = jnp.sum(jnp.where(hit, scores, 0.0), axis=-1, keepdims=True)
        cand = jnp.where(hit, neg, cand)
        onehot = jnp.where(hit, 1.0, onehot)
        idx_out = jnp.where(lane == k, sel, idx_out)
        w_out = jnp.where(lane == k, wk, w_out)
        sels.append(hit)
    wsum = jnp.sum(w_out, axis=-1, keepdims=True)
    wsel_ref[...] = w_out / wsum * ROUTED_SCALE
    idx_ref[...] = idx_out

    row = lax.broadcasted_iota(jnp.int32, (tm, tm), 0)
    col = lax.broadcasted_iota(jnp.int32, (tm, tm), 1)
    lower = (col < row).astype(BF16)
    before = jnp.dot(lower, onehot.astype(BF16), preferred_element_type=F32) + carry_ref[...]
    rank_out = jnp.zeros((tm, LANES), F32)
    for k in range(TOP_K):
        rk = jnp.sum(jnp.where(sels[k], before, 0.0), axis=-1, keepdims=True)
        rank_out = jnp.where(lane == k, rk, rank_out)
    rank_ref[...] = rank_out.astype(jnp.int32)
    carry_ref[...] = carry_ref[...] + jnp.sum(onehot, axis=0, keepdims=True)
    cnt_ref[...] = carry_ref[...]


def _router(x1, g, mod, w_router, router_bias):
    s, d = x1.shape
    tm = 256
    e = w_router.shape[1]
    wr = jnp.pad(w_router, ((0, 0), (0, LANES - e)))
    rb = jnp.pad(router_bias, (0, LANES - e)).reshape(1, LANES)
    row_spec = lambda w: pl.BlockSpec((tm, w), lambda i: (i, 0))
    return pl.pallas_call(
        _router_kernel,
        grid=(s // tm,),
        in_specs=[row_spec(d),
                  pl.BlockSpec((1, d), lambda i: (0, 0)),
                  pl.BlockSpec((1, N_MOD, d), lambda i: (0, 0, 0)),
                  pl.BlockSpec((d, LANES), lambda i: (0, 0)),
                  pl.BlockSpec((1, LANES), lambda i: (0, 0))],
        out_specs=[row_spec(d), row_spec(d // 2), row_spec(LANES), row_spec(LANES), row_spec(LANES),
                   pl.BlockSpec((1, LANES), lambda i: (0, 0))],
        out_shape=[jax.ShapeDtypeStruct((s, d), BF16),
                   jax.ShapeDtypeStruct((s, d // 2), jnp.uint32),
                   jax.ShapeDtypeStruct((s, LANES), jnp.int32),
                   jax.ShapeDtypeStruct((s, LANES), F32),
                   jax.ShapeDtypeStruct((s, LANES), jnp.int32),
                   jax.ShapeDtypeStruct((1, LANES), F32)],
        scratch_shapes=[pltpu.VMEM((1, LANES), F32)],
        compiler_params=_cparams(("arbitrary",)),
        name="ffn_router",
    )(x1, g.reshape(1, d), mod, wr, rb)


def _unpack_rows(words):
    lo = lax.bitcast_convert_type(words << 16, F32).astype(BF16)
    hi = lax.bitcast_convert_type(words & jnp.uint32(0xFFFF0000), F32).astype(BF16)
    return jnp.concatenate([lo, hi], axis=1)


def _moe_kernel(be_ref, nused_ref, tok_hbm, hp_hbm, w_ref, wg_ref, wu_ref, wd_ref, o_ref,
                idx_smem, xbuf, wgb, wub, wdb, sem_idx, sem_x):
    b = pl.program_id(0)
    nrows = xbuf.shape[0]

    @pl.when(b < nused_ref[0])
    def _():
        idx_copy = pltpu.make_async_copy(tok_hbm.at[b], idx_smem, sem_idx)
        idx_copy.start()
        first = (b == 0) | (be_ref[jnp.maximum(b - 1, 0)] != be_ref[b])

        @pl.when(first)
        def _():
            wgb[...] = wg_ref[0].astype(BF16)
            wub[...] = wu_ref[0].astype(BF16)
            wdb[...] = wd_ref[0].astype(BF16)

        idx_copy.wait()

        def row_copy(r):
            return pltpu.make_async_copy(hp_hbm.at[pl.ds(idx_smem[r], 1), :],
                                         xbuf.at[pl.ds(r, 1), :], sem_x)

        def issue(r, carry):
            row_copy(r).start()
            return carry

        lax.fori_loop(0, nrows, issue, 0)

        def drain(r, carry):
            row_copy(r).wait()
            return carry

        lax.fori_loop(0, nrows, drain, 0)

        x = _unpack_rows(xbuf[...])
        gate = jnp.dot(x, wgb[...], preferred_element_type=F32)
        up = jnp.dot(x, wub[...], preferred_element_type=F32)
        act = (gate * jax.nn.sigmoid(gate) * up).astype(BF16)
        o_ref[...] = jnp.dot(act, wdb[...], preferred_element_type=F32) * w_ref[0]

    @pl.when(b >= nused_ref[0])
    def _():
        o_ref[...] = jnp.zeros_like(o_ref)


def _moe(block_e, n_used, tok_buf, w_buf, hp, weg, weu, wed):
    n_blocks = block_e.shape[0]
    e, d, f = weg.shape
    rows = MOE_BLOCK
    wspec = lambda shp: pl.BlockSpec((1,) + shp, lambda b, be, nu: (be[b], 0, 0))
    grid_spec = pltpu.PrefetchScalarGridSpec(
        num_scalar_prefetch=2,
        grid=(n_blocks,),
        in_specs=[pl.BlockSpec(memory_space=pl.ANY),
                  pl.BlockSpec(memory_space=pl.ANY),
                  pl.BlockSpec((1, rows, 1), lambda b, be, nu: (b, 0, 0)),
                  wspec((d, f)), wspec((d, f)), wspec((f, d))],
        out_specs=pl.BlockSpec((rows, d), lambda b, be, nu: (b, 0)),
        scratch_shapes=[pltpu.SMEM((rows,), jnp.int32),
                        pltpu.VMEM((rows, d // 2), jnp.uint32),
                        pltpu.VMEM((d, f), BF16), pltpu.VMEM((d, f), BF16), pltpu.VMEM((f, d), BF16),
                        pltpu.SemaphoreType.DMA, pltpu.SemaphoreType.DMA])
    return pl.pallas_call(
        _moe_kernel,
        grid_spec=grid_spec,
        out_shape=jax.ShapeDtypeStruct((n_blocks * rows, d), F32),
        compiler_params=_cparams(("arbitrary",)),
        name="moe_experts",
    )(block_e, n_used, tok_buf, hp, w_buf, weg, weu, wed)


def _combine_kernel(dest_hbm, y_hbm, h_ref, wg_ref, wu_ref, wd_ref, x_ref, gate_ref, o_ref,
                    idx_smem, ybuf, sem_idx, sem_y):
    i = pl.program_id(0)
    tm = h_ref.shape[0]
    n = idx_smem.shape[0]
    idx_copy = pltpu.make_async_copy(dest_hbm.at[i], idx_smem, sem_idx)
    idx_copy.start()
    idx_copy.wait()

    def row_copy(r):
        return pltpu.make_async_copy(y_hbm.at[pl.ds(idx_smem[r], 1), :], ybuf.at[pl.ds(r, 1), :], sem_y)

    def issue(r, carry):
        row_copy(r).start()
        return carry

    lax.fori_loop(0, n, issue, 0)

    h = h_ref[...]
    gate = jnp.dot(h, wg_ref[...], preferred_element_type=F32)
    up = jnp.dot(h, wu_ref[...], preferred_element_type=F32)
    act = (gate * jax.nn.sigmoid(gate) * up).astype(BF16)
    shared = jnp.dot(act, wd_ref[...], preferred_element_type=F32)

    def drain(r, carry):
        row_copy(r).wait()
        return carry

    lax.fori_loop(0, n, drain, 0)

    routed = ybuf[0:tm, :]
    for k in range(1, TOP_K):
        routed = routed + ybuf[k * tm:(k + 1) * tm, :]
    o_ref[...] = x_ref[...] + gate_ref[...] * (routed + shared)


def _combine(dest_km, y_sorted, h2, wsg, wsu, wsd, x1, gate):
    s, d = x1.shape
    tm = 128
    f = wsg.shape[1]
    return pl.pallas_call(
        _combine_kernel,
        grid=(s // tm,),
        in_specs=[pl.BlockSpec(memory_space=pl.ANY),
                  pl.BlockSpec(memory_space=pl.ANY),
                  pl.BlockSpec((tm, d), lambda i: (i, 0)),
                  pl.BlockSpec((d, f), lambda i: (0, 0)),
                  pl.BlockSpec((d, f), lambda i: (0, 0)),
                  pl.BlockSpec((f, d), lambda i: (0, 0)),
                  pl.BlockSpec((tm, d), lambda i: (i, 0)),
                  pl.BlockSpec((1, d), lambda i: (0, 0))],
        out_specs=pl.BlockSpec((tm, d), lambda i: (i, 0)),
        out_shape=jax.ShapeDtypeStruct((s, d), F32),
        scratch_shapes=[pltpu.SMEM((tm * TOP_K,), jnp.int32),
                        pltpu.VMEM((tm * TOP_K, d), F32),
                        pltpu.SemaphoreType.DMA, pltpu.SemaphoreType.DMA],
        compiler_params=_cparams(("arbitrary",)),
        name="moe_combine",
    )(dest_km, y_sorted, h2, wsg, wsu, wsd, x1, gate)


def _dispatch_tables(idx, rank, wsel, counts, s):
    n_assign = s * TOP_K
    n_blocks = -(-n_assign // MOE_BLOCK) + N_EXPERTS
    total = n_blocks * MOE_BLOCK
    padded = (counts + MOE_BLOCK - 1) // MOE_BLOCK * MOE_BLOCK
    pad_end = jnp.cumsum(padded)
    pad_start = pad_end - padded
    dest = pad_start[idx] + rank
    flat = dest.reshape(-1)
    tok = jnp.repeat(jnp.arange(s, dtype=jnp.int32), TOP_K)
    tok_buf = jnp.zeros((total,), jnp.int32).at[flat].set(tok)
    w_buf = jnp.zeros((total,), F32).at[flat].set(wsel.reshape(-1))
    block_e = jnp.minimum(
        jnp.searchsorted(pad_end, jnp.arange(n_blocks, dtype=jnp.int32) * MOE_BLOCK, side='right'),
        N_EXPERTS - 1).astype(jnp.int32)
    n_used = (pad_end[-1] // MOE_BLOCK).astype(jnp.int32).reshape(1)
    return dest, tok_buf.reshape(n_blocks, MOE_BLOCK), w_buf.reshape(n_blocks, MOE_BLOCK, 1), block_e, n_used


def kernel(x, c, ctx, c_ctx, w_ada, b_ada, norm_mix, norm_ffn, w_in, q_norm_a, k_norm_a, q_norm_b, k_norm_b, lambda_q1, lambda_k1, lambda_q2, lambda_k2, subln_b, w_branch_a, w_branch_b, w_out, w_router, router_bias, w_exp_gate, w_exp_up, w_exp_down, w_sh_gate, w_sh_up, w_sh_down):
    depth = w_ada.shape[0]
    assert depth == 1 and x.shape[0] == 1 and ctx.shape[0] == 1
    s, d = x.shape[1], x.shape[2]
    n_ctx = ctx.shape[1]
    i = 0
    lam_init = 0.8 - 0.6 * math.exp(-0.3 * i)
    xs = x[0]

    mod = _adaln(jnp.concatenate([c, c_ctx[None, :]], axis=0), w_ada[i], b_ada[i]).reshape(2, N_MOD, d)

    h = _prenorm(xs, ctx[0], norm_mix[i], mod)
    tc, tsa, tsb = _rope_tables(s, n_ctx)
    gains = _head_gains(q_norm_a[i], k_norm_a[i], q_norm_b[i], k_norm_b[i])
    proj = _inproj(h, w_in[i].astype(BF16), gains, tc, tsa, tsb)
    oa = _gqa(proj, s)
    lam_vecs = jnp.stack([lambda_q1[i], lambda_k1[i], lambda_q2[i], lambda_k2[i]]).astype(F32)
    ob = _diff(proj, s, lam_vecs, subln_b[i], lam_init)
    x1 = _merge(oa, ob, proj, w_branch_a[i].astype(BF16), w_branch_b[i].astype(BF16),
                w_out[i].astype(BF16), xs, mod[0, 2:3, :])

    h2, h2p, idx, wsel, rank, cnt = _router(x1, norm_ffn[i], mod[0:1], w_router[i], router_bias[i])
    counts = cnt[0, :N_EXPERTS].astype(jnp.int32)
    dest, tok_buf, w_buf, block_e, n_used = _dispatch_tables(
        idx[:, :TOP_K], rank[:, :TOP_K], wsel[:, :TOP_K], counts, s)
    y_sorted = _moe(block_e, n_used, tok_buf, w_buf, h2p, w_exp_gate[i], w_exp_up[i], w_exp_down[i])
    tm_c = 128
    dest_km = dest.reshape(s // tm_c, tm_c, TOP_K).transpose(0, 2, 1).reshape(s // tm_c, tm_c * TOP_K)
    out = _combine(dest_km, y_sorted, h2, w_sh_gate[i].astype(BF16), w_sh_up[i].astype(BF16),
                   w_sh_down[i].astype(BF16), x1, mod[0, 5:6, :])
    return out[None]
```

```python
import functools
import math

import jax
import jax.numpy as jnp
from jax import lax
from jax.experimental import pallas as pl
from jax.experimental.pallas import tpu as pltpu

F32 = jnp.float32
BF16 = jnp.bfloat16

D_MODEL = 2048
GRID_W = 64
HEAD_DIM = 128
ROPE_PAIRS = HEAD_DIM // 4
ROPE_THETA = 10000.0
A_HEADS = 8
A_KV_HEADS = 2
A_GROUP = A_HEADS // A_KV_HEADS
B_HEADS = 4
B_V_DIM = 2 * HEAD_DIM
N_EXPERTS = 64
TOP_K = 8
N_GROUPS = 8
TOPK_GROUPS = 4
EXPERT_DIM = 512
SHARED_DIM = 512
ROUTED_SCALE = 2.5
N_MOD = 6
EPS = 1e-6

A_Q_W = A_HEADS * HEAD_DIM
A_KV_W = A_KV_HEADS * HEAD_DIM
B_QK_W = B_HEADS * 2 * HEAD_DIM
B_V_W = B_HEADS * B_V_DIM
IN_W = A_Q_W + 2 * A_KV_W + 2 * B_QK_W + B_V_W + 2 * D_MODEL

COL_AQ = 0
COL_AK = A_Q_W // HEAD_DIM
COL_AV = COL_AK + A_KV_HEADS
COL_BQ = COL_AV + A_KV_HEADS
COL_BK = COL_BQ + 2 * B_HEADS
COL_BV = COL_BK + 2 * B_HEADS
COL_GA = COL_BV + B_V_W // HEAD_DIM
COL_GB = COL_GA + D_MODEL // HEAD_DIM

LANES = 128
SUBLANES = 8
VMEM_LIMIT = 56 * 1024 * 1024

PROJ_TN = 512
MOE_BLOCK = 128
ONES_ROWS = 16
PACK_TILES = D_MODEL // 2 // LANES
ROW_TILES = D_MODEL // LANES
LOG2E = 1.4426950408889634


def _cparams(sem, vmem=VMEM_LIMIT):
    return pltpu.CompilerParams(dimension_semantics=sem, vmem_limit_bytes=vmem)


def _adaln_kernel(cb_ref, w_ref, b_ref, o_ref):
    tn = w_ref.shape[1]
    nl = tn // LANES
    rows = 32

    def body(g, accs):
        accs = list(accs)
        r0 = pl.multiple_of(g * rows, rows)
        for u in range(rows // SUBLANES):
            r = r0 + u * SUBLANES
            w = w_ref[pl.ds(r, SUBLANES), :]
            for v in range(2):
                c = cb_ref[v, pl.ds(r, SUBLANES), :]
                s = c * jax.nn.sigmoid(c)
                for j in range(nl):
                    accs[v * nl + j] = accs[v * nl + j] + w[:, j * LANES:(j + 1) * LANES] * s
        return tuple(accs)

    init = tuple(jnp.zeros((SUBLANES, LANES), F32) for _ in range(2 * nl))
    accs = lax.fori_loop(0, w_ref.shape[0] // rows, body, init)
    for v in range(2):
        row = jnp.concatenate(
            [jnp.sum(accs[v * nl + j], axis=0, keepdims=True) for j in range(nl)], axis=1)
        o_ref[v:v + 1, :] = row + b_ref[...]


def _adaln(cvecs, w, b):
    d, n = w.shape
    tn = 1536
    cb = jnp.broadcast_to(cvecs[:, :, None], (2, d, LANES))
    return pl.pallas_call(
        _adaln_kernel,
        grid=(n // tn,),
        in_specs=[pl.BlockSpec((2, d, LANES), lambda j: (0, 0, 0)),
                  pl.BlockSpec((d, tn), lambda j: (0, j)),
                  pl.BlockSpec((1, tn), lambda j: (0, j))],
        out_specs=pl.BlockSpec((2, tn), lambda j: (0, j)),
        out_shape=jax.ShapeDtypeStruct((2, n), F32),
        compiler_params=_cparams(("arbitrary",)),
        name="adaln",
    )(cb, w, b.reshape(1, n))


def _rms_mod(x, g, shift, scale):
    y = x * lax.rsqrt(jnp.mean(x * x, axis=-1, keepdims=True) + EPS) * g
    return y * (1.0 + scale) + shift


def _prenorm_kernel(x_ref, c_ref, g_ref, mod_ref, o_ref, *, n_lat_tiles):
    is_ctx = pl.program_id(0) >= n_lat_tiles
    x = jnp.where(is_ctx, c_ref[...], x_ref[...])
    o_ref[...] = _rms_mod(x, g_ref[...], mod_ref[0, 0:1, :], mod_ref[0, 1:2, :]).astype(o_ref.dtype)


def _prenorm(x, ctx, g, mod):
    s, d = x.shape
    c = ctx.shape[0]
    tm = 256
    nl, nc = s // tm, c // tm
    return pl.pallas_call(
        functools.partial(_prenorm_kernel, n_lat_tiles=nl),
        grid=(nl + nc,),
        in_specs=[pl.BlockSpec((tm, d), lambda i: (jnp.minimum(i, nl - 1), 0)),
                  pl.BlockSpec((tm, d), lambda i: (jnp.maximum(i - nl, 0), 0)),
                  pl.BlockSpec((1, d), lambda i: (0, 0)),
                  pl.BlockSpec((1, N_MOD, d), lambda i: (i // nl, 0, 0))],
        out_specs=pl.BlockSpec((tm, d), lambda i: (i, 0)),
        out_shape=jax.ShapeDtypeStruct((s + c, d), BF16),
        compiler_params=_cparams(("arbitrary",)),
        name="prenorm_mix",
    )(x, ctx, g.reshape(1, d), mod)


def _inproj_kernel(h_ref, w_ref, gain_ref, c_ref, sa_ref, sb_ref, o_ref):
    j = pl.program_id(1)
    acc = jnp.dot(h_ref[...], w_ref[...], preferred_element_type=F32)
    nh = acc.shape[1] // HEAD_DIM

    def norm_rope(a, gain):
        y = a * lax.rsqrt(jnp.mean(a * a, axis=-1, keepdims=True) + EPS) * gain
        return (y * c_ref[...] + pltpu.roll(y, ROPE_PAIRS, 1) * sa_ref[...]
                + pltpu.roll(y, HEAD_DIM - ROPE_PAIRS, 1) * sb_ref[...])

    def store(n_normed):
        for hd in range(nh):
            sl = slice(hd * HEAD_DIM, (hd + 1) * HEAD_DIM)
            a = acc[:, sl]
            if hd < n_normed:
                a = norm_rope(a, gain_ref[0, :, sl])
            o_ref[:, sl] = a.astype(o_ref.dtype)

    all_normed = (j < 2) | ((j >= 3) & (j < 7))
    pl.when(all_normed)(lambda: store(nh))
    pl.when(j == 2)(lambda: store(A_KV_HEADS))
    pl.when(j >= 7)(lambda: store(0))


def _inproj(h, w_bf16, gains, rope_c, rope_sa, rope_sb):
    t, d = h.shape
    n = w_bf16.shape[1]
    tm = t // 8
    tn = PROJ_TN
    return pl.pallas_call(
        _inproj_kernel,
        grid=(t // tm, n // tn),
        in_specs=[pl.BlockSpec((tm, d), lambda i, j: (i, 0)),
                  pl.BlockSpec((d, tn), lambda i, j: (0, j)),
                  pl.BlockSpec((1, 1, tn), lambda i, j: (j, 0, 0)),
                  pl.BlockSpec((tm, HEAD_DIM), lambda i, j: (i, 0)),
                  pl.BlockSpec((tm, HEAD_DIM), lambda i, j: (i, 0)),
                  pl.BlockSpec((tm, HEAD_DIM), lambda i, j: (i, 0))],
        out_specs=pl.BlockSpec((tm, tn), lambda i, j: (i, j)),
        out_shape=jax.ShapeDtypeStruct((t, n), BF16),
        compiler_params=_cparams(("arbitrary", "arbitrary")),
        name="inproj",
    )(h, w_bf16, gains, rope_c, rope_sa, rope_sb)


def _rope_tables(s, c):
    rows_n = s // GRID_W
    rows = jnp.repeat(jnp.arange(rows_n, dtype=F32), GRID_W)
    cols = jnp.tile(jnp.arange(GRID_W, dtype=F32), rows_n)
    inv = ROPE_THETA ** (-jnp.arange(ROPE_PAIRS, dtype=F32) / ROPE_PAIRS)
    ang_r = rows[:, None] * inv
    ang_c = cols[:, None] * inv
    cr, sr, cc, sc = jnp.cos(ang_r), jnp.sin(ang_r), jnp.cos(ang_c), jnp.sin(ang_c)
    z = jnp.zeros_like(sr)
    tc = jnp.concatenate([cr, cr, cc, cc], axis=1)
    tsa = jnp.concatenate([z, sr, z, sc], axis=1)
    tsb = jnp.concatenate([-sr, z, -sc, z], axis=1)
    pad = lambda a, v: jnp.concatenate([a, jnp.full((c, HEAD_DIM), v, F32)], axis=0)
    return pad(tc, 1.0), pad(tsa, 0.0), pad(tsb, 0.0)


def _head_gains(qn_a, kn_a, qn_b, kn_b):
    qs = HEAD_DIM ** -0.5 * LOG2E
    one = jnp.ones((HEAD_DIM,), F32)
    heads = ([qn_a * qs] * A_HEADS + [kn_a] * A_KV_HEADS + [one] * A_KV_HEADS
             + [qn_b * qs] * (2 * B_HEADS) + [kn_b] * (2 * B_HEADS))
    heads = heads + [one] * (IN_W // HEAD_DIM - len(heads))
    return jnp.concatenate(heads).reshape(IN_W // PROJ_TN, 1, PROJ_TN)


def _build_vt(v_ref, vt_ref, tk):
    n_chunks, rows, _ = vt_ref.shape
    dv = rows - ONES_ROWS
    tail = (lax.broadcasted_iota(jnp.int32, (ONES_ROWS, tk), 0) == 0).astype(vt_ref.dtype)
    for c in range(n_chunks):
        vt_ref[c, 0:dv, :] = v_ref[c * tk:(c + 1) * tk, :].astype(F32).T.astype(vt_ref.dtype)
        vt_ref[c, dv:rows, :] = tail


def _attend(q, k_ref, vt_ref, m_ref, acc_ref):
    n_chunks, _, tk = vt_ref.shape
    m_ref[...] = jnp.full(m_ref.shape, -jnp.inf, F32)
    acc_ref[...] = jnp.zeros(acc_ref.shape, F32)

    def body(c, carry):
        off = pl.multiple_of(c * tk, tk)
        s = lax.dot_general(k_ref[pl.ds(off, tk), :], q, (((1,), (1,)), ((), ())),
                            preferred_element_type=F32)
        m_old = m_ref[...]
        m_new = jnp.maximum(m_old, jnp.max(s, axis=0, keepdims=True))
        p = jnp.exp2(s - m_new).astype(vt_ref.dtype)
        acc_ref[...] = (acc_ref[...] * jnp.exp2(m_old - m_new)
                        + jnp.dot(vt_ref[c], p, preferred_element_type=F32))
        m_ref[...] = m_new
        return carry

    lax.fori_loop(0, n_chunks, body, 0)


def _gqa_kernel(q_ref, k_ref, v_ref, o_ref, vt_ref, m_ref, acc_ref):
    tk = vt_ref.shape[2]
    pl.when((pl.program_id(1) == 0) & (pl.program_id(2) == 0))(lambda: _build_vt(v_ref, vt_ref, tk))
    _attend(q_ref[...], k_ref, vt_ref, m_ref, acc_ref)
    o_t = acc_ref[0:HEAD_DIM, :] / acc_ref[HEAD_DIM:HEAD_DIM + 1, :]
    o_ref[...] = o_t.T.astype(o_ref.dtype)


def _key_chunk(t):
    for tk in (768, 1024, 512, 640, 384, 256, 128):
        if t % tk == 0:
            return tk
    raise ValueError(f"unsupported key count {t}")


def _gqa(proj, s):
    t = proj.shape[0]
    tq = 512
    tk = _key_chunk(t)
    return pl.pallas_call(
        _gqa_kernel,
        grid=(A_KV_HEADS, s // tq, A_GROUP),
        in_specs=[pl.BlockSpec((tq, HEAD_DIM), lambda g, i, hh: (i, COL_AQ + g * A_GROUP + hh)),
                  pl.BlockSpec((t, HEAD_DIM), lambda g, i, hh: (0, COL_AK + g)),
                  pl.BlockSpec((t, HEAD_DIM), lambda g, i, hh: (0, COL_AV + g))],
        out_specs=pl.BlockSpec((tq, HEAD_DIM), lambda g, i, hh: (i, g * A_GROUP + hh)),
        out_shape=jax.ShapeDtypeStruct((s, A_Q_W), BF16),
        scratch_shapes=[pltpu.VMEM((t // tk, HEAD_DIM + ONES_ROWS, tk), BF16),
                        pltpu.VMEM((1, tq), F32),
                        pltpu.VMEM((HEAD_DIM + ONES_ROWS, tq), F32)],
        compiler_params=_cparams(("arbitrary", "arbitrary", "arbitrary")),
        name="gqa_attn",
    )(proj, proj, proj)


def _diff_kernel(lam_ref, q0_ref, q1_ref, k0_ref, k1_ref, v_ref, g_ref, o_ref,
                 vt_ref, m_ref, acc0_ref, acc1_ref, *, lam_init):
    tk = vt_ref.shape[2]
    pl.when(pl.program_id(1) == 0)(lambda: _build_vt(v_ref, vt_ref, tk))
    lv = lam_ref[...]
    lam = (jnp.exp(jnp.sum(lv[0:1, :] * lv[1:2, :], axis=-1, keepdims=True))
           - jnp.exp(jnp.sum(lv[2:3, :] * lv[3:4, :], axis=-1, keepdims=True)) + lam_init)
    _attend(q0_ref[...], k0_ref, vt_ref, m_ref, acc0_ref)
    _attend(q1_ref[...], k1_ref, vt_ref, m_ref, acc1_ref)
    o_t = (acc0_ref[0:B_V_DIM, :] / acc0_ref[B_V_DIM:B_V_DIM + 1, :]
           - lam * (acc1_ref[0:B_V_DIM, :] / acc1_ref[B_V_DIM:B_V_DIM + 1, :]))
    o = o_t.T
    y = o * lax.rsqrt(jnp.mean(o * o, axis=-1, keepdims=True) + EPS) * g_ref[...]
    o_ref[...] = (y * (1.0 - lam_init)).astype(o_ref.dtype)


def _diff(proj, s, lam_vecs, subln_g, lam_init):
    t = proj.shape[0]
    tq = 512
    tk = _key_chunk(t)
    vb = B_V_DIM // HEAD_DIM
    return pl.pallas_call(
        functools.partial(_diff_kernel, lam_init=lam_init),
        grid=(B_HEADS, s // tq),
        in_specs=[pl.BlockSpec((4, HEAD_DIM), lambda h, i: (0, 0)),
                  pl.BlockSpec((tq, HEAD_DIM), lambda h, i: (i, COL_BQ + 2 * h)),
                  pl.BlockSpec((tq, HEAD_DIM), lambda h, i: (i, COL_BQ + 2 * h + 1)),
                  pl.BlockSpec((t, HEAD_DIM), lambda h, i: (0, COL_BK + 2 * h)),
                  pl.BlockSpec((t, HEAD_DIM), lambda h, i: (0, COL_BK + 2 * h + 1)),
                  pl.BlockSpec((t, B_V_DIM), lambda h, i: (0, COL_BV // vb + h)),
                  pl.BlockSpec((1, B_V_DIM), lambda h, i: (0, 0))],
        out_specs=pl.BlockSpec((tq, B_V_DIM), lambda h, i: (i, h)),
        out_shape=jax.ShapeDtypeStruct((s, B_V_W), BF16),
        scratch_shapes=[pltpu.VMEM((t // tk, B_V_DIM + ONES_ROWS, tk), BF16),
                        pltpu.VMEM((1, tq), F32),
                        pltpu.VMEM((B_V_DIM + ONES_ROWS, tq), F32),
                        pltpu.VMEM((B_V_DIM + ONES_ROWS, tq), F32)],
        compiler_params=_cparams(("arbitrary", "arbitrary")),
        name="diff_attn",
    )(lam_vecs, proj, proj, proj, proj, proj, subln_g.reshape(1, B_V_DIM))


def _merge_kernel(oa_ref, ob_ref, ga_ref, gb_ref, wa_ref, wb_ref, wo_ref, x_ref, gate_ref, o_ref, t_ref):
    c = pl.program_id(1)
    nc, _, tn = t_ref.shape
    ya = jnp.dot(oa_ref[...], wa_ref[...], preferred_element_type=F32)
    yb = jnp.dot(ob_ref[...], wb_ref[...], preferred_element_type=F32)
    t = (jax.nn.sigmoid(ga_ref[...].astype(F32)) * ya + jax.nn.sigmoid(gb_ref[...].astype(F32)) * yb)
    t_ref[c] = t.astype(t_ref.dtype)

    @pl.when(c == nc - 1)
    def _():
        y = jnp.dot(t_ref[0], wo_ref[0:tn, :], preferred_element_type=F32)
        for cc in range(1, nc):
            y = y + jnp.dot(t_ref[cc], wo_ref[cc * tn:(cc + 1) * tn, :], preferred_element_type=F32)
        o_ref[...] = x_ref[...] + gate_ref[...] * y


def _merge(oa, ob, proj, wa, wb, wo, x, gate):
    s, d = x.shape
    tm = 512
    tn = PROJ_TN
    nc = d // tn
    ga0 = COL_GA * HEAD_DIM // tn
    gb0 = COL_GB * HEAD_DIM // tn
    return pl.pallas_call(
        _merge_kernel,
        grid=(s // tm, nc),
        in_specs=[pl.BlockSpec((tm, A_Q_W), lambda i, c: (i, 0)),
                  pl.BlockSpec((tm, B_V_W), lambda i, c: (i, 0)),
                  pl.BlockSpec((tm, tn), lambda i, c: (i, ga0 + c)),
                  pl.BlockSpec((tm, tn), lambda i, c: (i, gb0 + c)),
                  pl.BlockSpec((A_Q_W, tn), lambda i, c: (0, c)),
                  pl.BlockSpec((B_V_W, tn), lambda i, c: (0, c)),
                  pl.BlockSpec((d, d), lambda i, c: (0, 0)),
                  pl.BlockSpec((tm, d), lambda i, c: (i, 0)),
                  pl.BlockSpec((1, d), lambda i, c: (0, 0))],
        out_specs=pl.BlockSpec((tm, d), lambda i, c: (i, 0)),
        out_shape=jax.ShapeDtypeStruct((s, d), F32),
        scratch_shapes=[pltpu.VMEM((nc, tm, tn), BF16)],
        compiler_params=_cparams(("arbitrary", "arbitrary")),
        name="merge_out",
    )(oa, ob, proj, proj, wa, wb, wo, x, gate)


def _lane_max(x):
    return jnp.max(x, axis=-1, keepdims=True)


def _lane_min(x):
    return jnp.min(x, axis=-1, keepdims=True)


def _group_allreduce(x, lane, op):
    for sft in (1, 2, 4):
        up = pltpu.roll(x, sft, 1)
        dn = pltpu.roll(x, LANES - sft, 1)
        x = op(x, jnp.where((lane & sft) != 0, up, dn))
    return x


def _router_kernel(x_ref, g_ref, mod_ref, wr_ref, rb_ref,
                   h_ref, hp_ref, idx_ref, wsel_ref, rank_ref, cnt_ref, carry_ref):
    @pl.when(pl.program_id(0) == 0)
    def _():
        carry_ref[...] = jnp.zeros_like(carry_ref)

    h = _rms_mod(x_ref[...], g_ref[...], mod_ref[0, 3:4, :], mod_ref[0, 4:5, :])
    hb = h.astype(BF16)
    h_ref[...] = hb
    bits = lax.bitcast_convert_type(hb.astype(F32), jnp.uint32)
    half = bits.shape[1] // 2
    words = (bits[:, :half] >> 16) | (bits[:, half:] & jnp.uint32(0xFFFF0000))
    tm = h.shape[0]
    for c in range(PACK_TILES):
        hp_ref[pl.ds(c, tm, stride=PACK_TILES), :] = words[:, c * LANES:(c + 1) * LANES]

    logits = jnp.dot(h, wr_ref[...], preferred_element_type=F32, precision=lax.Precision.HIGHEST)
    scores = jax.nn.sigmoid(logits)
    lane = lax.broadcasted_iota(jnp.int32, (tm, LANES), 1)
    valid = lane < N_EXPERTS
    neg = jnp.float32(-jnp.inf)
    big = jnp.int32(LANES)
    biased = jnp.where(valid, scores + rb_ref[...], neg)

    m1 = _group_allreduce(biased, lane, jnp.maximum)
    a1 = _group_allreduce(jnp.where(biased == m1, lane, big), lane, jnp.minimum)
    m2 = _group_allreduce(jnp.where(lane == a1, neg, biased), lane, jnp.maximum)
    gscore = jnp.where(valid, m1 + m2, neg)
    gid = lane >> 3
    keep = jnp.zeros((tm, LANES), jnp.bool_)
    for _ in range(TOPK_GROUPS):
        best = _lane_max(gscore)
        gsel = _lane_min(jnp.where(gscore == best, gid, big))
        hit = gid == gsel
        keep = keep | hit
        gscore = jnp.where(hit, neg, gscore)
    cand = jnp.where(keep & valid, biased, neg)

    onehot = jnp.zeros((tm, LANES), F32)
    idx_out = jnp.zeros((tm, LANES), jnp.int32)
    w_out = jnp.zeros((tm, LANES), F32)
    sels = []
    for k in range(TOP_K):
        best = _lane_max(cand)
        sel = _lane_min(jnp.where(cand == best, lane, big))
        hit = lane == sel
        wk = jnp.sum(jnp.where(hit, scores, 0.0), axis=-1, keepdims=True)
        cand = jnp.where(hit, neg, cand)
        onehot = jnp.where(hit, 1.0, onehot)
        idx_out = jnp.where(lane == k, sel, idx_out)
        w_out = jnp.where(lane == k, wk, w_out)
        sels.append(hit)
    wsum = jnp.sum(w_out, axis=-1, keepdims=True)
    wsel_ref[...] = w_out / wsum * ROUTED_SCALE
    idx_ref[...] = idx_out

    row = lax.broadcasted_iota(jnp.int32, (tm, tm), 0)
    col = lax.broadcasted_iota(jnp.int32, (tm, tm), 1)
    lower = (col < row).astype(BF16)
    before = jnp.dot(lower, onehot.astype(BF16), preferred_element_type=F32) + carry_ref[...]
    rank_out = jnp.zeros((tm, LANES), F32)
    for k in range(TOP_K):
        rk = jnp.sum(jnp.where(sels[k], before, 0.0), axis=-1, keepdims=True)
        rank_out = jnp.where(lane == k, rk, rank_out)
    rank_ref[...] = rank_out.astype(jnp.int32)
    carry_ref[...] = carry_ref[...] + jnp.sum(onehot, axis=0, keepdims=True)
    cnt_ref[...] = carry_ref[...]


def _router(x1, g, mod, w_router, router_bias):
    s, d = x1.shape
    tm = 256
    e = w_router.shape[1]
    wr = jnp.pad(w_router, ((0, 0), (0, LANES - e)))
    rb = jnp.pad(router_bias, (0, LANES - e)).reshape(1, LANES)
    row_spec = lambda w: pl.BlockSpec((tm, w), lambda i: (i, 0))
    return pl.pallas_call(
        _router_kernel,
        grid=(s // tm,),
        in_specs=[row_spec(d),
                  pl.BlockSpec((1, d), lambda i: (0, 0)),
                  pl.BlockSpec((1, N_MOD, d), lambda i: (0, 0, 0)),
                  pl.BlockSpec((d, LANES), lambda i: (0, 0)),
                  pl.BlockSpec((1, LANES), lambda i: (0, 0))],
        out_specs=[row_spec(d), pl.BlockSpec((tm * PACK_TILES, LANES), lambda i: (i, 0)),
                   row_spec(LANES), row_spec(LANES), row_spec(LANES),
                   pl.BlockSpec((1, LANES), lambda i: (0, 0))],
        out_shape=[jax.ShapeDtypeStruct((s, d), BF16),
                   jax.ShapeDtypeStruct((s * PACK_TILES, LANES), jnp.uint32),
                   jax.ShapeDtypeStruct((s, LANES), jnp.int32),
                   jax.ShapeDtypeStruct((s, LANES), F32),
                   jax.ShapeDtypeStruct((s, LANES), jnp.int32),
                   jax.ShapeDtypeStruct((1, LANES), F32)],
        scratch_shapes=[pltpu.VMEM((1, LANES), F32)],
        compiler_params=_cparams(("arbitrary",)),
        name="ffn_router",
    )(x1, g.reshape(1, d), mod, wr, rb)


def _unpack_rows(words):
    lo = lax.bitcast_convert_type(words << 16, F32).astype(BF16)
    hi = lax.bitcast_convert_type(words & jnp.uint32(0xFFFF0000), F32).astype(BF16)
    return jnp.concatenate([lo, hi], axis=1)


def _token_rows(row, tiles):
    return pl.ds(pl.multiple_of(row * tiles, tiles), tiles)


def _dispatch_kernel(ps_ref, pe_ref, nused_ref, idx_hbm, rank_hbm, hp_ref, xs_hbm,
                     idx_smem, rank_smem, zbuf, sem_i, sem_z, sem_x, *, n_blocks):
    i = pl.program_id(0)
    n = idx_smem.shape[0]
    tm = n // TOP_K
    copy_idx = pltpu.make_async_copy(idx_hbm.at[i], idx_smem, sem_i.at[0])
    copy_rank = pltpu.make_async_copy(rank_hbm.at[i], rank_smem, sem_i.at[1])
    copy_idx.start()
    copy_rank.start()

    @pl.when(i == 0)
    def _():
        zbuf[...] = jnp.zeros_like(zbuf)

        def zero_block(row0):
            span = MOE_BLOCK * PACK_TILES
            return pltpu.make_async_copy(
                zbuf, xs_hbm.at[pl.ds(pl.multiple_of(row0 * PACK_TILES, span), span), :], sem_z)

        def per_expert(action):
            def body(e, carry):
                @pl.when(pe_ref[e] > ps_ref[e])
                def _():
                    action(zero_block(pe_ref[e] - MOE_BLOCK))
                return carry
            lax.fori_loop(0, N_EXPERTS, body, 0)

        def per_tail(action):
            def body(b, carry):
                action(zero_block(b * MOE_BLOCK))
                return carry
            lax.fori_loop(nused_ref[0], n_blocks, body, 0)

        per_expert(lambda cp: cp.start())
        per_tail(lambda cp: cp.start())
        per_expert(lambda cp: cp.wait())
        per_tail(lambda cp: cp.wait())

    copy_idx.wait()
    copy_rank.wait()

    def issue(t, carry):
        for k in range(TOP_K):
            j = t * TOP_K + k
            dest = ps_ref[idx_smem[j]] + rank_smem[j]
            pltpu.make_async_copy(hp_ref.at[_token_rows(t, PACK_TILES), :],
                                  xs_hbm.at[_token_rows(dest, PACK_TILES), :], sem_x).start()
        return carry

    lax.fori_loop(0, tm, issue, 0)
    for _ in range(TOP_K):
        pltpu.make_async_copy(hp_ref, xs_hbm.at[pl.ds(0, tm * PACK_TILES), :], sem_x).wait()


def _dispatch(pad_start, pad_end, n_used, idx_tm, rank_tm, hp, n_blocks):
    n_tiles, n = idx_tm.shape
    tm = n // TOP_K
    grid_spec = pltpu.PrefetchScalarGridSpec(
        num_scalar_prefetch=3,
        grid=(n_tiles,),
        in_specs=[pl.BlockSpec(memory_space=pl.ANY),
                  pl.BlockSpec(memory_space=pl.ANY),
                  pl.BlockSpec((tm * PACK_TILES, LANES), lambda i, ps, pe, nu: (i, 0))],
        out_specs=pl.BlockSpec(memory_space=pl.ANY),
        scratch_shapes=[pltpu.SMEM((n,), jnp.int32),
                        pltpu.SMEM((n,), jnp.int32),
                        pltpu.VMEM((MOE_BLOCK * PACK_TILES, LANES), jnp.uint32),
                        pltpu.SemaphoreType.DMA((2,)),
                        pltpu.SemaphoreType.DMA,
                        pltpu.SemaphoreType.DMA])
    return pl.pallas_call(
        functools.partial(_dispatch_kernel, n_blocks=n_blocks),
        grid_spec=grid_spec,
        out_shape=jax.ShapeDtypeStruct((n_blocks * MOE_BLOCK * PACK_TILES, LANES), jnp.uint32),
        compiler_params=_cparams(("arbitrary",)),
        name="moe_dispatch",
    )(pad_start, pad_end, n_used, idx_tm, rank_tm, hp)


def _moe_kernel(be_ref, first_ref, nxt_ref, slot_ref, nused_ref, x_ref, wg_hbm, wu_hbm, wd_hbm, o_ref,
                wg32, wu32, wd32, wgb, wub, wdb, sems):
    b = pl.program_id(0)

    def fetch(e, slot):
        return (pltpu.make_async_copy(wg_hbm.at[e], wg32.at[slot], sems.at[slot, 0]),
                pltpu.make_async_copy(wu_hbm.at[e], wu32.at[slot], sems.at[slot, 1]),
                pltpu.make_async_copy(wd_hbm.at[e], wd32.at[slot], sems.at[slot, 2]))

    @pl.when(b < nused_ref[0])
    def _():
        @pl.when(first_ref[b] == 1)
        def _():
            slot = slot_ref[b]

            @pl.when(b == 0)
            def _():
                for cp in fetch(be_ref[0], 0):
                    cp.start()

            for cp in fetch(be_ref[b], slot):
                cp.wait()

            @pl.when(nxt_ref[b] >= 0)
            def _():
                for cp in fetch(nxt_ref[b], 1 - slot):
                    cp.start()

            wgb[...] = wg32[slot].astype(BF16)
            wub[...] = wu32[slot].astype(BF16)
            wdb[...] = wd32[slot].astype(BF16)

        words = jnp.concatenate(
            [x_ref[pl.ds(c, MOE_BLOCK, stride=PACK_TILES), :] for c in range(PACK_TILES)], axis=1)
        x = _unpack_rows(words)
        gate = jnp.dot(x, wgb[...], preferred_element_type=F32)
        up = jnp.dot(x, wub[...], preferred_element_type=F32)
        act = (gate * jax.nn.sigmoid(gate) * up).astype(BF16)
        y = jnp.dot(act, wdb[...], preferred_element_type=F32)
        for c in range(ROW_TILES):
            o_ref[pl.ds(c, MOE_BLOCK, stride=ROW_TILES), :] = y[:, c * LANES:(c + 1) * LANES]

    @pl.when(b >= nused_ref[0])
    def _():
        o_ref[...] = jnp.zeros_like(o_ref)


def _moe(block_e, first, nxt, slot, n_used, xs, weg, weu, wed):
    n_blocks = block_e.shape[0]
    _, d, f = weg.shape
    grid_spec = pltpu.PrefetchScalarGridSpec(
        num_scalar_prefetch=5,
        grid=(n_blocks,),
        in_specs=[pl.BlockSpec((MOE_BLOCK * PACK_TILES, LANES), lambda b, *_: (b, 0)),
                  pl.BlockSpec(memory_space=pl.ANY),
                  pl.BlockSpec(memory_space=pl.ANY),
                  pl.BlockSpec(memory_space=pl.ANY)],
        out_specs=pl.BlockSpec((MOE_BLOCK * ROW_TILES, LANES), lambda b, *_: (b, 0)),
        scratch_shapes=[pltpu.VMEM((2, d, f), F32), pltpu.VMEM((2, d, f), F32), pltpu.VMEM((2, f, d), F32),
                        pltpu.VMEM((d, f), BF16), pltpu.VMEM((d, f), BF16), pltpu.VMEM((f, d), BF16),
                        pltpu.SemaphoreType.DMA((2, 3))])
    return pl.pallas_call(
        _moe_kernel,
        grid_spec=grid_spec,
        out_shape=jax.ShapeDtypeStruct((n_blocks * MOE_BLOCK * ROW_TILES, LANES), F32),
        compiler_params=_cparams(("arbitrary",)),
        name="moe_experts",
    )(block_e, first, nxt, slot, n_used, xs, weg, weu, wed)


def _combine_kernel(ps_ref, idx_hbm, rank_hbm, y_hbm, w_ref, h_ref, wg_ref, wu_ref, wd_ref, x_ref, gate_ref,
                    o_ref, idx_smem, rank_smem, ybuf, sem_i, sem_y):
    i = pl.program_id(0)
    tm = h_ref.shape[0]
    n = tm * TOP_K
    span = n * ROW_TILES

    def start_tile(tile, slot):
        copy_idx = pltpu.make_async_copy(idx_hbm.at[tile], idx_smem.at[slot], sem_i.at[0])
        copy_rank = pltpu.make_async_copy(rank_hbm.at[tile], rank_smem.at[slot], sem_i.at[1])
        copy_idx.start()
        copy_rank.start()
        copy_idx.wait()
        copy_rank.wait()

        def issue(r, carry):
            src = ps_ref[idx_smem[slot, r]] + rank_smem[slot, r]
            pltpu.make_async_copy(y_hbm.at[_token_rows(src, ROW_TILES), :],
                                  ybuf.at[slot, _token_rows(r, ROW_TILES), :], sem_y.at[slot]).start()
            return carry

        lax.fori_loop(0, n, issue, 0, unroll=8)

    pl.when(i == 0)(lambda: start_tile(0, 0))
    pl.when(i + 1 < pl.num_programs(0))(lambda: start_tile(i + 1, (i + 1) % 2))

    h = h_ref[...]
    gate = jnp.dot(h, wg_ref[...], preferred_element_type=F32)
    up = jnp.dot(h, wu_ref[...], preferred_element_type=F32)
    act = (gate * jax.nn.sigmoid(gate) * up).astype(BF16)
    shared = jnp.dot(act, wd_ref[...], preferred_element_type=F32)

    slot = i % 2
    pltpu.make_async_copy(y_hbm.at[pl.ds(0, span), :], ybuf.at[slot], sem_y.at[slot]).wait()
    yb = ybuf.at[slot]
    chunks = []
    for c in range(ROW_TILES):
        acc = yb[pl.ds(c, tm, stride=ROW_TILES), :] * w_ref[:, 0:1]
        for k in range(1, TOP_K):
            acc = acc + yb[pl.ds(k * tm * ROW_TILES + c, tm, stride=ROW_TILES), :] * w_ref[:, k:k + 1]
        chunks.append(acc)
    routed = jnp.concatenate(chunks, axis=1)
    o_ref[...] = x_ref[...] + gate_ref[...] * (routed + shared)


def _combine(pad_start, idx_km, rank_km, y_sorted, wsel, h2, wsg, wsu, wsd, x1, gate):
    s, d = x1.shape
    n_tiles, n = idx_km.shape
    tm = n // TOP_K
    f = wsg.shape[1]
    grid_spec = pltpu.PrefetchScalarGridSpec(
        num_scalar_prefetch=1,
        grid=(n_tiles,),
        in_specs=[pl.BlockSpec(memory_space=pl.ANY),
                  pl.BlockSpec(memory_space=pl.ANY),
                  pl.BlockSpec(memory_space=pl.ANY),
                  pl.BlockSpec((tm, LANES), lambda i, ps: (i, 0)),
                  pl.BlockSpec((tm, d), lambda i, ps: (i, 0)),
                  pl.BlockSpec((d, f), lambda i, ps: (0, 0)),
                  pl.BlockSpec((d, f), lambda i, ps: (0, 0)),
                  pl.BlockSpec((f, d), lambda i, ps: (0, 0)),
                  pl.BlockSpec((tm, d), lambda i, ps: (i, 0)),
                  pl.BlockSpec((1, d), lambda i, ps: (0, 0))],
        out_specs=pl.BlockSpec((tm, d), lambda i, ps: (i, 0)),
        scratch_shapes=[pltpu.SMEM((2, n), jnp.int32),
                        pltpu.SMEM((2, n), jnp.int32),
                        pltpu.VMEM((2, n * ROW_TILES, LANES), F32),
                        pltpu.SemaphoreType.DMA((2,)),
                        pltpu.SemaphoreType.DMA((2,))])
    return pl.pallas_call(
        _combine_kernel,
        grid_spec=grid_spec,
        out_shape=jax.ShapeDtypeStruct((s, d), F32),
        compiler_params=_cparams(("arbitrary",)),
        name="moe_combine",
    )(pad_start, idx_km, rank_km, y_sorted, wsel, h2, wsg, wsu, wsd, x1, gate)


def _block_tables(counts, n_blocks):
    padded = (counts + MOE_BLOCK - 1) // MOE_BLOCK * MOE_BLOCK
    pad_end = jnp.cumsum(padded).astype(jnp.int32)
    pad_start = pad_end - padded
    blk = jnp.arange(n_blocks, dtype=jnp.int32)
    block_e = jnp.minimum(jnp.sum(pad_end[None, :] <= (blk * MOE_BLOCK)[:, None], axis=1),
                          N_EXPERTS - 1).astype(jnp.int32)
    n_used = pad_end[-1] // MOE_BLOCK
    prev = jnp.concatenate([jnp.full((1,), -1, jnp.int32), block_e[:-1]])
    first = ((blk < n_used) & (block_e != prev)).astype(jnp.int32)
    slot = jnp.maximum(jnp.cumsum(first) - 1, 0).astype(jnp.int32) % 2
    after = pad_end[block_e] // MOE_BLOCK
    nxt = jnp.where(after < n_used, block_e[jnp.minimum(after, n_blocks - 1)], -1).astype(jnp.int32)
    return pad_start, pad_end, block_e, first, nxt, slot, n_used.reshape(1)


def kernel(x, c, ctx, c_ctx, w_ada, b_ada, norm_mix, norm_ffn, w_in, q_norm_a, k_norm_a, q_norm_b, k_norm_b, lambda_q1, lambda_k1, lambda_q2, lambda_k2, subln_b, w_branch_a, w_branch_b, w_out, w_router, router_bias, w_exp_gate, w_exp_up, w_exp_down, w_sh_gate, w_sh_up, w_sh_down):
    depth = w_ada.shape[0]
    assert depth == 1 and x.shape[0] == 1 and ctx.shape[0] == 1
    s, d = x.shape[1], x.shape[2]
    n_ctx = ctx.shape[1]
    i = 0
    lam_init = 0.8 - 0.6 * math.exp(-0.3 * i)
    xs = x[0]

    mod = _adaln(jnp.concatenate([c, c_ctx[None, :]], axis=0), w_ada[i], b_ada[i]).reshape(2, N_MOD, d)

    h = _prenorm(xs, ctx[0], norm_mix[i], mod)
    tc, tsa, tsb = _rope_tables(s, n_ctx)
    gains = _head_gains(q_norm_a[i], k_norm_a[i], q_norm_b[i], k_norm_b[i])
    proj = _inproj(h, w_in[i].astype(BF16), gains, tc, tsa, tsb)
    oa = _gqa(proj, s)
    lam_vecs = jnp.stack([lambda_q1[i], lambda_k1[i], lambda_q2[i], lambda_k2[i]]).astype(F32)
    ob = _diff(proj, s, lam_vecs, subln_b[i], lam_init)
    x1 = _merge(oa, ob, proj, w_branch_a[i].astype(BF16), w_branch_b[i].astype(BF16),
                w_out[i].astype(BF16), xs, mod[0, 2:3, :])

    h2, h2p, idx, wsel, rank, cnt = _router(x1, norm_ffn[i], mod[0:1], w_router[i], router_bias[i])
    counts = cnt[0, :N_EXPERTS].astype(jnp.int32)
    n_blocks = -(-(s * TOP_K) // MOE_BLOCK) + N_EXPERTS
    pad_start, pad_end, block_e, first, nxt, slot, n_used = _block_tables(counts, n_blocks)
    tm_r = 128
    tiled = lambda a: a[:, :TOP_K].reshape(s // tm_r, tm_r, TOP_K)
    token_major = lambda a: tiled(a).reshape(s // tm_r, tm_r * TOP_K)
    k_major = lambda a: tiled(a).transpose(0, 2, 1).reshape(s // tm_r, tm_r * TOP_K)
    xs = _dispatch(pad_start, pad_end, n_used, token_major(idx), token_major(rank), h2p, n_blocks)
    y_sorted = _moe(block_e, first, nxt, slot, n_used, xs, w_exp_gate[i], w_exp_up[i], w_exp_down[i])
    out = _combine(pad_start, k_major(idx), k_major(rank), y_sorted, wsel, h2,
                   w_sh_gate[i].astype(BF16), w_sh_up[i].astype(BF16), w_sh_down[i].astype(BF16),
                   x1, mod[0, 5:6, :])
    return out[None]
```

```python
import functools
import math

import jax
import jax.numpy as jnp
from jax import lax
from jax.experimental import pallas as pl
from jax.experimental.pallas import tpu as pltpu

F32 = jnp.float32
BF16 = jnp.bfloat16

D_MODEL = 2048
GRID_W = 64
HEAD_DIM = 128
ROPE_PAIRS = HEAD_DIM // 4
ROPE_THETA = 10000.0
A_HEADS = 8
A_KV_HEADS = 2
A_GROUP = A_HEADS // A_KV_HEADS
B_HEADS = 4
B_V_DIM = 2 * HEAD_DIM
N_EXPERTS = 64
TOP_K = 8
N_GROUPS = 8
TOPK_GROUPS = 4
EXPERT_DIM = 512
SHARED_DIM = 512
ROUTED_SCALE = 2.5
N_MOD = 6
EPS = 1e-6

A_Q_W = A_HEADS * HEAD_DIM
A_KV_W = A_KV_HEADS * HEAD_DIM
B_QK_W = B_HEADS * 2 * HEAD_DIM
B_V_W = B_HEADS * B_V_DIM
IN_W = A_Q_W + 2 * A_KV_W + 2 * B_QK_W + B_V_W + 2 * D_MODEL

COL_AQ = 0
COL_AK = A_Q_W // HEAD_DIM
COL_AV = COL_AK + A_KV_HEADS
COL_BQ = COL_AV + A_KV_HEADS
COL_BK = COL_BQ + 2 * B_HEADS
COL_BV = COL_BK + 2 * B_HEADS
COL_GA = COL_BV + B_V_W // HEAD_DIM
COL_GB = COL_GA + D_MODEL // HEAD_DIM

LANES = 128
SUBLANES = 8
VMEM_LIMIT = 56 * 1024 * 1024

PROJ_TN = 512
MOE_BLOCK = 128
ONES_ROWS = 16
PACK_TILES = D_MODEL // 2 // LANES
ROW_TILES = D_MODEL // LANES
LOG2E = 1.4426950408889634


def _cparams(sem, vmem=VMEM_LIMIT):
    return pltpu.CompilerParams(dimension_semantics=sem, vmem_limit_bytes=vmem)


def _adaln_kernel(cb_ref, w_ref, b_ref, o_ref):
    tn = w_ref.shape[1]
    nl = tn // LANES
    rows = 32

    def body(g, accs):
        accs = list(accs)
        r0 = pl.multiple_of(g * rows, rows)
        for u in range(rows // SUBLANES):
            r = r0 + u * SUBLANES
            w = w_ref[pl.ds(r, SUBLANES), :]
            for v in range(2):
                c = cb_ref[v, pl.ds(r, SUBLANES), :]
                s = c * jax.nn.sigmoid(c)
                for j in range(nl):
                    accs[v * nl + j] = accs[v * nl + j] + w[:, j * LANES:(j + 1) * LANES] * s
        return tuple(accs)

    init = tuple(jnp.zeros((SUBLANES, LANES), F32) for _ in range(2 * nl))
    accs = lax.fori_loop(0, w_ref.shape[0] // rows, body, init)
    for v in range(2):
        row = jnp.concatenate(
            [jnp.sum(accs[v * nl + j], axis=0, keepdims=True) for j in range(nl)], axis=1)
        o_ref[v:v + 1, :] = row + b_ref[...]


def _adaln(cvecs, w, b):
    d, n = w.shape
    tn = 1536
    cb = jnp.broadcast_to(cvecs[:, :, None], (2, d, LANES))
    return pl.pallas_call(
        _adaln_kernel,
        grid=(n // tn,),
        in_specs=[pl.BlockSpec((2, d, LANES), lambda j: (0, 0, 0)),
                  pl.BlockSpec((d, tn), lambda j: (0, j)),
                  pl.BlockSpec((1, tn), lambda j: (0, j))],
        out_specs=pl.BlockSpec((2, tn), lambda j: (0, j)),
        out_shape=jax.ShapeDtypeStruct((2, n), F32),
        compiler_params=_cparams(("arbitrary",)),
        name="adaln",
    )(cb, w, b.reshape(1, n))


def _rms_mod(x, g, shift, scale):
    y = x * lax.rsqrt(jnp.mean(x * x, axis=-1, keepdims=True) + EPS) * g
    return y * (1.0 + scale) + shift


def _prenorm_kernel(x_ref, c_ref, g_ref, mod_ref, o_ref, *, n_lat_tiles):
    is_ctx = pl.program_id(0) >= n_lat_tiles
    x = jnp.where(is_ctx, c_ref[...], x_ref[...])
    o_ref[...] = _rms_mod(x, g_ref[...], mod_ref[0, 0:1, :], mod_ref[0, 1:2, :]).astype(o_ref.dtype)


def _prenorm(x, ctx, g, mod):
    s, d = x.shape
    c = ctx.shape[0]
    tm = 256
    nl, nc = s // tm, c // tm
    return pl.pallas_call(
        functools.partial(_prenorm_kernel, n_lat_tiles=nl),
        grid=(nl + nc,),
        in_specs=[pl.BlockSpec((tm, d), lambda i: (jnp.minimum(i, nl - 1), 0)),
                  pl.BlockSpec((tm, d), lambda i: (jnp.maximum(i - nl, 0), 0)),
                  pl.BlockSpec((1, d), lambda i: (0, 0)),
                  pl.BlockSpec((1, N_MOD, d), lambda i: (i // nl, 0, 0))],
        out_specs=pl.BlockSpec((tm, d), lambda i: (i, 0)),
        out_shape=jax.ShapeDtypeStruct((s + c, d), BF16),
        compiler_params=_cparams(("arbitrary",)),
        name="prenorm_mix",
    )(x, ctx, g.reshape(1, d), mod)


def _inproj_kernel(h_ref, w_ref, gain_ref, c_ref, sa_ref, sb_ref, o_ref):
    j = pl.program_id(1)
    acc = jnp.dot(h_ref[...], w_ref[...], preferred_element_type=F32)
    nh = acc.shape[1] // HEAD_DIM

    def norm_rope(a, gain):
        y = a * lax.rsqrt(jnp.mean(a * a, axis=-1, keepdims=True) + EPS) * gain
        return (y * c_ref[...] + pltpu.roll(y, ROPE_PAIRS, 1) * sa_ref[...]
                + pltpu.roll(y, HEAD_DIM - ROPE_PAIRS, 1) * sb_ref[...])

    def store(n_normed):
        for hd in range(nh):
            sl = slice(hd * HEAD_DIM, (hd + 1) * HEAD_DIM)
            a = acc[:, sl]
            if hd < n_normed:
                a = norm_rope(a, gain_ref[0, :, sl])
            o_ref[:, sl] = a.astype(o_ref.dtype)

    all_normed = (j < 2) | ((j >= 3) & (j < 7))
    pl.when(all_normed)(lambda: store(nh))
    pl.when(j == 2)(lambda: store(A_KV_HEADS))
    pl.when(j >= 7)(lambda: store(0))


def _inproj(h, w_bf16, gains, rope_c, rope_sa, rope_sb):
    t, d = h.shape
    n = w_bf16.shape[1]
    tm = t // 8
    tn = PROJ_TN
    return pl.pallas_call(
        _inproj_kernel,
        grid=(t // tm, n // tn),
        in_specs=[pl.BlockSpec((tm, d), lambda i, j: (i, 0)),
                  pl.BlockSpec((d, tn), lambda i, j: (0, j)),
                  pl.BlockSpec((1, 1, tn), lambda i, j: (j, 0, 0)),
                  pl.BlockSpec((tm, HEAD_DIM), lambda i, j: (i, 0)),
                  pl.BlockSpec((tm, HEAD_DIM), lambda i, j: (i, 0)),
                  pl.BlockSpec((tm, HEAD_DIM), lambda i, j: (i, 0))],
        out_specs=pl.BlockSpec((tm, tn), lambda i, j: (i, j)),
        out_shape=jax.ShapeDtypeStruct((t, n), BF16),
        compiler_params=_cparams(("arbitrary", "arbitrary")),
        name="inproj",
    )(h, w_bf16, gains, rope_c, rope_sa, rope_sb)


def _rope_tables(s, c):
    rows_n = s // GRID_W
    rows = jnp.repeat(jnp.arange(rows_n, dtype=F32), GRID_W)
    cols = jnp.tile(jnp.arange(GRID_W, dtype=F32), rows_n)
    inv = ROPE_THETA ** (-jnp.arange(ROPE_PAIRS, dtype=F32) / ROPE_PAIRS)
    ang_r = rows[:, None] * inv
    ang_c = cols[:, None] * inv
    cr, sr, cc, sc = jnp.cos(ang_r), jnp.sin(ang_r), jnp.cos(ang_c), jnp.sin(ang_c)
    z = jnp.zeros_like(sr)
    tc = jnp.concatenate([cr, cr, cc, cc], axis=1)
    tsa = jnp.concatenate([z, sr, z, sc], axis=1)
    tsb = jnp.concatenate([-sr, z, -sc, z], axis=1)
    pad = lambda a, v: jnp.concatenate([a, jnp.full((c, HEAD_DIM), v, F32)], axis=0)
    return pad(tc, 1.0), pad(tsa, 0.0), pad(tsb, 0.0)


def _head_gains(qn_a, kn_a, qn_b, kn_b):
    qs = HEAD_DIM ** -0.5 * LOG2E
    one = jnp.ones((HEAD_DIM,), F32)
    heads = ([qn_a * qs] * A_HEADS + [kn_a] * A_KV_HEADS + [one] * A_KV_HEADS
             + [qn_b * qs] * (2 * B_HEADS) + [kn_b] * (2 * B_HEADS))
    heads = heads + [one] * (IN_W // HEAD_DIM - len(heads))
    return jnp.concatenate(heads).reshape(IN_W // PROJ_TN, 1, PROJ_TN)


def _build_vt(v_ref, vt_ref, tk):
    n_chunks, rows, _ = vt_ref.shape
    dv = rows - ONES_ROWS
    tail = (lax.broadcasted_iota(jnp.int32, (ONES_ROWS, tk), 0) == 0).astype(vt_ref.dtype)
    for c in range(n_chunks):
        vt_ref[c, 0:dv, :] = v_ref[c * tk:(c + 1) * tk, :].astype(F32).T.astype(vt_ref.dtype)
        vt_ref[c, dv:rows, :] = tail


def _attend(qs, k_refs, vt_ref, s_ref, m_ref, acc_ref):
    n_chunks, _, tk = vt_ref.shape
    n_st = len(qs)
    m_ref[...] = jnp.full(m_ref.shape, -jnp.inf, F32)
    acc_ref[...] = jnp.zeros(acc_ref.shape, F32)

    def scores(i, c, slot):
        off = c * tk if isinstance(c, int) else pl.multiple_of(c * tk, tk)
        s_ref[i, slot] = lax.dot_general(k_refs[i][pl.ds(off, tk), :], qs[i], (((1,), (1,)), ((), ())),
                                         preferred_element_type=F32)

    def update(i, c, slot):
        s = s_ref[i, slot]
        m_old = m_ref[i]
        m_new = jnp.maximum(m_old, jnp.max(s, axis=0, keepdims=True))
        p = jnp.exp2(s - m_new).astype(vt_ref.dtype)
        acc_ref[i] = (acc_ref[i] * jnp.exp2(m_old - m_new)
                      + jnp.dot(vt_ref[c], p, preferred_element_type=F32))
        m_ref[i] = m_new

    for i in range(n_st):
        scores(i, 0, 0)

    def pair(j, carry):
        c = 2 * j
        for i in range(n_st):
            scores(i, c + 1, 1)
        for i in range(n_st):
            update(i, c, 0)
        for i in range(n_st):
            scores(i, c + 2, 0)
        for i in range(n_st):
            update(i, c + 1, 1)
        return carry

    n_pairs = (n_chunks - 1) // 2
    lax.fori_loop(0, n_pairs, pair, 0)
    done = 2 * n_pairs
    if n_chunks - done == 2:
        for i in range(n_st):
            scores(i, done + 1, 1)
    for i in range(n_st):
        update(i, done, 0)
    if n_chunks - done == 2:
        for i in range(n_st):
            update(i, done + 1, 1)


def _gqa_kernel(q_ref, k_ref, v_ref, o_ref, vt_ref, s_ref, m_ref, acc_ref):
    tk = vt_ref.shape[2]
    n_st = s_ref.shape[0]
    pl.when((pl.program_id(1) == 0) & (pl.program_id(2) == 0))(lambda: _build_vt(v_ref, vt_ref, tk))
    qs = [q_ref[:, i * HEAD_DIM:(i + 1) * HEAD_DIM] for i in range(n_st)]
    _attend(qs, [k_ref] * n_st, vt_ref, s_ref, m_ref, acc_ref)
    for i in range(n_st):
        o_t = acc_ref[i, 0:HEAD_DIM, :] / acc_ref[i, HEAD_DIM:HEAD_DIM + 1, :]
        o_ref[:, i * HEAD_DIM:(i + 1) * HEAD_DIM] = o_t.T.astype(o_ref.dtype)


def _key_chunk(t):
    for tk in (768, 1024, 512, 640, 384, 256, 128):
        if t % tk == 0:
            return tk
    raise ValueError(f"unsupported key count {t}")


def _gqa(proj, s):
    t = proj.shape[0]
    tq = 512
    tk = _key_chunk(t)
    n_st = 2
    per_g = A_GROUP // n_st
    return pl.pallas_call(
        _gqa_kernel,
        grid=(A_KV_HEADS, per_g, s // tq),
        in_specs=[pl.BlockSpec((tq, n_st * HEAD_DIM), lambda g, hh, i: (i, COL_AQ // n_st + g * per_g + hh)),
                  pl.BlockSpec((t, HEAD_DIM), lambda g, hh, i: (0, COL_AK + g)),
                  pl.BlockSpec((t, HEAD_DIM), lambda g, hh, i: (0, COL_AV + g))],
        out_specs=pl.BlockSpec((tq, n_st * HEAD_DIM), lambda g, hh, i: (i, g * per_g + hh)),
        out_shape=jax.ShapeDtypeStruct((s, A_Q_W), BF16),
        scratch_shapes=[pltpu.VMEM((t // tk, HEAD_DIM + ONES_ROWS, tk), BF16),
                        pltpu.VMEM((n_st, 2, tk, tq), F32),
                        pltpu.VMEM((n_st, 1, tq), F32),
                        pltpu.VMEM((n_st, HEAD_DIM + ONES_ROWS, tq), F32)],
        compiler_params=_cparams(("arbitrary", "arbitrary", "arbitrary")),
        name="gqa_attn",
    )(proj, proj, proj)


def _diff_kernel(lam_ref, q0_ref, q1_ref, k0_ref, k1_ref, v_ref, g_ref, o_ref,
                 vt_ref, s_ref, m_ref, acc_ref, *, lam_init):
    tk = vt_ref.shape[2]
    pl.when(pl.program_id(1) == 0)(lambda: _build_vt(v_ref, vt_ref, tk))
    lv = lam_ref[...]
    lam = (jnp.exp(jnp.sum(lv[0:1, :] * lv[1:2, :], axis=-1, keepdims=True))
           - jnp.exp(jnp.sum(lv[2:3, :] * lv[3:4, :], axis=-1, keepdims=True)) + lam_init)
    _attend([q0_ref[...], q1_ref[...]], [k0_ref, k1_ref], vt_ref, s_ref, m_ref, acc_ref)
    o_t = (acc_ref[0, 0:B_V_DIM, :] / acc_ref[0, B_V_DIM:B_V_DIM + 1, :]
           - lam * (acc_ref[1, 0:B_V_DIM, :] / acc_ref[1, B_V_DIM:B_V_DIM + 1, :]))
    o = o_t.T
    y = o * lax.rsqrt(jnp.mean(o * o, axis=-1, keepdims=True) + EPS) * g_ref[...]
    o_ref[...] = (y * (1.0 - lam_init)).astype(o_ref.dtype)


def _diff(proj, s, lam_vecs, subln_g, lam_init):
    t = proj.shape[0]
    tq = 512
    tk = _key_chunk(t)
    vb = B_V_DIM // HEAD_DIM
    return pl.pallas_call(
        functools.partial(_diff_kernel, lam_init=lam_init),
        grid=(B_HEADS, s // tq),
        in_specs=[pl.BlockSpec((4, HEAD_DIM), lambda h, i: (0, 0)),
                  pl.BlockSpec((tq, HEAD_DIM), lambda h, i: (i, COL_BQ + 2 * h)),
                  pl.BlockSpec((tq, HEAD_DIM), lambda h, i: (i, COL_BQ + 2 * h + 1)),
                  pl.BlockSpec((t, HEAD_DIM), lambda h, i: (0, COL_BK + 2 * h)),
                  pl.BlockSpec((t, HEAD_DIM), lambda h, i: (0, COL_BK + 2 * h + 1)),
                  pl.BlockSpec((t, B_V_DIM), lambda h, i: (0, COL_BV // vb + h)),
                  pl.BlockSpec((1, B_V_DIM), lambda h, i: (0, 0))],
        out_specs=pl.BlockSpec((tq, B_V_DIM), lambda h, i: (i, h)),
        out_shape=jax.ShapeDtypeStruct((s, B_V_W), BF16),
        scratch_shapes=[pltpu.VMEM((t // tk, B_V_DIM + ONES_ROWS, tk), BF16),
                        pltpu.VMEM((2, 2, tk, tq), F32),
                        pltpu.VMEM((2, 1, tq), F32),
                        pltpu.VMEM((2, B_V_DIM + ONES_ROWS, tq), F32)],
        compiler_params=_cparams(("arbitrary", "arbitrary")),
        name="diff_attn",
    )(lam_vecs, proj, proj, proj, proj, proj, subln_g.reshape(1, B_V_DIM))


def _merge_kernel(oa_ref, ob_ref, ga_ref, gb_ref, wa_ref, wb_ref, wo_ref, x_ref, gate_ref, o_ref, t_ref):
    c = pl.program_id(1)
    nc, _, tn = t_ref.shape
    ya = jnp.dot(oa_ref[...], wa_ref[...], preferred_element_type=F32)
    yb = jnp.dot(ob_ref[...], wb_ref[...], preferred_element_type=F32)
    t = (jax.nn.sigmoid(ga_ref[...].astype(F32)) * ya + jax.nn.sigmoid(gb_ref[...].astype(F32)) * yb)
    t_ref[c] = t.astype(t_ref.dtype)

    @pl.when(c == nc - 1)
    def _():
        y = jnp.dot(t_ref[0], wo_ref[0:tn, :], preferred_element_type=F32)
        for cc in range(1, nc):
            y = y + jnp.dot(t_ref[cc], wo_ref[cc * tn:(cc + 1) * tn, :], preferred_element_type=F32)
        o_ref[...] = x_ref[...] + gate_ref[...] * y


def _merge(oa, ob, proj, wa, wb, wo, x, gate):
    s, d = x.shape
    tm = 512
    tn = PROJ_TN
    nc = d // tn
    ga0 = COL_GA * HEAD_DIM // tn
    gb0 = COL_GB * HEAD_DIM // tn
    return pl.pallas_call(
        _merge_kernel,
        grid=(s // tm, nc),
        in_specs=[pl.BlockSpec((tm, A_Q_W), lambda i, c: (i, 0)),
                  pl.BlockSpec((tm, B_V_W), lambda i, c: (i, 0)),
                  pl.BlockSpec((tm, tn), lambda i, c: (i, ga0 + c)),
                  pl.BlockSpec((tm, tn), lambda i, c: (i, gb0 + c)),
                  pl.BlockSpec((A_Q_W, tn), lambda i, c: (0, c)),
                  pl.BlockSpec((B_V_W, tn), lambda i, c: (0, c)),
                  pl.BlockSpec((d, d), lambda i, c: (0, 0)),
                  pl.BlockSpec((tm, d), lambda i, c: (i, 0)),
                  pl.BlockSpec((1, d), lambda i, c: (0, 0))],
        out_specs=pl.BlockSpec((tm, d), lambda i, c: (i, 0)),
        out_shape=jax.ShapeDtypeStruct((s, d), F32),
        scratch_shapes=[pltpu.VMEM((nc, tm, tn), BF16)],
        compiler_params=_cparams(("arbitrary", "arbitrary")),
        name="merge_out",
    )(oa, ob, proj, proj, wa, wb, wo, x, gate)


def _lane_max(x):
    return jnp.max(x, axis=-1, keepdims=True)


def _lane_min(x):
    return jnp.min(x, axis=-1, keepdims=True)


def _group_allreduce(x, lane, op):
    for sft in (1, 2, 4):
        up = pltpu.roll(x, sft, 1)
        dn = pltpu.roll(x, LANES - sft, 1)
        x = op(x, jnp.where((lane & sft) != 0, up, dn))
    return x


def _router_kernel(x_ref, g_ref, mod_ref, wr_ref, rb_ref,
                   h_ref, hp_ref, idx_ref, wsel_ref, rank_ref, cnt_ref, carry_ref):
    @pl.when(pl.program_id(0) == 0)
    def _():
        carry_ref[...] = jnp.zeros_like(carry_ref)

    h = _rms_mod(x_ref[...], g_ref[...], mod_ref[0, 3:4, :], mod_ref[0, 4:5, :])
    hb = h.astype(BF16)
    h_ref[...] = hb
    bits = lax.bitcast_convert_type(hb.astype(F32), jnp.uint32)
    half = bits.shape[1] // 2
    words = (bits[:, :half] >> 16) | (bits[:, half:] & jnp.uint32(0xFFFF0000))
    tm = h.shape[0]
    for c in range(PACK_TILES):
        hp_ref[pl.ds(c, tm, stride=PACK_TILES), :] = words[:, c * LANES:(c + 1) * LANES]

    logits = jnp.dot(h, wr_ref[...], preferred_element_type=F32, precision=lax.Precision.HIGHEST)
    scores = jax.nn.sigmoid(logits)
    lane = lax.broadcasted_iota(jnp.int32, (tm, LANES), 1)
    valid = lane < N_EXPERTS
    neg = jnp.float32(-jnp.inf)
    big = jnp.int32(LANES)
    biased = jnp.where(valid, scores + rb_ref[...], neg)

    m1 = _group_allreduce(biased, lane, jnp.maximum)
    a1 = _group_allreduce(jnp.where(biased == m1, lane, big), lane, jnp.minimum)
    m2 = _group_allreduce(jnp.where(lane == a1, neg, biased), lane, jnp.maximum)
    gscore = jnp.where(valid, m1 + m2, neg)
    gid = lane >> 3
    keep = jnp.zeros((tm, LANES), jnp.bool_)
    for _ in range(TOPK_GROUPS):
        best = _lane_max(gscore)
        gsel = _lane_min(jnp.where(gscore == best, gid, big))
        hit = gid == gsel
        keep = keep | hit
        gscore = jnp.where(hit, neg, gscore)
    cand = jnp.where(keep & valid, biased, neg)

    onehot = jnp.zeros((tm, LANES), F32)
    idx_out = jnp.zeros((tm, LANES), jnp.int32)
    w_out = jnp.zeros((tm, LANES), F32)
    sels = []
    for k in range(TOP_K):
        best = _lane_max(cand)
        sel = _lane_min(jnp.where(cand == best, lane, big))
        hit = lane == sel
        wk = jnp.sum(jnp.where(hit, scores, 0.0), axis=-1, keepdims=True)
        cand = jnp.where(hit, neg, cand)
        onehot = jnp.where(hit, 1.0, onehot)
        idx_out = jnp.where(lane == k, sel, idx_out)
        w_out = jnp.where(lane == k, wk, w_out)
        sels.append(hit)
    wsum = jnp.sum(w_out, axis=-1, keepdims=True)
    wsel_ref[...] = w_out / wsum * ROUTED_SCALE
    idx_ref[...] = idx_out

    row = lax.broadcasted_iota(jnp.int32, (tm, tm), 0)
    col = lax.broadcasted_iota(jnp.int32, (tm, tm), 1)
    lower = (col < row).astype(BF16)
    before = jnp.dot(lower, onehot.astype(BF16), preferred_element_type=F32) + carry_ref[...]
    rank_out = jnp.zeros((tm, LANES), F32)
    for k in range(TOP_K):
        rk = jnp.sum(jnp.where(sels[k], before, 0.0), axis=-1, keepdims=True)
        rank_out = jnp.where(lane == k, rk, rank_out)
    rank_ref[...] = rank_out.astype(jnp.int32)
    carry_ref[...] = carry_ref[...] + jnp.sum(onehot, axis=0, keepdims=True)
    cnt_ref[...] = carry_ref[...]


def _router(x1, g, mod, w_router, router_bias):
    s, d = x1.shape
    tm = 256
    e = w_router.shape[1]
    wr = jnp.pad(w_router, ((0, 0), (0, LANES - e)))
    rb = jnp.pad(router_bias, (0, LANES - e)).reshape(1, LANES)
    row_spec = lambda w: pl.BlockSpec((tm, w), lambda i: (i, 0))
    return pl.pallas_call(
        _router_kernel,
        grid=(s // tm,),
        in_specs=[row_spec(d),
                  pl.BlockSpec((1, d), lambda i: (0, 0)),
                  pl.BlockSpec((1, N_MOD, d), lambda i: (0, 0, 0)),
                  pl.BlockSpec((d, LANES), lambda i: (0, 0)),
                  pl.BlockSpec((1, LANES), lambda i: (0, 0))],
        out_specs=[row_spec(d), pl.BlockSpec((tm * PACK_TILES, LANES), lambda i: (i, 0)),
                   row_spec(LANES), row_spec(LANES), row_spec(LANES),
                   pl.BlockSpec((1, LANES), lambda i: (0, 0))],
        out_shape=[jax.ShapeDtypeStruct((s, d), BF16),
                   jax.ShapeDtypeStruct((s * PACK_TILES, LANES), jnp.uint32),
                   jax.ShapeDtypeStruct((s, LANES), jnp.int32),
                   jax.ShapeDtypeStruct((s, LANES), F32),
                   jax.ShapeDtypeStruct((s, LANES), jnp.int32),
                   jax.ShapeDtypeStruct((1, LANES), F32)],
        scratch_shapes=[pltpu.VMEM((1, LANES), F32)],
        compiler_params=_cparams(("arbitrary",)),
        name="ffn_router",
    )(x1, g.reshape(1, d), mod, wr, rb)


def _unpack_rows(words):
    lo = lax.bitcast_convert_type(words << 16, F32).astype(BF16)
    hi = lax.bitcast_convert_type(words & jnp.uint32(0xFFFF0000), F32).astype(BF16)
    return jnp.concatenate([lo, hi], axis=1)


def _token_rows(row, tiles):
    return pl.ds(pl.multiple_of(row * tiles, tiles), tiles)


def _dispatch_kernel(ps_ref, pe_ref, nused_ref, idx_hbm, rank_hbm, hp_ref, xs_hbm,
                     idx_smem, rank_smem, zbuf, sem_i, sem_z, sem_x, *, n_blocks):
    i = pl.program_id(0)
    n = idx_smem.shape[0]
    tm = n // TOP_K
    copy_idx = pltpu.make_async_copy(idx_hbm.at[i], idx_smem, sem_i.at[0])
    copy_rank = pltpu.make_async_copy(rank_hbm.at[i], rank_smem, sem_i.at[1])
    copy_idx.start()
    copy_rank.start()

    @pl.when(i == 0)
    def _():
        zbuf[...] = jnp.zeros_like(zbuf)

        def zero_block(row0):
            span = MOE_BLOCK * PACK_TILES
            return pltpu.make_async_copy(
                zbuf, xs_hbm.at[pl.ds(pl.multiple_of(row0 * PACK_TILES, span), span), :], sem_z)

        def per_expert(action):
            def body(e, carry):
                @pl.when(pe_ref[e] > ps_ref[e])
                def _():
                    action(zero_block(pe_ref[e] - MOE_BLOCK))
                return carry
            lax.fori_loop(0, N_EXPERTS, body, 0)

        def per_tail(action):
            def body(b, carry):
                action(zero_block(b * MOE_BLOCK))
                return carry
            lax.fori_loop(nused_ref[0], n_blocks, body, 0)

        per_expert(lambda cp: cp.start())
        per_tail(lambda cp: cp.start())
        per_expert(lambda cp: cp.wait())
        per_tail(lambda cp: cp.wait())

    copy_idx.wait()
    copy_rank.wait()

    def issue(t, carry):
        for k in range(TOP_K):
            j = t * TOP_K + k
            dest = ps_ref[idx_smem[j]] + rank_smem[j]
            pltpu.make_async_copy(hp_ref.at[_token_rows(t, PACK_TILES), :],
                                  xs_hbm.at[_token_rows(dest, PACK_TILES), :], sem_x).start()
        return carry

    lax.fori_loop(0, tm, issue, 0)
    for _ in range(TOP_K):
        pltpu.make_async_copy(hp_ref, xs_hbm.at[pl.ds(0, tm * PACK_TILES), :], sem_x).wait()


def _dispatch(pad_start, pad_end, n_used, idx_tm, rank_tm, hp, n_blocks):
    n_tiles, n = idx_tm.shape
    tm = n // TOP_K
    grid_spec = pltpu.PrefetchScalarGridSpec(
        num_scalar_prefetch=3,
        grid=(n_tiles,),
        in_specs=[pl.BlockSpec(memory_space=pl.ANY),
                  pl.BlockSpec(memory_space=pl.ANY),
                  pl.BlockSpec((tm * PACK_TILES, LANES), lambda i, ps, pe, nu: (i, 0))],
        out_specs=pl.BlockSpec(memory_space=pl.ANY),
        scratch_shapes=[pltpu.SMEM((n,), jnp.int32),
                        pltpu.SMEM((n,), jnp.int32),
                        pltpu.VMEM((MOE_BLOCK * PACK_TILES, LANES), jnp.uint32),
                        pltpu.SemaphoreType.DMA((2,)),
                        pltpu.SemaphoreType.DMA,
                        pltpu.SemaphoreType.DMA])
    return pl.pallas_call(
        functools.partial(_dispatch_kernel, n_blocks=n_blocks),
        grid_spec=grid_spec,
        out_shape=jax.ShapeDtypeStruct((n_blocks * MOE_BLOCK * PACK_TILES, LANES), jnp.uint32),
        compiler_params=_cparams(("arbitrary",)),
        name="moe_dispatch",
    )(pad_start, pad_end, n_used, idx_tm, rank_tm, hp)


def _moe_kernel(be_ref, first_ref, nxt_ref, slot_ref, nused_ref, x_ref, wg_hbm, wu_hbm, wd_hbm, o_ref,
                wg32, wu32, wd32, wgb, wub, wdb, sems):
    b = pl.program_id(0)

    def fetch(e, slot):
        return (pltpu.make_async_copy(wg_hbm.at[e], wg32.at[slot], sems.at[slot, 0]),
                pltpu.make_async_copy(wu_hbm.at[e], wu32.at[slot], sems.at[slot, 1]),
                pltpu.make_async_copy(wd_hbm.at[e], wd32.at[slot], sems.at[slot, 2]))

    @pl.when(b < nused_ref[0])
    def _():
        @pl.when(first_ref[b] == 1)
        def _():
            slot = slot_ref[b]

            @pl.when(b == 0)
            def _():
                for cp in fetch(be_ref[0], 0):
                    cp.start()

            for cp in fetch(be_ref[b], slot):
                cp.wait()

            @pl.when(nxt_ref[b] >= 0)
            def _():
                for cp in fetch(nxt_ref[b], 1 - slot):
                    cp.start()

            wgb[...] = wg32[slot].astype(BF16)
            wub[...] = wu32[slot].astype(BF16)
            wdb[...] = wd32[slot].astype(BF16)

        words = jnp.concatenate(
            [x_ref[pl.ds(c, MOE_BLOCK, stride=PACK_TILES), :] for c in range(PACK_TILES)], axis=1)
        x = _unpack_rows(words)
        gate = jnp.dot(x, wgb[...], preferred_element_type=F32)
        up = jnp.dot(x, wub[...], preferred_element_type=F32)
        act = (gate * jax.nn.sigmoid(gate) * up).astype(BF16)
        y = jnp.dot(act, wdb[...], preferred_element_type=F32)
        for c in range(ROW_TILES):
            o_ref[pl.ds(c, MOE_BLOCK, stride=ROW_TILES), :] = y[:, c * LANES:(c + 1) * LANES]

    @pl.when(b >= nused_ref[0])
    def _():
        o_ref[...] = jnp.zeros_like(o_ref)


def _moe(block_e, first, nxt, slot, n_used, xs, weg, weu, wed):
    n_blocks = block_e.shape[0]
    _, d, f = weg.shape
    grid_spec = pltpu.PrefetchScalarGridSpec(
        num_scalar_prefetch=5,
        grid=(n_blocks,),
        in_specs=[pl.BlockSpec((MOE_BLOCK * PACK_TILES, LANES), lambda b, *_: (b, 0)),
                  pl.BlockSpec(memory_space=pl.ANY),
                  pl.BlockSpec(memory_space=pl.ANY),
                  pl.BlockSpec(memory_space=pl.ANY)],
        out_specs=pl.BlockSpec((MOE_BLOCK * ROW_TILES, LANES), lambda b, *_: (b, 0)),
        scratch_shapes=[pltpu.VMEM((2, d, f), F32), pltpu.VMEM((2, d, f), F32), pltpu.VMEM((2, f, d), F32),
                        pltpu.VMEM((d, f), BF16), pltpu.VMEM((d, f), BF16), pltpu.VMEM((f, d), BF16),
                        pltpu.SemaphoreType.DMA((2, 3))])
    return pl.pallas_call(
        _moe_kernel,
        grid_spec=grid_spec,
        out_shape=jax.ShapeDtypeStruct((n_blocks * MOE_BLOCK * ROW_TILES, LANES), F32),
        compiler_params=_cparams(("arbitrary",)),
        name="moe_experts",
    )(block_e, first, nxt, slot, n_used, xs, weg, weu, wed)


def _combine_kernel(ps_ref, idx_hbm, rank_hbm, y_hbm, w_ref, h_ref, wg_ref, wu_ref, wd_ref, x_ref, gate_ref,
                    o_ref, idx_smem, rank_smem, ybuf, sem_i, sem_y):
    i = pl.program_id(0)
    tm = h_ref.shape[0]
    n = tm * TOP_K
    span = n * ROW_TILES

    def start_tile(tile, slot):
        copy_idx = pltpu.make_async_copy(idx_hbm.at[tile], idx_smem.at[slot], sem_i.at[0])
        copy_rank = pltpu.make_async_copy(rank_hbm.at[tile], rank_smem.at[slot], sem_i.at[1])
        copy_idx.start()
        copy_rank.start()
        copy_idx.wait()
        copy_rank.wait()

        def issue(r, carry):
            src = ps_ref[idx_smem[slot, r]] + rank_smem[slot, r]
            pltpu.make_async_copy(y_hbm.at[_token_rows(src, ROW_TILES), :],
                                  ybuf.at[slot, _token_rows(r, ROW_TILES), :], sem_y.at[slot]).start()
            return carry

        lax.fori_loop(0, n, issue, 0, unroll=8)

    pl.when(i == 0)(lambda: start_tile(0, 0))
    pl.when(i + 1 < pl.num_programs(0))(lambda: start_tile(i + 1, (i + 1) % 2))

    h = h_ref[...]
    gate = jnp.dot(h, wg_ref[...], preferred_element_type=F32)
    up = jnp.dot(h, wu_ref[...], preferred_element_type=F32)
    act = (gate * jax.nn.sigmoid(gate) * up).astype(BF16)
    shared = jnp.dot(act, wd_ref[...], preferred_element_type=F32)

    slot = i % 2
    pltpu.make_async_copy(y_hbm.at[pl.ds(0, span), :], ybuf.at[slot], sem_y.at[slot]).wait()
    yb = ybuf.at[slot]
    chunks = []
    for c in range(ROW_TILES):
        acc = yb[pl.ds(c, tm, stride=ROW_TILES), :] * w_ref[:, 0:1]
        for k in range(1, TOP_K):
            acc = acc + yb[pl.ds(k * tm * ROW_TILES + c, tm, stride=ROW_TILES), :] * w_ref[:, k:k + 1]
        chunks.append(acc)
    routed = jnp.concatenate(chunks, axis=1)
    o_ref[...] = x_ref[...] + gate_ref[...] * (routed + shared)


def _combine(pad_start, idx_km, rank_km, y_sorted, wsel, h2, wsg, wsu, wsd, x1, gate):
    s, d = x1.shape
    n_tiles, n = idx_km.shape
    tm = n // TOP_K
    f = wsg.shape[1]
    grid_spec = pltpu.PrefetchScalarGridSpec(
        num_scalar_prefetch=1,
        grid=(n_tiles,),
        in_specs=[pl.BlockSpec(memory_space=pl.ANY),
                  pl.BlockSpec(memory_space=pl.ANY),
                  pl.BlockSpec(memory_space=pl.ANY),
                  pl.BlockSpec((tm, LANES), lambda i, ps: (i, 0)),
                  pl.BlockSpec((tm, d), lambda i, ps: (i, 0)),
                  pl.BlockSpec((d, f), lambda i, ps: (0, 0)),
                  pl.BlockSpec((d, f), lambda i, ps: (0, 0)),
                  pl.BlockSpec((f, d), lambda i, ps: (0, 0)),
                  pl.BlockSpec((tm, d), lambda i, ps: (i, 0)),
                  pl.BlockSpec((1, d), lambda i, ps: (0, 0))],
        out_specs=pl.BlockSpec((tm, d), lambda i, ps: (i, 0)),
        scratch_shapes=[pltpu.SMEM((2, n), jnp.int32),
                        pltpu.SMEM((2, n), jnp.int32),
                        pltpu.VMEM((2, n * ROW_TILES, LANES), F32),
                        pltpu.SemaphoreType.DMA((2,)),
                        pltpu.SemaphoreType.DMA((2,))])
    return pl.pallas_call(
        _combine_kernel,
        grid_spec=grid_spec,
        out_shape=jax.ShapeDtypeStruct((s, d), F32),
        compiler_params=_cparams(("arbitrary",)),
        name="moe_combine",
    )(pad_start, idx_km, rank_km, y_sorted, wsel, h2, wsg, wsu, wsd, x1, gate)


def _block_tables(counts, n_blocks):
    padded = (counts + MOE_BLOCK - 1) // MOE_BLOCK * MOE_BLOCK
    pad_end = jnp.cumsum(padded).astype(jnp.int32)
    pad_start = pad_end - padded
    blk = jnp.arange(n_blocks, dtype=jnp.int32)
    block_e = jnp.minimum(jnp.sum(pad_end[None, :] <= (blk * MOE_BLOCK)[:, None], axis=1),
                          N_EXPERTS - 1).astype(jnp.int32)
    n_used = pad_end[-1] // MOE_BLOCK
    prev = jnp.concatenate([jnp.full((1,), -1, jnp.int32), block_e[:-1]])
    first = ((blk < n_used) & (block_e != prev)).astype(jnp.int32)
    slot = jnp.maximum(jnp.cumsum(first) - 1, 0).astype(jnp.int32) % 2
    after = pad_end[block_e] // MOE_BLOCK
    nxt = jnp.where(after < n_used, block_e[jnp.minimum(after, n_blocks - 1)], -1).astype(jnp.int32)
    return pad_start, pad_end, block_e, first, nxt, slot, n_used.reshape(1)


def kernel(x, c, ctx, c_ctx, w_ada, b_ada, norm_mix, norm_ffn, w_in, q_norm_a, k_norm_a, q_norm_b, k_norm_b, lambda_q1, lambda_k1, lambda_q2, lambda_k2, subln_b, w_branch_a, w_branch_b, w_out, w_router, router_bias, w_exp_gate, w_exp_up, w_exp_down, w_sh_gate, w_sh_up, w_sh_down):
    depth = w_ada.shape[0]
    assert depth == 1 and x.shape[0] == 1 and ctx.shape[0] == 1
    s, d = x.shape[1], x.shape[2]
    n_ctx = ctx.shape[1]
    i = 0
    lam_init = 0.8 - 0.6 * math.exp(-0.3 * i)
    xs = x[0]

    mod = _adaln(jnp.concatenate([c, c_ctx[None, :]], axis=0), w_ada[i], b_ada[i]).reshape(2, N_MOD, d)

    h = _prenorm(xs, ctx[0], norm_mix[i], mod)
    tc, tsa, tsb = _rope_tables(s, n_ctx)
    gains = _head_gains(q_norm_a[i], k_norm_a[i], q_norm_b[i], k_norm_b[i])
    proj = _inproj(h, w_in[i].astype(BF16), gains, tc, tsa, tsb)
    oa = _gqa(proj, s)
    lam_vecs = jnp.stack([lambda_q1[i], lambda_k1[i], lambda_q2[i], lambda_k2[i]]).astype(F32)
    ob = _diff(proj, s, lam_vecs, subln_b[i], lam_init)
    x1 = _merge(oa, ob, proj, w_branch_a[i].astype(BF16), w_branch_b[i].astype(BF16),
                w_out[i].astype(BF16), xs, mod[0, 2:3, :])

    h2, h2p, idx, wsel, rank, cnt = _router(x1, norm_ffn[i], mod[0:1], w_router[i], router_bias[i])
    counts = cnt[0, :N_EXPERTS].astype(jnp.int32)
    n_blocks = -(-(s * TOP_K) // MOE_BLOCK) + N_EXPERTS
    pad_start, pad_end, block_e, first, nxt, slot, n_used = _block_tables(counts, n_blocks)
    tm_r = 128
    tiled = lambda a: a[:, :TOP_K].reshape(s // tm_r, tm_r, TOP_K)
    token_major = lambda a: tiled(a).reshape(s // tm_r, tm_r * TOP_K)
    k_major = lambda a: tiled(a).transpose(0, 2, 1).reshape(s // tm_r, tm_r * TOP_K)
    xs = _dispatch(pad_start, pad_end, n_used, token_major(idx), token_major(rank), h2p, n_blocks)
    y_sorted = _moe(block_e, first, nxt, slot, n_used, xs, w_exp_gate[i], w_exp_up[i], w_exp_down[i])
    out = _combine(pad_start, k_major(idx), k_major(rank), y_sorted, wsel, h2,
                   w_sh_gate[i].astype(BF16), w_sh_up[i].astype(BF16), w_sh_down[i].astype(BF16),
                   x1, mod[0, 5:6, :])
    return out[None]
```

```python
import functools
import math

import jax
import jax.numpy as jnp
from jax import lax
from jax.experimental import pallas as pl
from jax.experimental.pallas import tpu as pltpu

F32 = jnp.float32
BF16 = jnp.bfloat16

D_MODEL = 2048
GRID_W = 64
HEAD_DIM = 128
ROPE_PAIRS = HEAD_DIM // 4
ROPE_THETA = 10000.0
A_HEADS = 8
A_KV_HEADS = 2
A_GROUP = A_HEADS // A_KV_HEADS
B_HEADS = 4
B_V_DIM = 2 * HEAD_DIM
N_EXPERTS = 64
TOP_K = 8
N_GROUPS = 8
TOPK_GROUPS = 4
EXPERT_DIM = 512
SHARED_DIM = 512
ROUTED_SCALE = 2.5
N_MOD = 6
EPS = 1e-6

A_Q_W = A_HEADS * HEAD_DIM
A_KV_W = A_KV_HEADS * HEAD_DIM
B_QK_W = B_HEADS * 2 * HEAD_DIM
B_V_W = B_HEADS * B_V_DIM
IN_W = A_Q_W + 2 * A_KV_W + 2 * B_QK_W + B_V_W + 2 * D_MODEL

COL_AQ = 0
COL_AK = A_Q_W // HEAD_DIM
COL_AV = COL_AK + A_KV_HEADS
COL_BQ = COL_AV + A_KV_HEADS
COL_BK = COL_BQ + 2 * B_HEADS
COL_BV = COL_BK + 2 * B_HEADS
COL_GA = COL_BV + B_V_W // HEAD_DIM
COL_GB = COL_GA + D_MODEL // HEAD_DIM

LANES = 128
SUBLANES = 8
VMEM_LIMIT = 56 * 1024 * 1024

PROJ_TN = 512
MOE_BLOCK = 256
ONES_ROWS = 16
PACK_TILES = D_MODEL // 2 // LANES
YBUF_PITCH = PACK_TILES + 4
TOKEN_PITCH = TOP_K * YBUF_PITCH + 4
LOG2E = 1.4426950408889634


def _cparams(sem, vmem=VMEM_LIMIT):
    return pltpu.CompilerParams(dimension_semantics=sem, vmem_limit_bytes=vmem)


def _adaln_kernel(cb_ref, w_ref, b_ref, o_ref):
    tn = w_ref.shape[1]
    nl = tn // LANES
    rows = 32

    def body(g, accs):
        accs = list(accs)
        r0 = pl.multiple_of(g * rows, rows)
        for u in range(rows // SUBLANES):
            r = r0 + u * SUBLANES
            w = w_ref[pl.ds(r, SUBLANES), :]
            for v in range(2):
                c = cb_ref[v, pl.ds(r, SUBLANES), :]
                s = c * jax.nn.sigmoid(c)
                for j in range(nl):
                    accs[v * nl + j] = accs[v * nl + j] + w[:, j * LANES:(j + 1) * LANES] * s
        return tuple(accs)

    init = tuple(jnp.zeros((SUBLANES, LANES), F32) for _ in range(2 * nl))
    accs = lax.fori_loop(0, w_ref.shape[0] // rows, body, init)
    for v in range(2):
        row = jnp.concatenate(
            [jnp.sum(accs[v * nl + j], axis=0, keepdims=True) for j in range(nl)], axis=1)
        o_ref[v:v + 1, :] = row + b_ref[...]


def _adaln(cvecs, w, b):
    d, n = w.shape
    tn = 1536
    cb = jnp.broadcast_to(cvecs[:, :, None], (2, d, LANES))
    return pl.pallas_call(
        _adaln_kernel,
        grid=(n // tn,),
        in_specs=[pl.BlockSpec((2, d, LANES), lambda j: (0, 0, 0)),
                  pl.BlockSpec((d, tn), lambda j: (0, j)),
                  pl.BlockSpec((1, tn), lambda j: (0, j))],
        out_specs=pl.BlockSpec((2, tn), lambda j: (0, j)),
        out_shape=jax.ShapeDtypeStruct((2, n), F32),
        compiler_params=_cparams(("arbitrary",)),
        name="adaln",
    )(cb, w, b.reshape(1, n))


def _rms_mod(x, g, shift, scale):
    y = x * lax.rsqrt(jnp.mean(x * x, axis=-1, keepdims=True) + EPS) * g
    return y * (1.0 + scale) + shift


def _prenorm_kernel(x_ref, c_ref, g_ref, mod_ref, o_ref, *, n_lat_tiles):
    is_ctx = pl.program_id(0) >= n_lat_tiles
    x = jnp.where(is_ctx, c_ref[...], x_ref[...])
    o_ref[...] = _rms_mod(x, g_ref[...], mod_ref[0, 0:1, :], mod_ref[0, 1:2, :]).astype(o_ref.dtype)


def _prenorm(x, ctx, g, mod):
    s, d = x.shape
    c = ctx.shape[0]
    tm = 256
    nl, nc = s // tm, c // tm
    return pl.pallas_call(
        functools.partial(_prenorm_kernel, n_lat_tiles=nl),
        grid=(nl + nc,),
        in_specs=[pl.BlockSpec((tm, d), lambda i: (jnp.minimum(i, nl - 1), 0)),
                  pl.BlockSpec((tm, d), lambda i: (jnp.maximum(i - nl, 0), 0)),
                  pl.BlockSpec((1, d), lambda i: (0, 0)),
                  pl.BlockSpec((1, N_MOD, d), lambda i: (i // nl, 0, 0))],
        out_specs=pl.BlockSpec((tm, d), lambda i: (i, 0)),
        out_shape=jax.ShapeDtypeStruct((s + c, d), BF16),
        compiler_params=_cparams(("arbitrary",)),
        name="prenorm_mix",
    )(x, ctx, g.reshape(1, d), mod)


def _inproj_kernel(h_ref, w_ref, gain_ref, c_ref, sa_ref, sb_ref, o_ref):
    j = pl.program_id(1)
    acc = jnp.dot(h_ref[...], w_ref[...], preferred_element_type=F32)
    nh = acc.shape[1] // HEAD_DIM

    def norm_rope(a, gain):
        y = a * lax.rsqrt(jnp.mean(a * a, axis=-1, keepdims=True) + EPS) * gain
        return (y * c_ref[...] + pltpu.roll(y, ROPE_PAIRS, 1) * sa_ref[...]
                + pltpu.roll(y, HEAD_DIM - ROPE_PAIRS, 1) * sb_ref[...])

    def store(n_normed):
        for hd in range(nh):
            sl = slice(hd * HEAD_DIM, (hd + 1) * HEAD_DIM)
            a = acc[:, sl]
            if hd < n_normed:
                a = norm_rope(a, gain_ref[0, :, sl])
            o_ref[:, sl] = a.astype(o_ref.dtype)

    all_normed = (j < 2) | ((j >= 3) & (j < 7))
    pl.when(all_normed)(lambda: store(nh))
    pl.when(j == 2)(lambda: store(A_KV_HEADS))
    pl.when(j >= 7)(lambda: store(0))


def _inproj(h, w_bf16, gains, rope_c, rope_sa, rope_sb):
    t, d = h.shape
    n = w_bf16.shape[1]
    tm = t // 8
    tn = PROJ_TN
    return pl.pallas_call(
        _inproj_kernel,
        grid=(t // tm, n // tn),
        in_specs=[pl.BlockSpec((tm, d), lambda i, j: (i, 0)),
                  pl.BlockSpec((d, tn), lambda i, j: (0, j)),
                  pl.BlockSpec((1, 1, tn), lambda i, j: (j, 0, 0)),
                  pl.BlockSpec((tm, HEAD_DIM), lambda i, j: (i, 0)),
                  pl.BlockSpec((tm, HEAD_DIM), lambda i, j: (i, 0)),
                  pl.BlockSpec((tm, HEAD_DIM), lambda i, j: (i, 0))],
        out_specs=pl.BlockSpec((tm, tn), lambda i, j: (i, j)),
        out_shape=jax.ShapeDtypeStruct((t, n), BF16),
        compiler_params=_cparams(("arbitrary", "arbitrary")),
        name="inproj",
    )(h, w_bf16, gains, rope_c, rope_sa, rope_sb)


def _rope_tables(s, c):
    rows_n = s // GRID_W
    rows = jnp.repeat(jnp.arange(rows_n, dtype=F32), GRID_W)
    cols = jnp.tile(jnp.arange(GRID_W, dtype=F32), rows_n)
    inv = ROPE_THETA ** (-jnp.arange(ROPE_PAIRS, dtype=F32) / ROPE_PAIRS)
    ang_r = rows[:, None] * inv
    ang_c = cols[:, None] * inv
    cr, sr, cc, sc = jnp.cos(ang_r), jnp.sin(ang_r), jnp.cos(ang_c), jnp.sin(ang_c)
    z = jnp.zeros_like(sr)
    tc = jnp.concatenate([cr, cr, cc, cc], axis=1)
    tsa = jnp.concatenate([z, sr, z, sc], axis=1)
    tsb = jnp.concatenate([-sr, z, -sc, z], axis=1)
    pad = lambda a, v: jnp.concatenate([a, jnp.full((c, HEAD_DIM), v, F32)], axis=0)
    return pad(tc, 1.0), pad(tsa, 0.0), pad(tsb, 0.0)


def _head_gains(qn_a, kn_a, qn_b, kn_b):
    qs = HEAD_DIM ** -0.5 * LOG2E
    one = jnp.ones((HEAD_DIM,), F32)
    heads = ([qn_a * qs] * A_HEADS + [kn_a] * A_KV_HEADS + [one] * A_KV_HEADS
             + [qn_b * qs] * (2 * B_HEADS) + [kn_b] * (2 * B_HEADS))
    heads = heads + [one] * (IN_W // HEAD_DIM - len(heads))
    return jnp.concatenate(heads).reshape(IN_W // PROJ_TN, 1, PROJ_TN)


def _build_vt(v_ref, vt_ref, tk):
    n_chunks, rows, _ = vt_ref.shape
    dv = rows - ONES_ROWS
    tail = (lax.broadcasted_iota(jnp.int32, (ONES_ROWS, tk), 0) == 0).astype(vt_ref.dtype)
    for c in range(n_chunks):
        vt_ref[c, 0:dv, :] = v_ref[c * tk:(c + 1) * tk, :].astype(F32).T.astype(vt_ref.dtype)
        vt_ref[c, dv:rows, :] = tail


def _attend(qs, k_refs, vt_ref, s_ref, m_ref, acc_ref):
    n_chunks, _, tk = vt_ref.shape
    n_st = len(qs)
    m_ref[...] = jnp.full(m_ref.shape, -jnp.inf, F32)
    acc_ref[...] = jnp.zeros(acc_ref.shape, F32)

    def scores(i, c, slot):
        off = c * tk if isinstance(c, int) else pl.multiple_of(c * tk, tk)
        s_ref[i, slot] = lax.dot_general(k_refs[i][pl.ds(off, tk), :], qs[i], (((1,), (1,)), ((), ())),
                                         preferred_element_type=F32)

    def update(i, c, slot):
        s = s_ref[i, slot]
        m_old = m_ref[i]
        m_new = jnp.maximum(m_old, jnp.max(s, axis=0, keepdims=True))
        p = jnp.exp2(s - m_new).astype(vt_ref.dtype)
        acc_ref[i] = (acc_ref[i] * jnp.exp2(m_old - m_new)
                      + jnp.dot(vt_ref[c], p, preferred_element_type=F32))
        m_ref[i] = m_new

    for i in range(n_st):
        scores(i, 0, 0)

    def pair(j, carry):
        c = 2 * j
        for i in range(n_st):
            scores(i, c + 1, 1)
        for i in range(n_st):
            update(i, c, 0)
        for i in range(n_st):
            scores(i, c + 2, 0)
        for i in range(n_st):
            update(i, c + 1, 1)
        return carry

    n_pairs = (n_chunks - 1) // 2
    lax.fori_loop(0, n_pairs, pair, 0)
    done = 2 * n_pairs
    if n_chunks - done == 2:
        for i in range(n_st):
            scores(i, done + 1, 1)
    for i in range(n_st):
        update(i, done, 0)
    if n_chunks - done == 2:
        for i in range(n_st):
            update(i, done + 1, 1)


def _gqa_kernel(q_ref, k_ref, v_ref, o_ref, vt_ref, s_ref, m_ref, acc_ref):
    tk = vt_ref.shape[2]
    n_st = s_ref.shape[0]
    pl.when((pl.program_id(1) == 0) & (pl.program_id(2) == 0))(lambda: _build_vt(v_ref, vt_ref, tk))
    qs = [q_ref[:, i * HEAD_DIM:(i + 1) * HEAD_DIM] for i in range(n_st)]
    _attend(qs, [k_ref] * n_st, vt_ref, s_ref, m_ref, acc_ref)
    for i in range(n_st):
        o_t = acc_ref[i, 0:HEAD_DIM, :] / acc_ref[i, HEAD_DIM:HEAD_DIM + 1, :]
        o_ref[:, i * HEAD_DIM:(i + 1) * HEAD_DIM] = o_t.T.astype(o_ref.dtype)


def _key_chunk(t):
    for tk in (768, 1024, 512, 640, 384, 256, 128):
        if t % tk == 0:
            return tk
    raise ValueError(f"unsupported key count {t}")


def _gqa(proj, s):
    t = proj.shape[0]
    tq = 512
    tk = _key_chunk(t)
    n_st = 2
    per_g = A_GROUP // n_st
    return pl.pallas_call(
        _gqa_kernel,
        grid=(A_KV_HEADS, per_g, s // tq),
        in_specs=[pl.BlockSpec((tq, n_st * HEAD_DIM), lambda g, hh, i: (i, COL_AQ // n_st + g * per_g + hh)),
                  pl.BlockSpec((t, HEAD_DIM), lambda g, hh, i: (0, COL_AK + g)),
                  pl.BlockSpec((t, HEAD_DIM), lambda g, hh, i: (0, COL_AV + g))],
        out_specs=pl.BlockSpec((tq, n_st * HEAD_DIM), lambda g, hh, i: (i, g * per_g + hh)),
        out_shape=jax.ShapeDtypeStruct((s, A_Q_W), BF16),
        scratch_shapes=[pltpu.VMEM((t // tk, HEAD_DIM + ONES_ROWS, tk), BF16),
                        pltpu.VMEM((n_st, 2, tk, tq), F32),
                        pltpu.VMEM((n_st, 1, tq), F32),
                        pltpu.VMEM((n_st, HEAD_DIM + ONES_ROWS, tq), F32)],
        compiler_params=_cparams(("arbitrary", "arbitrary", "arbitrary")),
        name="gqa_attn",
    )(proj, proj, proj)


def _diff_kernel(lam_ref, q0_ref, q1_ref, k0_ref, k1_ref, v_ref, g_ref, o_ref,
                 vt_ref, s_ref, m_ref, acc_ref, *, lam_init):
    tk = vt_ref.shape[2]
    pl.when(pl.program_id(1) == 0)(lambda: _build_vt(v_ref, vt_ref, tk))
    lv = lam_ref[...]
    lam = (jnp.exp(jnp.sum(lv[0:1, :] * lv[1:2, :], axis=-1, keepdims=True))
           - jnp.exp(jnp.sum(lv[2:3, :] * lv[3:4, :], axis=-1, keepdims=True)) + lam_init)
    _attend([q0_ref[...], q1_ref[...]], [k0_ref, k1_ref], vt_ref, s_ref, m_ref, acc_ref)
    o_t = (acc_ref[0, 0:B_V_DIM, :] / acc_ref[0, B_V_DIM:B_V_DIM + 1, :]
           - lam * (acc_ref[1, 0:B_V_DIM, :] / acc_ref[1, B_V_DIM:B_V_DIM + 1, :]))
    o = o_t.T
    y = o * lax.rsqrt(jnp.mean(o * o, axis=-1, keepdims=True) + EPS) * g_ref[...]
    o_ref[...] = (y * (1.0 - lam_init)).astype(o_ref.dtype)


def _diff(proj, s, lam_vecs, subln_g, lam_init):
    t = proj.shape[0]
    tq = 512
    tk = _key_chunk(t)
    vb = B_V_DIM // HEAD_DIM
    return pl.pallas_call(
        functools.partial(_diff_kernel, lam_init=lam_init),
        grid=(B_HEADS, s // tq),
        in_specs=[pl.BlockSpec((4, HEAD_DIM), lambda h, i: (0, 0)),
                  pl.BlockSpec((tq, HEAD_DIM), lambda h, i: (i, COL_BQ + 2 * h)),
                  pl.BlockSpec((tq, HEAD_DIM), lambda h, i: (i, COL_BQ + 2 * h + 1)),
                  pl.BlockSpec((t, HEAD_DIM), lambda h, i: (0, COL_BK + 2 * h)),
                  pl.BlockSpec((t, HEAD_DIM), lambda h, i: (0, COL_BK + 2 * h + 1)),
                  pl.BlockSpec((t, B_V_DIM), lambda h, i: (0, COL_BV // vb + h)),
                  pl.BlockSpec((1, B_V_DIM), lambda h, i: (0, 0))],
        out_specs=pl.BlockSpec((tq, B_V_DIM), lambda h, i: (i, h)),
        out_shape=jax.ShapeDtypeStruct((s, B_V_W), BF16),
        scratch_shapes=[pltpu.VMEM((t // tk, B_V_DIM + ONES_ROWS, tk), BF16),
                        pltpu.VMEM((2, 2, tk, tq), F32),
                        pltpu.VMEM((2, 1, tq), F32),
                        pltpu.VMEM((2, B_V_DIM + ONES_ROWS, tq), F32)],
        compiler_params=_cparams(("arbitrary", "arbitrary")),
        name="diff_attn",
    )(lam_vecs, proj, proj, proj, proj, proj, subln_g.reshape(1, B_V_DIM))


def _merge_kernel(oa_ref, ob_ref, ga_ref, gb_ref, wa_ref, wb_ref, wo_ref, x_ref, gate_ref, o_ref, t_ref):
    c = pl.program_id(1)
    nc, _, tn = t_ref.shape
    ya = jnp.dot(oa_ref[...], wa_ref[...], preferred_element_type=F32)
    yb = jnp.dot(ob_ref[...], wb_ref[...], preferred_element_type=F32)
    t = (jax.nn.sigmoid(ga_ref[...].astype(F32)) * ya + jax.nn.sigmoid(gb_ref[...].astype(F32)) * yb)
    t_ref[c] = t.astype(t_ref.dtype)

    @pl.when(c == nc - 1)
    def _():
        y = jnp.dot(t_ref[0], wo_ref[0:tn, :], preferred_element_type=F32)
        for cc in range(1, nc):
            y = y + jnp.dot(t_ref[cc], wo_ref[cc * tn:(cc + 1) * tn, :], preferred_element_type=F32)
        o_ref[...] = x_ref[...] + gate_ref[...] * y


def _merge(oa, ob, proj, wa, wb, wo, x, gate):
    s, d = x.shape
    tm = 512
    tn = PROJ_TN
    nc = d // tn
    ga0 = COL_GA * HEAD_DIM // tn
    gb0 = COL_GB * HEAD_DIM // tn
    return pl.pallas_call(
        _merge_kernel,
        grid=(s // tm, nc),
        in_specs=[pl.BlockSpec((tm, A_Q_W), lambda i, c: (i, 0)),
                  pl.BlockSpec((tm, B_V_W), lambda i, c: (i, 0)),
                  pl.BlockSpec((tm, tn), lambda i, c: (i, ga0 + c)),
                  pl.BlockSpec((tm, tn), lambda i, c: (i, gb0 + c)),
                  pl.BlockSpec((A_Q_W, tn), lambda i, c: (0, c)),
                  pl.BlockSpec((B_V_W, tn), lambda i, c: (0, c)),
                  pl.BlockSpec((d, d), lambda i, c: (0, 0)),
                  pl.BlockSpec((tm, d), lambda i, c: (i, 0)),
                  pl.BlockSpec((1, d), lambda i, c: (0, 0))],
        out_specs=pl.BlockSpec((tm, d), lambda i, c: (i, 0)),
        out_shape=jax.ShapeDtypeStruct((s, d), F32),
        scratch_shapes=[pltpu.VMEM((nc, tm, tn), BF16)],
        compiler_params=_cparams(("arbitrary", "arbitrary")),
        name="merge_out",
    )(oa, ob, proj, proj, wa, wb, wo, x, gate)


def _store_packed_tiles(ref, x):
    rows = x.shape[0]
    bits = lax.bitcast_convert_type(x.astype(BF16).astype(F32), jnp.uint32)
    half = bits.shape[1] // 2
    words = (bits[:, :half] >> 16) | (bits[:, half:] & jnp.uint32(0xFFFF0000))
    for c in range(PACK_TILES):
        ref[pl.ds(c, rows, stride=PACK_TILES), :] = words[:, c * LANES:(c + 1) * LANES]


def _lane_max(x):
    return jnp.max(x, axis=-1, keepdims=True)


def _lane_min(x):
    return jnp.min(x, axis=-1, keepdims=True)


def _group_allreduce(x, lane, op):
    for sft in (1, 2, 4):
        up = pltpu.roll(x, sft, 1)
        dn = pltpu.roll(x, LANES - sft, 1)
        x = op(x, jnp.where((lane & sft) != 0, up, dn))
    return x


def _router_kernel(x_ref, g_ref, mod_ref, wr_ref, rb_ref,
                   h_ref, hp_ref, idx_ref, wsel_ref, rank_ref, cnt_ref, carry_ref):
    @pl.when(pl.program_id(0) == 0)
    def _():
        carry_ref[...] = jnp.zeros_like(carry_ref)

    h = _rms_mod(x_ref[...], g_ref[...], mod_ref[0, 3:4, :], mod_ref[0, 4:5, :])
    h_ref[...] = h.astype(BF16)
    _store_packed_tiles(hp_ref, h)
    tm = h.shape[0]

    logits = jnp.dot(h, wr_ref[...], preferred_element_type=F32, precision=lax.Precision.HIGHEST)
    scores = jax.nn.sigmoid(logits)
    lane = lax.broadcasted_iota(jnp.int32, (tm, LANES), 1)
    valid = lane < N_EXPERTS
    neg = jnp.float32(-jnp.inf)
    big = jnp.int32(LANES)
    biased = jnp.where(valid, scores + rb_ref[...], neg)

    m1 = _group_allreduce(biased, lane, jnp.maximum)
    a1 = _group_allreduce(jnp.where(biased == m1, lane, big), lane, jnp.minimum)
    m2 = _group_allreduce(jnp.where(lane == a1, neg, biased), lane, jnp.maximum)
    gscore = jnp.where(valid, m1 + m2, neg)
    gid = lane >> 3
    keep = jnp.zeros((tm, LANES), jnp.bool_)
    for _ in range(TOPK_GROUPS):
        best = _lane_max(gscore)
        gsel = _lane_min(jnp.where(gscore == best, gid, big))
        hit = gid == gsel
        keep = keep | hit
        gscore = jnp.where(hit, neg, gscore)
    cand = jnp.where(keep & valid, biased, neg)

    onehot = jnp.zeros((tm, LANES), F32)
    idx_out = jnp.zeros((tm, LANES), jnp.int32)
    w_out = jnp.zeros((tm, LANES), F32)
    sels = []
    for k in range(TOP_K):
        best = _lane_max(cand)
        sel = _lane_min(jnp.where(cand == best, lane, big))
        hit = lane == sel
        wk = jnp.sum(jnp.where(hit, scores, 0.0), axis=-1, keepdims=True)
        cand = jnp.where(hit, neg, cand)
        onehot = jnp.where(hit, 1.0, onehot)
        idx_out = jnp.where(lane == k, sel, idx_out)
        w_out = jnp.where(lane == k, wk, w_out)
        sels.append(hit)
    wsum = jnp.sum(w_out, axis=-1, keepdims=True)
    wsel_ref[...] = w_out / wsum * ROUTED_SCALE
    idx_ref[...] = idx_out

    row = lax.broadcasted_iota(jnp.int32, (tm, tm), 0)
    col = lax.broadcasted_iota(jnp.int32, (tm, tm), 1)
    lower = (col < row).astype(BF16)
    before = jnp.dot(lower, onehot.astype(BF16), preferred_element_type=F32) + carry_ref[...]
    rank_out = jnp.zeros((tm, LANES), F32)
    for k in range(TOP_K):
        rk = jnp.sum(jnp.where(sels[k], before, 0.0), axis=-1, keepdims=True)
        rank_out = jnp.where(lane == k, rk, rank_out)
    rank_ref[...] = rank_out.astype(jnp.int32)
    carry_ref[...] = carry_ref[...] + jnp.sum(onehot, axis=0, keepdims=True)
    cnt_ref[...] = carry_ref[...]


def _router(x1, g, mod, w_router, router_bias):
    s, d = x1.shape
    tm = 256
    e = w_router.shape[1]
    wr = jnp.pad(w_router, ((0, 0), (0, LANES - e)))
    rb = jnp.pad(router_bias, (0, LANES - e)).reshape(1, LANES)
    row_spec = lambda w: pl.BlockSpec((tm, w), lambda i: (i, 0))
    return pl.pallas_call(
        _router_kernel,
        grid=(s // tm,),
        in_specs=[row_spec(d),
                  pl.BlockSpec((1, d), lambda i: (0, 0)),
                  pl.BlockSpec((1, N_MOD, d), lambda i: (0, 0, 0)),
                  pl.BlockSpec((d, LANES), lambda i: (0, 0)),
                  pl.BlockSpec((1, LANES), lambda i: (0, 0))],
        out_specs=[row_spec(d), pl.BlockSpec((tm * PACK_TILES, LANES), lambda i: (i, 0)),
                   row_spec(LANES), row_spec(LANES), row_spec(LANES),
                   pl.BlockSpec((1, LANES), lambda i: (0, 0))],
        out_shape=[jax.ShapeDtypeStruct((s, d), BF16),
                   jax.ShapeDtypeStruct((s * PACK_TILES, LANES), jnp.uint32),
                   jax.ShapeDtypeStruct((s, LANES), jnp.int32),
                   jax.ShapeDtypeStruct((s, LANES), F32),
                   jax.ShapeDtypeStruct((s, LANES), jnp.int32),
                   jax.ShapeDtypeStruct((1, LANES), F32)],
        scratch_shapes=[pltpu.VMEM((1, LANES), F32)],
        compiler_params=_cparams(("arbitrary",)),
        name="ffn_router",
    )(x1, g.reshape(1, d), mod, wr, rb)


def _unpack_halves(words):
    lo = lax.bitcast_convert_type(words << 16, F32)
    hi = lax.bitcast_convert_type(words & jnp.uint32(0xFFFF0000), F32)
    return lo, hi


def _unpack_rows(words):
    lo, hi = _unpack_halves(words)
    return jnp.concatenate([lo.astype(BF16), hi.astype(BF16)], axis=1)


def _token_rows(row, tiles):
    return pl.ds(pl.multiple_of(row * tiles, tiles), tiles)


def _dispatch_kernel(ps_ref, pe_ref, nused_ref, dest_hbm, hp_ref, xs_hbm,
                     dest_smem, zbuf, sem_i, sem_z, sem_x, *, n_blocks):
    i = pl.program_id(0)
    n = dest_smem.shape[0]
    tm = n // TOP_K
    copy_dest = pltpu.make_async_copy(dest_hbm.at[i], dest_smem, sem_i)
    copy_dest.start()

    @pl.when(i == 0)
    def _():
        zbuf[...] = jnp.zeros_like(zbuf)

        def zero_block(row0):
            span = MOE_BLOCK * PACK_TILES
            return pltpu.make_async_copy(
                zbuf, xs_hbm.at[pl.ds(pl.multiple_of(row0 * PACK_TILES, span), span), :], sem_z)

        def per_expert(action):
            def body(e, carry):
                @pl.when(pe_ref[e] > ps_ref[e])
                def _():
                    action(zero_block(pe_ref[e] - MOE_BLOCK))
                return carry
            lax.fori_loop(0, N_EXPERTS, body, 0)

        def per_tail(action):
            def body(b, carry):
                action(zero_block(b * MOE_BLOCK))
                return carry
            lax.fori_loop(nused_ref[0], n_blocks, body, 0)

        per_expert(lambda cp: cp.start())
        per_tail(lambda cp: cp.start())
        per_expert(lambda cp: cp.wait())
        per_tail(lambda cp: cp.wait())

    copy_dest.wait()

    def issue(t, carry):
        for k in range(TOP_K):
            dest = dest_smem[t * TOP_K + k]
            pltpu.make_async_copy(hp_ref.at[_token_rows(t, PACK_TILES), :],
                                  xs_hbm.at[_token_rows(dest, PACK_TILES), :], sem_x).start(priority=k % 2)
        return carry

    lax.fori_loop(0, tm, issue, 0)
    for _ in range(TOP_K):
        pltpu.make_async_copy(hp_ref, xs_hbm.at[pl.ds(0, tm * PACK_TILES), :], sem_x).wait()


def _dispatch(pad_start, pad_end, n_used, dest_tm, hp, n_blocks):
    n_tiles, n = dest_tm.shape
    tm = n // TOP_K
    grid_spec = pltpu.PrefetchScalarGridSpec(
        num_scalar_prefetch=3,
        grid=(n_tiles,),
        in_specs=[pl.BlockSpec(memory_space=pl.ANY),
                  pl.BlockSpec((tm * PACK_TILES, LANES), lambda i, ps, pe, nu: (i, 0))],
        out_specs=pl.BlockSpec(memory_space=pl.ANY),
        scratch_shapes=[pltpu.SMEM((n,), jnp.int32),
                        pltpu.VMEM((MOE_BLOCK * PACK_TILES, LANES), jnp.uint32),
                        pltpu.SemaphoreType.DMA,
                        pltpu.SemaphoreType.DMA,
                        pltpu.SemaphoreType.DMA])
    return pl.pallas_call(
        functools.partial(_dispatch_kernel, n_blocks=n_blocks),
        grid_spec=grid_spec,
        out_shape=jax.ShapeDtypeStruct((n_blocks * MOE_BLOCK * PACK_TILES, LANES), jnp.uint32),
        compiler_params=_cparams(("arbitrary",)),
        name="moe_dispatch",
    )(pad_start, pad_end, n_used, dest_tm, hp)


def _moe_kernel(be_ref, first_ref, nxt_ref, slot_ref, nused_ref, x_ref, wg_hbm, wu_hbm, wd_hbm, o_ref,
                wg32, wu32, wd32, wgb, wub, wdb, sems):
    b = pl.program_id(0)

    def fetch(e, slot):
        return (pltpu.make_async_copy(wg_hbm.at[e], wg32.at[slot], sems.at[slot, 0]),
                pltpu.make_async_copy(wu_hbm.at[e], wu32.at[slot], sems.at[slot, 1]),
                pltpu.make_async_copy(wd_hbm.at[e], wd32.at[slot], sems.at[slot, 2]))

    @pl.when(b < nused_ref[0])
    def _():
        @pl.when(first_ref[b] == 1)
        def _():
            slot = slot_ref[b]

            @pl.when(b == 0)
            def _():
                for cp in fetch(be_ref[0], 0):
                    cp.start()

            for cp in fetch(be_ref[b], slot):
                cp.wait()

            @pl.when(nxt_ref[b] >= 0)
            def _():
                for cp in fetch(nxt_ref[b], 1 - slot):
                    cp.start()

            wgb[...] = wg32[slot].astype(BF16)
            wub[...] = wu32[slot].astype(BF16)
            wdb[...] = wd32[slot].astype(BF16)

        words = jnp.concatenate(
            [x_ref[pl.ds(c, MOE_BLOCK, stride=PACK_TILES), :] for c in range(PACK_TILES)], axis=1)
        x = _unpack_rows(words)
        gate = jnp.dot(x, wgb[...], preferred_element_type=F32)
        up = jnp.dot(x, wub[...], preferred_element_type=F32)
        act = (gate * jax.nn.sigmoid(gate) * up).astype(BF16)
        _store_packed_tiles(o_ref, jnp.dot(act, wdb[...], preferred_element_type=F32))

    @pl.when(b >= nused_ref[0])
    def _():
        o_ref[...] = jnp.zeros_like(o_ref)


def _moe(block_e, first, nxt, slot, n_used, xs, weg, weu, wed):
    n_blocks = block_e.shape[0]
    _, d, f = weg.shape
    grid_spec = pltpu.PrefetchScalarGridSpec(
        num_scalar_prefetch=5,
        grid=(n_blocks,),
        in_specs=[pl.BlockSpec((MOE_BLOCK * PACK_TILES, LANES), lambda b, *_: (b, 0)),
                  pl.BlockSpec(memory_space=pl.ANY),
                  pl.BlockSpec(memory_space=pl.ANY),
                  pl.BlockSpec(memory_space=pl.ANY)],
        out_specs=pl.BlockSpec((MOE_BLOCK * PACK_TILES, LANES), lambda b, *_: (b, 0)),
        scratch_shapes=[pltpu.VMEM((2, d, f), F32), pltpu.VMEM((2, d, f), F32), pltpu.VMEM((2, f, d), F32),
                        pltpu.VMEM((d, f), BF16), pltpu.VMEM((d, f), BF16), pltpu.VMEM((f, d), BF16),
                        pltpu.SemaphoreType.DMA((2, 3))])
    return pl.pallas_call(
        _moe_kernel,
        grid_spec=grid_spec,
        out_shape=jax.ShapeDtypeStruct((n_blocks * MOE_BLOCK * PACK_TILES, LANES), jnp.uint32),
        compiler_params=_cparams(("arbitrary",)),
        name="moe_experts",
    )(block_e, first, nxt, slot, n_used, xs, weg, weu, wed)


def _combine_kernel(dest_hbm, y_hbm, w_ref, h_ref, wg_ref, wu_ref, wd_ref, x_ref, gate_ref, o_ref,
                    dest0, dest1, ybuf0, ybuf1, sem_i, sem_y):
    i = pl.program_id(0)
    tm = h_ref.shape[0]
    n = tm * TOP_K
    span = n * PACK_TILES
    dests = (dest0, dest1)
    ybufs = (ybuf0, ybuf1)

    def start_tile(tile, slot):
        copy_dest = pltpu.make_async_copy(dest_hbm.at[tile], dests[slot], sem_i)
        copy_dest.start()
        copy_dest.wait()

        def issue(t, carry):
            base = t * TOKEN_PITCH
            for k in range(TOP_K):
                src = dests[slot][t * TOP_K + k]
                pltpu.make_async_copy(
                    y_hbm.at[_token_rows(src, PACK_TILES), :],
                    ybufs[slot].at[pl.ds(pl.multiple_of(base + k * YBUF_PITCH, 4), PACK_TILES), :],
                    sem_y.at[slot]).start(priority=k % 2)
            return carry

        lax.fori_loop(0, tm, issue, 0)

    def step(cur):
        nxt = 1 - cur
        pl.when(i == 0)(lambda: start_tile(0, cur))
        pl.when(i + 1 < pl.num_programs(0))(lambda: start_tile(i + 1, nxt))

        h = h_ref[...]
        gate = jnp.dot(h, wg_ref[...], preferred_element_type=F32)
        up = jnp.dot(h, wu_ref[...], preferred_element_type=F32)
        act = (gate * jax.nn.sigmoid(gate) * up).astype(BF16)
        shared = jnp.dot(act, wd_ref[...], preferred_element_type=F32)

        yb = ybufs[cur]
        pltpu.make_async_copy(y_hbm.at[pl.ds(0, span), :], yb.at[pl.ds(0, span), :], sem_y.at[cur]).wait()
        lows, highs = [], []
        for c in range(PACK_TILES):
            acc_lo = acc_hi = None
            for k in range(TOP_K):
                lo, hi = _unpack_halves(yb[pl.ds(k * YBUF_PITCH + c, tm, stride=TOKEN_PITCH), :])
                wk = w_ref[:, k:k + 1]
                acc_lo = lo * wk if acc_lo is None else acc_lo + lo * wk
                acc_hi = hi * wk if acc_hi is None else acc_hi + hi * wk
            lows.append(acc_lo)
            highs.append(acc_hi)
        routed = jnp.concatenate(lows + highs, axis=1)
        o_ref[...] = x_ref[...] + gate_ref[...] * (routed + shared)

    pl.when(i % 2 == 0)(lambda: step(0))
    pl.when(i % 2 == 1)(lambda: step(1))


def _combine(dest_tm, y_sorted, wsel, h2, wsg, wsu, wsd, x1, gate):
    s, d = x1.shape
    n_tiles, n = dest_tm.shape
    tm = n // TOP_K
    f = wsg.shape[1]
    return pl.pallas_call(
        _combine_kernel,
        grid=(n_tiles,),
        in_specs=[pl.BlockSpec(memory_space=pl.ANY),
                  pl.BlockSpec(memory_space=pl.ANY),
                  pl.BlockSpec((tm, LANES), lambda i: (i, 0)),
                  pl.BlockSpec((tm, d), lambda i: (i, 0)),
                  pl.BlockSpec((d, f), lambda i: (0, 0)),
                  pl.BlockSpec((d, f), lambda i: (0, 0)),
                  pl.BlockSpec((f, d), lambda i: (0, 0)),
                  pl.BlockSpec((tm, d), lambda i: (i, 0)),
                  pl.BlockSpec((1, d), lambda i: (0, 0))],
        out_specs=pl.BlockSpec((tm, d), lambda i: (i, 0)),
        out_shape=jax.ShapeDtypeStruct((s, d), F32),
        scratch_shapes=[pltpu.SMEM((n,), jnp.int32),
                        pltpu.SMEM((n,), jnp.int32),
                        pltpu.VMEM((tm * TOKEN_PITCH, LANES), jnp.uint32),
                        pltpu.VMEM((tm * TOKEN_PITCH, LANES), jnp.uint32),
                        pltpu.SemaphoreType.DMA,
                        pltpu.SemaphoreType.DMA((2,))],
        compiler_params=_cparams(("arbitrary",)),
        name="moe_combine",
    )(dest_tm, y_sorted, wsel, h2, wsg, wsu, wsd, x1, gate)


def _block_tables(counts, n_blocks):
    padded = (counts + MOE_BLOCK - 1) // MOE_BLOCK * MOE_BLOCK
    pad_end = jnp.cumsum(padded).astype(jnp.int32)
    pad_start = pad_end - padded
    blk = jnp.arange(n_blocks, dtype=jnp.int32)
    block_e = jnp.minimum(jnp.sum(pad_end[None, :] <= (blk * MOE_BLOCK)[:, None], axis=1),
                          N_EXPERTS - 1).astype(jnp.int32)
    n_used = pad_end[-1] // MOE_BLOCK
    prev = jnp.concatenate([jnp.full((1,), -1, jnp.int32), block_e[:-1]])
    first = ((blk < n_used) & (block_e != prev)).astype(jnp.int32)
    slot = jnp.maximum(jnp.cumsum(first) - 1, 0).astype(jnp.int32) % 2
    after = pad_end[block_e] // MOE_BLOCK
    nxt = jnp.where(after < n_used, block_e[jnp.minimum(after, n_blocks - 1)], -1).astype(jnp.int32)
    return pad_start, pad_end, block_e, first, nxt, slot, n_used.reshape(1)


def kernel(x, c, ctx, c_ctx, w_ada, b_ada, norm_mix, norm_ffn, w_in, q_norm_a, k_norm_a, q_norm_b, k_norm_b, lambda_q1, lambda_k1, lambda_q2, lambda_k2, subln_b, w_branch_a, w_branch_b, w_out, w_router, router_bias, w_exp_gate, w_exp_up, w_exp_down, w_sh_gate, w_sh_up, w_sh_down):
    depth = w_ada.shape[0]
    assert depth == 1 and x.shape[0] == 1 and ctx.shape[0] == 1
    s, d = x.shape[1], x.shape[2]
    n_ctx = ctx.shape[1]
    i = 0
    lam_init = 0.8 - 0.6 * math.exp(-0.3 * i)
    xs = x[0]

    mod = _adaln(jnp.concatenate([c, c_ctx[None, :]], axis=0), w_ada[i], b_ada[i]).reshape(2, N_MOD, d)

    h = _prenorm(xs, ctx[0], norm_mix[i], mod)
    tc, tsa, tsb = _rope_tables(s, n_ctx)
    gains = _head_gains(q_norm_a[i], k_norm_a[i], q_norm_b[i], k_norm_b[i])
    proj = _inproj(h, w_in[i].astype(BF16), gains, tc, tsa, tsb)
    oa = _gqa(proj, s)
    lam_vecs = jnp.stack([lambda_q1[i], lambda_k1[i], lambda_q2[i], lambda_k2[i]]).astype(F32)
    ob = _diff(proj, s, lam_vecs, subln_b[i], lam_init)
    x1 = _merge(oa, ob, proj, w_branch_a[i].astype(BF16), w_branch_b[i].astype(BF16),
                w_out[i].astype(BF16), xs, mod[0, 2:3, :])

    h2, h2p, idx, wsel, rank, cnt = _router(x1, norm_ffn[i], mod[0:1], w_router[i], router_bias[i])
    counts = cnt[0, :N_EXPERTS].astype(jnp.int32)
    n_blocks = -(-(s * TOP_K) // MOE_BLOCK) + N_EXPERTS
    pad_start, pad_end, block_e, first, nxt, slot, n_used = _block_tables(counts, n_blocks)
    tm_r = 128
    e_ids = jnp.arange(N_EXPERTS, dtype=jnp.int32)
    starts = jnp.sum(jnp.where(idx[:, :TOP_K, None] == e_ids, pad_start, 0), axis=-1)
    dest_tm = (starts + rank[:, :TOP_K]).astype(jnp.int32).reshape(s // tm_r, tm_r * TOP_K)
    xs = _dispatch(pad_start, pad_end, n_used, dest_tm, h2p, n_blocks)
    y_sorted = _moe(block_e, first, nxt, slot, n_used, xs, w_exp_gate[i], w_exp_up[i], w_exp_down[i])
    out = _combine(dest_tm, y_sorted, wsel, h2,
                   w_sh_gate[i].astype(BF16), w_sh_up[i].astype(BF16), w_sh_down[i].astype(BF16),
                   x1, mod[0, 5:6, :])
    return out[None]
```

```python
import functools
import math

import jax
import jax.numpy as jnp
from jax import lax
from jax.experimental import pallas as pl
from jax.experimental.pallas import tpu as pltpu

F32 = jnp.float32
BF16 = jnp.bfloat16

D_MODEL = 2048
GRID_W = 64
HEAD_DIM = 128
ROPE_PAIRS = HEAD_DIM // 4
ROPE_THETA = 10000.0
A_HEADS = 8
A_KV_HEADS = 2
A_GROUP = A_HEADS // A_KV_HEADS
B_HEADS = 4
B_V_DIM = 2 * HEAD_DIM
N_EXPERTS = 64
TOP_K = 8
N_GROUPS = 8
TOPK_GROUPS = 4
EXPERT_DIM = 512
SHARED_DIM = 512
ROUTED_SCALE = 2.5
N_MOD = 6
EPS = 1e-6

A_Q_W = A_HEADS * HEAD_DIM
A_KV_W = A_KV_HEADS * HEAD_DIM
B_QK_W = B_HEADS * 2 * HEAD_DIM
B_V_W = B_HEADS * B_V_DIM
IN_W = A_Q_W + 2 * A_KV_W + 2 * B_QK_W + B_V_W + 2 * D_MODEL

COL_AQ = 0
COL_AK = A_Q_W // HEAD_DIM
COL_AV = COL_AK + A_KV_HEADS
COL_BQ = COL_AV + A_KV_HEADS
COL_BK = COL_BQ + 2 * B_HEADS
COL_BV = COL_BK + 2 * B_HEADS
COL_GA = COL_BV + B_V_W // HEAD_DIM
COL_GB = COL_GA + D_MODEL // HEAD_DIM

LANES = 128
SUBLANES = 8
VMEM_LIMIT = 56 * 1024 * 1024

PROJ_TN = 512
MOE_BLOCK = 256
ONES_ROWS = 16
PACK_TILES = D_MODEL // 2 // LANES
YBUF_PITCH = PACK_TILES + 4
TOKEN_PITCH = TOP_K * YBUF_PITCH + 4
LOG2E = 1.4426950408889634


def _cparams(sem, vmem=VMEM_LIMIT):
    return pltpu.CompilerParams(dimension_semantics=sem, vmem_limit_bytes=vmem)


def _adaln_kernel(cb_ref, w_ref, b_ref, o_ref):
    tn = w_ref.shape[1]
    nl = tn // LANES
    rows = 32

    def body(g, accs):
        accs = list(accs)
        r0 = pl.multiple_of(g * rows, rows)
        for u in range(rows // SUBLANES):
            r = r0 + u * SUBLANES
            w = w_ref[pl.ds(r, SUBLANES), :]
            for v in range(2):
                c = cb_ref[v, pl.ds(r, SUBLANES), :]
                s = c * jax.nn.sigmoid(c)
                for j in range(nl):
                    accs[v * nl + j] = accs[v * nl + j] + w[:, j * LANES:(j + 1) * LANES] * s
        return tuple(accs)

    init = tuple(jnp.zeros((SUBLANES, LANES), F32) for _ in range(2 * nl))
    accs = lax.fori_loop(0, w_ref.shape[0] // rows, body, init)
    for v in range(2):
        row = jnp.concatenate(
            [jnp.sum(accs[v * nl + j], axis=0, keepdims=True) for j in range(nl)], axis=1)
        o_ref[v:v + 1, :] = row + b_ref[...]


def _adaln(cvecs, w, b):
    d, n = w.shape
    tn = 1536
    cb = jnp.broadcast_to(cvecs[:, :, None], (2, d, LANES))
    return pl.pallas_call(
        _adaln_kernel,
        grid=(n // tn,),
        in_specs=[pl.BlockSpec((2, d, LANES), lambda j: (0, 0, 0)),
                  pl.BlockSpec((d, tn), lambda j: (0, j)),
                  pl.BlockSpec((1, tn), lambda j: (0, j))],
        out_specs=pl.BlockSpec((2, tn), lambda j: (0, j)),
        out_shape=jax.ShapeDtypeStruct((2, n), F32),
        compiler_params=_cparams(("arbitrary",)),
        name="adaln",
    )(cb, w, b.reshape(1, n))


def _rms_mod(x, g, shift, scale):
    y = x * lax.rsqrt(jnp.mean(x * x, axis=-1, keepdims=True) + EPS) * g
    return y * (1.0 + scale) + shift


def _prenorm_kernel(x_ref, c_ref, g_ref, mod_ref, o_ref, *, n_lat_tiles):
    is_ctx = pl.program_id(0) >= n_lat_tiles
    x = jnp.where(is_ctx, c_ref[...], x_ref[...])
    o_ref[...] = _rms_mod(x, g_ref[...], mod_ref[0, 0:1, :], mod_ref[0, 1:2, :]).astype(o_ref.dtype)


def _prenorm(x, ctx, g, mod):
    s, d = x.shape
    c = ctx.shape[0]
    tm = 256
    nl, nc = s // tm, c // tm
    return pl.pallas_call(
        functools.partial(_prenorm_kernel, n_lat_tiles=nl),
        grid=(nl + nc,),
        in_specs=[pl.BlockSpec((tm, d), lambda i: (jnp.minimum(i, nl - 1), 0)),
                  pl.BlockSpec((tm, d), lambda i: (jnp.maximum(i - nl, 0), 0)),
                  pl.BlockSpec((1, d), lambda i: (0, 0)),
                  pl.BlockSpec((1, N_MOD, d), lambda i: (i // nl, 0, 0))],
        out_specs=pl.BlockSpec((tm, d), lambda i: (i, 0)),
        out_shape=jax.ShapeDtypeStruct((s + c, d), BF16),
        compiler_params=_cparams(("arbitrary",)),
        name="prenorm_mix",
    )(x, ctx, g.reshape(1, d), mod)


def _inproj_kernel(h_ref, w_ref, gain_ref, c_ref, sa_ref, sb_ref, o_ref):
    j = pl.program_id(1)
    acc = jnp.dot(h_ref[...], w_ref[...].astype(h_ref.dtype), preferred_element_type=F32)
    nh = acc.shape[1] // HEAD_DIM

    def norm_rope(a, gain):
        y = a * lax.rsqrt(jnp.mean(a * a, axis=-1, keepdims=True) + EPS) * gain
        return (y * c_ref[...] + pltpu.roll(y, ROPE_PAIRS, 1) * sa_ref[...]
                + pltpu.roll(y, HEAD_DIM - ROPE_PAIRS, 1) * sb_ref[...])

    def store(n_normed):
        for hd in range(nh):
            sl = slice(hd * HEAD_DIM, (hd + 1) * HEAD_DIM)
            a = acc[:, sl]
            if hd < n_normed:
                a = norm_rope(a, gain_ref[0, :, sl])
            o_ref[:, sl] = a.astype(o_ref.dtype)

    all_normed = (j < 2) | ((j >= 3) & (j < 7))
    pl.when(all_normed)(lambda: store(nh))
    pl.when(j == 2)(lambda: store(A_KV_HEADS))
    pl.when(j >= 7)(lambda: store(0))


def _inproj(h, w, gains, rope_c, rope_sa, rope_sb):
    t, d = h.shape
    n = w.shape[1]
    tm = t // 8
    tn = PROJ_TN
    return pl.pallas_call(
        _inproj_kernel,
        grid=(t // tm, n // tn),
        in_specs=[pl.BlockSpec((tm, d), lambda i, j: (i, 0)),
                  pl.BlockSpec((d, tn), lambda i, j: (0, j)),
                  pl.BlockSpec((1, 1, tn), lambda i, j: (j, 0, 0)),
                  pl.BlockSpec((tm, HEAD_DIM), lambda i, j: (i, 0)),
                  pl.BlockSpec((tm, HEAD_DIM), lambda i, j: (i, 0)),
                  pl.BlockSpec((tm, HEAD_DIM), lambda i, j: (i, 0))],
        out_specs=pl.BlockSpec((tm, tn), lambda i, j: (i, j)),
        out_shape=jax.ShapeDtypeStruct((t, n), BF16),
        compiler_params=_cparams(("arbitrary", "arbitrary")),
        name="inproj",
    )(h, w, gains, rope_c, rope_sa, rope_sb)


def _rope_tables(s, c):
    rows_n = s // GRID_W
    inv = ROPE_THETA ** (-jnp.arange(ROPE_PAIRS, dtype=F32) / ROPE_PAIRS)
    ang_r = jnp.arange(rows_n, dtype=F32)[:, None] * inv
    ang_c = jnp.arange(GRID_W, dtype=F32)[:, None] * inv
    per_row = lambda a: jnp.repeat(a, GRID_W, axis=0)
    per_col = lambda a: jnp.tile(a, (rows_n, 1))
    cr, sr = per_row(jnp.cos(ang_r)), per_row(jnp.sin(ang_r))
    cc, sc = per_col(jnp.cos(ang_c)), per_col(jnp.sin(ang_c))
    z = jnp.zeros_like(sr)
    tc = jnp.concatenate([cr, cr, cc, cc], axis=1)
    tsa = jnp.concatenate([z, sr, z, sc], axis=1)
    tsb = jnp.concatenate([-sr, z, -sc, z], axis=1)
    pad = lambda a, v: jnp.concatenate([a, jnp.full((c, HEAD_DIM), v, F32)], axis=0)
    return pad(tc, 1.0), pad(tsa, 0.0), pad(tsb, 0.0)


def _head_gains(qn_a, kn_a, qn_b, kn_b):
    qs = HEAD_DIM ** -0.5 * LOG2E
    one = jnp.ones((HEAD_DIM,), F32)
    heads = ([qn_a * qs] * A_HEADS + [kn_a] * A_KV_HEADS + [one] * A_KV_HEADS
             + [qn_b * qs] * (2 * B_HEADS) + [kn_b] * (2 * B_HEADS))
    heads = heads + [one] * (IN_W // HEAD_DIM - len(heads))
    return jnp.concatenate(heads).reshape(IN_W // PROJ_TN, 1, PROJ_TN)


def _build_vt(v_ref, vt_ref, tk):
    n_chunks, rows, _ = vt_ref.shape
    dv = rows - ONES_ROWS
    tail = (lax.broadcasted_iota(jnp.int32, (ONES_ROWS, tk), 0) == 0).astype(vt_ref.dtype)
    for c in range(n_chunks):
        vt_ref[c, 0:dv, :] = v_ref[c * tk:(c + 1) * tk, :].astype(F32).T.astype(vt_ref.dtype)
        vt_ref[c, dv:rows, :] = tail


def _attend(qs, k_refs, vt_ref, s_ref, m_ref, acc_ref):
    n_chunks, _, tk = vt_ref.shape
    n_st = len(qs)
    m_ref[...] = jnp.full(m_ref.shape, -jnp.inf, F32)
    acc_ref[...] = jnp.zeros(acc_ref.shape, F32)

    def scores(i, c, slot):
        off = c * tk if isinstance(c, int) else pl.multiple_of(c * tk, tk)
        s_ref[i, slot] = lax.dot_general(k_refs[i][pl.ds(off, tk), :], qs[i], (((1,), (1,)), ((), ())),
                                         preferred_element_type=F32)

    def update(i, c, slot):
        s = s_ref[i, slot]
        m_old = m_ref[i]
        m_new = jnp.maximum(m_old, jnp.max(s, axis=0, keepdims=True))
        p = jnp.exp2(s - m_new).astype(vt_ref.dtype)
        acc_ref[i] = (acc_ref[i] * jnp.exp2(m_old - m_new)
                      + jnp.dot(vt_ref[c], p, preferred_element_type=F32))
        m_ref[i] = m_new

    for i in range(n_st):
        scores(i, 0, 0)

    def pair(j, carry):
        c = 2 * j
        for i in range(n_st):
            scores(i, c + 1, 1)
        for i in range(n_st):
            update(i, c, 0)
        for i in range(n_st):
            scores(i, c + 2, 0)
        for i in range(n_st):
            update(i, c + 1, 1)
        return carry

    n_pairs = (n_chunks - 1) // 2
    lax.fori_loop(0, n_pairs, pair, 0)
    done = 2 * n_pairs
    if n_chunks - done == 2:
        for i in range(n_st):
            scores(i, done + 1, 1)
    for i in range(n_st):
        update(i, done, 0)
    if n_chunks - done == 2:
        for i in range(n_st):
            update(i, done + 1, 1)


def _gqa_kernel(q_ref, k_ref, v_ref, o_ref, vt_ref, s_ref, m_ref, acc_ref):
    tk = vt_ref.shape[2]
    n_st = s_ref.shape[0]
    pl.when((pl.program_id(1) == 0) & (pl.program_id(2) == 0))(lambda: _build_vt(v_ref, vt_ref, tk))
    qs = [q_ref[:, i * HEAD_DIM:(i + 1) * HEAD_DIM] for i in range(n_st)]
    _attend(qs, [k_ref] * n_st, vt_ref, s_ref, m_ref, acc_ref)
    for i in range(n_st):
        o_t = acc_ref[i, 0:HEAD_DIM, :] / acc_ref[i, HEAD_DIM:HEAD_DIM + 1, :]
        o_ref[:, i * HEAD_DIM:(i + 1) * HEAD_DIM] = o_t.T.astype(o_ref.dtype)


def _key_chunk(t):
    for tk in (1408, 768, 1024, 512, 640, 384, 256, 128):
        if t % tk == 0:
            return tk
    raise ValueError(f"unsupported key count {t}")


def _gqa(proj, s):
    t = proj.shape[0]
    tq = 512
    tk = _key_chunk(t)
    n_st = 2
    per_g = A_GROUP // n_st
    return pl.pallas_call(
        _gqa_kernel,
        grid=(A_KV_HEADS, per_g, s // tq),
        in_specs=[pl.BlockSpec((tq, n_st * HEAD_DIM), lambda g, hh, i: (i, COL_AQ // n_st + g * per_g + hh)),
                  pl.BlockSpec((t, HEAD_DIM), lambda g, hh, i: (0, COL_AK + g)),
                  pl.BlockSpec((t, HEAD_DIM), lambda g, hh, i: (0, COL_AV + g))],
        out_specs=pl.BlockSpec((tq, n_st * HEAD_DIM), lambda g, hh, i: (i, g * per_g + hh)),
        out_shape=jax.ShapeDtypeStruct((s, A_Q_W), BF16),
        scratch_shapes=[pltpu.VMEM((t // tk, HEAD_DIM + ONES_ROWS, tk), BF16),
                        pltpu.VMEM((n_st, 2, tk, tq), F32),
                        pltpu.VMEM((n_st, 1, tq), F32),
                        pltpu.VMEM((n_st, HEAD_DIM + ONES_ROWS, tq), F32)],
        compiler_params=_cparams(("arbitrary", "arbitrary", "arbitrary")),
        name="gqa_attn",
    )(proj, proj, proj)


def _diff_kernel(lam_ref, q0_ref, q1_ref, k0_ref, k1_ref, v_ref, g_ref, o_ref,
                 vt_ref, s_ref, m_ref, acc_ref, *, lam_init):
    tk = vt_ref.shape[2]
    pl.when(pl.program_id(1) == 0)(lambda: _build_vt(v_ref, vt_ref, tk))
    lv = lam_ref[...]
    lam = (jnp.exp(jnp.sum(lv[0:1, :] * lv[1:2, :], axis=-1, keepdims=True))
           - jnp.exp(jnp.sum(lv[2:3, :] * lv[3:4, :], axis=-1, keepdims=True)) + lam_init)
    _attend([q0_ref[...], q1_ref[...]], [k0_ref, k1_ref], vt_ref, s_ref, m_ref, acc_ref)
    o_t = (acc_ref[0, 0:B_V_DIM, :] / acc_ref[0, B_V_DIM:B_V_DIM + 1, :]
           - lam * (acc_ref[1, 0:B_V_DIM, :] / acc_ref[1, B_V_DIM:B_V_DIM + 1, :]))
    o = o_t.T
    y = o * lax.rsqrt(jnp.mean(o * o, axis=-1, keepdims=True) + EPS) * g_ref[...]
    o_ref[...] = (y * (1.0 - lam_init)).astype(o_ref.dtype)


def _diff(proj, s, lam_vecs, subln_g, lam_init):
    t = proj.shape[0]
    tq = 512
    tk = _key_chunk(t)
    vb = B_V_DIM // HEAD_DIM
    return pl.pallas_call(
        functools.partial(_diff_kernel, lam_init=lam_init),
        grid=(B_HEADS, s // tq),
        in_specs=[pl.BlockSpec((4, HEAD_DIM), lambda h, i: (0, 0)),
                  pl.BlockSpec((tq, HEAD_DIM), lambda h, i: (i, COL_BQ + 2 * h)),
                  pl.BlockSpec((tq, HEAD_DIM), lambda h, i: (i, COL_BQ + 2 * h + 1)),
                  pl.BlockSpec((t, HEAD_DIM), lambda h, i: (0, COL_BK + 2 * h)),
                  pl.BlockSpec((t, HEAD_DIM), lambda h, i: (0, COL_BK + 2 * h + 1)),
                  pl.BlockSpec((t, B_V_DIM), lambda h, i: (0, COL_BV // vb + h)),
                  pl.BlockSpec((1, B_V_DIM), lambda h, i: (0, 0))],
        out_specs=pl.BlockSpec((tq, B_V_DIM), lambda h, i: (i, h)),
        out_shape=jax.ShapeDtypeStruct((s, B_V_W), BF16),
        scratch_shapes=[pltpu.VMEM((t // tk, B_V_DIM + ONES_ROWS, tk), BF16),
                        pltpu.VMEM((2, 2, tk, tq), F32),
                        pltpu.VMEM((2, 1, tq), F32),
                        pltpu.VMEM((2, B_V_DIM + ONES_ROWS, tq), F32)],
        compiler_params=_cparams(("arbitrary", "arbitrary")),
        name="diff_attn",
    )(lam_vecs, proj, proj, proj, proj, proj, subln_g.reshape(1, B_V_DIM))


def _merge_kernel(oa_ref, ob_ref, ga_ref, gb_ref, wa_ref, wb_ref, wo_ref, x_ref, gate_ref, o_ref, t_ref):
    c = pl.program_id(1)
    nc, _, tn = t_ref.shape
    ya = jnp.dot(oa_ref[...], wa_ref[...], preferred_element_type=F32)
    yb = jnp.dot(ob_ref[...], wb_ref[...], preferred_element_type=F32)
    t = (jax.nn.sigmoid(ga_ref[...].astype(F32)) * ya + jax.nn.sigmoid(gb_ref[...].astype(F32)) * yb)
    t_ref[c] = t.astype(t_ref.dtype)

    @pl.when(c == nc - 1)
    def _():
        y = jnp.dot(t_ref[0], wo_ref[0:tn, :], preferred_element_type=F32)
        for cc in range(1, nc):
            y = y + jnp.dot(t_ref[cc], wo_ref[cc * tn:(cc + 1) * tn, :], preferred_element_type=F32)
        o_ref[...] = x_ref[...] + gate_ref[...] * y


def _merge(oa, ob, proj, wa, wb, wo, x, gate):
    s, d = x.shape
    tm = 512
    tn = PROJ_TN
    nc = d // tn
    ga0 = COL_GA * HEAD_DIM // tn
    gb0 = COL_GB * HEAD_DIM // tn
    return pl.pallas_call(
        _merge_kernel,
        grid=(s // tm, nc),
        in_specs=[pl.BlockSpec((tm, A_Q_W), lambda i, c: (i, 0)),
                  pl.BlockSpec((tm, B_V_W), lambda i, c: (i, 0)),
                  pl.BlockSpec((tm, tn), lambda i, c: (i, ga0 + c)),
                  pl.BlockSpec((tm, tn), lambda i, c: (i, gb0 + c)),
                  pl.BlockSpec((A_Q_W, tn), lambda i, c: (0, c)),
                  pl.BlockSpec((B_V_W, tn), lambda i, c: (0, c)),
                  pl.BlockSpec((d, d), lambda i, c: (0, 0)),
                  pl.BlockSpec((tm, d), lambda i, c: (i, 0)),
                  pl.BlockSpec((1, d), lambda i, c: (0, 0))],
        out_specs=pl.BlockSpec((tm, d), lambda i, c: (i, 0)),
        out_shape=jax.ShapeDtypeStruct((s, d), F32),
        scratch_shapes=[pltpu.VMEM((nc, tm, tn), BF16)],
        compiler_params=_cparams(("arbitrary", "arbitrary")),
        name="merge_out",
    )(oa, ob, proj, proj, wa, wb, wo, x, gate)


def _store_packed_tiles(ref, x):
    rows = x.shape[0]
    bits = lax.bitcast_convert_type(x.astype(BF16).astype(F32), jnp.uint32)
    half = bits.shape[1] // 2
    words = (bits[:, :half] >> 16) | (bits[:, half:] & jnp.uint32(0xFFFF0000))
    for c in range(PACK_TILES):
        ref[pl.ds(c, rows, stride=PACK_TILES), :] = words[:, c * LANES:(c + 1) * LANES]


def _lane_max(x):
    return jnp.max(x, axis=-1, keepdims=True)


def _lane_min(x):
    return jnp.min(x, axis=-1, keepdims=True)


def _group_allreduce(x, lane, op):
    for sft in (1, 2, 4):
        up = pltpu.roll(x, sft, 1)
        dn = pltpu.roll(x, LANES - sft, 1)
        x = op(x, jnp.where((lane & sft) != 0, up, dn))
    return x


def _router_kernel(x_ref, g_ref, mod_ref, wr_ref, rb_ref,
                   h_ref, hp_ref, idx_ref, wsel_ref, rank_ref, cnt_ref, carry_ref):
    @pl.when(pl.program_id(0) == 0)
    def _():
        carry_ref[...] = jnp.zeros_like(carry_ref)

    h = _rms_mod(x_ref[...], g_ref[...], mod_ref[0, 3:4, :], mod_ref[0, 4:5, :])
    hb = h.astype(BF16)
    h_ref[...] = hb
    _store_packed_tiles(hp_ref, h)
    tm = h.shape[0]

    h_lo = (h - hb.astype(F32)).astype(BF16)
    logits = (jnp.dot(hb, wr_ref[0], preferred_element_type=F32)
              + jnp.dot(hb, wr_ref[1], preferred_element_type=F32)
              + jnp.dot(h_lo, wr_ref[0], preferred_element_type=F32))
    scores = jax.nn.sigmoid(logits)
    lane = lax.broadcasted_iota(jnp.int32, (tm, LANES), 1)
    lane_f = lane.astype(F32)
    gid_f = (lane >> 3).astype(F32)
    valid = lane < N_EXPERTS
    neg = jnp.float32(-jnp.inf)
    big = jnp.float32(LANES)
    biased = jnp.where(valid, scores + rb_ref[...], neg)

    m1 = _group_allreduce(biased, lane, jnp.maximum)
    a1 = _group_allreduce(jnp.where(biased == m1, lane_f, big), lane, jnp.minimum)
    m2 = _group_allreduce(jnp.where(lane_f == a1, neg, biased), lane, jnp.maximum)
    gscore = jnp.where(valid, m1 + m2, neg)
    keep = jnp.zeros((tm, LANES), jnp.bool_)
    for _ in range(TOPK_GROUPS):
        best = _lane_max(gscore)
        gsel = _lane_min(jnp.where(gscore == best, gid_f, big))
        hit = gid_f == gsel
        keep = keep | hit
        gscore = jnp.where(hit, neg, gscore)
    cand = jnp.where(keep & valid, biased, neg)

    onehot = jnp.zeros((tm, LANES), F32)
    idx_out = jnp.zeros((tm, LANES), F32)
    w_out = jnp.zeros((tm, LANES), F32)
    sels = []
    for k in range(TOP_K):
        best = _lane_max(cand)
        sel = _lane_min(jnp.where(cand == best, lane_f, big))
        hit = lane_f == sel
        wk = jnp.sum(jnp.where(hit, scores, 0.0), axis=-1, keepdims=True)
        cand = jnp.where(hit, neg, cand)
        onehot = jnp.where(hit, 1.0, onehot)
        idx_out = jnp.where(lane == k, sel, idx_out)
        w_out = jnp.where(lane == k, wk, w_out)
        sels.append(hit)
    wsum = jnp.sum(w_out, axis=-1, keepdims=True)
    wsel_ref[...] = w_out / wsum * ROUTED_SCALE
    idx_ref[...] = idx_out.astype(jnp.int32)

    row = lax.broadcasted_iota(jnp.int32, (tm, tm), 0)
    col = lax.broadcasted_iota(jnp.int32, (tm, tm), 1)
    lower = (col < row).astype(BF16)
    before = jnp.dot(lower, onehot.astype(BF16), preferred_element_type=F32) + carry_ref[...]
    rank_out = jnp.zeros((tm, LANES), F32)
    for k in range(TOP_K):
        rk = jnp.sum(jnp.where(sels[k], before, 0.0), axis=-1, keepdims=True)
        rank_out = jnp.where(lane == k, rk, rank_out)
    rank_ref[...] = rank_out.astype(jnp.int32)
    carry_ref[...] = carry_ref[...] + jnp.sum(onehot, axis=0, keepdims=True)
    cnt_ref[...] = carry_ref[...]


def _router(x1, g, mod, w_router, router_bias):
    s, d = x1.shape
    tm = 256
    e = w_router.shape[1]
    wr = jnp.pad(w_router, ((0, 0), (0, LANES - e)))
    wr_hi = wr.astype(BF16)
    wr = jnp.stack([wr_hi, (wr - wr_hi.astype(F32)).astype(BF16)])
    rb = jnp.pad(router_bias, (0, LANES - e)).reshape(1, LANES)
    row_spec = lambda w: pl.BlockSpec((tm, w), lambda i: (i, 0))
    return pl.pallas_call(
        _router_kernel,
        grid=(s // tm,),
        in_specs=[row_spec(d),
                  pl.BlockSpec((1, d), lambda i: (0, 0)),
                  pl.BlockSpec((1, N_MOD, d), lambda i: (0, 0, 0)),
                  pl.BlockSpec((2, d, LANES), lambda i: (0, 0, 0)),
                  pl.BlockSpec((1, LANES), lambda i: (0, 0))],
        out_specs=[row_spec(d), pl.BlockSpec((tm * PACK_TILES, LANES), lambda i: (i, 0)),
                   row_spec(LANES), row_spec(LANES), row_spec(LANES),
                   pl.BlockSpec((1, LANES), lambda i: (0, 0))],
        out_shape=[jax.ShapeDtypeStruct((s, d), BF16),
                   jax.ShapeDtypeStruct((s * PACK_TILES, LANES), jnp.uint32),
                   jax.ShapeDtypeStruct((s, LANES), jnp.int32),
                   jax.ShapeDtypeStruct((s, LANES), F32),
                   jax.ShapeDtypeStruct((s, LANES), jnp.int32),
                   jax.ShapeDtypeStruct((1, LANES), F32)],
        scratch_shapes=[pltpu.VMEM((1, LANES), F32)],
        compiler_params=_cparams(("arbitrary",)),
        name="ffn_router",
    )(x1, g.reshape(1, d), mod, wr, rb)


def _unpack_halves(words):
    lo = lax.bitcast_convert_type(words << 16, F32)
    hi = lax.bitcast_convert_type(words & jnp.uint32(0xFFFF0000), F32)
    return lo, hi


def _unpack_rows(words):
    lo, hi = _unpack_halves(words)
    return jnp.concatenate([lo.astype(BF16), hi.astype(BF16)], axis=1)


def _token_rows(row, tiles):
    return pl.ds(pl.multiple_of(row * tiles, tiles), tiles)


def _dispatch_kernel(ps_ref, pe_ref, nused_ref, dest_hbm, hp_ref, xs_hbm,
                     dest_smem, zbuf, sem_i, sem_z, sem_x, *, n_blocks):
    i = pl.program_id(0)
    n = dest_smem.shape[0]
    tm = n // TOP_K
    copy_dest = pltpu.make_async_copy(dest_hbm.at[i], dest_smem, sem_i)
    copy_dest.start()

    @pl.when(i == 0)
    def _():
        zbuf[...] = jnp.zeros_like(zbuf)

        def zero_block(row0):
            span = MOE_BLOCK * PACK_TILES
            return pltpu.make_async_copy(
                zbuf, xs_hbm.at[pl.ds(pl.multiple_of(row0 * PACK_TILES, span), span), :], sem_z)

        def per_expert(action):
            def body(e, carry):
                @pl.when(pe_ref[e] > ps_ref[e])
                def _():
                    action(zero_block(pe_ref[e] - MOE_BLOCK))
                return carry
            lax.fori_loop(0, N_EXPERTS, body, 0)

        def per_tail(action):
            def body(b, carry):
                action(zero_block(b * MOE_BLOCK))
                return carry
            lax.fori_loop(nused_ref[0], n_blocks, body, 0)

        per_expert(lambda cp: cp.start())
        per_tail(lambda cp: cp.start())
        per_expert(lambda cp: cp.wait())
        per_tail(lambda cp: cp.wait())

    copy_dest.wait()

    def issue(t, carry):
        for k in range(TOP_K):
            dest = dest_smem[t * TOP_K + k]
            pltpu.make_async_copy(hp_ref.at[_token_rows(t, PACK_TILES), :],
                                  xs_hbm.at[_token_rows(dest, PACK_TILES), :], sem_x).start(priority=k % 2)
        return carry

    lax.fori_loop(0, tm, issue, 0)
    for _ in range(TOP_K):
        pltpu.make_async_copy(hp_ref, xs_hbm.at[pl.ds(0, tm * PACK_TILES), :], sem_x).wait()


def _dispatch(pad_start, pad_end, n_used, dest_tm, hp, n_blocks):
    n_tiles, n = dest_tm.shape
    tm = n // TOP_K
    grid_spec = pltpu.PrefetchScalarGridSpec(
        num_scalar_prefetch=3,
        grid=(n_tiles,),
        in_specs=[pl.BlockSpec(memory_space=pl.ANY),
                  pl.BlockSpec((tm * PACK_TILES, LANES), lambda i, ps, pe, nu: (i, 0))],
        out_specs=pl.BlockSpec(memory_space=pl.ANY),
        scratch_shapes=[pltpu.SMEM((n,), jnp.int32),
                        pltpu.VMEM((MOE_BLOCK * PACK_TILES, LANES), jnp.uint32),
                        pltpu.SemaphoreType.DMA,
                        pltpu.SemaphoreType.DMA,
                        pltpu.SemaphoreType.DMA])
    return pl.pallas_call(
        functools.partial(_dispatch_kernel, n_blocks=n_blocks),
        grid_spec=grid_spec,
        out_shape=jax.ShapeDtypeStruct((n_blocks * MOE_BLOCK * PACK_TILES, LANES), jnp.uint32),
        compiler_params=_cparams(("arbitrary",)),
        name="moe_dispatch",
    )(pad_start, pad_end, n_used, dest_tm, hp)


def _moe_kernel(be_ref, first_ref, nxt_ref, slot_ref, nused_ref, x_ref, wg_hbm, wu_hbm, wd_hbm, o_ref,
                wg32, wu32, wd32, wgb, wub, wdb, sems):
    b = pl.program_id(0)

    def fetch(e, slot):
        return (pltpu.make_async_copy(wg_hbm.at[e], wg32.at[slot], sems.at[slot, 0]),
                pltpu.make_async_copy(wu_hbm.at[e], wu32.at[slot], sems.at[slot, 1]),
                pltpu.make_async_copy(wd_hbm.at[e], wd32.at[slot], sems.at[slot, 2]))

    @pl.when(b < nused_ref[0])
    def _():
        @pl.when(first_ref[b] == 1)
        def _():
            slot = slot_ref[b]

            @pl.when(b == 0)
            def _():
                for cp in fetch(be_ref[0], 0):
                    cp.start()

            for cp in fetch(be_ref[b], slot):
                cp.wait()

            @pl.when(nxt_ref[b] >= 0)
            def _():
                for cp in fetch(nxt_ref[b], 1 - slot):
                    cp.start()

            wgb[...] = wg32[slot].astype(BF16)
            wub[...] = wu32[slot].astype(BF16)
            wdb[...] = wd32[slot].astype(BF16)

        words = jnp.concatenate(
            [x_ref[pl.ds(c, MOE_BLOCK, stride=PACK_TILES), :] for c in range(PACK_TILES)], axis=1)
        x = _unpack_rows(words)
        gate = jnp.dot(x, wgb[...], preferred_element_type=F32)
        up = jnp.dot(x, wub[...], preferred_element_type=F32)
        act = (gate * jax.nn.sigmoid(gate) * up).astype(BF16)
        _store_packed_tiles(o_ref, jnp.dot(act, wdb[...], preferred_element_type=F32))

    @pl.when(b >= nused_ref[0])
    def _():
        o_ref[...] = jnp.zeros_like(o_ref)


def _moe(block_e, first, nxt, slot, n_used, xs, weg, weu, wed):
    n_blocks = block_e.shape[0]
    _, d, f = weg.shape
    grid_spec = pltpu.PrefetchScalarGridSpec(
        num_scalar_prefetch=5,
        grid=(n_blocks,),
        in_specs=[pl.BlockSpec((MOE_BLOCK * PACK_TILES, LANES), lambda b, *_: (b, 0)),
                  pl.BlockSpec(memory_space=pl.ANY),
                  pl.BlockSpec(memory_space=pl.ANY),
                  pl.BlockSpec(memory_space=pl.ANY)],
        out_specs=pl.BlockSpec((MOE_BLOCK * PACK_TILES, LANES), lambda b, *_: (b, 0)),
        scratch_shapes=[pltpu.VMEM((2, d, f), F32), pltpu.VMEM((2, d, f), F32), pltpu.VMEM((2, f, d), F32),
                        pltpu.VMEM((d, f), BF16), pltpu.VMEM((d, f), BF16), pltpu.VMEM((f, d), BF16),
                        pltpu.SemaphoreType.DMA((2, 3))])
    return pl.pallas_call(
        _moe_kernel,
        grid_spec=grid_spec,
        out_shape=jax.ShapeDtypeStruct((n_blocks * MOE_BLOCK * PACK_TILES, LANES), jnp.uint32),
        compiler_params=_cparams(("arbitrary",)),
        name="moe_experts",
    )(block_e, first, nxt, slot, n_used, xs, weg, weu, wed)


def _combine_kernel(dest_hbm, y_hbm, w_ref, h_ref, wg_ref, wu_ref, wd_ref, x_ref, gate_ref, o_ref,
                    dest0, dest1, ybuf0, ybuf1, sem_i, sem_y):
    i = pl.program_id(0)
    tm = h_ref.shape[0]
    n = tm * TOP_K
    span = n * PACK_TILES
    dests = (dest0, dest1)
    ybufs = (ybuf0, ybuf1)

    def start_tile(tile, slot):
        copy_dest = pltpu.make_async_copy(dest_hbm.at[tile], dests[slot], sem_i)
        copy_dest.start()
        copy_dest.wait()

        def issue(t, carry):
            base = t * TOKEN_PITCH
            for k in range(TOP_K):
                src = dests[slot][t * TOP_K + k]
                pltpu.make_async_copy(
                    y_hbm.at[_token_rows(src, PACK_TILES), :],
                    ybufs[slot].at[pl.ds(pl.multiple_of(base + k * YBUF_PITCH, 4), PACK_TILES), :],
                    sem_y.at[slot]).start(priority=k % 2)
            return carry

        lax.fori_loop(0, tm, issue, 0)

    def step(cur):
        nxt = 1 - cur
        pl.when(i == 0)(lambda: start_tile(0, cur))
        pl.when(i + 1 < pl.num_programs(0))(lambda: start_tile(i + 1, nxt))

        h = h_ref[...]
        gate = jnp.dot(h, wg_ref[...], preferred_element_type=F32)
        up = jnp.dot(h, wu_ref[...], preferred_element_type=F32)
        act = (gate * jax.nn.sigmoid(gate) * up).astype(BF16)
        shared = jnp.dot(act, wd_ref[...], preferred_element_type=F32)

        yb = ybufs[cur]
        pltpu.make_async_copy(y_hbm.at[pl.ds(0, span), :], yb.at[pl.ds(0, span), :], sem_y.at[cur]).wait()
        lows, highs = [], []
        for c in range(PACK_TILES):
            acc_lo = acc_hi = None
            for k in range(TOP_K):
                lo, hi = _unpack_halves(yb[pl.ds(k * YBUF_PITCH + c, tm, stride=TOKEN_PITCH), :])
                wk = w_ref[:, k:k + 1]
                acc_lo = lo * wk if acc_lo is None else acc_lo + lo * wk
                acc_hi = hi * wk if acc_hi is None else acc_hi + hi * wk
            lows.append(acc_lo)
            highs.append(acc_hi)
        routed = jnp.concatenate(lows + highs, axis=1)
        o_ref[...] = x_ref[...] + gate_ref[...] * (routed + shared)

    pl.when(i % 2 == 0)(lambda: step(0))
    pl.when(i % 2 == 1)(lambda: step(1))


def _combine(dest_tm, y_sorted, wsel, h2, wsg, wsu, wsd, x1, gate):
    s, d = x1.shape
    n_tiles, n = dest_tm.shape
    tm = n // TOP_K
    f = wsg.shape[1]
    return pl.pallas_call(
        _combine_kernel,
        grid=(n_tiles,),
        in_specs=[pl.BlockSpec(memory_space=pl.ANY),
                  pl.BlockSpec(memory_space=pl.ANY),
                  pl.BlockSpec((tm, LANES), lambda i: (i, 0)),
                  pl.BlockSpec((tm, d), lambda i: (i, 0)),
                  pl.BlockSpec((d, f), lambda i: (0, 0)),
                  pl.BlockSpec((d, f), lambda i: (0, 0)),
                  pl.BlockSpec((f, d), lambda i: (0, 0)),
                  pl.BlockSpec((tm, d), lambda i: (i, 0)),
                  pl.BlockSpec((1, d), lambda i: (0, 0))],
        out_specs=pl.BlockSpec((tm, d), lambda i: (i, 0)),
        out_shape=jax.ShapeDtypeStruct((s, d), F32),
        scratch_shapes=[pltpu.SMEM((n,), jnp.int32),
                        pltpu.SMEM((n,), jnp.int32),
                        pltpu.VMEM((tm * TOKEN_PITCH, LANES), jnp.uint32),
                        pltpu.VMEM((tm * TOKEN_PITCH, LANES), jnp.uint32),
                        pltpu.SemaphoreType.DMA,
                        pltpu.SemaphoreType.DMA((2,))],
        compiler_params=_cparams(("arbitrary",)),
        name="moe_combine",
    )(dest_tm, y_sorted, wsel, h2, wsg, wsu, wsd, x1, gate)


def _block_tables(counts, n_blocks):
    padded = (counts + MOE_BLOCK - 1) // MOE_BLOCK * MOE_BLOCK
    pad_end = jnp.cumsum(padded).astype(jnp.int32)
    pad_start = pad_end - padded
    blk = jnp.arange(n_blocks, dtype=jnp.int32)
    block_e = jnp.minimum(jnp.sum(pad_end[None, :] <= (blk * MOE_BLOCK)[:, None], axis=1),
                          N_EXPERTS - 1).astype(jnp.int32)
    n_used = pad_end[-1] // MOE_BLOCK
    prev = jnp.concatenate([jnp.full((1,), -1, jnp.int32), block_e[:-1]])
    first = ((blk < n_used) & (block_e != prev)).astype(jnp.int32)
    slot = jnp.maximum(jnp.cumsum(first) - 1, 0).astype(jnp.int32) % 2
    after = pad_end[block_e] // MOE_BLOCK
    nxt = jnp.where(after < n_used, block_e[jnp.minimum(after, n_blocks - 1)], -1).astype(jnp.int32)
    return pad_start, pad_end, block_e, first, nxt, slot, n_used.reshape(1)


def kernel(x, c, ctx, c_ctx, w_ada, b_ada, norm_mix, norm_ffn, w_in, q_norm_a, k_norm_a, q_norm_b, k_norm_b, lambda_q1, lambda_k1, lambda_q2, lambda_k2, subln_b, w_branch_a, w_branch_b, w_out, w_router, router_bias, w_exp_gate, w_exp_up, w_exp_down, w_sh_gate, w_sh_up, w_sh_down):
    depth = w_ada.shape[0]
    assert depth == 1 and x.shape[0] == 1 and ctx.shape[0] == 1
    s, d = x.shape[1], x.shape[2]
    n_ctx = ctx.shape[1]
    i = 0
    lam_init = 0.8 - 0.6 * math.exp(-0.3 * i)
    xs = x[0]

    mod = _adaln(jnp.concatenate([c, c_ctx[None, :]], axis=0), w_ada[i], b_ada[i]).reshape(2, N_MOD, d)

    h = _prenorm(xs, ctx[0], norm_mix[i], mod)
    tc, tsa, tsb = _rope_tables(s, n_ctx)
    gains = _head_gains(q_norm_a[i], k_norm_a[i], q_norm_b[i], k_norm_b[i])
    proj = _inproj(h, w_in[i], gains, tc, tsa, tsb)
    oa = _gqa(proj, s)
    lam_vecs = jnp.stack([lambda_q1[i], lambda_k1[i], lambda_q2[i], lambda_k2[i]]).astype(F32)
    ob = _diff(proj, s, lam_vecs, subln_b[i], lam_init)
    x1 = _merge(oa, ob, proj, w_branch_a[i].astype(BF16), w_branch_b[i].astype(BF16),
                w_out[i].astype(BF16), xs, mod[0, 2:3, :])

    h2, h2p, idx, wsel, rank, cnt = _router(x1, norm_ffn[i], mod[0:1], w_router[i], router_bias[i])
    counts = cnt[0, :N_EXPERTS].astype(jnp.int32)
    n_blocks = -(-(s * TOP_K) // MOE_BLOCK) + N_EXPERTS
    pad_start, pad_end, block_e, first, nxt, slot, n_used = _block_tables(counts, n_blocks)
    tm_r = 128
    e_ids = jnp.arange(N_EXPERTS, dtype=jnp.int32)
    starts = jnp.sum(jnp.where(idx[:, :TOP_K, None] == e_ids, pad_start, 0), axis=-1)
    dest_tm = (starts + rank[:, :TOP_K]).astype(jnp.int32).reshape(s // tm_r, tm_r * TOP_K)
    xs = _dispatch(pad_start, pad_end, n_used, dest_tm, h2p, n_blocks)
    y_sorted = _moe(block_e, first, nxt, slot, n_used, xs, w_exp_gate[i], w_exp_up[i], w_exp_down[i])
    out = _combine(dest_tm, y_sorted, wsel, h2,
                   w_sh_gate[i].astype(BF16), w_sh_up[i].astype(BF16), w_sh_down[i].astype(BF16),
                   x1, mod[0, 5:6, :])
    return out[None]
```

```python
import functools
import math

import jax
import jax.numpy as jnp
from jax import lax
from jax.experimental import pallas as pl
from jax.experimental.pallas import tpu as pltpu

F32 = jnp.float32
BF16 = jnp.bfloat16

D_MODEL = 2048
GRID_W = 64
HEAD_DIM = 128
ROPE_PAIRS = HEAD_DIM // 4
ROPE_THETA = 10000.0
A_HEADS = 8
A_KV_HEADS = 2
A_GROUP = A_HEADS // A_KV_HEADS
B_HEADS = 4
B_V_DIM = 2 * HEAD_DIM
N_EXPERTS = 64
TOP_K = 8
N_GROUPS = 8
TOPK_GROUPS = 4
EXPERT_DIM = 512
SHARED_DIM = 512
ROUTED_SCALE = 2.5
N_MOD = 6
EPS = 1e-6

A_Q_W = A_HEADS * HEAD_DIM
A_KV_W = A_KV_HEADS * HEAD_DIM
B_QK_W = B_HEADS * 2 * HEAD_DIM
B_V_W = B_HEADS * B_V_DIM
IN_W = A_Q_W + 2 * A_KV_W + 2 * B_QK_W + B_V_W + 2 * D_MODEL

COL_AQ = 0
COL_AK = A_Q_W // HEAD_DIM
COL_AV = COL_AK + A_KV_HEADS
COL_BQ = COL_AV + A_KV_HEADS
COL_BK = COL_BQ + 2 * B_HEADS
COL_BV = COL_BK + 2 * B_HEADS
COL_GA = COL_BV + B_V_W // HEAD_DIM
COL_GB = COL_GA + D_MODEL // HEAD_DIM

LANES = 128
SUBLANES = 8
VMEM_LIMIT = 56 * 1024 * 1024

PROJ_TN = 512
MOE_BLOCK = 256
ONES_ROWS = 16
PACK_TILES = D_MODEL // 2 // LANES
YBUF_PITCH = PACK_TILES + 4
TOKEN_PITCH = TOP_K * YBUF_PITCH + 4
LOG2E = 1.4426950408889634


def _cparams(sem, vmem=VMEM_LIMIT):
    return pltpu.CompilerParams(dimension_semantics=sem, vmem_limit_bytes=vmem)


def _adaln_kernel(cb_ref, w_ref, b_ref, o_ref):
    tn = w_ref.shape[1]
    nl = tn // LANES
    rows = 32

    def body(g, accs):
        accs = list(accs)
        r0 = pl.multiple_of(g * rows, rows)
        for u in range(rows // SUBLANES):
            r = r0 + u * SUBLANES
            w = w_ref[pl.ds(r, SUBLANES), :]
            for v in range(2):
                c = cb_ref[v, pl.ds(r, SUBLANES), :]
                s = c * jax.nn.sigmoid(c)
                for j in range(nl):
                    accs[v * nl + j] = accs[v * nl + j] + w[:, j * LANES:(j + 1) * LANES] * s
        return tuple(accs)

    init = tuple(jnp.zeros((SUBLANES, LANES), F32) for _ in range(2 * nl))
    accs = lax.fori_loop(0, w_ref.shape[0] // rows, body, init)
    for v in range(2):
        row = jnp.concatenate(
            [jnp.sum(accs[v * nl + j], axis=0, keepdims=True) for j in range(nl)], axis=1)
        o_ref[v:v + 1, :] = row + b_ref[...]


def _adaln(cvecs, w, b):
    d, n = w.shape
    tn = 1536
    cb = jnp.broadcast_to(cvecs[:, :, None], (2, d, LANES))
    return pl.pallas_call(
        _adaln_kernel,
        grid=(n // tn,),
        in_specs=[pl.BlockSpec((2, d, LANES), lambda j: (0, 0, 0)),
                  pl.BlockSpec((d, tn), lambda j: (0, j)),
                  pl.BlockSpec((1, tn), lambda j: (0, j))],
        out_specs=pl.BlockSpec((2, tn), lambda j: (0, j)),
        out_shape=jax.ShapeDtypeStruct((2, n), F32),
        compiler_params=_cparams(("arbitrary",)),
        name="adaln",
    )(cb, w, b.reshape(1, n))


def _rms_mod(x, g, shift, scale):
    y = x * lax.rsqrt(jnp.mean(x * x, axis=-1, keepdims=True) + EPS) * g
    return y * (1.0 + scale) + shift


def _prenorm_kernel(x_ref, c_ref, g_ref, mod_ref, o_ref, *, n_lat_tiles):
    is_ctx = pl.program_id(0) >= n_lat_tiles
    x = jnp.where(is_ctx, c_ref[...], x_ref[...])
    o_ref[...] = _rms_mod(x, g_ref[...], mod_ref[0, 0:1, :], mod_ref[0, 1:2, :]).astype(o_ref.dtype)


def _prenorm(x, ctx, g, mod):
    s, d = x.shape
    c = ctx.shape[0]
    tm = 256
    nl, nc = s // tm, c // tm
    return pl.pallas_call(
        functools.partial(_prenorm_kernel, n_lat_tiles=nl),
        grid=(nl + nc,),
        in_specs=[pl.BlockSpec((tm, d), lambda i: (jnp.minimum(i, nl - 1), 0)),
                  pl.BlockSpec((tm, d), lambda i: (jnp.maximum(i - nl, 0), 0)),
                  pl.BlockSpec((1, d), lambda i: (0, 0)),
                  pl.BlockSpec((1, N_MOD, d), lambda i: (i // nl, 0, 0))],
        out_specs=pl.BlockSpec((tm, d), lambda i: (i, 0)),
        out_shape=jax.ShapeDtypeStruct((s + c, d), BF16),
        compiler_params=_cparams(("arbitrary",)),
        name="prenorm_mix",
    )(x, ctx, g.reshape(1, d), mod)


def _inproj_kernel(h_ref, w_ref, gain_ref, c_ref, sa_ref, sb_ref, o_ref):
    j = pl.program_id(1)
    acc = jnp.dot(h_ref[...], w_ref[...].astype(h_ref.dtype), preferred_element_type=F32)
    nh = acc.shape[1] // HEAD_DIM

    def norm_rope(a, gain):
        y = a * lax.rsqrt(jnp.mean(a * a, axis=-1, keepdims=True) + EPS) * gain
        return (y * c_ref[...] + pltpu.roll(y, ROPE_PAIRS, 1) * sa_ref[...]
                + pltpu.roll(y, HEAD_DIM - ROPE_PAIRS, 1) * sb_ref[...])

    def store(n_normed):
        for hd in range(nh):
            sl = slice(hd * HEAD_DIM, (hd + 1) * HEAD_DIM)
            a = acc[:, sl]
            if hd < n_normed:
                a = norm_rope(a, gain_ref[0, :, sl])
            o_ref[:, sl] = a.astype(o_ref.dtype)

    all_normed = (j < 2) | ((j >= 3) & (j < 7))
    pl.when(all_normed)(lambda: store(nh))
    pl.when(j == 2)(lambda: store(A_KV_HEADS))
    pl.when(j >= 7)(lambda: store(0))


def _inproj(h, w, gains, rope_c, rope_sa, rope_sb):
    t, d = h.shape
    n = w.shape[1]
    tm = t // 8
    tn = PROJ_TN
    return pl.pallas_call(
        _inproj_kernel,
        grid=(t // tm, n // tn),
        in_specs=[pl.BlockSpec((tm, d), lambda i, j: (i, 0)),
                  pl.BlockSpec((d, tn), lambda i, j: (0, j)),
                  pl.BlockSpec((1, 1, tn), lambda i, j: (j, 0, 0)),
                  pl.BlockSpec((tm, HEAD_DIM), lambda i, j: (i, 0)),
                  pl.BlockSpec((tm, HEAD_DIM), lambda i, j: (i, 0)),
                  pl.BlockSpec((tm, HEAD_DIM), lambda i, j: (i, 0))],
        out_specs=pl.BlockSpec((tm, tn), lambda i, j: (i, j)),
        out_shape=jax.ShapeDtypeStruct((t, n), BF16),
        compiler_params=_cparams(("arbitrary", "arbitrary")),
        name="inproj",
    )(h, w, gains, rope_c, rope_sa, rope_sb)


def _rope_tables(s, c):
    rows_n = s // GRID_W
    inv = ROPE_THETA ** (-jnp.arange(ROPE_PAIRS, dtype=F32) / ROPE_PAIRS)
    ang_r = jnp.arange(rows_n, dtype=F32)[:, None] * inv
    ang_c = jnp.arange(GRID_W, dtype=F32)[:, None] * inv
    cr, sr, cc, sc = jnp.cos(ang_r), jnp.sin(ang_r), jnp.cos(ang_c), jnp.sin(ang_c)
    zr, zc = jnp.zeros_like(sr), jnp.zeros_like(sc)

    def table(row_parts, col_parts, ctx_value):
        by_row = jnp.concatenate(row_parts + [zr, zr], axis=1)
        by_col = jnp.concatenate([zc, zc] + col_parts, axis=1)
        lat = (by_row[:, None, :] + by_col[None, :, :]).reshape(s, HEAD_DIM)
        return jnp.concatenate([lat, jnp.full((c, HEAD_DIM), ctx_value, F32)], axis=0)

    return (table([cr, cr], [cc, cc], 1.0), table([zr, sr], [zc, sc], 0.0), table([-sr, zr], [-sc, zc], 0.0))


def _head_gains(qn_a, kn_a, qn_b, kn_b):
    qs = HEAD_DIM ** -0.5 * LOG2E
    one = jnp.ones((HEAD_DIM,), F32)
    heads = ([qn_a * qs] * A_HEADS + [kn_a] * A_KV_HEADS + [one] * A_KV_HEADS
             + [qn_b * qs] * (2 * B_HEADS) + [kn_b] * (2 * B_HEADS))
    heads = heads + [one] * (IN_W // HEAD_DIM - len(heads))
    return jnp.concatenate(heads).reshape(IN_W // PROJ_TN, 1, PROJ_TN)


def _build_vt(v_ref, vt_ref, tk):
    n_chunks, rows, _ = vt_ref.shape
    dv = rows - ONES_ROWS
    tail = (lax.broadcasted_iota(jnp.int32, (ONES_ROWS, tk), 0) == 0).astype(vt_ref.dtype)
    for c in range(n_chunks):
        vt_ref[c, 0:dv, :] = v_ref[c * tk:(c + 1) * tk, :].astype(F32).T.astype(vt_ref.dtype)
        vt_ref[c, dv:rows, :] = tail


def _attend(qs, k_refs, vt_ref, s_ref, m_ref, acc_ref):
    n_chunks, _, tk = vt_ref.shape
    n_st = len(qs)
    m_ref[...] = jnp.full(m_ref.shape, -jnp.inf, F32)
    acc_ref[...] = jnp.zeros(acc_ref.shape, F32)

    def scores(i, c, slot):
        off = c * tk if isinstance(c, int) else pl.multiple_of(c * tk, tk)
        s_ref[i, slot] = lax.dot_general(k_refs[i][pl.ds(off, tk), :], qs[i], (((1,), (1,)), ((), ())),
                                         preferred_element_type=F32)

    def update(i, c, slot):
        s = s_ref[i, slot]
        m_old = m_ref[i]
        m_new = jnp.maximum(m_old, jnp.max(s, axis=0, keepdims=True))
        p = jnp.exp2(s - m_new).astype(vt_ref.dtype)
        acc_ref[i] = (acc_ref[i] * jnp.exp2(m_old - m_new)
                      + jnp.dot(vt_ref[c], p, preferred_element_type=F32))
        m_ref[i] = m_new

    for i in range(n_st):
        scores(i, 0, 0)

    def pair(j, carry):
        c = 2 * j
        for i in range(n_st):
            scores(i, c + 1, 1)
        for i in range(n_st):
            update(i, c, 0)
        for i in range(n_st):
            scores(i, c + 2, 0)
        for i in range(n_st):
            update(i, c + 1, 1)
        return carry

    n_pairs = (n_chunks - 1) // 2
    lax.fori_loop(0, n_pairs, pair, 0)
    done = 2 * n_pairs
    if n_chunks - done == 2:
        for i in range(n_st):
            scores(i, done + 1, 1)
    for i in range(n_st):
        update(i, done, 0)
    if n_chunks - done == 2:
        for i in range(n_st):
            update(i, done + 1, 1)


def _gqa_kernel(q_ref, k_ref, v_ref, o_ref, vt_ref, s_ref, m_ref, acc_ref):
    tk = vt_ref.shape[2]
    n_st = s_ref.shape[0]
    pl.when((pl.program_id(1) == 0) & (pl.program_id(2) == 0))(lambda: _build_vt(v_ref, vt_ref, tk))
    qs = [q_ref[:, i * HEAD_DIM:(i + 1) * HEAD_DIM] for i in range(n_st)]
    _attend(qs, [k_ref] * n_st, vt_ref, s_ref, m_ref, acc_ref)
    for i in range(n_st):
        o_t = acc_ref[i, 0:HEAD_DIM, :] / acc_ref[i, HEAD_DIM:HEAD_DIM + 1, :]
        o_ref[:, i * HEAD_DIM:(i + 1) * HEAD_DIM] = o_t.T.astype(o_ref.dtype)


def _key_chunk(t):
    for tk in (1408, 768, 1024, 512, 640, 384, 256, 128):
        if t % tk == 0:
            return tk
    raise ValueError(f"unsupported key count {t}")


def _gqa(proj, s):
    t = proj.shape[0]
    tq = 512
    tk = _key_chunk(t)
    n_st = 4
    per_g = A_GROUP // n_st
    return pl.pallas_call(
        _gqa_kernel,
        grid=(A_KV_HEADS, per_g, s // tq),
        in_specs=[pl.BlockSpec((tq, n_st * HEAD_DIM), lambda g, hh, i: (i, COL_AQ // n_st + g * per_g + hh)),
                  pl.BlockSpec((t, HEAD_DIM), lambda g, hh, i: (0, COL_AK + g)),
                  pl.BlockSpec((t, HEAD_DIM), lambda g, hh, i: (0, COL_AV + g))],
        out_specs=pl.BlockSpec((tq, n_st * HEAD_DIM), lambda g, hh, i: (i, g * per_g + hh)),
        out_shape=jax.ShapeDtypeStruct((s, A_Q_W), BF16),
        scratch_shapes=[pltpu.VMEM((t // tk, HEAD_DIM + ONES_ROWS, tk), BF16),
                        pltpu.VMEM((n_st, 2, tk, tq), F32),
                        pltpu.VMEM((n_st, 1, tq), F32),
                        pltpu.VMEM((n_st, HEAD_DIM + ONES_ROWS, tq), F32)],
        compiler_params=_cparams(("arbitrary", "arbitrary", "arbitrary")),
        name="gqa_attn",
    )(proj, proj, proj)


def _diff_kernel(lam_ref, q0_ref, q1_ref, k0_ref, k1_ref, v_ref, g_ref, o_ref,
                 vt_ref, s_ref, m_ref, acc_ref, *, lam_init):
    tk = vt_ref.shape[2]
    pl.when(pl.program_id(1) == 0)(lambda: _build_vt(v_ref, vt_ref, tk))
    lv = lam_ref[...]
    lam = (jnp.exp(jnp.sum(lv[0:1, :] * lv[1:2, :], axis=-1, keepdims=True))
           - jnp.exp(jnp.sum(lv[2:3, :] * lv[3:4, :], axis=-1, keepdims=True)) + lam_init)
    _attend([q0_ref[...], q1_ref[...]], [k0_ref, k1_ref], vt_ref, s_ref, m_ref, acc_ref)
    o_t = (acc_ref[0, 0:B_V_DIM, :] / acc_ref[0, B_V_DIM:B_V_DIM + 1, :]
           - lam * (acc_ref[1, 0:B_V_DIM, :] / acc_ref[1, B_V_DIM:B_V_DIM + 1, :]))
    o = o_t.T
    y = o * lax.rsqrt(jnp.mean(o * o, axis=-1, keepdims=True) + EPS) * g_ref[...]
    o_ref[...] = (y * (1.0 - lam_init)).astype(o_ref.dtype)


def _diff(proj, s, lam_vecs, subln_g, lam_init):
    t = proj.shape[0]
    tq = 512
    tk = _key_chunk(t)
    vb = B_V_DIM // HEAD_DIM
    return pl.pallas_call(
        functools.partial(_diff_kernel, lam_init=lam_init),
        grid=(B_HEADS, s // tq),
        in_specs=[pl.BlockSpec((4, HEAD_DIM), lambda h, i: (0, 0)),
                  pl.BlockSpec((tq, HEAD_DIM), lambda h, i: (i, COL_BQ + 2 * h)),
                  pl.BlockSpec((tq, HEAD_DIM), lambda h, i: (i, COL_BQ + 2 * h + 1)),
                  pl.BlockSpec((t, HEAD_DIM), lambda h, i: (0, COL_BK + 2 * h)),
                  pl.BlockSpec((t, HEAD_DIM), lambda h, i: (0, COL_BK + 2 * h + 1)),
                  pl.BlockSpec((t, B_V_DIM), lambda h, i: (0, COL_BV // vb + h)),
                  pl.BlockSpec((1, B_V_DIM), lambda h, i: (0, 0))],
        out_specs=pl.BlockSpec((tq, B_V_DIM), lambda h, i: (i, h)),
        out_shape=jax.ShapeDtypeStruct((s, B_V_W), BF16),
        scratch_shapes=[pltpu.VMEM((t // tk, B_V_DIM + ONES_ROWS, tk), BF16),
                        pltpu.VMEM((2, 2, tk, tq), F32),
                        pltpu.VMEM((2, 1, tq), F32),
                        pltpu.VMEM((2, B_V_DIM + ONES_ROWS, tq), F32)],
        compiler_params=_cparams(("arbitrary", "arbitrary")),
        name="diff_attn",
    )(lam_vecs, proj, proj, proj, proj, proj, subln_g.reshape(1, B_V_DIM))


def _merge_kernel(oa_ref, ob_ref, ga_ref, gb_ref, wa_ref, wb_ref, wo_ref, x_ref, gate_ref, o_ref, t_ref):
    c = pl.program_id(1)
    nc, _, tn = t_ref.shape
    ya = jnp.dot(oa_ref[...], wa_ref[...], preferred_element_type=F32)
    yb = jnp.dot(ob_ref[...], wb_ref[...], preferred_element_type=F32)
    t = (jax.nn.sigmoid(ga_ref[...].astype(F32)) * ya + jax.nn.sigmoid(gb_ref[...].astype(F32)) * yb)
    t_ref[c] = t.astype(t_ref.dtype)

    @pl.when(c == nc - 1)
    def _():
        y = jnp.dot(t_ref[0], wo_ref[0:tn, :], preferred_element_type=F32)
        for cc in range(1, nc):
            y = y + jnp.dot(t_ref[cc], wo_ref[cc * tn:(cc + 1) * tn, :], preferred_element_type=F32)
        o_ref[...] = x_ref[...] + gate_ref[...] * y


def _merge(oa, ob, proj, wa, wb, wo, x, gate):
    s, d = x.shape
    tm = 512
    tn = PROJ_TN
    nc = d // tn
    ga0 = COL_GA * HEAD_DIM // tn
    gb0 = COL_GB * HEAD_DIM // tn
    return pl.pallas_call(
        _merge_kernel,
        grid=(s // tm, nc),
        in_specs=[pl.BlockSpec((tm, A_Q_W), lambda i, c: (i, 0)),
                  pl.BlockSpec((tm, B_V_W), lambda i, c: (i, 0)),
                  pl.BlockSpec((tm, tn), lambda i, c: (i, ga0 + c)),
                  pl.BlockSpec((tm, tn), lambda i, c: (i, gb0 + c)),
                  pl.BlockSpec((A_Q_W, tn), lambda i, c: (0, c)),
                  pl.BlockSpec((B_V_W, tn), lambda i, c: (0, c)),
                  pl.BlockSpec((d, d), lambda i, c: (0, 0)),
                  pl.BlockSpec((tm, d), lambda i, c: (i, 0)),
                  pl.BlockSpec((1, d), lambda i, c: (0, 0))],
        out_specs=pl.BlockSpec((tm, d), lambda i, c: (i, 0)),
        out_shape=jax.ShapeDtypeStruct((s, d), F32),
        scratch_shapes=[pltpu.VMEM((nc, tm, tn), BF16)],
        compiler_params=_cparams(("arbitrary", "arbitrary")),
        name="merge_out",
    )(oa, ob, proj, proj, wa, wb, wo, x, gate)


def _store_packed_tiles(ref, x):
    rows = x.shape[0]
    bits = lax.bitcast_convert_type(x.astype(BF16).astype(F32), jnp.uint32)
    half = bits.shape[1] // 2
    words = (bits[:, :half] >> 16) | (bits[:, half:] & jnp.uint32(0xFFFF0000))
    for c in range(PACK_TILES):
        ref[pl.ds(c, rows, stride=PACK_TILES), :] = words[:, c * LANES:(c + 1) * LANES]


def _lane_max(x):
    return jnp.max(x, axis=-1, keepdims=True)


def _lane_min(x):
    return jnp.min(x, axis=-1, keepdims=True)


def _group_allreduce(x, lane, op):
    for sft in (1, 2, 4):
        up = pltpu.roll(x, sft, 1)
        dn = pltpu.roll(x, LANES - sft, 1)
        x = op(x, jnp.where((lane & sft) != 0, up, dn))
    return x


def _router_kernel(x_ref, g_ref, mod_ref, wr_ref, rb_ref,
                   h_ref, hp_ref, idx_ref, wsel_ref, rank_ref, cnt_ref, carry_ref):
    @pl.when(pl.program_id(0) == 0)
    def _():
        carry_ref[...] = jnp.zeros_like(carry_ref)

    h = _rms_mod(x_ref[...], g_ref[...], mod_ref[0, 3:4, :], mod_ref[0, 4:5, :])
    hb = h.astype(BF16)
    h_ref[...] = hb
    _store_packed_tiles(hp_ref, h)
    tm = h.shape[0]

    h_lo = (h - hb.astype(F32)).astype(BF16)
    logits = (jnp.dot(hb, wr_ref[0], preferred_element_type=F32)
              + jnp.dot(hb, wr_ref[1], preferred_element_type=F32)
              + jnp.dot(h_lo, wr_ref[0], preferred_element_type=F32))
    scores = jax.nn.sigmoid(logits)
    lane = lax.broadcasted_iota(jnp.int32, (tm, LANES), 1)
    lane_f = lane.astype(F32)
    gid_f = (lane >> 3).astype(F32)
    valid = lane < N_EXPERTS
    neg = jnp.float32(-jnp.inf)
    big = jnp.float32(LANES)
    biased = jnp.where(valid, scores + rb_ref[...], neg)

    m1 = _group_allreduce(biased, lane, jnp.maximum)
    a1 = _group_allreduce(jnp.where(biased == m1, lane_f, big), lane, jnp.minimum)
    m2 = _group_allreduce(jnp.where(lane_f == a1, neg, biased), lane, jnp.maximum)
    gscore = jnp.where(valid, m1 + m2, neg)
    keep = jnp.zeros((tm, LANES), jnp.bool_)
    for _ in range(TOPK_GROUPS):
        best = _lane_max(gscore)
        gsel = _lane_min(jnp.where(gscore == best, gid_f, big))
        hit = gid_f == gsel
        keep = keep | hit
        gscore = jnp.where(hit, neg, gscore)
    cand = jnp.where(keep & valid, biased, neg)

    onehot = jnp.zeros((tm, LANES), F32)
    idx_out = jnp.zeros((tm, LANES), F32)
    w_out = jnp.zeros((tm, LANES), F32)
    sels = []
    for k in range(TOP_K):
        best = _lane_max(cand)
        sel = _lane_min(jnp.where(cand == best, lane_f, big))
        hit = lane_f == sel
        wk = jnp.sum(jnp.where(hit, scores, 0.0), axis=-1, keepdims=True)
        cand = jnp.where(hit, neg, cand)
        onehot = jnp.where(hit, 1.0, onehot)
        idx_out = jnp.where(lane == k, sel, idx_out)
        w_out = jnp.where(lane == k, wk, w_out)
        sels.append(hit)
    wsum = jnp.sum(w_out, axis=-1, keepdims=True)
    wsel_ref[...] = w_out / wsum * ROUTED_SCALE
    idx_ref[...] = idx_out.astype(jnp.int32)

    row = lax.broadcasted_iota(jnp.int32, (tm, tm), 0)
    col = lax.broadcasted_iota(jnp.int32, (tm, tm), 1)
    lower = (col < row).astype(BF16)
    before = jnp.dot(lower, onehot.astype(BF16), preferred_element_type=F32) + carry_ref[...]
    rank_out = jnp.zeros((tm, LANES), F32)
    for k in range(TOP_K):
        rk = jnp.sum(jnp.where(sels[k], before, 0.0), axis=-1, keepdims=True)
        rank_out = jnp.where(lane == k, rk, rank_out)
    rank_ref[...] = rank_out.astype(jnp.int32)
    carry_ref[...] = carry_ref[...] + jnp.sum(onehot, axis=0, keepdims=True)
    cnt_ref[...] = carry_ref[...]


def _router(x1, g, mod, w_router, router_bias):
    s, d = x1.shape
    tm = 256
    e = w_router.shape[1]
    wr = jnp.pad(w_router, ((0, 0), (0, LANES - e)))
    wr_hi = wr.astype(BF16)
    wr = jnp.stack([wr_hi, (wr - wr_hi.astype(F32)).astype(BF16)])
    rb = jnp.pad(router_bias, (0, LANES - e)).reshape(1, LANES)
    row_spec = lambda w: pl.BlockSpec((tm, w), lambda i: (i, 0))
    return pl.pallas_call(
        _router_kernel,
        grid=(s // tm,),
        in_specs=[row_spec(d),
                  pl.BlockSpec((1, d), lambda i: (0, 0)),
                  pl.BlockSpec((1, N_MOD, d), lambda i: (0, 0, 0)),
                  pl.BlockSpec((2, d, LANES), lambda i: (0, 0, 0)),
                  pl.BlockSpec((1, LANES), lambda i: (0, 0))],
        out_specs=[row_spec(d), pl.BlockSpec((tm * PACK_TILES, LANES), lambda i: (i, 0)),
                   row_spec(LANES), row_spec(LANES), row_spec(LANES),
                   pl.BlockSpec((1, LANES), lambda i: (0, 0))],
        out_shape=[jax.ShapeDtypeStruct((s, d), BF16),
                   jax.ShapeDtypeStruct((s * PACK_TILES, LANES), jnp.uint32),
                   jax.ShapeDtypeStruct((s, LANES), jnp.int32),
                   jax.ShapeDtypeStruct((s, LANES), F32),
                   jax.ShapeDtypeStruct((s, LANES), jnp.int32),
                   jax.ShapeDtypeStruct((1, LANES), F32)],
        scratch_shapes=[pltpu.VMEM((1, LANES), F32)],
        compiler_params=_cparams(("arbitrary",)),
        name="ffn_router",
    )(x1, g.reshape(1, d), mod, wr, rb)


def _unpack_halves(words):
    lo = lax.bitcast_convert_type(words << 16, F32)
    hi = lax.bitcast_convert_type(words & jnp.uint32(0xFFFF0000), F32)
    return lo, hi


def _unpack_rows(words):
    lo, hi = _unpack_halves(words)
    return jnp.concatenate([lo.astype(BF16), hi.astype(BF16)], axis=1)


def _token_rows(row, tiles):
    return pl.ds(pl.multiple_of(row * tiles, tiles), tiles)


def _dispatch_kernel(ps_ref, pe_ref, nused_ref, dest_hbm, hp_ref, xs_hbm,
                     dest_smem, zbuf, sem_i, sem_z, sem_x, *, n_blocks):
    i = pl.program_id(0)
    n = dest_smem.shape[0]
    tm = n // TOP_K
    copy_dest = pltpu.make_async_copy(dest_hbm.at[i], dest_smem, sem_i)
    copy_dest.start()

    @pl.when(i == 0)
    def _():
        zbuf[...] = jnp.zeros_like(zbuf)

        def zero_block(row0):
            span = MOE_BLOCK * PACK_TILES
            return pltpu.make_async_copy(
                zbuf, xs_hbm.at[pl.ds(pl.multiple_of(row0 * PACK_TILES, span), span), :], sem_z)

        def per_expert(action):
            def body(e, carry):
                @pl.when(pe_ref[e] > ps_ref[e])
                def _():
                    action(zero_block(pe_ref[e] - MOE_BLOCK))
                return carry
            lax.fori_loop(0, N_EXPERTS, body, 0)

        def per_tail(action):
            def body(b, carry):
                action(zero_block(b * MOE_BLOCK))
                return carry
            lax.fori_loop(nused_ref[0], n_blocks, body, 0)

        per_expert(lambda cp: cp.start())
        per_tail(lambda cp: cp.start())
        per_expert(lambda cp: cp.wait())
        per_tail(lambda cp: cp.wait())

    copy_dest.wait()

    def issue(t, carry):
        for k in range(TOP_K):
            dest = dest_smem[t * TOP_K + k]
            pltpu.make_async_copy(hp_ref.at[_token_rows(t, PACK_TILES), :],
                                  xs_hbm.at[_token_rows(dest, PACK_TILES), :], sem_x).start(priority=k % 2)
        return carry

    lax.fori_loop(0, tm, issue, 0)
    for _ in range(TOP_K):
        pltpu.make_async_copy(hp_ref, xs_hbm.at[pl.ds(0, tm * PACK_TILES), :], sem_x).wait()


def _dispatch(pad_start, pad_end, n_used, dest_tm, hp, n_blocks):
    n_tiles, n = dest_tm.shape
    tm = n // TOP_K
    grid_spec = pltpu.PrefetchScalarGridSpec(
        num_scalar_prefetch=3,
        grid=(n_tiles,),
        in_specs=[pl.BlockSpec(memory_space=pl.ANY),
                  pl.BlockSpec((tm * PACK_TILES, LANES), lambda i, ps, pe, nu: (i, 0))],
        out_specs=pl.BlockSpec(memory_space=pl.ANY),
        scratch_shapes=[pltpu.SMEM((n,), jnp.int32),
                        pltpu.VMEM((MOE_BLOCK * PACK_TILES, LANES), jnp.uint32),
                        pltpu.SemaphoreType.DMA,
                        pltpu.SemaphoreType.DMA,
                        pltpu.SemaphoreType.DMA])
    return pl.pallas_call(
        functools.partial(_dispatch_kernel, n_blocks=n_blocks),
        grid_spec=grid_spec,
        out_shape=jax.ShapeDtypeStruct((n_blocks * MOE_BLOCK * PACK_TILES, LANES), jnp.uint32),
        compiler_params=_cparams(("arbitrary",)),
        name="moe_dispatch",
    )(pad_start, pad_end, n_used, dest_tm, hp)


def _moe_kernel(be_ref, first_ref, nxt_ref, slot_ref, nused_ref, x_ref, wg_hbm, wu_hbm, wd_hbm, o_ref,
                wg32, wu32, wd32, wgb, wub, wdb, sems):
    b = pl.program_id(0)

    def fetch(e, slot):
        return (pltpu.make_async_copy(wg_hbm.at[e], wg32.at[slot], sems.at[slot, 0]),
                pltpu.make_async_copy(wu_hbm.at[e], wu32.at[slot], sems.at[slot, 1]),
                pltpu.make_async_copy(wd_hbm.at[e], wd32.at[slot], sems.at[slot, 2]))

    @pl.when(b < nused_ref[0])
    def _():
        @pl.when(first_ref[b] == 1)
        def _():
            slot = slot_ref[b]

            @pl.when(b == 0)
            def _():
                for cp in fetch(be_ref[0], 0):
                    cp.start(priority=1)

            for cp in fetch(be_ref[b], slot):
                cp.wait()

            @pl.when(nxt_ref[b] >= 0)
            def _():
                for cp in fetch(nxt_ref[b], 1 - slot):
                    cp.start(priority=1)

            wgb[...] = wg32[slot].astype(BF16)
            wub[...] = wu32[slot].astype(BF16)
            wdb[...] = wd32[slot].astype(BF16)

        words = jnp.concatenate(
            [x_ref[pl.ds(c, MOE_BLOCK, stride=PACK_TILES), :] for c in range(PACK_TILES)], axis=1)
        x = _unpack_rows(words)
        gate = jnp.dot(x, wgb[...], preferred_element_type=F32)
        up = jnp.dot(x, wub[...], preferred_element_type=F32)
        act = (gate * jax.nn.sigmoid(gate) * up).astype(BF16)
        _store_packed_tiles(o_ref, jnp.dot(act, wdb[...], preferred_element_type=F32))

    @pl.when(b >= nused_ref[0])
    def _():
        o_ref[...] = jnp.zeros_like(o_ref)


def _moe(block_e, first, nxt, slot, n_used, xs, weg, weu, wed):
    n_blocks = block_e.shape[0]
    _, d, f = weg.shape
    grid_spec = pltpu.PrefetchScalarGridSpec(
        num_scalar_prefetch=5,
        grid=(n_blocks,),
        in_specs=[pl.BlockSpec((MOE_BLOCK * PACK_TILES, LANES), lambda b, *_: (b, 0)),
                  pl.BlockSpec(memory_space=pl.ANY),
                  pl.BlockSpec(memory_space=pl.ANY),
                  pl.BlockSpec(memory_space=pl.ANY)],
        out_specs=pl.BlockSpec((MOE_BLOCK * PACK_TILES, LANES), lambda b, *_: (b, 0)),
        scratch_shapes=[pltpu.VMEM((2, d, f), F32), pltpu.VMEM((2, d, f), F32), pltpu.VMEM((2, f, d), F32),
                        pltpu.VMEM((d, f), BF16), pltpu.VMEM((d, f), BF16), pltpu.VMEM((f, d), BF16),
                        pltpu.SemaphoreType.DMA((2, 3))])
    return pl.pallas_call(
        _moe_kernel,
        grid_spec=grid_spec,
        out_shape=jax.ShapeDtypeStruct((n_blocks * MOE_BLOCK * PACK_TILES, LANES), jnp.uint32),
        compiler_params=_cparams(("arbitrary",)),
        name="moe_experts",
    )(block_e, first, nxt, slot, n_used, xs, weg, weu, wed)


def _combine_kernel(dest_hbm, y_hbm, w_ref, h_ref, wg_ref, wu_ref, wd_ref, x_ref, gate_ref, o_ref,
                    dest0, dest1, ybuf0, ybuf1, sem_i, sem_y):
    i = pl.program_id(0)
    tm = h_ref.shape[0]
    n = tm * TOP_K
    span = n * PACK_TILES
    dests = (dest0, dest1)
    ybufs = (ybuf0, ybuf1)

    def start_tile(tile, slot):
        copy_dest = pltpu.make_async_copy(dest_hbm.at[tile], dests[slot], sem_i)
        copy_dest.start()
        copy_dest.wait()

        def issue(t, carry):
            base = t * TOKEN_PITCH
            for k in range(TOP_K):
                src = dests[slot][t * TOP_K + k]
                pltpu.make_async_copy(
                    y_hbm.at[_token_rows(src, PACK_TILES), :],
                    ybufs[slot].at[pl.ds(pl.multiple_of(base + k * YBUF_PITCH, 4), PACK_TILES), :],
                    sem_y.at[slot]).start(priority=k % 2)
            return carry

        lax.fori_loop(0, tm, issue, 0)

    def step(cur):
        nxt = 1 - cur
        pl.when(i == 0)(lambda: start_tile(0, cur))
        pl.when(i + 1 < pl.num_programs(0))(lambda: start_tile(i + 1, nxt))

        h = h_ref[...]
        gate = jnp.dot(h, wg_ref[...], preferred_element_type=F32)
        up = jnp.dot(h, wu_ref[...], preferred_element_type=F32)
        act = (gate * jax.nn.sigmoid(gate) * up).astype(BF16)
        shared = jnp.dot(act, wd_ref[...], preferred_element_type=F32)

        yb = ybufs[cur]
        pltpu.make_async_copy(y_hbm.at[pl.ds(0, span), :], yb.at[pl.ds(0, span), :], sem_y.at[cur]).wait()
        lows, highs = [], []
        for c in range(PACK_TILES):
            acc_lo = acc_hi = None
            for k in range(TOP_K):
                lo, hi = _unpack_halves(yb[pl.ds(k * YBUF_PITCH + c, tm, stride=TOKEN_PITCH), :])
                wk = w_ref[:, k:k + 1]
                acc_lo = lo * wk if acc_lo is None else acc_lo + lo * wk
                acc_hi = hi * wk if acc_hi is None else acc_hi + hi * wk
            lows.append(acc_lo)
            highs.append(acc_hi)
        routed = jnp.concatenate(lows + highs, axis=1)
        o_ref[...] = x_ref[...] + gate_ref[...] * (routed + shared)

    pl.when(i % 2 == 0)(lambda: step(0))
    pl.when(i % 2 == 1)(lambda: step(1))


def _combine(dest_tm, y_sorted, wsel, h2, wsg, wsu, wsd, x1, gate):
    s, d = x1.shape
    n_tiles, n = dest_tm.shape
    tm = n // TOP_K
    f = wsg.shape[1]
    return pl.pallas_call(
        _combine_kernel,
        grid=(n_tiles,),
        in_specs=[pl.BlockSpec(memory_space=pl.ANY),
                  pl.BlockSpec(memory_space=pl.ANY),
                  pl.BlockSpec((tm, LANES), lambda i: (i, 0)),
                  pl.BlockSpec((tm, d), lambda i: (i, 0)),
                  pl.BlockSpec((d, f), lambda i: (0, 0)),
                  pl.BlockSpec((d, f), lambda i: (0, 0)),
                  pl.BlockSpec((f, d), lambda i: (0, 0)),
                  pl.BlockSpec((tm, d), lambda i: (i, 0)),
                  pl.BlockSpec((1, d), lambda i: (0, 0))],
        out_specs=pl.BlockSpec((tm, d), lambda i: (i, 0)),
        out_shape=jax.ShapeDtypeStruct((s, d), F32),
        scratch_shapes=[pltpu.SMEM((n,), jnp.int32),
                        pltpu.SMEM((n,), jnp.int32),
                        pltpu.VMEM((tm * TOKEN_PITCH, LANES), jnp.uint32),
                        pltpu.VMEM((tm * TOKEN_PITCH, LANES), jnp.uint32),
                        pltpu.SemaphoreType.DMA,
                        pltpu.SemaphoreType.DMA((2,))],
        compiler_params=_cparams(("arbitrary",)),
        name="moe_combine",
    )(dest_tm, y_sorted, wsel, h2, wsg, wsu, wsd, x1, gate)


def _block_tables(counts, n_blocks):
    padded = (counts + MOE_BLOCK - 1) // MOE_BLOCK * MOE_BLOCK
    pad_end = jnp.cumsum(padded).astype(jnp.int32)
    pad_start = pad_end - padded
    blk = jnp.arange(n_blocks, dtype=jnp.int32)
    block_e = jnp.minimum(jnp.sum(pad_end[None, :] <= (blk * MOE_BLOCK)[:, None], axis=1),
                          N_EXPERTS - 1).astype(jnp.int32)
    n_used = pad_end[-1] // MOE_BLOCK
    prev = jnp.concatenate([jnp.full((1,), -1, jnp.int32), block_e[:-1]])
    first = ((blk < n_used) & (block_e != prev)).astype(jnp.int32)
    slot = jnp.maximum(jnp.cumsum(first) - 1, 0).astype(jnp.int32) % 2
    after = pad_end[block_e] // MOE_BLOCK
    nxt = jnp.where(after < n_used, block_e[jnp.minimum(after, n_blocks - 1)], -1).astype(jnp.int32)
    return pad_start, pad_end, block_e, first, nxt, slot, n_used.reshape(1)


def kernel(x, c, ctx, c_ctx, w_ada, b_ada, norm_mix, norm_ffn, w_in, q_norm_a, k_norm_a, q_norm_b, k_norm_b, lambda_q1, lambda_k1, lambda_q2, lambda_k2, subln_b, w_branch_a, w_branch_b, w_out, w_router, router_bias, w_exp_gate, w_exp_up, w_exp_down, w_sh_gate, w_sh_up, w_sh_down):
    depth = w_ada.shape[0]
    assert depth == 1 and x.shape[0] == 1 and ctx.shape[0] == 1
    s, d = x.shape[1], x.shape[2]
    n_ctx = ctx.shape[1]
    i = 0
    lam_init = 0.8 - 0.6 * math.exp(-0.3 * i)
    xs = x[0]

    mod = _adaln(jnp.concatenate([c, c_ctx[None, :]], axis=0), w_ada[i], b_ada[i]).reshape(2, N_MOD, d)

    h = _prenorm(xs, ctx[0], norm_mix[i], mod)
    tc, tsa, tsb = _rope_tables(s, n_ctx)
    gains = _head_gains(q_norm_a[i], k_norm_a[i], q_norm_b[i], k_norm_b[i])
    proj = _inproj(h, w_in[i], gains, tc, tsa, tsb)
    oa = _gqa(proj, s)
    lam_vecs = jnp.stack([lambda_q1[i], lambda_k1[i], lambda_q2[i], lambda_k2[i]]).astype(F32)
    ob = _diff(proj, s, lam_vecs, subln_b[i], lam_init)
    x1 = _merge(oa, ob, proj, w_branch_a[i].astype(BF16), w_branch_b[i].astype(BF16),
                w_out[i].astype(BF16), xs, mod[0, 2:3, :])

    h2, h2p, idx, wsel, rank, cnt = _router(x1, norm_ffn[i], mod[0:1], w_router[i], router_bias[i])
    counts = cnt[0, :N_EXPERTS].astype(jnp.int32)
    n_blocks = -(-(s * TOP_K) // MOE_BLOCK) + N_EXPERTS
    pad_start, pad_end, block_e, first, nxt, slot, n_used = _block_tables(counts, n_blocks)
    tm_r = 128
    e_ids = jnp.arange(N_EXPERTS, dtype=jnp.int32)
    starts = jnp.sum(jnp.where(idx[:, :TOP_K, None] == e_ids, pad_start, 0), axis=-1)
    dest_tm = (starts + rank[:, :TOP_K]).astype(jnp.int32).reshape(s // tm_r, tm_r * TOP_K)
    xs = _dispatch(pad_start, pad_end, n_used, dest_tm, h2p, n_blocks)
    y_sorted = _moe(block_e, first, nxt, slot, n_used, xs, w_exp_gate[i], w_exp_up[i], w_exp_down[i])
    out = _combine(dest_tm, y_sorted, wsel, h2,
                   w_sh_gate[i].astype(BF16), w_sh_up[i].astype(BF16), w_sh_down[i].astype(BF16),
                   x1, mod[0, 5:6, :])
    return out[None]
```

```python
import functools
import math

import jax
import jax.numpy as jnp
from jax import lax
from jax.experimental import pallas as pl
from jax.experimental.pallas import tpu as pltpu

F32 = jnp.float32
BF16 = jnp.bfloat16

D_MODEL = 2048
GRID_W = 64
HEAD_DIM = 128
ROPE_PAIRS = HEAD_DIM // 4
ROPE_THETA = 10000.0
A_HEADS = 8
A_KV_HEADS = 2
A_GROUP = A_HEADS // A_KV_HEADS
B_HEADS = 4
B_V_DIM = 2 * HEAD_DIM
N_EXPERTS = 64
TOP_K = 8
N_GROUPS = 8
TOPK_GROUPS = 4
EXPERT_DIM = 512
SHARED_DIM = 512
ROUTED_SCALE = 2.5
N_MOD = 6
EPS = 1e-6

A_Q_W = A_HEADS * HEAD_DIM
A_KV_W = A_KV_HEADS * HEAD_DIM
B_QK_W = B_HEADS * 2 * HEAD_DIM
B_V_W = B_HEADS * B_V_DIM
IN_W = A_Q_W + 2 * A_KV_W + 2 * B_QK_W + B_V_W + 2 * D_MODEL

COL_AQ = 0
COL_AK = A_Q_W // HEAD_DIM
COL_AV = COL_AK + A_KV_HEADS
COL_BQ = COL_AV + A_KV_HEADS
COL_BK = COL_BQ + 2 * B_HEADS
COL_BV = COL_BK + 2 * B_HEADS
COL_GA = COL_BV + B_V_W // HEAD_DIM
COL_GB = COL_GA + D_MODEL // HEAD_DIM

LANES = 128
SUBLANES = 8
VMEM_LIMIT = 56 * 1024 * 1024

PROJ_TN = 512
MOE_BLOCK = 256
ONES_ROWS = 16
PACK_TILES = D_MODEL // 2 // LANES
YBUF_PITCH = PACK_TILES + 4
TOKEN_PITCH = TOP_K * YBUF_PITCH + 4
LOG2E = 1.4426950408889634


def _cparams(sem, vmem=VMEM_LIMIT):
    return pltpu.CompilerParams(dimension_semantics=sem, vmem_limit_bytes=vmem)


def _adaln_kernel(cb_ref, w_ref, b_ref, o_ref):
    tn = w_ref.shape[1]
    nl = tn // LANES
    rows = 32

    def body(g, accs):
        accs = list(accs)
        r0 = pl.multiple_of(g * rows, rows)
        for u in range(rows // SUBLANES):
            r = r0 + u * SUBLANES
            w = w_ref[pl.ds(r, SUBLANES), :]
            for v in range(2):
                c = cb_ref[v, pl.ds(r, SUBLANES), :]
                s = c * jax.nn.sigmoid(c)
                for j in range(nl):
                    accs[v * nl + j] = accs[v * nl + j] + w[:, j * LANES:(j + 1) * LANES] * s
        return tuple(accs)

    init = tuple(jnp.zeros((SUBLANES, LANES), F32) for _ in range(2 * nl))
    accs = lax.fori_loop(0, w_ref.shape[0] // rows, body, init)
    for v in range(2):
        row = jnp.concatenate(
            [jnp.sum(accs[v * nl + j], axis=0, keepdims=True) for j in range(nl)], axis=1)
        o_ref[v:v + 1, :] = row + b_ref[...]


def _adaln(cvecs, w, b):
    d, n = w.shape
    tn = 1536
    cb = jnp.broadcast_to(cvecs[:, :, None], (2, d, LANES))
    return pl.pallas_call(
        _adaln_kernel,
        grid=(n // tn,),
        in_specs=[pl.BlockSpec((2, d, LANES), lambda j: (0, 0, 0)),
                  pl.BlockSpec((d, tn), lambda j: (0, j)),
                  pl.BlockSpec((1, tn), lambda j: (0, j))],
        out_specs=pl.BlockSpec((2, tn), lambda j: (0, j)),
        out_shape=jax.ShapeDtypeStruct((2, n), F32),
        compiler_params=_cparams(("arbitrary",)),
        name="adaln",
    )(cb, w, b.reshape(1, n))


def _rms_mod(x, g, shift, scale):
    y = x * lax.rsqrt(jnp.mean(x * x, axis=-1, keepdims=True) + EPS) * g
    return y * (1.0 + scale) + shift


def _prenorm_kernel(x_ref, c_ref, g_ref, mod_ref, o_ref, *, n_lat_tiles):
    is_ctx = pl.program_id(0) >= n_lat_tiles
    x = jnp.where(is_ctx, c_ref[...], x_ref[...])
    o_ref[...] = _rms_mod(x, g_ref[...], mod_ref[0, 0:1, :], mod_ref[0, 1:2, :]).astype(o_ref.dtype)


def _prenorm(x, ctx, g, mod):
    s, d = x.shape
    c = ctx.shape[0]
    tm = 256
    nl, nc = s // tm, c // tm
    return pl.pallas_call(
        functools.partial(_prenorm_kernel, n_lat_tiles=nl),
        grid=(nl + nc,),
        in_specs=[pl.BlockSpec((tm, d), lambda i: (jnp.minimum(i, nl - 1), 0)),
                  pl.BlockSpec((tm, d), lambda i: (jnp.maximum(i - nl, 0), 0)),
                  pl.BlockSpec((1, d), lambda i: (0, 0)),
                  pl.BlockSpec((1, N_MOD, d), lambda i: (i // nl, 0, 0))],
        out_specs=pl.BlockSpec((tm, d), lambda i: (i, 0)),
        out_shape=jax.ShapeDtypeStruct((s + c, d), BF16),
        compiler_params=_cparams(("arbitrary",)),
        name="prenorm_mix",
    )(x, ctx, g.reshape(1, d), mod)


def _inproj_kernel(h_ref, w_ref, gain_ref, c_ref, sa_ref, sb_ref, o_ref):
    j = pl.program_id(1)
    acc = jnp.dot(h_ref[...], w_ref[...].astype(h_ref.dtype), preferred_element_type=F32)
    nh = acc.shape[1] // HEAD_DIM

    def norm_rope(a, gain):
        y = a * lax.rsqrt(jnp.mean(a * a, axis=-1, keepdims=True) + EPS) * gain
        return (y * c_ref[...] + pltpu.roll(y, ROPE_PAIRS, 1) * sa_ref[...]
                + pltpu.roll(y, HEAD_DIM - ROPE_PAIRS, 1) * sb_ref[...])

    def store(n_normed):
        for hd in range(nh):
            sl = slice(hd * HEAD_DIM, (hd + 1) * HEAD_DIM)
            a = acc[:, sl]
            if hd < n_normed:
                a = norm_rope(a, gain_ref[0, :, sl])
            o_ref[:, sl] = a.astype(o_ref.dtype)

    all_normed = (j < 2) | ((j >= 3) & (j < 7))
    pl.when(all_normed)(lambda: store(nh))
    pl.when(j == 2)(lambda: store(A_KV_HEADS))
    pl.when(j >= 7)(lambda: store(0))


def _inproj(h, w, gains, rope_c, rope_sa, rope_sb):
    t, d = h.shape
    n = w.shape[1]
    tm = t // 8
    tn = PROJ_TN
    return pl.pallas_call(
        _inproj_kernel,
        grid=(t // tm, n // tn),
        in_specs=[pl.BlockSpec((tm, d), lambda i, j: (i, 0)),
                  pl.BlockSpec((d, tn), lambda i, j: (0, j)),
                  pl.BlockSpec((1, 1, tn), lambda i, j: (j, 0, 0)),
                  pl.BlockSpec((tm, HEAD_DIM), lambda i, j: (i, 0)),
                  pl.BlockSpec((tm, HEAD_DIM), lambda i, j: (i, 0)),
                  pl.BlockSpec((tm, HEAD_DIM), lambda i, j: (i, 0))],
        out_specs=pl.BlockSpec((tm, tn), lambda i, j: (i, j)),
        out_shape=jax.ShapeDtypeStruct((t, n), BF16),
        compiler_params=_cparams(("arbitrary", "arbitrary")),
        name="inproj",
    )(h, w, gains, rope_c, rope_sa, rope_sb)


def _rope_tables(s, c):
    rows_n = s // GRID_W
    inv = ROPE_THETA ** (-jnp.arange(ROPE_PAIRS, dtype=F32) / ROPE_PAIRS)
    ang_r = jnp.arange(rows_n, dtype=F32)[:, None] * inv
    ang_c = jnp.arange(GRID_W, dtype=F32)[:, None] * inv
    cr, sr, cc, sc = jnp.cos(ang_r), jnp.sin(ang_r), jnp.cos(ang_c), jnp.sin(ang_c)
    zr, zc = jnp.zeros_like(sr), jnp.zeros_like(sc)

    def table(row_parts, col_parts, ctx_value):
        by_row = jnp.concatenate(row_parts + [zr, zr], axis=1)
        by_col = jnp.concatenate([zc, zc] + col_parts, axis=1)
        lat = (by_row[:, None, :] + by_col[None, :, :]).reshape(s, HEAD_DIM)
        return jnp.concatenate([lat, jnp.full((c, HEAD_DIM), ctx_value, F32)], axis=0)

    return (table([cr, cr], [cc, cc], 1.0), table([zr, sr], [zc, sc], 0.0), table([-sr, zr], [-sc, zc], 0.0))


def _head_gains(qn_a, kn_a, qn_b, kn_b):
    qs = HEAD_DIM ** -0.5 * LOG2E
    one = jnp.ones((HEAD_DIM,), F32)
    heads = ([qn_a * qs] * A_HEADS + [kn_a] * A_KV_HEADS + [one] * A_KV_HEADS
             + [qn_b * qs] * (2 * B_HEADS) + [kn_b] * (2 * B_HEADS))
    heads = heads + [one] * (IN_W // HEAD_DIM - len(heads))
    return jnp.concatenate(heads).reshape(IN_W // PROJ_TN, 1, PROJ_TN)


def _build_vt(v_ref, vt_ref, tk):
    n_chunks, rows, _ = vt_ref.shape
    dv = rows - ONES_ROWS
    tail = (lax.broadcasted_iota(jnp.int32, (ONES_ROWS, tk), 0) == 0).astype(vt_ref.dtype)
    for c in range(n_chunks):
        vt_ref[c, 0:dv, :] = v_ref[c * tk:(c + 1) * tk, :].astype(F32).T.astype(vt_ref.dtype)
        vt_ref[c, dv:rows, :] = tail


def _attend(qs, k_refs, vt_ref, s_ref, m_ref, acc_ref):
    n_chunks, _, tk = vt_ref.shape
    n_st = len(qs)
    m_ref[...] = jnp.full(m_ref.shape, -jnp.inf, F32)
    acc_ref[...] = jnp.zeros(acc_ref.shape, F32)

    def scores(i, c, slot):
        off = c * tk if isinstance(c, int) else pl.multiple_of(c * tk, tk)
        s_ref[i, slot] = lax.dot_general(k_refs[i][pl.ds(off, tk), :], qs[i], (((1,), (1,)), ((), ())),
                                         preferred_element_type=F32)

    def update(i, c, slot):
        s = s_ref[i, slot]
        m_old = m_ref[i]
        m_new = jnp.maximum(m_old, jnp.max(s, axis=0, keepdims=True))
        p = jnp.exp2(s - m_new).astype(vt_ref.dtype)
        acc_ref[i] = (acc_ref[i] * jnp.exp2(m_old - m_new)
                      + jnp.dot(vt_ref[c], p, preferred_element_type=F32))
        m_ref[i] = m_new

    for i in range(n_st):
        scores(i, 0, 0)

    def pair(j, carry):
        c = 2 * j
        for i in range(n_st):
            scores(i, c + 1, 1)
        for i in range(n_st):
            update(i, c, 0)
        for i in range(n_st):
            scores(i, c + 2, 0)
        for i in range(n_st):
            update(i, c + 1, 1)
        return carry

    n_pairs = (n_chunks - 1) // 2
    lax.fori_loop(0, n_pairs, pair, 0)
    done = 2 * n_pairs
    if n_chunks - done == 2:
        for i in range(n_st):
            scores(i, done + 1, 1)
    for i in range(n_st):
        update(i, done, 0)
    if n_chunks - done == 2:
        for i in range(n_st):
            update(i, done + 1, 1)


def _gqa_kernel(q_ref, k_ref, v_ref, o_ref, vt_ref, s_ref, m_ref, acc_ref):
    tk = vt_ref.shape[2]
    n_st = s_ref.shape[0]
    pl.when((pl.program_id(1) == 0) & (pl.program_id(2) == 0))(lambda: _build_vt(v_ref, vt_ref, tk))
    qs = [q_ref[:, i * HEAD_DIM:(i + 1) * HEAD_DIM] for i in range(n_st)]
    _attend(qs, [k_ref] * n_st, vt_ref, s_ref, m_ref, acc_ref)
    for i in range(n_st):
        o_t = acc_ref[i, 0:HEAD_DIM, :] / acc_ref[i, HEAD_DIM:HEAD_DIM + 1, :]
        o_ref[:, i * HEAD_DIM:(i + 1) * HEAD_DIM] = o_t.T.astype(o_ref.dtype)


def _key_chunk(t):
    for tk in (1408, 768, 1024, 512, 640, 384, 256, 128):
        if t % tk == 0:
            return tk
    raise ValueError(f"unsupported key count {t}")


def _gqa(proj, s):
    t = proj.shape[0]
    tq = 512
    tk = _key_chunk(t)
    n_st = 4
    per_g = A_GROUP // n_st
    return pl.pallas_call(
        _gqa_kernel,
        grid=(A_KV_HEADS, per_g, s // tq),
        in_specs=[pl.BlockSpec((tq, n_st * HEAD_DIM), lambda g, hh, i: (i, COL_AQ // n_st + g * per_g + hh)),
                  pl.BlockSpec((t, HEAD_DIM), lambda g, hh, i: (0, COL_AK + g)),
                  pl.BlockSpec((t, HEAD_DIM), lambda g, hh, i: (0, COL_AV + g))],
        out_specs=pl.BlockSpec((tq, n_st * HEAD_DIM), lambda g, hh, i: (i, g * per_g + hh)),
        out_shape=jax.ShapeDtypeStruct((s, A_Q_W), BF16),
        scratch_shapes=[pltpu.VMEM((t // tk, HEAD_DIM + ONES_ROWS, tk), BF16),
                        pltpu.VMEM((n_st, 2, tk, tq), F32),
                        pltpu.VMEM((n_st, 1, tq), F32),
                        pltpu.VMEM((n_st, HEAD_DIM + ONES_ROWS, tq), F32)],
        compiler_params=_cparams(("arbitrary", "arbitrary", "arbitrary")),
        name="gqa_attn",
    )(proj, proj, proj)


def _diff_kernel(lam_ref, q0_ref, q1_ref, k0_ref, k1_ref, v_ref, g_ref, o_ref,
                 vt_ref, s_ref, m_ref, acc_ref, *, lam_init):
    tk = vt_ref.shape[2]
    pl.when(pl.program_id(1) == 0)(lambda: _build_vt(v_ref, vt_ref, tk))
    lv = lam_ref[...]
    lam = (jnp.exp(jnp.sum(lv[0:1, :] * lv[1:2, :], axis=-1, keepdims=True))
           - jnp.exp(jnp.sum(lv[2:3, :] * lv[3:4, :], axis=-1, keepdims=True)) + lam_init)
    _attend([q0_ref[...], q1_ref[...]], [k0_ref, k1_ref], vt_ref, s_ref, m_ref, acc_ref)
    o_t = (acc_ref[0, 0:B_V_DIM, :] / acc_ref[0, B_V_DIM:B_V_DIM + 1, :]
           - lam * (acc_ref[1, 0:B_V_DIM, :] / acc_ref[1, B_V_DIM:B_V_DIM + 1, :]))
    o = o_t.T
    y = o * lax.rsqrt(jnp.mean(o * o, axis=-1, keepdims=True) + EPS) * g_ref[...]
    o_ref[...] = (y * (1.0 - lam_init)).astype(o_ref.dtype)


def _diff(proj, s, lam_vecs, subln_g, lam_init):
    t = proj.shape[0]
    tq = 512
    tk = _key_chunk(t)
    vb = B_V_DIM // HEAD_DIM
    return pl.pallas_call(
        functools.partial(_diff_kernel, lam_init=lam_init),
        grid=(B_HEADS, s // tq),
        in_specs=[pl.BlockSpec((4, HEAD_DIM), lambda h, i: (0, 0)),
                  pl.BlockSpec((tq, HEAD_DIM), lambda h, i: (i, COL_BQ + 2 * h)),
                  pl.BlockSpec((tq, HEAD_DIM), lambda h, i: (i, COL_BQ + 2 * h + 1)),
                  pl.BlockSpec((t, HEAD_DIM), lambda h, i: (0, COL_BK + 2 * h)),
                  pl.BlockSpec((t, HEAD_DIM), lambda h, i: (0, COL_BK + 2 * h + 1)),
                  pl.BlockSpec((t, B_V_DIM), lambda h, i: (0, COL_BV // vb + h)),
                  pl.BlockSpec((1, B_V_DIM), lambda h, i: (0, 0))],
        out_specs=pl.BlockSpec((tq, B_V_DIM), lambda h, i: (i, h)),
        out_shape=jax.ShapeDtypeStruct((s, B_V_W), BF16),
        scratch_shapes=[pltpu.VMEM((t // tk, B_V_DIM + ONES_ROWS, tk), BF16),
                        pltpu.VMEM((2, 2, tk, tq), F32),
                        pltpu.VMEM((2, 1, tq), F32),
                        pltpu.VMEM((2, B_V_DIM + ONES_ROWS, tq), F32)],
        compiler_params=_cparams(("arbitrary", "arbitrary")),
        name="diff_attn",
    )(lam_vecs, proj, proj, proj, proj, proj, subln_g.reshape(1, B_V_DIM))


def _merge_kernel(oa_ref, ob_ref, ga_ref, gb_ref, wa_ref, wb_ref, wo_ref, x_ref, gate_ref, o_ref, t_ref):
    c = pl.program_id(1)
    nc, _, tn = t_ref.shape
    ya = jnp.dot(oa_ref[...], wa_ref[...], preferred_element_type=F32)
    yb = jnp.dot(ob_ref[...], wb_ref[...], preferred_element_type=F32)
    t = (jax.nn.sigmoid(ga_ref[...].astype(F32)) * ya + jax.nn.sigmoid(gb_ref[...].astype(F32)) * yb)
    t_ref[c] = t.astype(t_ref.dtype)

    @pl.when(c == nc - 1)
    def _():
        y = jnp.dot(t_ref[0], wo_ref[0:tn, :], preferred_element_type=F32)
        for cc in range(1, nc):
            y = y + jnp.dot(t_ref[cc], wo_ref[cc * tn:(cc + 1) * tn, :], preferred_element_type=F32)
        o_ref[...] = x_ref[...] + gate_ref[...] * y


def _merge(oa, ob, proj, wa, wb, wo, x, gate):
    s, d = x.shape
    tm = 512
    tn = PROJ_TN
    nc = d // tn
    ga0 = COL_GA * HEAD_DIM // tn
    gb0 = COL_GB * HEAD_DIM // tn
    return pl.pallas_call(
        _merge_kernel,
        grid=(s // tm, nc),
        in_specs=[pl.BlockSpec((tm, A_Q_W), lambda i, c: (i, 0)),
                  pl.BlockSpec((tm, B_V_W), lambda i, c: (i, 0)),
                  pl.BlockSpec((tm, tn), lambda i, c: (i, ga0 + c)),
                  pl.BlockSpec((tm, tn), lambda i, c: (i, gb0 + c)),
                  pl.BlockSpec((A_Q_W, tn), lambda i, c: (0, c)),
                  pl.BlockSpec((B_V_W, tn), lambda i, c: (0, c)),
                  pl.BlockSpec((d, d), lambda i, c: (0, 0)),
                  pl.BlockSpec((tm, d), lambda i, c: (i, 0)),
                  pl.BlockSpec((1, d), lambda i, c: (0, 0))],
        out_specs=pl.BlockSpec((tm, d), lambda i, c: (i, 0)),
        out_shape=jax.ShapeDtypeStruct((s, d), F32),
        scratch_shapes=[pltpu.VMEM((nc, tm, tn), BF16)],
        compiler_params=_cparams(("arbitrary", "arbitrary")),
        name="merge_out",
    )(oa, ob, proj, proj, wa, wb, wo, x, gate)


def _store_packed_tiles(ref, x):
    rows = x.shape[0]
    bits = lax.bitcast_convert_type(x.astype(BF16).astype(F32), jnp.uint32)
    half = bits.shape[1] // 2
    words = (bits[:, :half] >> 16) | (bits[:, half:] & jnp.uint32(0xFFFF0000))
    for c in range(PACK_TILES):
        ref[pl.ds(c, rows, stride=PACK_TILES), :] = words[:, c * LANES:(c + 1) * LANES]


def _lane_max(x):
    return jnp.max(x, axis=-1, keepdims=True)


def _lane_min(x):
    return jnp.min(x, axis=-1, keepdims=True)


def _group_allreduce(x, lane, op):
    for sft in (1, 2, 4):
        up = pltpu.roll(x, sft, 1)
        dn = pltpu.roll(x, LANES - sft, 1)
        x = op(x, jnp.where((lane & sft) != 0, up, dn))
    return x


def _router_kernel(x_ref, g_ref, mod_ref, wr_ref, rb_ref,
                   h_ref, hp_ref, idx_ref, wsel_ref, rank_ref, cnt_ref, carry_ref):
    @pl.when(pl.program_id(0) == 0)
    def _():
        carry_ref[...] = jnp.zeros_like(carry_ref)

    h = _rms_mod(x_ref[...], g_ref[...], mod_ref[0, 3:4, :], mod_ref[0, 4:5, :])
    hb = h.astype(BF16)
    h_ref[...] = hb
    _store_packed_tiles(hp_ref, h)
    tm = h.shape[0]

    h_lo = (h - hb.astype(F32)).astype(BF16)
    logits = (jnp.dot(hb, wr_ref[0], preferred_element_type=F32)
              + jnp.dot(hb, wr_ref[1], preferred_element_type=F32)
              + jnp.dot(h_lo, wr_ref[0], preferred_element_type=F32))
    scores = jax.nn.sigmoid(logits)
    lane = lax.broadcasted_iota(jnp.int32, (tm, LANES), 1)
    lane_f = lane.astype(F32)
    gid_f = (lane >> 3).astype(F32)
    valid = lane < N_EXPERTS
    neg = jnp.float32(-jnp.inf)
    big = jnp.float32(LANES)
    biased = jnp.where(valid, scores + rb_ref[...], neg)

    m1 = _group_allreduce(biased, lane, jnp.maximum)
    a1 = _group_allreduce(jnp.where(biased == m1, lane_f, big), lane, jnp.minimum)
    m2 = _group_allreduce(jnp.where(lane_f == a1, neg, biased), lane, jnp.maximum)
    gscore = jnp.where(valid, m1 + m2, neg)
    keep = jnp.zeros((tm, LANES), jnp.bool_)
    for _ in range(TOPK_GROUPS):
        best = _lane_max(gscore)
        gsel = _lane_min(jnp.where(gscore == best, gid_f, big))
        hit = gid_f == gsel
        keep = keep | hit
        gscore = jnp.where(hit, neg, gscore)
    cand = jnp.where(keep & valid, biased, neg)

    onehot = jnp.zeros((tm, LANES), F32)
    idx_out = jnp.zeros((tm, LANES), F32)
    w_out = jnp.zeros((tm, LANES), F32)
    sels = []
    for k in range(TOP_K):
        best = _lane_max(cand)
        sel = _lane_min(jnp.where(cand == best, lane_f, big))
        hit = lane_f == sel
        wk = jnp.sum(jnp.where(hit, scores, 0.0), axis=-1, keepdims=True)
        cand = jnp.where(hit, neg, cand)
        onehot = jnp.where(hit, 1.0, onehot)
        idx_out = jnp.where(lane == k, sel, idx_out)
        w_out = jnp.where(lane == k, wk, w_out)
        sels.append(hit)
    wsum = jnp.sum(w_out, axis=-1, keepdims=True)
    wsel_ref[...] = w_out / wsum * ROUTED_SCALE
    idx_ref[...] = idx_out.astype(jnp.int32)

    row = lax.broadcasted_iota(jnp.int32, (tm, tm), 0)
    col = lax.broadcasted_iota(jnp.int32, (tm, tm), 1)
    lower = (col < row).astype(BF16)
    before = jnp.dot(lower, onehot.astype(BF16), preferred_element_type=F32) + carry_ref[...]
    rank_out = jnp.zeros((tm, LANES), F32)
    for k in range(TOP_K):
        rk = jnp.sum(jnp.where(sels[k], before, 0.0), axis=-1, keepdims=True)
        rank_out = jnp.where(lane == k, rk, rank_out)
    rank_ref[...] = rank_out.astype(jnp.int32)
    carry_ref[...] = carry_ref[...] + jnp.sum(onehot, axis=0, keepdims=True)
    cnt_ref[...] = carry_ref[...]


def _router(x1, g, mod, w_router, router_bias):
    s, d = x1.shape
    tm = 256
    e = w_router.shape[1]
    wr = jnp.pad(w_router, ((0, 0), (0, LANES - e)))
    wr_hi = wr.astype(BF16)
    wr = jnp.stack([wr_hi, (wr - wr_hi.astype(F32)).astype(BF16)])
    rb = jnp.pad(router_bias, (0, LANES - e)).reshape(1, LANES)
    row_spec = lambda w: pl.BlockSpec((tm, w), lambda i: (i, 0))
    return pl.pallas_call(
        _router_kernel,
        grid=(s // tm,),
        in_specs=[row_spec(d),
                  pl.BlockSpec((1, d), lambda i: (0, 0)),
                  pl.BlockSpec((1, N_MOD, d), lambda i: (0, 0, 0)),
                  pl.BlockSpec((2, d, LANES), lambda i: (0, 0, 0)),
                  pl.BlockSpec((1, LANES), lambda i: (0, 0))],
        out_specs=[row_spec(d), pl.BlockSpec((tm * PACK_TILES, LANES), lambda i: (i, 0)),
                   row_spec(LANES), row_spec(LANES), row_spec(LANES),
                   pl.BlockSpec((1, LANES), lambda i: (0, 0))],
        out_shape=[jax.ShapeDtypeStruct((s, d), BF16),
                   jax.ShapeDtypeStruct((s * PACK_TILES, LANES), jnp.uint32),
                   jax.ShapeDtypeStruct((s, LANES), jnp.int32),
                   jax.ShapeDtypeStruct((s, LANES), F32),
                   jax.ShapeDtypeStruct((s, LANES), jnp.int32),
                   jax.ShapeDtypeStruct((1, LANES), F32)],
        scratch_shapes=[pltpu.VMEM((1, LANES), F32)],
        compiler_params=_cparams(("arbitrary",)),
        name="ffn_router",
    )(x1, g.reshape(1, d), mod, wr, rb)


def _unpack_halves(words):
    lo = lax.bitcast_convert_type(words << 16, F32)
    hi = lax.bitcast_convert_type(words & jnp.uint32(0xFFFF0000), F32)
    return lo, hi


def _unpack_rows(words):
    lo, hi = _unpack_halves(words)
    return jnp.concatenate([lo.astype(BF16), hi.astype(BF16)], axis=1)


def _token_rows(row, tiles):
    return pl.ds(pl.multiple_of(row * tiles, tiles), tiles)


def _dispatch_kernel(ps_ref, pe_ref, nused_ref, dest_hbm, hp_ref, xs_hbm,
                     dest_smem, zbuf, sem_i, sem_z, sem_x, *, n_blocks):
    i = pl.program_id(0)
    n = dest_smem.shape[0]
    tm = n // TOP_K
    copy_dest = pltpu.make_async_copy(dest_hbm.at[i], dest_smem, sem_i)
    copy_dest.start()

    @pl.when(i == 0)
    def _():
        zbuf[...] = jnp.zeros_like(zbuf)

        def zero_block(row0):
            span = MOE_BLOCK * PACK_TILES
            return pltpu.make_async_copy(
                zbuf, xs_hbm.at[pl.ds(pl.multiple_of(row0 * PACK_TILES, span), span), :], sem_z)

        def per_expert(action):
            def body(e, carry):
                @pl.when(pe_ref[e] > ps_ref[e])
                def _():
                    action(zero_block(pe_ref[e] - MOE_BLOCK))
                return carry
            lax.fori_loop(0, N_EXPERTS, body, 0)

        def per_tail(action):
            def body(b, carry):
                action(zero_block(b * MOE_BLOCK))
                return carry
            lax.fori_loop(nused_ref[0], n_blocks, body, 0)

        per_expert(lambda cp: cp.start())
        per_tail(lambda cp: cp.start())
        per_expert(lambda cp: cp.wait())
        per_tail(lambda cp: cp.wait())

    copy_dest.wait()

    def issue(t, carry):
        for k in range(TOP_K):
            dest = dest_smem[t * TOP_K + k]
            pltpu.make_async_copy(hp_ref.at[_token_rows(t, PACK_TILES), :],
                                  xs_hbm.at[_token_rows(dest, PACK_TILES), :], sem_x).start(priority=k % 2)
        return carry

    lax.fori_loop(0, tm, issue, 0)
    for _ in range(TOP_K):
        pltpu.make_async_copy(hp_ref, xs_hbm.at[pl.ds(0, tm * PACK_TILES), :], sem_x).wait()


def _dispatch(pad_start, pad_end, n_used, dest_tm, hp, n_blocks):
    n_tiles, n = dest_tm.shape
    tm = n // TOP_K
    grid_spec = pltpu.PrefetchScalarGridSpec(
        num_scalar_prefetch=3,
        grid=(n_tiles,),
        in_specs=[pl.BlockSpec(memory_space=pl.ANY),
                  pl.BlockSpec((tm * PACK_TILES, LANES), lambda i, ps, pe, nu: (i, 0))],
        out_specs=pl.BlockSpec(memory_space=pl.ANY),
        scratch_shapes=[pltpu.SMEM((n,), jnp.int32),
                        pltpu.VMEM((MOE_BLOCK * PACK_TILES, LANES), jnp.uint32),
                        pltpu.SemaphoreType.DMA,
                        pltpu.SemaphoreType.DMA,
                        pltpu.SemaphoreType.DMA])
    return pl.pallas_call(
        functools.partial(_dispatch_kernel, n_blocks=n_blocks),
        grid_spec=grid_spec,
        out_shape=jax.ShapeDtypeStruct((n_blocks * MOE_BLOCK * PACK_TILES, LANES), jnp.uint32),
        compiler_params=_cparams(("arbitrary",)),
        name="moe_dispatch",
    )(pad_start, pad_end, n_used, dest_tm, hp)


def _moe_kernel(nblk_ref, blk0_ref, nxt_ref, slot_ref, first_ref, nused_ref,
                xs_hbm, wg_hbm, wu_hbm, wd_hbm, y_hbm,
                xbuf, ybuf, wg32, wu32, wd32, wgb, wub, wdb, sem_w, sem_x, sem_y, *, n_blocks):
    e = pl.program_id(0)
    nb = nblk_ref[e]
    span = MOE_BLOCK * PACK_TILES

    def block_rows(b):
        return pl.ds(pl.multiple_of(b * span, span), span)

    def fetch(ex, slot):
        return (pltpu.make_async_copy(wg_hbm.at[ex], wg32.at[slot], sem_w.at[slot, 0]),
                pltpu.make_async_copy(wu_hbm.at[ex], wu32.at[slot], sem_w.at[slot, 1]),
                pltpu.make_async_copy(wd_hbm.at[ex], wd32.at[slot], sem_w.at[slot, 2]))

    def x_copy(b, s):
        return pltpu.make_async_copy(xs_hbm.at[block_rows(b), :], xbuf.at[s], sem_x.at[s])

    def y_copy(b, s):
        return pltpu.make_async_copy(ybuf.at[s], y_hbm.at[block_rows(b), :], sem_y.at[s])

    @pl.when(nb > 0)
    def _():
        b0 = blk0_ref[e]
        slot = slot_ref[e]
        x_copy(b0, 0).start()

        @pl.when(e == first_ref[0])
        def _():
            for cp in fetch(e, 0):
                cp.start(priority=1)

        for cp in fetch(e, slot):
            cp.wait()

        @pl.when(nxt_ref[e] >= 0)
        def _():
            for cp in fetch(nxt_ref[e], 1 - slot):
                cp.start(priority=1)

        wgb[...] = wg32[slot].astype(BF16)
        wub[...] = wu32[slot].astype(BF16)
        wdb[...] = wd32[slot].astype(BF16)

        def body(j, carry):
            s = j % 2
            x_copy(b0 + j, s).wait()
            pl.when(j + 1 < nb)(lambda: x_copy(b0 + j + 1, 1 - s).start())
            pl.when(j >= 2)(lambda: y_copy(b0 + j - 2, s).wait())
            xb = xbuf.at[s]
            words = jnp.concatenate(
                [xb[pl.ds(c, MOE_BLOCK, stride=PACK_TILES), :] for c in range(PACK_TILES)], axis=1)
            x = _unpack_rows(words)
            gate = jnp.dot(x, wgb[...], preferred_element_type=F32)
            up = jnp.dot(x, wub[...], preferred_element_type=F32)
            act = (gate * jax.nn.sigmoid(gate) * up).astype(BF16)
            _store_packed_tiles(ybuf.at[s], jnp.dot(act, wdb[...], preferred_element_type=F32))
            y_copy(b0 + j, s).start()
            return carry

        lax.fori_loop(0, nb, body, 0)
        pl.when(nb >= 2)(lambda: y_copy(b0 + nb - 2, nb % 2).wait())
        y_copy(b0 + nb - 1, (nb - 1) % 2).wait()

    @pl.when(e == pl.num_programs(0) - 1)
    def _():
        ybuf[0] = jnp.zeros(ybuf.shape[1:], ybuf.dtype)

        def tail(action):
            def step(b, carry):
                action(y_copy(b, 0))
                return carry
            lax.fori_loop(nused_ref[0], n_blocks, step, 0)

        tail(lambda cp: cp.start())
        tail(lambda cp: cp.wait())


def _moe(nblk, blk0, nxt, slot, first, n_used, xs, weg, weu, wed, n_blocks):
    n_exp, d, f = weg.shape
    span = MOE_BLOCK * PACK_TILES
    grid_spec = pltpu.PrefetchScalarGridSpec(
        num_scalar_prefetch=6,
        grid=(n_exp,),
        in_specs=[pl.BlockSpec(memory_space=pl.ANY)] * 4,
        out_specs=pl.BlockSpec(memory_space=pl.ANY),
        scratch_shapes=[pltpu.VMEM((2, span, LANES), jnp.uint32), pltpu.VMEM((2, span, LANES), jnp.uint32),
                        pltpu.VMEM((2, d, f), F32), pltpu.VMEM((2, d, f), F32), pltpu.VMEM((2, f, d), F32),
                        pltpu.VMEM((d, f), BF16), pltpu.VMEM((d, f), BF16), pltpu.VMEM((f, d), BF16),
                        pltpu.SemaphoreType.DMA((2, 3)), pltpu.SemaphoreType.DMA((2,)),
                        pltpu.SemaphoreType.DMA((2,))])
    return pl.pallas_call(
        functools.partial(_moe_kernel, n_blocks=n_blocks),
        grid_spec=grid_spec,
        out_shape=jax.ShapeDtypeStruct((n_blocks * span, LANES), jnp.uint32),
        compiler_params=_cparams(("arbitrary",)),
        name="moe_experts",
    )(nblk, blk0, nxt, slot, first, n_used, xs, weg, weu, wed)


def _combine_kernel(dest_hbm, y_hbm, w_ref, h_ref, wg_ref, wu_ref, wd_ref, x_ref, gate_ref, o_ref,
                    dest0, dest1, ybuf0, ybuf1, sem_i, sem_y):
    i = pl.program_id(0)
    tm = h_ref.shape[0]
    n = tm * TOP_K
    span = n * PACK_TILES
    dests = (dest0, dest1)
    ybufs = (ybuf0, ybuf1)

    def start_tile(tile, slot):
        copy_dest = pltpu.make_async_copy(dest_hbm.at[tile], dests[slot], sem_i)
        copy_dest.start()
        copy_dest.wait()

        def issue(t, carry):
            base = t * TOKEN_PITCH
            for k in range(TOP_K):
                src = dests[slot][t * TOP_K + k]
                pltpu.make_async_copy(
                    y_hbm.at[_token_rows(src, PACK_TILES), :],
                    ybufs[slot].at[pl.ds(pl.multiple_of(base + k * YBUF_PITCH, 4), PACK_TILES), :],
                    sem_y.at[slot]).start(priority=k % 2)
            return carry

        lax.fori_loop(0, tm, issue, 0)

    def step(cur):
        nxt = 1 - cur
        pl.when(i == 0)(lambda: start_tile(0, cur))
        pl.when(i + 1 < pl.num_programs(0))(lambda: start_tile(i + 1, nxt))

        h = h_ref[...]
        gate = jnp.dot(h, wg_ref[...], preferred_element_type=F32)
        up = jnp.dot(h, wu_ref[...], preferred_element_type=F32)
        act = (gate * jax.nn.sigmoid(gate) * up).astype(BF16)
        shared = jnp.dot(act, wd_ref[...], preferred_element_type=F32)

        yb = ybufs[cur]
        pltpu.make_async_copy(y_hbm.at[pl.ds(0, span), :], yb.at[pl.ds(0, span), :], sem_y.at[cur]).wait()
        lows, highs = [], []
        for c in range(PACK_TILES):
            acc_lo = acc_hi = None
            for k in range(TOP_K):
                lo, hi = _unpack_halves(yb[pl.ds(k * YBUF_PITCH + c, tm, stride=TOKEN_PITCH), :])
                wk = w_ref[:, k:k + 1]
                acc_lo = lo * wk if acc_lo is None else acc_lo + lo * wk
                acc_hi = hi * wk if acc_hi is None else acc_hi + hi * wk
            lows.append(acc_lo)
            highs.append(acc_hi)
        routed = jnp.concatenate(lows + highs, axis=1)
        o_ref[...] = x_ref[...] + gate_ref[...] * (routed + shared)

    pl.when(i % 2 == 0)(lambda: step(0))
    pl.when(i % 2 == 1)(lambda: step(1))


def _combine(dest_tm, y_sorted, wsel, h2, wsg, wsu, wsd, x1, gate):
    s, d = x1.shape
    n_tiles, n = dest_tm.shape
    tm = n // TOP_K
    f = wsg.shape[1]
    return pl.pallas_call(
        _combine_kernel,
        grid=(n_tiles,),
        in_specs=[pl.BlockSpec(memory_space=pl.ANY),
                  pl.BlockSpec(memory_space=pl.ANY),
                  pl.BlockSpec((tm, LANES), lambda i: (i, 0)),
                  pl.BlockSpec((tm, d), lambda i: (i, 0)),
                  pl.BlockSpec((d, f), lambda i: (0, 0)),
                  pl.BlockSpec((d, f), lambda i: (0, 0)),
                  pl.BlockSpec((f, d), lambda i: (0, 0)),
                  pl.BlockSpec((tm, d), lambda i: (i, 0)),
                  pl.BlockSpec((1, d), lambda i: (0, 0))],
        out_specs=pl.BlockSpec((tm, d), lambda i: (i, 0)),
        out_shape=jax.ShapeDtypeStruct((s, d), F32),
        scratch_shapes=[pltpu.SMEM((n,), jnp.int32),
                        pltpu.SMEM((n,), jnp.int32),
                        pltpu.VMEM((tm * TOKEN_PITCH, LANES), jnp.uint32),
                        pltpu.VMEM((tm * TOKEN_PITCH, LANES), jnp.uint32),
                        pltpu.SemaphoreType.DMA,
                        pltpu.SemaphoreType.DMA((2,))],
        compiler_params=_cparams(("arbitrary",)),
        name="moe_combine",
    )(dest_tm, y_sorted, wsel, h2, wsg, wsu, wsd, x1, gate)


def _expert_tables(counts):
    i32 = jnp.int32
    padded = ((counts + MOE_BLOCK - 1) // MOE_BLOCK * MOE_BLOCK).astype(i32)
    pad_end = jnp.cumsum(padded).astype(i32)
    pad_start = pad_end - padded
    ids = jnp.arange(N_EXPERTS, dtype=i32)
    busy = padded > 0
    later_busy = busy[None, :] & (ids[None, :] > ids[:, None])
    nxt = jnp.min(jnp.where(later_busy, ids[None, :], N_EXPERTS), axis=1)
    nxt = jnp.where(nxt < N_EXPERTS, nxt, -1).astype(i32)
    slot = (jnp.maximum(jnp.cumsum(busy.astype(i32)) - 1, 0) % 2).astype(i32)
    first = jnp.min(jnp.where(busy, ids, N_EXPERTS)).astype(i32).reshape(1)
    n_used = (pad_end[-1] // MOE_BLOCK).reshape(1)
    return pad_start, pad_end, padded // MOE_BLOCK, pad_start // MOE_BLOCK, nxt, slot, first, n_used


def kernel(x, c, ctx, c_ctx, w_ada, b_ada, norm_mix, norm_ffn, w_in, q_norm_a, k_norm_a, q_norm_b, k_norm_b, lambda_q1, lambda_k1, lambda_q2, lambda_k2, subln_b, w_branch_a, w_branch_b, w_out, w_router, router_bias, w_exp_gate, w_exp_up, w_exp_down, w_sh_gate, w_sh_up, w_sh_down):
    depth = w_ada.shape[0]
    assert depth == 1 and x.shape[0] == 1 and ctx.shape[0] == 1
    s, d = x.shape[1], x.shape[2]
    n_ctx = ctx.shape[1]
    i = 0
    lam_init = 0.8 - 0.6 * math.exp(-0.3 * i)
    xs = x[0]

    mod = _adaln(jnp.concatenate([c, c_ctx[None, :]], axis=0), w_ada[i], b_ada[i]).reshape(2, N_MOD, d)

    h = _prenorm(xs, ctx[0], norm_mix[i], mod)
    tc, tsa, tsb = _rope_tables(s, n_ctx)
    gains = _head_gains(q_norm_a[i], k_norm_a[i], q_norm_b[i], k_norm_b[i])
    proj = _inproj(h, w_in[i], gains, tc, tsa, tsb)
    oa = _gqa(proj, s)
    lam_vecs = jnp.stack([lambda_q1[i], lambda_k1[i], lambda_q2[i], lambda_k2[i]]).astype(F32)
    ob = _diff(proj, s, lam_vecs, subln_b[i], lam_init)
    x1 = _merge(oa, ob, proj, w_branch_a[i].astype(BF16), w_branch_b[i].astype(BF16),
                w_out[i].astype(BF16), xs, mod[0, 2:3, :])

    h2, h2p, idx, wsel, rank, cnt = _router(x1, norm_ffn[i], mod[0:1], w_router[i], router_bias[i])
    counts = cnt[0, :N_EXPERTS].astype(jnp.int32)
    n_blocks = -(-(s * TOP_K) // MOE_BLOCK) + N_EXPERTS
    pad_start, pad_end, nblk, blk0, nxt, slot, first, n_used = _expert_tables(counts)
    tm_r = 128
    e_ids = jnp.arange(N_EXPERTS, dtype=jnp.int32)
    starts = jnp.sum(jnp.where(idx[:, :TOP_K, None] == e_ids, pad_start, 0), axis=-1)
    dest_tm = (starts + rank[:, :TOP_K]).astype(jnp.int32).reshape(s // tm_r, tm_r * TOP_K)
    xs = _dispatch(pad_start, pad_end, n_used, dest_tm, h2p, n_blocks)
    y_sorted = _moe(nblk, blk0, nxt, slot, first, n_used, xs,
                    w_exp_gate[i], w_exp_up[i], w_exp_down[i], n_blocks)
    out = _combine(dest_tm, y_sorted, wsel, h2,
                   w_sh_gate[i].astype(BF16), w_sh_up[i].astype(BF16), w_sh_down[i].astype(BF16),
                   x1, mod[0, 5:6, :])
    return out[None]
```

```python
import functools
import math

import jax
import jax.numpy as jnp
from jax import lax
from jax.experimental import pallas as pl
from jax.experimental.pallas import tpu as pltpu

F32 = jnp.float32
BF16 = jnp.bfloat16

D_MODEL = 2048
GRID_W = 64
HEAD_DIM = 128
ROPE_PAIRS = HEAD_DIM // 4
ROPE_THETA = 10000.0
A_HEADS = 8
A_KV_HEADS = 2
A_GROUP = A_HEADS // A_KV_HEADS
B_HEADS = 4
B_V_DIM = 2 * HEAD_DIM
N_EXPERTS = 64
TOP_K = 8
N_GROUPS = 8
TOPK_GROUPS = 4
EXPERT_DIM = 512
SHARED_DIM = 512
ROUTED_SCALE = 2.5
N_MOD = 6
EPS = 1e-6

A_Q_W = A_HEADS * HEAD_DIM
A_KV_W = A_KV_HEADS * HEAD_DIM
B_QK_W = B_HEADS * 2 * HEAD_DIM
B_V_W = B_HEADS * B_V_DIM
IN_W = A_Q_W + 2 * A_KV_W + 2 * B_QK_W + B_V_W + 2 * D_MODEL

COL_AQ = 0
COL_AK = A_Q_W // HEAD_DIM
COL_AV = COL_AK + A_KV_HEADS
COL_BQ = COL_AV + A_KV_HEADS
COL_BK = COL_BQ + 2 * B_HEADS
COL_BV = COL_BK + 2 * B_HEADS
COL_GA = COL_BV + B_V_W // HEAD_DIM
COL_GB = COL_GA + D_MODEL // HEAD_DIM

LANES = 128
SUBLANES = 8
VMEM_LIMIT = 56 * 1024 * 1024

PROJ_TN = 512
MOE_BLOCK = 256
ONES_ROWS = 16
PACK_TILES = D_MODEL // 2 // LANES
YBUF_PITCH = PACK_TILES + 4
TOKEN_PITCH = TOP_K * YBUF_PITCH + 4
LOG2E = 1.4426950408889634


def _cparams(sem, vmem=VMEM_LIMIT):
    return pltpu.CompilerParams(dimension_semantics=sem, vmem_limit_bytes=vmem)


def _adaln_kernel(cb_ref, w_ref, b_ref, o_ref):
    tn = w_ref.shape[1]
    nl = tn // LANES
    rows = 32

    def body(g, accs):
        accs = list(accs)
        r0 = pl.multiple_of(g * rows, rows)
        for u in range(rows // SUBLANES):
            r = r0 + u * SUBLANES
            w = w_ref[pl.ds(r, SUBLANES), :]
            for v in range(2):
                c = cb_ref[v, pl.ds(r, SUBLANES), :]
                s = c * jax.nn.sigmoid(c)
                for j in range(nl):
                    accs[v * nl + j] = accs[v * nl + j] + w[:, j * LANES:(j + 1) * LANES] * s
        return tuple(accs)

    init = tuple(jnp.zeros((SUBLANES, LANES), F32) for _ in range(2 * nl))
    accs = lax.fori_loop(0, w_ref.shape[0] // rows, body, init)
    for v in range(2):
        row = jnp.concatenate(
            [jnp.sum(accs[v * nl + j], axis=0, keepdims=True) for j in range(nl)], axis=1)
        o_ref[v:v + 1, :] = row + b_ref[...]


def _adaln(cvecs, w, b):
    d, n = w.shape
    tn = 1536
    cb = jnp.broadcast_to(cvecs[:, :, None], (2, d, LANES))
    return pl.pallas_call(
        _adaln_kernel,
        grid=(n // tn,),
        in_specs=[pl.BlockSpec((2, d, LANES), lambda j: (0, 0, 0)),
                  pl.BlockSpec((d, tn), lambda j: (0, j)),
                  pl.BlockSpec((1, tn), lambda j: (0, j))],
        out_specs=pl.BlockSpec((2, tn), lambda j: (0, j)),
        out_shape=jax.ShapeDtypeStruct((2, n), F32),
        compiler_params=_cparams(("arbitrary",)),
        name="adaln",
    )(cb, w, b.reshape(1, n))


def _rms_mod(x, g, shift, scale):
    y = x * lax.rsqrt(jnp.mean(x * x, axis=-1, keepdims=True) + EPS) * g
    return y * (1.0 + scale) + shift


def _prenorm_kernel(x_ref, c_ref, g_ref, mod_ref, o_ref, *, n_lat_tiles):
    is_ctx = pl.program_id(0) >= n_lat_tiles
    x = jnp.where(is_ctx, c_ref[...], x_ref[...])
    o_ref[...] = _rms_mod(x, g_ref[...], mod_ref[0, 0:1, :], mod_ref[0, 1:2, :]).astype(o_ref.dtype)


def _prenorm(x, ctx, g, mod):
    s, d = x.shape
    c = ctx.shape[0]
    tm = 256
    nl, nc = s // tm, c // tm
    return pl.pallas_call(
        functools.partial(_prenorm_kernel, n_lat_tiles=nl),
        grid=(nl + nc,),
        in_specs=[pl.BlockSpec((tm, d), lambda i: (jnp.minimum(i, nl - 1), 0)),
                  pl.BlockSpec((tm, d), lambda i: (jnp.maximum(i - nl, 0), 0)),
                  pl.BlockSpec((1, d), lambda i: (0, 0)),
                  pl.BlockSpec((1, N_MOD, d), lambda i: (i // nl, 0, 0))],
        out_specs=pl.BlockSpec((tm, d), lambda i: (i, 0)),
        out_shape=jax.ShapeDtypeStruct((s + c, d), BF16),
        compiler_params=_cparams(("arbitrary",)),
        name="prenorm_mix",
    )(x, ctx, g.reshape(1, d), mod)


def _inproj_kernel(h_ref, w_ref, gain_ref, c_ref, sa_ref, sb_ref, o_ref):
    j = pl.program_id(1)
    acc = jnp.dot(h_ref[...], w_ref[...].astype(h_ref.dtype), preferred_element_type=F32)
    nh = acc.shape[1] // HEAD_DIM

    def norm_rope(a, gain):
        y = a * lax.rsqrt(jnp.mean(a * a, axis=-1, keepdims=True) + EPS) * gain
        return (y * c_ref[...] + pltpu.roll(y, ROPE_PAIRS, 1) * sa_ref[...]
                + pltpu.roll(y, HEAD_DIM - ROPE_PAIRS, 1) * sb_ref[...])

    def store(n_normed):
        for hd in range(nh):
            sl = slice(hd * HEAD_DIM, (hd + 1) * HEAD_DIM)
            a = acc[:, sl]
            if hd < n_normed:
                a = norm_rope(a, gain_ref[0, :, sl])
            o_ref[:, sl] = a.astype(o_ref.dtype)

    all_normed = (j < 2) | ((j >= 3) & (j < 7))
    pl.when(all_normed)(lambda: store(nh))
    pl.when(j == 2)(lambda: store(A_KV_HEADS))
    pl.when(j >= 7)(lambda: store(0))


def _inproj(h, w, gains, rope_c, rope_sa, rope_sb):
    t, d = h.shape
    n = w.shape[1]
    tm = t // 8
    tn = PROJ_TN
    return pl.pallas_call(
        _inproj_kernel,
        grid=(t // tm, n // tn),
        in_specs=[pl.BlockSpec((tm, d), lambda i, j: (i, 0)),
                  pl.BlockSpec((d, tn), lambda i, j: (0, j)),
                  pl.BlockSpec((1, 1, tn), lambda i, j: (j, 0, 0)),
                  pl.BlockSpec((tm, HEAD_DIM), lambda i, j: (i, 0)),
                  pl.BlockSpec((tm, HEAD_DIM), lambda i, j: (i, 0)),
                  pl.BlockSpec((tm, HEAD_DIM), lambda i, j: (i, 0))],
        out_specs=pl.BlockSpec((tm, tn), lambda i, j: (i, j)),
        out_shape=jax.ShapeDtypeStruct((t, n), BF16),
        compiler_params=_cparams(("arbitrary", "arbitrary")),
        name="inproj",
    )(h, w, gains, rope_c, rope_sa, rope_sb)


def _rope_tables(s, c):
    rows_n = s // GRID_W
    inv = ROPE_THETA ** (-jnp.arange(ROPE_PAIRS, dtype=F32) / ROPE_PAIRS)
    ang_r = jnp.arange(rows_n, dtype=F32)[:, None] * inv
    ang_c = jnp.arange(GRID_W, dtype=F32)[:, None] * inv
    cr, sr, cc, sc = jnp.cos(ang_r), jnp.sin(ang_r), jnp.cos(ang_c), jnp.sin(ang_c)
    zr, zc = jnp.zeros_like(sr), jnp.zeros_like(sc)

    def table(row_parts, col_parts, ctx_value):
        by_row = jnp.concatenate(row_parts + [zr, zr], axis=1)
        by_col = jnp.concatenate([zc, zc] + col_parts, axis=1)
        lat = (by_row[:, None, :] + by_col[None, :, :]).reshape(s, HEAD_DIM)
        return jnp.concatenate([lat, jnp.full((c, HEAD_DIM), ctx_value, F32)], axis=0)

    return (table([cr, cr], [cc, cc], 1.0), table([zr, sr], [zc, sc], 0.0), table([-sr, zr], [-sc, zc], 0.0))


def _head_gains(qn_a, kn_a, qn_b, kn_b):
    qs = HEAD_DIM ** -0.5 * LOG2E
    one = jnp.ones((HEAD_DIM,), F32)
    heads = ([qn_a * qs] * A_HEADS + [kn_a] * A_KV_HEADS + [one] * A_KV_HEADS
             + [qn_b * qs] * (2 * B_HEADS) + [kn_b] * (2 * B_HEADS))
    heads = heads + [one] * (IN_W // HEAD_DIM - len(heads))
    return jnp.concatenate(heads).reshape(IN_W // PROJ_TN, 1, PROJ_TN)


def _build_vt(v_ref, vt_ref, tk):
    n_chunks, rows, _ = vt_ref.shape
    dv = rows - ONES_ROWS
    tail = (lax.broadcasted_iota(jnp.int32, (ONES_ROWS, tk), 0) == 0).astype(vt_ref.dtype)
    for c in range(n_chunks):
        vt_ref[c, 0:dv, :] = v_ref[c * tk:(c + 1) * tk, :].astype(F32).T.astype(vt_ref.dtype)
        vt_ref[c, dv:rows, :] = tail


def _attend(qs, k_refs, vt_ref, s_ref, m_ref, acc_ref):
    n_chunks, _, tk = vt_ref.shape
    n_st = len(qs)
    m_ref[...] = jnp.full(m_ref.shape, -jnp.inf, F32)
    acc_ref[...] = jnp.zeros(acc_ref.shape, F32)

    def scores(i, c, slot):
        off = c * tk if isinstance(c, int) else pl.multiple_of(c * tk, tk)
        s_ref[i, slot] = lax.dot_general(k_refs[i][pl.ds(off, tk), :], qs[i], (((1,), (1,)), ((), ())),
                                         preferred_element_type=F32)

    def update(i, c, slot):
        s = s_ref[i, slot]
        m_old = m_ref[i]
        m_new = jnp.maximum(m_old, jnp.max(s, axis=0, keepdims=True))
        p = jnp.exp2(s - m_new).astype(vt_ref.dtype)
        acc_ref[i] = (acc_ref[i] * jnp.exp2(m_old - m_new)
                      + jnp.dot(vt_ref[c], p, preferred_element_type=F32))
        m_ref[i] = m_new

    for i in range(n_st):
        scores(i, 0, 0)

    def pair(j, carry):
        c = 2 * j
        for i in range(n_st):
            scores(i, c + 1, 1)
        for i in range(n_st):
            update(i, c, 0)
        for i in range(n_st):
            scores(i, c + 2, 0)
        for i in range(n_st):
            update(i, c + 1, 1)
        return carry

    n_pairs = (n_chunks - 1) // 2
    lax.fori_loop(0, n_pairs, pair, 0)
    done = 2 * n_pairs
    if n_chunks - done == 2:
        for i in range(n_st):
            scores(i, done + 1, 1)
    for i in range(n_st):
        update(i, done, 0)
    if n_chunks - done == 2:
        for i in range(n_st):
            update(i, done + 1, 1)


def _gqa_kernel(q_ref, k_ref, v_ref, o_ref, vt_ref, s_ref, m_ref, acc_ref):
    tk = vt_ref.shape[2]
    n_st = s_ref.shape[0]
    pl.when((pl.program_id(1) == 0) & (pl.program_id(2) == 0))(lambda: _build_vt(v_ref, vt_ref, tk))
    qs = [q_ref[:, i * HEAD_DIM:(i + 1) * HEAD_DIM] for i in range(n_st)]
    _attend(qs, [k_ref] * n_st, vt_ref, s_ref, m_ref, acc_ref)
    for i in range(n_st):
        o_t = acc_ref[i, 0:HEAD_DIM, :] / acc_ref[i, HEAD_DIM:HEAD_DIM + 1, :]
        o_ref[:, i * HEAD_DIM:(i + 1) * HEAD_DIM] = o_t.T.astype(o_ref.dtype)


def _key_chunk(t):
    for tk in (1408, 768, 1024, 512, 640, 384, 256, 128):
        if t % tk == 0:
            return tk
    raise ValueError(f"unsupported key count {t}")


def _gqa(proj, s):
    t = proj.shape[0]
    tq = 512
    tk = _key_chunk(t)
    n_st = 4
    per_g = A_GROUP // n_st
    return pl.pallas_call(
        _gqa_kernel,
        grid=(A_KV_HEADS, per_g, s // tq),
        in_specs=[pl.BlockSpec((tq, n_st * HEAD_DIM), lambda g, hh, i: (i, COL_AQ // n_st + g * per_g + hh)),
                  pl.BlockSpec((t, HEAD_DIM), lambda g, hh, i: (0, COL_AK + g)),
                  pl.BlockSpec((t, HEAD_DIM), lambda g, hh, i: (0, COL_AV + g))],
        out_specs=pl.BlockSpec((tq, n_st * HEAD_DIM), lambda g, hh, i: (i, g * per_g + hh)),
        out_shape=jax.ShapeDtypeStruct((s, A_Q_W), BF16),
        scratch_shapes=[pltpu.VMEM((t // tk, HEAD_DIM + ONES_ROWS, tk), BF16),
                        pltpu.VMEM((n_st, 2, tk, tq), F32),
                        pltpu.VMEM((n_st, 1, tq), F32),
                        pltpu.VMEM((n_st, HEAD_DIM + ONES_ROWS, tq), F32)],
        compiler_params=_cparams(("arbitrary", "arbitrary", "arbitrary")),
        name="gqa_attn",
    )(proj, proj, proj)


def _diff_kernel(lam_ref, q0_ref, q1_ref, k0_ref, k1_ref, v_ref, g_ref, o_ref,
                 vt_ref, s_ref, m_ref, acc_ref, *, lam_init):
    tk = vt_ref.shape[2]
    pl.when(pl.program_id(1) == 0)(lambda: _build_vt(v_ref, vt_ref, tk))
    lv = lam_ref[...]
    lam = (jnp.exp(jnp.sum(lv[0:1, :] * lv[1:2, :], axis=-1, keepdims=True))
           - jnp.exp(jnp.sum(lv[2:3, :] * lv[3:4, :], axis=-1, keepdims=True)) + lam_init)
    _attend([q0_ref[...], q1_ref[...]], [k0_ref, k1_ref], vt_ref, s_ref, m_ref, acc_ref)
    o_t = (acc_ref[0, 0:B_V_DIM, :] / acc_ref[0, B_V_DIM:B_V_DIM + 1, :]
           - lam * (acc_ref[1, 0:B_V_DIM, :] / acc_ref[1, B_V_DIM:B_V_DIM + 1, :]))
    o = o_t.T
    y = o * lax.rsqrt(jnp.mean(o * o, axis=-1, keepdims=True) + EPS) * g_ref[...]
    o_ref[...] = (y * (1.0 - lam_init)).astype(o_ref.dtype)


def _diff(proj, s, lam_vecs, subln_g, lam_init):
    t = proj.shape[0]
    tq = 512
    tk = _key_chunk(t)
    vb = B_V_DIM // HEAD_DIM
    return pl.pallas_call(
        functools.partial(_diff_kernel, lam_init=lam_init),
        grid=(B_HEADS, s // tq),
        in_specs=[pl.BlockSpec((4, HEAD_DIM), lambda h, i: (0, 0)),
                  pl.BlockSpec((tq, HEAD_DIM), lambda h, i: (i, COL_BQ + 2 * h)),
                  pl.BlockSpec((tq, HEAD_DIM), lambda h, i: (i, COL_BQ + 2 * h + 1)),
                  pl.BlockSpec((t, HEAD_DIM), lambda h, i: (0, COL_BK + 2 * h)),
                  pl.BlockSpec((t, HEAD_DIM), lambda h, i: (0, COL_BK + 2 * h + 1)),
                  pl.BlockSpec((t, B_V_DIM), lambda h, i: (0, COL_BV // vb + h)),
                  pl.BlockSpec((1, B_V_DIM), lambda h, i: (0, 0))],
        out_specs=pl.BlockSpec((tq, B_V_DIM), lambda h, i: (i, h)),
        out_shape=jax.ShapeDtypeStruct((s, B_V_W), BF16),
        scratch_shapes=[pltpu.VMEM((t // tk, B_V_DIM + ONES_ROWS, tk), BF16),
                        pltpu.VMEM((2, 2, tk, tq), F32),
                        pltpu.VMEM((2, 1, tq), F32),
                        pltpu.VMEM((2, B_V_DIM + ONES_ROWS, tq), F32)],
        compiler_params=_cparams(("arbitrary", "arbitrary")),
        name="diff_attn",
    )(lam_vecs, proj, proj, proj, proj, proj, subln_g.reshape(1, B_V_DIM))


def _merge_kernel(oa_ref, ob_ref, ga_ref, gb_ref, wa_ref, wb_ref, wo_ref, x_ref, gate_ref, o_ref, t_ref):
    c = pl.program_id(1)
    nc, _, tn = t_ref.shape
    ya = jnp.dot(oa_ref[...], wa_ref[...], preferred_element_type=F32)
    yb = jnp.dot(ob_ref[...], wb_ref[...], preferred_element_type=F32)
    t = (jax.nn.sigmoid(ga_ref[...].astype(F32)) * ya + jax.nn.sigmoid(gb_ref[...].astype(F32)) * yb)
    t_ref[c] = t.astype(t_ref.dtype)

    @pl.when(c == nc - 1)
    def _():
        y = jnp.dot(t_ref[0], wo_ref[0:tn, :], preferred_element_type=F32)
        for cc in range(1, nc):
            y = y + jnp.dot(t_ref[cc], wo_ref[cc * tn:(cc + 1) * tn, :], preferred_element_type=F32)
        o_ref[...] = x_ref[...] + gate_ref[...] * y


def _merge(oa, ob, proj, wa, wb, wo, x, gate):
    s, d = x.shape
    tm = 512
    tn = PROJ_TN
    nc = d // tn
    ga0 = COL_GA * HEAD_DIM // tn
    gb0 = COL_GB * HEAD_DIM // tn
    return pl.pallas_call(
        _merge_kernel,
        grid=(s // tm, nc),
        in_specs=[pl.BlockSpec((tm, A_Q_W), lambda i, c: (i, 0)),
                  pl.BlockSpec((tm, B_V_W), lambda i, c: (i, 0)),
                  pl.BlockSpec((tm, tn), lambda i, c: (i, ga0 + c)),
                  pl.BlockSpec((tm, tn), lambda i, c: (i, gb0 + c)),
                  pl.BlockSpec((A_Q_W, tn), lambda i, c: (0, c)),
                  pl.BlockSpec((B_V_W, tn), lambda i, c: (0, c)),
                  pl.BlockSpec((d, d), lambda i, c: (0, 0)),
                  pl.BlockSpec((tm, d), lambda i, c: (i, 0)),
                  pl.BlockSpec((1, d), lambda i, c: (0, 0))],
        out_specs=pl.BlockSpec((tm, d), lambda i, c: (i, 0)),
        out_shape=jax.ShapeDtypeStruct((s, d), F32),
        scratch_shapes=[pltpu.VMEM((nc, tm, tn), BF16)],
        compiler_params=_cparams(("arbitrary", "arbitrary")),
        name="merge_out",
    )(oa, ob, proj, proj, wa, wb, wo, x, gate)


def _store_packed_tiles(ref, x):
    rows = x.shape[0]
    bits = lax.bitcast_convert_type(x.astype(BF16).astype(F32), jnp.uint32)
    half = bits.shape[1] // 2
    words = (bits[:, :half] >> 16) | (bits[:, half:] & jnp.uint32(0xFFFF0000))
    for c in range(PACK_TILES):
        ref[pl.ds(c, rows, stride=PACK_TILES), :] = words[:, c * LANES:(c + 1) * LANES]


def _lane_max(x):
    return jnp.max(x, axis=-1, keepdims=True)


def _lane_min(x):
    return jnp.min(x, axis=-1, keepdims=True)


def _group_allreduce(x, lane, op):
    for sft in (1, 2, 4):
        up = pltpu.roll(x, sft, 1)
        dn = pltpu.roll(x, LANES - sft, 1)
        x = op(x, jnp.where((lane & sft) != 0, up, dn))
    return x


def _router_kernel(x_ref, g_ref, mod_ref, wr_ref, rb_ref,
                   h_ref, hp_ref, idx_ref, wsel_ref, rank_ref, cnt_ref, carry_ref):
    @pl.when(pl.program_id(0) == 0)
    def _():
        carry_ref[...] = jnp.zeros_like(carry_ref)

    h = _rms_mod(x_ref[...], g_ref[...], mod_ref[0, 3:4, :], mod_ref[0, 4:5, :])
    hb = h.astype(BF16)
    h_ref[...] = hb
    _store_packed_tiles(hp_ref, h)
    tm = h.shape[0]

    h_lo = (h - hb.astype(F32)).astype(BF16)
    logits = (jnp.dot(hb, wr_ref[0], preferred_element_type=F32)
              + jnp.dot(hb, wr_ref[1], preferred_element_type=F32)
              + jnp.dot(h_lo, wr_ref[0], preferred_element_type=F32))
    scores = jax.nn.sigmoid(logits)
    lane = lax.broadcasted_iota(jnp.int32, (tm, LANES), 1)
    lane_f = lane.astype(F32)
    gid_f = (lane >> 3).astype(F32)
    valid = lane < N_EXPERTS
    neg = jnp.float32(-jnp.inf)
    big = jnp.float32(LANES)
    biased = jnp.where(valid, scores + rb_ref[...], neg)

    m1 = _group_allreduce(biased, lane, jnp.maximum)
    a1 = _group_allreduce(jnp.where(biased == m1, lane_f, big), lane, jnp.minimum)
    m2 = _group_allreduce(jnp.where(lane_f == a1, neg, biased), lane, jnp.maximum)
    gscore = jnp.where(valid, m1 + m2, neg)
    keep = jnp.zeros((tm, LANES), jnp.bool_)
    for _ in range(TOPK_GROUPS):
        best = _lane_max(gscore)
        gsel = _lane_min(jnp.where(gscore == best, gid_f, big))
        hit = gid_f == gsel
        keep = keep | hit
        gscore = jnp.where(hit, neg, gscore)
    cand = jnp.where(keep & valid, biased, neg)

    onehot = jnp.zeros((tm, LANES), F32)
    idx_out = jnp.zeros((tm, LANES), F32)
    w_out = jnp.zeros((tm, LANES), F32)
    sels = []
    for k in range(TOP_K):
        best = _lane_max(cand)
        sel = _lane_min(jnp.where(cand == best, lane_f, big))
        hit = lane_f == sel
        wk = jnp.sum(jnp.where(hit, scores, 0.0), axis=-1, keepdims=True)
        cand = jnp.where(hit, neg, cand)
        onehot = jnp.where(hit, 1.0, onehot)
        idx_out = jnp.where(lane == k, sel, idx_out)
        w_out = jnp.where(lane == k, wk, w_out)
        sels.append(hit)
    wsum = jnp.sum(w_out, axis=-1, keepdims=True)
    wsel_ref[...] = w_out / wsum * ROUTED_SCALE
    idx_ref[...] = idx_out.astype(jnp.int32)

    row = lax.broadcasted_iota(jnp.int32, (tm, tm), 0)
    col = lax.broadcasted_iota(jnp.int32, (tm, tm), 1)
    lower = (col < row).astype(BF16)
    before = jnp.dot(lower, onehot.astype(BF16), preferred_element_type=F32) + carry_ref[...]
    rank_out = jnp.zeros((tm, LANES), F32)
    for k in range(TOP_K):
        rk = jnp.sum(jnp.where(sels[k], before, 0.0), axis=-1, keepdims=True)
        rank_out = jnp.where(lane == k, rk, rank_out)
    rank_ref[...] = rank_out.astype(jnp.int32)
    carry_ref[...] = carry_ref[...] + jnp.sum(onehot, axis=0, keepdims=True)
    cnt_ref[...] = carry_ref[...]


def _router(x1, g, mod, w_router, router_bias):
    s, d = x1.shape
    tm = 256
    e = w_router.shape[1]
    wr = jnp.pad(w_router, ((0, 0), (0, LANES - e)))
    wr_hi = wr.astype(BF16)
    wr = jnp.stack([wr_hi, (wr - wr_hi.astype(F32)).astype(BF16)])
    rb = jnp.pad(router_bias, (0, LANES - e)).reshape(1, LANES)
    row_spec = lambda w: pl.BlockSpec((tm, w), lambda i: (i, 0))
    return pl.pallas_call(
        _router_kernel,
        grid=(s // tm,),
        in_specs=[row_spec(d),
                  pl.BlockSpec((1, d), lambda i: (0, 0)),
                  pl.BlockSpec((1, N_MOD, d), lambda i: (0, 0, 0)),
                  pl.BlockSpec((2, d, LANES), lambda i: (0, 0, 0)),
                  pl.BlockSpec((1, LANES), lambda i: (0, 0))],
        out_specs=[row_spec(d), pl.BlockSpec((tm * PACK_TILES, LANES), lambda i: (i, 0)),
                   row_spec(LANES), row_spec(LANES), row_spec(LANES),
                   pl.BlockSpec((1, LANES), lambda i: (0, 0))],
        out_shape=[jax.ShapeDtypeStruct((s, d), BF16),
                   jax.ShapeDtypeStruct((s * PACK_TILES, LANES), jnp.uint32),
                   jax.ShapeDtypeStruct((s, LANES), jnp.int32),
                   jax.ShapeDtypeStruct((s, LANES), F32),
                   jax.ShapeDtypeStruct((s, LANES), jnp.int32),
                   jax.ShapeDtypeStruct((1, LANES), F32)],
        scratch_shapes=[pltpu.VMEM((1, LANES), F32)],
        compiler_params=_cparams(("arbitrary",)),
        name="ffn_router",
    )(x1, g.reshape(1, d), mod, wr, rb)


def _unpack_halves(words):
    lo = lax.bitcast_convert_type(words << 16, F32)
    hi = lax.bitcast_convert_type(words & jnp.uint32(0xFFFF0000), F32)
    return lo, hi


def _unpack_rows(words):
    lo, hi = _unpack_halves(words)
    return jnp.concatenate([lo.astype(BF16), hi.astype(BF16)], axis=1)


def _token_rows(row, tiles):
    return pl.ds(pl.multiple_of(row * tiles, tiles), tiles)


def _dispatch_kernel(ps_ref, pe_ref, nused_ref, dest_hbm, hp_ref, xs_hbm,
                     dest_smem, zbuf, sem_i, sem_z, sem_x, *, n_blocks):
    i = pl.program_id(0)
    n = dest_smem.shape[0]
    tm = n // TOP_K
    copy_dest = pltpu.make_async_copy(dest_hbm.at[i], dest_smem, sem_i)
    copy_dest.start()

    @pl.when(i == 0)
    def _():
        zbuf[...] = jnp.zeros_like(zbuf)

        def zero_block(row0):
            span = MOE_BLOCK * PACK_TILES
            return pltpu.make_async_copy(
                zbuf, xs_hbm.at[pl.ds(pl.multiple_of(row0 * PACK_TILES, span), span), :], sem_z)

        def per_expert(action):
            def body(e, carry):
                @pl.when(pe_ref[e] > ps_ref[e])
                def _():
                    action(zero_block(pe_ref[e] - MOE_BLOCK))
                return carry
            lax.fori_loop(0, N_EXPERTS, body, 0)

        def per_tail(action):
            def body(b, carry):
                action(zero_block(b * MOE_BLOCK))
                return carry
            lax.fori_loop(nused_ref[0], n_blocks, body, 0)

        per_expert(lambda cp: cp.start())
        per_tail(lambda cp: cp.start())
        per_expert(lambda cp: cp.wait())
        per_tail(lambda cp: cp.wait())

    copy_dest.wait()

    def issue(t, carry):
        for k in range(TOP_K):
            dest = dest_smem[t * TOP_K + k]
            pltpu.make_async_copy(hp_ref.at[_token_rows(t, PACK_TILES), :],
                                  xs_hbm.at[_token_rows(dest, PACK_TILES), :], sem_x).start(priority=k % 2)
        return carry

    lax.fori_loop(0, tm, issue, 0)
    for _ in range(TOP_K):
        pltpu.make_async_copy(hp_ref, xs_hbm.at[pl.ds(0, tm * PACK_TILES), :], sem_x).wait()


def _dispatch(pad_start, pad_end, n_used, dest_tm, hp, n_blocks):
    n_tiles, n = dest_tm.shape
    tm = n // TOP_K
    grid_spec = pltpu.PrefetchScalarGridSpec(
        num_scalar_prefetch=3,
        grid=(n_tiles,),
        in_specs=[pl.BlockSpec(memory_space=pl.ANY),
                  pl.BlockSpec((tm * PACK_TILES, LANES), lambda i, ps, pe, nu: (i, 0))],
        out_specs=pl.BlockSpec(memory_space=pl.ANY),
        scratch_shapes=[pltpu.SMEM((n,), jnp.int32),
                        pltpu.VMEM((MOE_BLOCK * PACK_TILES, LANES), jnp.uint32),
                        pltpu.SemaphoreType.DMA,
                        pltpu.SemaphoreType.DMA,
                        pltpu.SemaphoreType.DMA])
    return pl.pallas_call(
        functools.partial(_dispatch_kernel, n_blocks=n_blocks),
        grid_spec=grid_spec,
        out_shape=jax.ShapeDtypeStruct((n_blocks * MOE_BLOCK * PACK_TILES, LANES), jnp.uint32),
        compiler_params=_cparams(("arbitrary",)),
        name="moe_dispatch",
    )(pad_start, pad_end, n_used, dest_tm, hp)


def _moe_kernel(nblk_ref, blk0_ref, nxt_ref, slot_ref, first_ref, nused_ref,
                xs_hbm, wg_hbm, wu_hbm, wd_hbm, y_hbm,
                xbuf, ybuf, wg32, wu32, wd32, wgb, wub, wdb, sem_w, sem_x, sem_y, *, n_blocks):
    e = pl.program_id(0)
    nb = nblk_ref[e]
    span = MOE_BLOCK * PACK_TILES

    def block_rows(b):
        return pl.ds(pl.multiple_of(b * span, span), span)

    def fetch(ex, slot):
        return (pltpu.make_async_copy(wg_hbm.at[ex], wg32.at[slot], sem_w.at[slot, 0]),
                pltpu.make_async_copy(wu_hbm.at[ex], wu32.at[slot], sem_w.at[slot, 1]),
                pltpu.make_async_copy(wd_hbm.at[ex], wd32.at[slot], sem_w.at[slot, 2]))

    def x_copy(b, s):
        return pltpu.make_async_copy(xs_hbm.at[block_rows(b), :], xbuf.at[s], sem_x.at[s])

    def y_copy(b, s):
        return pltpu.make_async_copy(ybuf.at[s], y_hbm.at[block_rows(b), :], sem_y.at[s])

    n_used = nused_ref[0]

    @pl.when(nb > 0)
    def _():
        b0 = blk0_ref[e]
        slot = slot_ref[e]

        @pl.when(e == first_ref[0])
        def _():
            x_copy(0, 0).start()
            for cp in fetch(e, 0):
                cp.start(priority=1)

        for cp in fetch(e, slot):
            cp.wait()

        @pl.when(nxt_ref[e] >= 0)
        def _():
            for cp in fetch(nxt_ref[e], 1 - slot):
                cp.start(priority=1)

        wgb[...] = wg32[slot].astype(BF16)
        wub[...] = wu32[slot].astype(BF16)
        wdb[...] = wd32[slot].astype(BF16)

        def body(j, carry):
            b = b0 + j
            s = b % 2
            x_copy(b, s).wait()
            pl.when(b + 1 < n_used)(lambda: x_copy(b + 1, 1 - s).start())
            pl.when(b >= 2)(lambda: y_copy(b - 2, s).wait())
            xb = xbuf.at[s]
            words = jnp.concatenate(
                [xb[pl.ds(c, MOE_BLOCK, stride=PACK_TILES), :] for c in range(PACK_TILES)], axis=1)
            x = _unpack_rows(words)
            gate = jnp.dot(x, wgb[...], preferred_element_type=F32)
            up = jnp.dot(x, wub[...], preferred_element_type=F32)
            act = (gate * jax.nn.sigmoid(gate) * up).astype(BF16)
            _store_packed_tiles(ybuf.at[s], jnp.dot(act, wdb[...], preferred_element_type=F32))
            y_copy(b, s).start()
            return carry

        lax.fori_loop(0, nb, body, 0)

    @pl.when(e == pl.num_programs(0) - 1)
    def _():
        pl.when(n_used >= 2)(lambda: y_copy(n_used - 2, n_used % 2).wait())
        pl.when(n_used >= 1)(lambda: y_copy(n_used - 1, (n_used - 1) % 2).wait())
        ybuf[0] = jnp.zeros(ybuf.shape[1:], ybuf.dtype)

        def tail(action):
            def step(b, carry):
                action(y_copy(b, 0))
                return carry
            lax.fori_loop(nused_ref[0], n_blocks, step, 0)

        tail(lambda cp: cp.start())
        tail(lambda cp: cp.wait())


def _moe(nblk, blk0, nxt, slot, first, n_used, xs, weg, weu, wed, n_blocks):
    n_exp, d, f = weg.shape
    span = MOE_BLOCK * PACK_TILES
    grid_spec = pltpu.PrefetchScalarGridSpec(
        num_scalar_prefetch=6,
        grid=(n_exp,),
        in_specs=[pl.BlockSpec(memory_space=pl.ANY)] * 4,
        out_specs=pl.BlockSpec(memory_space=pl.ANY),
        scratch_shapes=[pltpu.VMEM((2, span, LANES), jnp.uint32), pltpu.VMEM((2, span, LANES), jnp.uint32),
                        pltpu.VMEM((2, d, f), F32), pltpu.VMEM((2, d, f), F32), pltpu.VMEM((2, f, d), F32),
                        pltpu.VMEM((d, f), BF16), pltpu.VMEM((d, f), BF16), pltpu.VMEM((f, d), BF16),
                        pltpu.SemaphoreType.DMA((2, 3)), pltpu.SemaphoreType.DMA((2,)),
                        pltpu.SemaphoreType.DMA((2,))])
    return pl.pallas_call(
        functools.partial(_moe_kernel, n_blocks=n_blocks),
        grid_spec=grid_spec,
        out_shape=jax.ShapeDtypeStruct((n_blocks * span, LANES), jnp.uint32),
        compiler_params=_cparams(("arbitrary",)),
        name="moe_experts",
    )(nblk, blk0, nxt, slot, first, n_used, xs, weg, weu, wed)


def _combine_kernel(dest_hbm, y_hbm, w_ref, h_ref, wg_ref, wu_ref, wd_ref, x_ref, gate_ref, o_ref,
                    dest0, dest1, ybuf0, ybuf1, sem_i, sem_y):
    i = pl.program_id(0)
    tm = h_ref.shape[0]
    n = tm * TOP_K
    span = n * PACK_TILES
    dests = (dest0, dest1)
    ybufs = (ybuf0, ybuf1)

    def start_tile(tile, slot):
        copy_dest = pltpu.make_async_copy(dest_hbm.at[tile], dests[slot], sem_i)
        copy_dest.start()
        copy_dest.wait()

        def issue(t, carry):
            base = t * TOKEN_PITCH
            for k in range(TOP_K):
                src = dests[slot][t * TOP_K + k]
                pltpu.make_async_copy(
                    y_hbm.at[_token_rows(src, PACK_TILES), :],
                    ybufs[slot].at[pl.ds(pl.multiple_of(base + k * YBUF_PITCH, 4), PACK_TILES), :],
                    sem_y.at[slot]).start(priority=k % 2)
            return carry

        lax.fori_loop(0, tm, issue, 0)

    def step(cur):
        nxt = 1 - cur
        pl.when(i == 0)(lambda: start_tile(0, cur))
        pl.when(i + 1 < pl.num_programs(0))(lambda: start_tile(i + 1, nxt))

        h = h_ref[...]
        gate = jnp.dot(h, wg_ref[...], preferred_element_type=F32)
        up = jnp.dot(h, wu_ref[...], preferred_element_type=F32)
        act = (gate * jax.nn.sigmoid(gate) * up).astype(BF16)
        shared = jnp.dot(act, wd_ref[...], preferred_element_type=F32)

        yb = ybufs[cur]
        pltpu.make_async_copy(y_hbm.at[pl.ds(0, span), :], yb.at[pl.ds(0, span), :], sem_y.at[cur]).wait()
        lows, highs = [], []
        for c in range(PACK_TILES):
            acc_lo = acc_hi = None
            for k in range(TOP_K):
                lo, hi = _unpack_halves(yb[pl.ds(k * YBUF_PITCH + c, tm, stride=TOKEN_PITCH), :])
                wk = w_ref[:, k:k + 1]
                acc_lo = lo * wk if acc_lo is None else acc_lo + lo * wk
                acc_hi = hi * wk if acc_hi is None else acc_hi + hi * wk
            lows.append(acc_lo)
            highs.append(acc_hi)
        routed = jnp.concatenate(lows + highs, axis=1)
        o_ref[...] = x_ref[...] + gate_ref[...] * (routed + shared)

    pl.when(i % 2 == 0)(lambda: step(0))
    pl.when(i % 2 == 1)(lambda: step(1))


def _combine(dest_tm, y_sorted, wsel, h2, wsg, wsu, wsd, x1, gate):
    s, d = x1.shape
    n_tiles, n = dest_tm.shape
    tm = n // TOP_K
    f = wsg.shape[1]
    return pl.pallas_call(
        _combine_kernel,
        grid=(n_tiles,),
        in_specs=[pl.BlockSpec(memory_space=pl.ANY),
                  pl.BlockSpec(memory_space=pl.ANY),
                  pl.BlockSpec((tm, LANES), lambda i: (i, 0)),
                  pl.BlockSpec((tm, d), lambda i: (i, 0)),
                  pl.BlockSpec((d, f), lambda i: (0, 0)),
                  pl.BlockSpec((d, f), lambda i: (0, 0)),
                  pl.BlockSpec((f, d), lambda i: (0, 0)),
                  pl.BlockSpec((tm, d), lambda i: (i, 0)),
                  pl.BlockSpec((1, d), lambda i: (0, 0))],
        out_specs=pl.BlockSpec((tm, d), lambda i: (i, 0)),
        out_shape=jax.ShapeDtypeStruct((s, d), F32),
        scratch_shapes=[pltpu.SMEM((n,), jnp.int32),
                        pltpu.SMEM((n,), jnp.int32),
                        pltpu.VMEM((tm * TOKEN_PITCH, LANES), jnp.uint32),
                        pltpu.VMEM((tm * TOKEN_PITCH, LANES), jnp.uint32),
                        pltpu.SemaphoreType.DMA,
                        pltpu.SemaphoreType.DMA((2,))],
        compiler_params=_cparams(("arbitrary",)),
        name="moe_combine",
    )(dest_tm, y_sorted, wsel, h2, wsg, wsu, wsd, x1, gate)


def _expert_tables(counts):
    i32 = jnp.int32
    padded = ((counts + MOE_BLOCK - 1) // MOE_BLOCK * MOE_BLOCK).astype(i32)
    pad_end = jnp.cumsum(padded).astype(i32)
    pad_start = pad_end - padded
    ids = jnp.arange(N_EXPERTS, dtype=i32)
    busy = padded > 0
    later_busy = busy[None, :] & (ids[None, :] > ids[:, None])
    nxt = jnp.min(jnp.where(later_busy, ids[None, :], N_EXPERTS), axis=1)
    nxt = jnp.where(nxt < N_EXPERTS, nxt, -1).astype(i32)
    slot = (jnp.maximum(jnp.cumsum(busy.astype(i32)) - 1, 0) % 2).astype(i32)
    first = jnp.min(jnp.where(busy, ids, N_EXPERTS)).astype(i32).reshape(1)
    n_used = (pad_end[-1] // MOE_BLOCK).reshape(1)
    return pad_start, pad_end, padded // MOE_BLOCK, pad_start // MOE_BLOCK, nxt, slot, first, n_used


def kernel(x, c, ctx, c_ctx, w_ada, b_ada, norm_mix, norm_ffn, w_in, q_norm_a, k_norm_a, q_norm_b, k_norm_b, lambda_q1, lambda_k1, lambda_q2, lambda_k2, subln_b, w_branch_a, w_branch_b, w_out, w_router, router_bias, w_exp_gate, w_exp_up, w_exp_down, w_sh_gate, w_sh_up, w_sh_down):
    depth = w_ada.shape[0]
    assert depth == 1 and x.shape[0] == 1 and ctx.shape[0] == 1
    s, d = x.shape[1], x.shape[2]
    n_ctx = ctx.shape[1]
    i = 0
    lam_init = 0.8 - 0.6 * math.exp(-0.3 * i)
    xs = x[0]

    mod = _adaln(jnp.concatenate([c, c_ctx[None, :]], axis=0), w_ada[i], b_ada[i]).reshape(2, N_MOD, d)

    h = _prenorm(xs, ctx[0], norm_mix[i], mod)
    tc, tsa, tsb = _rope_tables(s, n_ctx)
    gains = _head_gains(q_norm_a[i], k_norm_a[i], q_norm_b[i], k_norm_b[i])
    proj = _inproj(h, w_in[i], gains, tc, tsa, tsb)
    oa = _gqa(proj, s)
    lam_vecs = jnp.stack([lambda_q1[i], lambda_k1[i], lambda_q2[i], lambda_k2[i]]).astype(F32)
    ob = _diff(proj, s, lam_vecs, subln_b[i], lam_init)
    x1 = _merge(oa, ob, proj, w_branch_a[i].astype(BF16), w_branch_b[i].astype(BF16),
                w_out[i].astype(BF16), xs, mod[0, 2:3, :])

    h2, h2p, idx, wsel, rank, cnt = _router(x1, norm_ffn[i], mod[0:1], w_router[i], router_bias[i])
    counts = cnt[0, :N_EXPERTS].astype(jnp.int32)
    n_blocks = -(-(s * TOP_K) // MOE_BLOCK) + N_EXPERTS
    pad_start, pad_end, nblk, blk0, nxt, slot, first, n_used = _expert_tables(counts)
    tm_r = 128
    e_ids = jnp.arange(N_EXPERTS, dtype=jnp.int32)
    starts = jnp.sum(jnp.where(idx[:, :TOP_K, None] == e_ids, pad_start, 0), axis=-1)
    dest_tm = (starts + rank[:, :TOP_K]).astype(jnp.int32).reshape(s // tm_r, tm_r * TOP_K)
    xs = _dispatch(pad_start, pad_end, n_used, dest_tm, h2p, n_blocks)
    y_sorted = _moe(nblk, blk0, nxt, slot, first, n_used, xs,
                    w_exp_gate[i], w_exp_up[i], w_exp_down[i], n_blocks)
    out = _combine(dest_tm, y_sorted, wsel, h2,
                   w_sh_gate[i].astype(BF16), w_sh_up[i].astype(BF16), w_sh_down[i].astype(BF16),
                   x1, mod[0, 5:6, :])
    return out[None]
```

```python
import functools
import math

import jax
import jax.numpy as jnp
from jax import lax
from jax.experimental import pallas as pl
from jax.experimental.pallas import tpu as pltpu

F32 = jnp.float32
BF16 = jnp.bfloat16

D_MODEL = 2048
GRID_W = 64
HEAD_DIM = 128
ROPE_PAIRS = HEAD_DIM // 4
ROPE_THETA = 10000.0
A_HEADS = 8
A_KV_HEADS = 2
A_GROUP = A_HEADS // A_KV_HEADS
B_HEADS = 4
B_V_DIM = 2 * HEAD_DIM
N_EXPERTS = 64
TOP_K = 8
N_GROUPS = 8
TOPK_GROUPS = 4
EXPERT_DIM = 512
SHARED_DIM = 512
ROUTED_SCALE = 2.5
N_MOD = 6
EPS = 1e-6

A_Q_W = A_HEADS * HEAD_DIM
A_KV_W = A_KV_HEADS * HEAD_DIM
B_QK_W = B_HEADS * 2 * HEAD_DIM
B_V_W = B_HEADS * B_V_DIM
IN_W = A_Q_W + 2 * A_KV_W + 2 * B_QK_W + B_V_W + 2 * D_MODEL

COL_AQ = 0
COL_AK = A_Q_W // HEAD_DIM
COL_AV = COL_AK + A_KV_HEADS
COL_BQ = COL_AV + A_KV_HEADS
COL_BK = COL_BQ + 2 * B_HEADS
COL_BV = COL_BK + 2 * B_HEADS
COL_GA = COL_BV + B_V_W // HEAD_DIM
COL_GB = COL_GA + D_MODEL // HEAD_DIM

LANES = 128
SUBLANES = 8
VMEM_LIMIT = 56 * 1024 * 1024

PROJ_TN = 512
MOE_BLOCK = 256
ONES_ROWS = 16
PACK_TILES = D_MODEL // 2 // LANES
YBUF_PITCH = PACK_TILES + 4
TOKEN_PITCH = TOP_K * YBUF_PITCH + 4
LOG2E = 1.4426950408889634


def _cparams(sem, vmem=VMEM_LIMIT):
    return pltpu.CompilerParams(dimension_semantics=sem, vmem_limit_bytes=vmem)


def _adaln_kernel(cb_ref, w_ref, b_ref, o_ref):
    tn = w_ref.shape[1]
    nl = tn // LANES
    rows = 32

    def body(g, accs):
        accs = list(accs)
        r0 = pl.multiple_of(g * rows, rows)
        for u in range(rows // SUBLANES):
            r = r0 + u * SUBLANES
            w = w_ref[pl.ds(r, SUBLANES), :]
            for v in range(2):
                c = cb_ref[v, pl.ds(r, SUBLANES), :]
                s = c * jax.nn.sigmoid(c)
                for j in range(nl):
                    accs[v * nl + j] = accs[v * nl + j] + w[:, j * LANES:(j + 1) * LANES] * s
        return tuple(accs)

    init = tuple(jnp.zeros((SUBLANES, LANES), F32) for _ in range(2 * nl))
    accs = lax.fori_loop(0, w_ref.shape[0] // rows, body, init)
    for v in range(2):
        row = jnp.concatenate(
            [jnp.sum(accs[v * nl + j], axis=0, keepdims=True) for j in range(nl)], axis=1)
        o_ref[v:v + 1, :] = row + b_ref[...]


def _adaln(cvecs, w, b):
    d, n = w.shape
    tn = 1536
    cb = jnp.broadcast_to(cvecs[:, :, None], (2, d, LANES))
    return pl.pallas_call(
        _adaln_kernel,
        grid=(n // tn,),
        in_specs=[pl.BlockSpec((2, d, LANES), lambda j: (0, 0, 0)),
                  pl.BlockSpec((d, tn), lambda j: (0, j)),
                  pl.BlockSpec((1, tn), lambda j: (0, j))],
        out_specs=pl.BlockSpec((2, tn), lambda j: (0, j)),
        out_shape=jax.ShapeDtypeStruct((2, n), F32),
        compiler_params=_cparams(("arbitrary",)),
        name="adaln",
    )(cb, w, b.reshape(1, n))


def _rms_mod(x, g, shift, scale):
    y = x * lax.rsqrt(jnp.mean(x * x, axis=-1, keepdims=True) + EPS) * g
    return y * (1.0 + scale) + shift


def _prenorm_kernel(x_ref, c_ref, g_ref, mod_ref, o_ref, *, n_lat_tiles):
    is_ctx = pl.program_id(0) >= n_lat_tiles
    x = jnp.where(is_ctx, c_ref[...], x_ref[...])
    o_ref[...] = _rms_mod(x, g_ref[...], mod_ref[0, 0:1, :], mod_ref[0, 1:2, :]).astype(o_ref.dtype)


def _prenorm(x, ctx, g, mod):
    s, d = x.shape
    c = ctx.shape[0]
    tm = 256
    nl, nc = s // tm, c // tm
    return pl.pallas_call(
        functools.partial(_prenorm_kernel, n_lat_tiles=nl),
        grid=(nl + nc,),
        in_specs=[pl.BlockSpec((tm, d), lambda i: (jnp.minimum(i, nl - 1), 0)),
                  pl.BlockSpec((tm, d), lambda i: (jnp.maximum(i - nl, 0), 0)),
                  pl.BlockSpec((1, d), lambda i: (0, 0)),
                  pl.BlockSpec((1, N_MOD, d), lambda i: (i // nl, 0, 0))],
        out_specs=pl.BlockSpec((tm, d), lambda i: (i, 0)),
        out_shape=jax.ShapeDtypeStruct((s + c, d), BF16),
        compiler_params=_cparams(("arbitrary",)),
        name="prenorm_mix",
    )(x, ctx, g.reshape(1, d), mod)


def _inproj_kernel(h_ref, w_ref, gain_ref, c_ref, sa_ref, sb_ref, o_ref):
    j = pl.program_id(1)
    acc = jnp.dot(h_ref[...], w_ref[...].astype(h_ref.dtype), preferred_element_type=F32)
    nh = acc.shape[1] // HEAD_DIM

    def norm_rope(a, gain):
        y = a * lax.rsqrt(jnp.mean(a * a, axis=-1, keepdims=True) + EPS) * gain
        return (y * c_ref[...] + pltpu.roll(y, ROPE_PAIRS, 1) * sa_ref[...]
                + pltpu.roll(y, HEAD_DIM - ROPE_PAIRS, 1) * sb_ref[...])

    def store(n_normed):
        for hd in range(nh):
            sl = slice(hd * HEAD_DIM, (hd + 1) * HEAD_DIM)
            a = acc[:, sl]
            if hd < n_normed:
                a = norm_rope(a, gain_ref[0, :, sl])
            o_ref[:, sl] = a.astype(o_ref.dtype)

    all_normed = (j < 2) | ((j >= 3) & (j < 7))
    pl.when(all_normed)(lambda: store(nh))
    pl.when(j == 2)(lambda: store(A_KV_HEADS))
    pl.when(j >= 7)(lambda: store(0))


def _inproj(h, w, gains, rope_c, rope_sa, rope_sb):
    t, d = h.shape
    n = w.shape[1]
    tm = t // 8
    tn = PROJ_TN
    return pl.pallas_call(
        _inproj_kernel,
        grid=(t // tm, n // tn),
        in_specs=[pl.BlockSpec((tm, d), lambda i, j: (i, 0)),
                  pl.BlockSpec((d, tn), lambda i, j: (0, j)),
                  pl.BlockSpec((1, 1, tn), lambda i, j: (j, 0, 0)),
                  pl.BlockSpec((tm, HEAD_DIM), lambda i, j: (i, 0)),
                  pl.BlockSpec((tm, HEAD_DIM), lambda i, j: (i, 0)),
                  pl.BlockSpec((tm, HEAD_DIM), lambda i, j: (i, 0))],
        out_specs=pl.BlockSpec((tm, tn), lambda i, j: (i, j)),
        out_shape=jax.ShapeDtypeStruct((t, n), BF16),
        compiler_params=_cparams(("arbitrary", "arbitrary")),
        name="inproj",
    )(h, w, gains, rope_c, rope_sa, rope_sb)


def _rope_tables(s, c):
    rows_n = s // GRID_W
    inv = ROPE_THETA ** (-jnp.arange(ROPE_PAIRS, dtype=F32) / ROPE_PAIRS)
    ang_r = jnp.arange(rows_n, dtype=F32)[:, None] * inv
    ang_c = jnp.arange(GRID_W, dtype=F32)[:, None] * inv
    cr, sr, cc, sc = jnp.cos(ang_r), jnp.sin(ang_r), jnp.cos(ang_c), jnp.sin(ang_c)
    zr, zc = jnp.zeros_like(sr), jnp.zeros_like(sc)

    def table(row_parts, col_parts, ctx_value):
        by_row = jnp.concatenate(row_parts + [zr, zr], axis=1)
        by_col = jnp.concatenate([zc, zc] + col_parts, axis=1)
        lat = (by_row[:, None, :] + by_col[None, :, :]).reshape(s, HEAD_DIM)
        return jnp.concatenate([lat, jnp.full((c, HEAD_DIM), ctx_value, F32)], axis=0)

    return (table([cr, cr], [cc, cc], 1.0), table([zr, sr], [zc, sc], 0.0), table([-sr, zr], [-sc, zc], 0.0))


def _head_gains(qn_a, kn_a, qn_b, kn_b):
    qs = HEAD_DIM ** -0.5 * LOG2E
    one = jnp.ones((HEAD_DIM,), F32)
    heads = ([qn_a * qs] * A_HEADS + [kn_a] * A_KV_HEADS + [one] * A_KV_HEADS
             + [qn_b * qs] * (2 * B_HEADS) + [kn_b] * (2 * B_HEADS))
    heads = heads + [one] * (IN_W // HEAD_DIM - len(heads))
    return jnp.concatenate(heads).reshape(IN_W // PROJ_TN, 1, PROJ_TN)


def _build_vt(v_ref, vt_ref, tk):
    n_chunks, rows, _ = vt_ref.shape
    dv = rows - ONES_ROWS
    tail = (lax.broadcasted_iota(jnp.int32, (ONES_ROWS, tk), 0) == 0).astype(vt_ref.dtype)
    for c in range(n_chunks):
        vt_ref[c, 0:dv, :] = v_ref[c * tk:(c + 1) * tk, :].astype(F32).T.astype(vt_ref.dtype)
        vt_ref[c, dv:rows, :] = tail


def _attend(qs, k_refs, vt_ref, s_ref, m_ref, acc_ref):
    n_chunks, _, tk = vt_ref.shape
    n_st = len(qs)
    m_ref[...] = jnp.full(m_ref.shape, -jnp.inf, F32)
    acc_ref[...] = jnp.zeros(acc_ref.shape, F32)

    def scores(i, c, slot):
        off = c * tk if isinstance(c, int) else pl.multiple_of(c * tk, tk)
        s_ref[i, slot] = lax.dot_general(k_refs[i][pl.ds(off, tk), :], qs[i], (((1,), (1,)), ((), ())),
                                         preferred_element_type=F32)

    def update(i, c, slot):
        s = s_ref[i, slot]
        m_old = m_ref[i]
        m_new = jnp.maximum(m_old, jnp.max(s, axis=0, keepdims=True))
        p = jnp.exp2(s - m_new).astype(vt_ref.dtype)
        acc_ref[i] = (acc_ref[i] * jnp.exp2(m_old - m_new)
                      + jnp.dot(vt_ref[c], p, preferred_element_type=F32))
        m_ref[i] = m_new

    for i in range(n_st):
        scores(i, 0, 0)

    def pair(j, carry):
        c = 2 * j
        for i in range(n_st):
            scores(i, c + 1, 1)
        for i in range(n_st):
            update(i, c, 0)
        for i in range(n_st):
            scores(i, c + 2, 0)
        for i in range(n_st):
            update(i, c + 1, 1)
        return carry

    n_pairs = (n_chunks - 1) // 2
    lax.fori_loop(0, n_pairs, pair, 0)
    done = 2 * n_pairs
    if n_chunks - done == 2:
        for i in range(n_st):
            scores(i, done + 1, 1)
    for i in range(n_st):
        update(i, done, 0)
    if n_chunks - done == 2:
        for i in range(n_st):
            update(i, done + 1, 1)


def _gqa_kernel(q_ref, k_ref, v_ref, o_ref, vt_ref, s_ref, m_ref, acc_ref):
    tk = vt_ref.shape[2]
    n_st = s_ref.shape[0]
    pl.when((pl.program_id(1) == 0) & (pl.program_id(2) == 0))(lambda: _build_vt(v_ref, vt_ref, tk))
    qs = [q_ref[:, i * HEAD_DIM:(i + 1) * HEAD_DIM] for i in range(n_st)]
    _attend(qs, [k_ref] * n_st, vt_ref, s_ref, m_ref, acc_ref)
    for i in range(n_st):
        o_t = acc_ref[i, 0:HEAD_DIM, :] / acc_ref[i, HEAD_DIM:HEAD_DIM + 1, :]
        o_ref[:, i * HEAD_DIM:(i + 1) * HEAD_DIM] = o_t.T.astype(o_ref.dtype)


def _key_chunk(t):
    for tk in (1408, 768, 1024, 512, 640, 384, 256, 128):
        if t % tk == 0:
            return tk
    raise ValueError(f"unsupported key count {t}")


def _gqa(proj, s):
    t = proj.shape[0]
    tq = 512
    tk = _key_chunk(t)
    n_st = 4
    per_g = A_GROUP // n_st
    return pl.pallas_call(
        _gqa_kernel,
        grid=(A_KV_HEADS, per_g, s // tq),
        in_specs=[pl.BlockSpec((tq, n_st * HEAD_DIM), lambda g, hh, i: (i, COL_AQ // n_st + g * per_g + hh)),
                  pl.BlockSpec((t, HEAD_DIM), lambda g, hh, i: (0, COL_AK + g)),
                  pl.BlockSpec((t, HEAD_DIM), lambda g, hh, i: (0, COL_AV + g))],
        out_specs=pl.BlockSpec((tq, n_st * HEAD_DIM), lambda g, hh, i: (i, g * per_g + hh)),
        out_shape=jax.ShapeDtypeStruct((s, A_Q_W), BF16),
        scratch_shapes=[pltpu.VMEM((t // tk, HEAD_DIM + ONES_ROWS, tk), BF16),
                        pltpu.VMEM((n_st, 2, tk, tq), F32),
                        pltpu.VMEM((n_st, 1, tq), F32),
                        pltpu.VMEM((n_st, HEAD_DIM + ONES_ROWS, tq), F32)],
        compiler_params=_cparams(("arbitrary", "arbitrary", "arbitrary")),
        name="gqa_attn",
    )(proj, proj, proj)


def _diff_kernel(lam_ref, q0_ref, q1_ref, k0_ref, k1_ref, v_ref, g_ref, o_ref,
                 vt_ref, s_ref, m_ref, acc_ref, *, lam_init):
    tk = vt_ref.shape[2]
    pl.when(pl.program_id(1) == 0)(lambda: _build_vt(v_ref, vt_ref, tk))
    lv = lam_ref[...]
    lam = (jnp.exp(jnp.sum(lv[0:1, :] * lv[1:2, :], axis=-1, keepdims=True))
           - jnp.exp(jnp.sum(lv[2:3, :] * lv[3:4, :], axis=-1, keepdims=True)) + lam_init)
    _attend([q0_ref[...], q1_ref[...]], [k0_ref, k1_ref], vt_ref, s_ref, m_ref, acc_ref)
    o_t = (acc_ref[0, 0:B_V_DIM, :] / acc_ref[0, B_V_DIM:B_V_DIM + 1, :]
           - lam * (acc_ref[1, 0:B_V_DIM, :] / acc_ref[1, B_V_DIM:B_V_DIM + 1, :]))
    o = o_t.T
    y = o * lax.rsqrt(jnp.mean(o * o, axis=-1, keepdims=True) + EPS) * g_ref[...]
    o_ref[...] = (y * (1.0 - lam_init)).astype(o_ref.dtype)


def _diff(proj, s, lam_vecs, subln_g, lam_init):
    t = proj.shape[0]
    tq = 512
    tk = _key_chunk(t)
    vb = B_V_DIM // HEAD_DIM
    return pl.pallas_call(
        functools.partial(_diff_kernel, lam_init=lam_init),
        grid=(B_HEADS, s // tq),
        in_specs=[pl.BlockSpec((4, HEAD_DIM), lambda h, i: (0, 0)),
                  pl.BlockSpec((tq, HEAD_DIM), lambda h, i: (i, COL_BQ + 2 * h)),
                  pl.BlockSpec((tq, HEAD_DIM), lambda h, i: (i, COL_BQ + 2 * h + 1)),
                  pl.BlockSpec((t, HEAD_DIM), lambda h, i: (0, COL_BK + 2 * h)),
                  pl.BlockSpec((t, HEAD_DIM), lambda h, i: (0, COL_BK + 2 * h + 1)),
                  pl.BlockSpec((t, B_V_DIM), lambda h, i: (0, COL_BV // vb + h)),
                  pl.BlockSpec((1, B_V_DIM), lambda h, i: (0, 0))],
        out_specs=pl.BlockSpec((tq, B_V_DIM), lambda h, i: (i, h)),
        out_shape=jax.ShapeDtypeStruct((s, B_V_W), BF16),
        scratch_shapes=[pltpu.VMEM((t // tk, B_V_DIM + ONES_ROWS, tk), BF16),
                        pltpu.VMEM((2, 2, tk, tq), F32),
                        pltpu.VMEM((2, 1, tq), F32),
                        pltpu.VMEM((2, B_V_DIM + ONES_ROWS, tq), F32)],
        compiler_params=_cparams(("arbitrary", "arbitrary")),
        name="diff_attn",
    )(lam_vecs, proj, proj, proj, proj, proj, subln_g.reshape(1, B_V_DIM))


def _merge_kernel(oa_ref, ob_ref, ga_ref, gb_ref, wa_ref, wb_ref, wo_ref, x_ref, gate_ref, o_ref, t_ref):
    c = pl.program_id(1)
    nc, _, tn = t_ref.shape
    ya = jnp.dot(oa_ref[...], wa_ref[...], preferred_element_type=F32)
    yb = jnp.dot(ob_ref[...], wb_ref[...], preferred_element_type=F32)
    t = (jax.nn.sigmoid(ga_ref[...].astype(F32)) * ya + jax.nn.sigmoid(gb_ref[...].astype(F32)) * yb)
    t_ref[c] = t.astype(t_ref.dtype)

    @pl.when(c == nc - 1)
    def _():
        y = jnp.dot(t_ref[0], wo_ref[0:tn, :], preferred_element_type=F32)
        for cc in range(1, nc):
            y = y + jnp.dot(t_ref[cc], wo_ref[cc * tn:(cc + 1) * tn, :], preferred_element_type=F32)
        o_ref[...] = x_ref[...] + gate_ref[...] * y


def _merge(oa, ob, proj, wa, wb, wo, x, gate):
    s, d = x.shape
    tm = 512
    tn = PROJ_TN
    nc = d // tn
    ga0 = COL_GA * HEAD_DIM // tn
    gb0 = COL_GB * HEAD_DIM // tn
    return pl.pallas_call(
        _merge_kernel,
        grid=(s // tm, nc),
        in_specs=[pl.BlockSpec((tm, A_Q_W), lambda i, c: (i, 0)),
                  pl.BlockSpec((tm, B_V_W), lambda i, c: (i, 0)),
                  pl.BlockSpec((tm, tn), lambda i, c: (i, ga0 + c)),
                  pl.BlockSpec((tm, tn), lambda i, c: (i, gb0 + c)),
                  pl.BlockSpec((A_Q_W, tn), lambda i, c: (0, c)),
                  pl.BlockSpec((B_V_W, tn), lambda i, c: (0, c)),
                  pl.BlockSpec((d, d), lambda i, c: (0, 0)),
                  pl.BlockSpec((tm, d), lambda i, c: (i, 0)),
                  pl.BlockSpec((1, d), lambda i, c: (0, 0))],
        out_specs=pl.BlockSpec((tm, d), lambda i, c: (i, 0)),
        out_shape=jax.ShapeDtypeStruct((s, d), F32),
        scratch_shapes=[pltpu.VMEM((nc, tm, tn), BF16)],
        compiler_params=_cparams(("arbitrary", "arbitrary")),
        name="merge_out",
    )(oa, ob, proj, proj, wa, wb, wo, x, gate)


def _store_packed_tiles(ref, x):
    rows = x.shape[0]
    bits = lax.bitcast_convert_type(x.astype(BF16).astype(F32), jnp.uint32)
    half = bits.shape[1] // 2
    words = (bits[:, :half] >> 16) | (bits[:, half:] & jnp.uint32(0xFFFF0000))
    for c in range(PACK_TILES):
        ref[pl.ds(c, rows, stride=PACK_TILES), :] = words[:, c * LANES:(c + 1) * LANES]


def _lane_max(x):
    return jnp.max(x, axis=-1, keepdims=True)


def _lane_min(x):
    return jnp.min(x, axis=-1, keepdims=True)


def _group_allreduce(x, lane, op):
    for sft in (1, 2, 4):
        up = pltpu.roll(x, sft, 1)
        dn = pltpu.roll(x, LANES - sft, 1)
        x = op(x, jnp.where((lane & sft) != 0, up, dn))
    return x


def _router_kernel(x_ref, g_ref, mod_ref, wr_ref, rb_ref,
                   h_ref, hp_ref, idx_ref, wsel_ref, rank_ref, cnt_ref, carry_ref):
    @pl.when(pl.program_id(0) == 0)
    def _():
        carry_ref[...] = jnp.zeros_like(carry_ref)

    h = _rms_mod(x_ref[...], g_ref[...], mod_ref[0, 3:4, :], mod_ref[0, 4:5, :])
    hb = h.astype(BF16)
    h_ref[...] = hb
    _store_packed_tiles(hp_ref, h)
    tm = h.shape[0]

    h_lo = (h - hb.astype(F32)).astype(BF16)
    logits = (jnp.dot(hb, wr_ref[0], preferred_element_type=F32)
              + jnp.dot(hb, wr_ref[1], preferred_element_type=F32)
              + jnp.dot(h_lo, wr_ref[0], preferred_element_type=F32))
    scores = jax.nn.sigmoid(logits)
    lane = lax.broadcasted_iota(jnp.int32, (tm, LANES), 1)
    lane_f = lane.astype(F32)
    gid_f = (lane >> 3).astype(F32)
    valid = lane < N_EXPERTS
    neg = jnp.float32(-jnp.inf)
    big = jnp.float32(LANES)
    biased = jnp.where(valid, scores + rb_ref[...], neg)

    m1 = _group_allreduce(biased, lane, jnp.maximum)
    a1 = _group_allreduce(jnp.where(biased == m1, lane_f, big), lane, jnp.minimum)
    m2 = _group_allreduce(jnp.where(lane_f == a1, neg, biased), lane, jnp.maximum)
    gscore = jnp.where(valid, m1 + m2, neg)
    keep = jnp.zeros((tm, LANES), jnp.bool_)
    for _ in range(TOPK_GROUPS):
        best = _lane_max(gscore)
        gsel = _lane_min(jnp.where(gscore == best, gid_f, big))
        hit = gid_f == gsel
        keep = keep | hit
        gscore = jnp.where(hit, neg, gscore)
    cand = jnp.where(keep & valid, biased, neg)

    onehot = jnp.zeros((tm, LANES), F32)
    idx_out = jnp.zeros((tm, LANES), F32)
    w_out = jnp.zeros((tm, LANES), F32)
    sels = []
    for k in range(TOP_K):
        best = _lane_max(cand)
        sel = _lane_min(jnp.where(cand == best, lane_f, big))
        hit = lane_f == sel
        wk = jnp.sum(jnp.where(hit, scores, 0.0), axis=-1, keepdims=True)
        cand = jnp.where(hit, neg, cand)
        onehot = jnp.where(hit, 1.0, onehot)
        idx_out = jnp.where(lane == k, sel, idx_out)
        w_out = jnp.where(lane == k, wk, w_out)
        sels.append(hit)
    wsum = jnp.sum(w_out, axis=-1, keepdims=True)
    wsel_ref[...] = w_out / wsum * ROUTED_SCALE
    idx_ref[...] = idx_out.astype(jnp.int32)

    row = lax.broadcasted_iota(jnp.int32, (tm, tm), 0)
    col = lax.broadcasted_iota(jnp.int32, (tm, tm), 1)
    lower = (col < row).astype(BF16)
    before = jnp.dot(lower, onehot.astype(BF16), preferred_element_type=F32) + carry_ref[...]
    rank_out = jnp.zeros((tm, LANES), F32)
    for k in range(TOP_K):
        rk = jnp.sum(jnp.where(sels[k], before, 0.0), axis=-1, keepdims=True)
        rank_out = jnp.where(lane == k, rk, rank_out)
    rank_ref[...] = rank_out.astype(jnp.int32)
    carry_ref[...] = carry_ref[...] + jnp.sum(onehot, axis=0, keepdims=True)
    cnt_ref[...] = carry_ref[...]


def _router(x1, g, mod, w_router, router_bias):
    s, d = x1.shape
    tm = 256
    e = w_router.shape[1]
    wr = jnp.pad(w_router, ((0, 0), (0, LANES - e)))
    wr_hi = wr.astype(BF16)
    wr = jnp.stack([wr_hi, (wr - wr_hi.astype(F32)).astype(BF16)])
    rb = jnp.pad(router_bias, (0, LANES - e)).reshape(1, LANES)
    row_spec = lambda w: pl.BlockSpec((tm, w), lambda i: (i, 0))
    return pl.pallas_call(
        _router_kernel,
        grid=(s // tm,),
        in_specs=[row_spec(d),
                  pl.BlockSpec((1, d), lambda i: (0, 0)),
                  pl.BlockSpec((1, N_MOD, d), lambda i: (0, 0, 0)),
                  pl.BlockSpec((2, d, LANES), lambda i: (0, 0, 0)),
                  pl.BlockSpec((1, LANES), lambda i: (0, 0))],
        out_specs=[row_spec(d), pl.BlockSpec((tm * PACK_TILES, LANES), lambda i: (i, 0)),
                   row_spec(LANES), row_spec(LANES), row_spec(LANES),
                   pl.BlockSpec((1, LANES), lambda i: (0, 0))],
        out_shape=[jax.ShapeDtypeStruct((s, d), BF16),
                   jax.ShapeDtypeStruct((s * PACK_TILES, LANES), jnp.uint32),
                   jax.ShapeDtypeStruct((s, LANES), jnp.int32),
                   jax.ShapeDtypeStruct((s, LANES), F32),
                   jax.ShapeDtypeStruct((s, LANES), jnp.int32),
                   jax.ShapeDtypeStruct((1, LANES), F32)],
        scratch_shapes=[pltpu.VMEM((1, LANES), F32)],
        compiler_params=_cparams(("arbitrary",)),
        name="ffn_router",
    )(x1, g.reshape(1, d), mod, wr, rb)


def _unpack_halves(words):
    lo = lax.bitcast_convert_type(words << 16, F32)
    hi = lax.bitcast_convert_type(words & jnp.uint32(0xFFFF0000), F32)
    return lo, hi


def _unpack_rows(words):
    lo, hi = _unpack_halves(words)
    return jnp.concatenate([lo.astype(BF16), hi.astype(BF16)], axis=1)


def _token_rows(row, tiles):
    return pl.ds(pl.multiple_of(row * tiles, tiles), tiles)


def _dispatch_kernel(ps_ref, pe_ref, nused_ref, dest_hbm, hp_ref, xs_hbm,
                     dest_smem, zbuf, sem_i, sem_z, sem_x, *, n_blocks):
    i = pl.program_id(0)
    n = dest_smem.shape[0]
    tm = n // TOP_K
    copy_dest = pltpu.make_async_copy(dest_hbm.at[i], dest_smem, sem_i)
    copy_dest.start()

    @pl.when(i == 0)
    def _():
        zbuf[...] = jnp.zeros_like(zbuf)

        def zero_block(row0):
            span = MOE_BLOCK * PACK_TILES
            return pltpu.make_async_copy(
                zbuf, xs_hbm.at[pl.ds(pl.multiple_of(row0 * PACK_TILES, span), span), :], sem_z)

        def per_expert(action):
            def body(e, carry):
                @pl.when(pe_ref[e] > ps_ref[e])
                def _():
                    action(zero_block(pe_ref[e] - MOE_BLOCK))
                return carry
            lax.fori_loop(0, N_EXPERTS, body, 0)

        def per_tail(action):
            def body(b, carry):
                action(zero_block(b * MOE_BLOCK))
                return carry
            lax.fori_loop(nused_ref[0], n_blocks, body, 0)

        per_expert(lambda cp: cp.start())
        per_tail(lambda cp: cp.start())
        per_expert(lambda cp: cp.wait())
        per_tail(lambda cp: cp.wait())

    copy_dest.wait()

    def issue(t, carry):
        for k in range(TOP_K):
            dest = dest_smem[t * TOP_K + k]
            pltpu.make_async_copy(hp_ref.at[_token_rows(t, PACK_TILES), :],
                                  xs_hbm.at[_token_rows(dest, PACK_TILES), :], sem_x).start(priority=k % 2)
        return carry

    lax.fori_loop(0, tm, issue, 0)
    for _ in range(TOP_K):
        pltpu.make_async_copy(hp_ref, xs_hbm.at[pl.ds(0, tm * PACK_TILES), :], sem_x).wait()


def _dispatch(pad_start, pad_end, n_used, dest_tm, hp, n_blocks):
    n_tiles, n = dest_tm.shape
    tm = n // TOP_K
    grid_spec = pltpu.PrefetchScalarGridSpec(
        num_scalar_prefetch=3,
        grid=(n_tiles,),
        in_specs=[pl.BlockSpec(memory_space=pl.ANY),
                  pl.BlockSpec((tm * PACK_TILES, LANES), lambda i, ps, pe, nu: (i, 0))],
        out_specs=pl.BlockSpec(memory_space=pl.ANY),
        scratch_shapes=[pltpu.SMEM((n,), jnp.int32),
                        pltpu.VMEM((MOE_BLOCK * PACK_TILES, LANES), jnp.uint32),
                        pltpu.SemaphoreType.DMA,
                        pltpu.SemaphoreType.DMA,
                        pltpu.SemaphoreType.DMA])
    return pl.pallas_call(
        functools.partial(_dispatch_kernel, n_blocks=n_blocks),
        grid_spec=grid_spec,
        out_shape=jax.ShapeDtypeStruct((n_blocks * MOE_BLOCK * PACK_TILES, LANES), jnp.uint32),
        compiler_params=_cparams(("arbitrary",)),
        name="moe_dispatch",
    )(pad_start, pad_end, n_used, dest_tm, hp)


def _moe_kernel(nblk_ref, blk0_ref, nxt_ref, slot_ref, first_ref, nused_ref,
                xs_hbm, wg_hbm, wu_hbm, wd_hbm, y_hbm,
                xbuf, ybuf, wg32, wu32, wd32, wgb, wub, wdb, sem_w, sem_x, sem_y, *, n_blocks):
    e = pl.program_id(0)
    nb = nblk_ref[e]
    span = MOE_BLOCK * PACK_TILES

    def block_rows(b):
        return pl.ds(pl.multiple_of(b * span, span), span)

    def fetch(ex, slot):
        return (pltpu.make_async_copy(wg_hbm.at[ex], wg32.at[slot], sem_w.at[slot, 0]),
                pltpu.make_async_copy(wu_hbm.at[ex], wu32.at[slot], sem_w.at[slot, 1]),
                pltpu.make_async_copy(wd_hbm.at[ex], wd32.at[slot], sem_w.at[slot, 2]))

    def x_copy(b, s):
        return pltpu.make_async_copy(xs_hbm.at[block_rows(b), :], xbuf.at[s], sem_x.at[s])

    def y_copy(b, s):
        return pltpu.make_async_copy(ybuf.at[s], y_hbm.at[block_rows(b), :], sem_y.at[s])

    n_used = nused_ref[0]
    n_xbuf = xbuf.shape[0]

    @pl.when(nb > 0)
    def _():
        b0 = blk0_ref[e]
        slot = slot_ref[e]

        @pl.when(e == first_ref[0])
        def _():
            for b in range(n_xbuf - 1):
                pl.when(b < n_used)(lambda b=b: x_copy(b, b).start())
            for cp in fetch(e, 0):
                cp.start(priority=1)

        for cp in fetch(e, slot):
            cp.wait()

        @pl.when(nxt_ref[e] >= 0)
        def _():
            for cp in fetch(nxt_ref[e], 1 - slot):
                cp.start(priority=1)

        wgb[...] = wg32[slot].astype(BF16)
        wub[...] = wu32[slot].astype(BF16)
        wdb[...] = wd32[slot].astype(BF16)

        def body(j, carry):
            b = b0 + j
            s = b % 2
            sx = b % n_xbuf
            ahead = b + n_xbuf - 1
            x_copy(b, sx).wait()
            pl.when(ahead < n_used)(lambda: x_copy(ahead, ahead % n_xbuf).start())
            pl.when(b >= 2)(lambda: y_copy(b - 2, s).wait())
            xb = xbuf.at[sx]
            words = jnp.concatenate(
                [xb[pl.ds(c, MOE_BLOCK, stride=PACK_TILES), :] for c in range(PACK_TILES)], axis=1)
            x = _unpack_rows(words)
            gate = jnp.dot(x, wgb[...], preferred_element_type=F32)
            up = jnp.dot(x, wub[...], preferred_element_type=F32)
            act = (gate * jax.nn.sigmoid(gate) * up).astype(BF16)
            _store_packed_tiles(ybuf.at[s], jnp.dot(act, wdb[...], preferred_element_type=F32))
            y_copy(b, s).start()
            return carry

        lax.fori_loop(0, nb, body, 0)

    @pl.when(e == pl.num_programs(0) - 1)
    def _():
        pl.when(n_used >= 2)(lambda: y_copy(n_used - 2, n_used % 2).wait())
        pl.when(n_used >= 1)(lambda: y_copy(n_used - 1, (n_used - 1) % 2).wait())
        ybuf[0] = jnp.zeros(ybuf.shape[1:], ybuf.dtype)

        def tail(action):
            def step(b, carry):
                action(y_copy(b, 0))
                return carry
            lax.fori_loop(nused_ref[0], n_blocks, step, 0)

        tail(lambda cp: cp.start())
        tail(lambda cp: cp.wait())


def _moe(nblk, blk0, nxt, slot, first, n_used, xs, weg, weu, wed, n_blocks):
    n_exp, d, f = weg.shape
    span = MOE_BLOCK * PACK_TILES
    grid_spec = pltpu.PrefetchScalarGridSpec(
        num_scalar_prefetch=6,
        grid=(n_exp,),
        in_specs=[pl.BlockSpec(memory_space=pl.ANY)] * 4,
        out_specs=pl.BlockSpec(memory_space=pl.ANY),
        scratch_shapes=[pltpu.VMEM((3, span, LANES), jnp.uint32), pltpu.VMEM((2, span, LANES), jnp.uint32),
                        pltpu.VMEM((2, d, f), F32), pltpu.VMEM((2, d, f), F32), pltpu.VMEM((2, f, d), F32),
                        pltpu.VMEM((d, f), BF16), pltpu.VMEM((d, f), BF16), pltpu.VMEM((f, d), BF16),
                        pltpu.SemaphoreType.DMA((2, 3)), pltpu.SemaphoreType.DMA((3,)),
                        pltpu.SemaphoreType.DMA((2,))])
    return pl.pallas_call(
        functools.partial(_moe_kernel, n_blocks=n_blocks),
        grid_spec=grid_spec,
        out_shape=jax.ShapeDtypeStruct((n_blocks * span, LANES), jnp.uint32),
        compiler_params=_cparams(("arbitrary",)),
        name="moe_experts",
    )(nblk, blk0, nxt, slot, first, n_used, xs, weg, weu, wed)


def _combine_kernel(dest_hbm, y_hbm, w_ref, h_ref, wg_ref, wu_ref, wd_ref, x_ref, gate_ref, o_ref,
                    dest0, dest1, ybuf0, ybuf1, sem_i, sem_y):
    i = pl.program_id(0)
    tm = h_ref.shape[0]
    n = tm * TOP_K
    span = n * PACK_TILES
    dests = (dest0, dest1)
    ybufs = (ybuf0, ybuf1)

    def start_tile(tile, slot):
        copy_dest = pltpu.make_async_copy(dest_hbm.at[tile], dests[slot], sem_i)
        copy_dest.start()
        copy_dest.wait()

        def issue(t, carry):
            base = t * TOKEN_PITCH
            for k in range(TOP_K):
                src = dests[slot][t * TOP_K + k]
                pltpu.make_async_copy(
                    y_hbm.at[_token_rows(src, PACK_TILES), :],
                    ybufs[slot].at[pl.ds(pl.multiple_of(base + k * YBUF_PITCH, 4), PACK_TILES), :],
                    sem_y.at[slot]).start(priority=k % 2)
            return carry

        lax.fori_loop(0, tm, issue, 0)

    def step(cur):
        nxt = 1 - cur
        pl.when(i == 0)(lambda: start_tile(0, cur))
        pl.when(i + 1 < pl.num_programs(0))(lambda: start_tile(i + 1, nxt))

        h = h_ref[...]
        gate = jnp.dot(h, wg_ref[...], preferred_element_type=F32)
        up = jnp.dot(h, wu_ref[...], preferred_element_type=F32)
        act = (gate * jax.nn.sigmoid(gate) * up).astype(BF16)
        shared = jnp.dot(act, wd_ref[...], preferred_element_type=F32)

        yb = ybufs[cur]
        pltpu.make_async_copy(y_hbm.at[pl.ds(0, span), :], yb.at[pl.ds(0, span), :], sem_y.at[cur]).wait()
        lows, highs = [], []
        for c in range(PACK_TILES):
            acc_lo = acc_hi = None
            for k in range(TOP_K):
                lo, hi = _unpack_halves(yb[pl.ds(k * YBUF_PITCH + c, tm, stride=TOKEN_PITCH), :])
                wk = w_ref[:, k:k + 1]
                acc_lo = lo * wk if acc_lo is None else acc_lo + lo * wk
                acc_hi = hi * wk if acc_hi is None else acc_hi + hi * wk
            lows.append(acc_lo)
            highs.append(acc_hi)
        routed = jnp.concatenate(lows + highs, axis=1)
        o_ref[...] = x_ref[...] + gate_ref[...] * (routed + shared)

    pl.when(i % 2 == 0)(lambda: step(0))
    pl.when(i % 2 == 1)(lambda: step(1))


def _combine(dest_tm, y_sorted, wsel, h2, wsg, wsu, wsd, x1, gate):
    s, d = x1.shape
    n_tiles, n = dest_tm.shape
    tm = n // TOP_K
    f = wsg.shape[1]
    return pl.pallas_call(
        _combine_kernel,
        grid=(n_tiles,),
        in_specs=[pl.BlockSpec(memory_space=pl.ANY),
                  pl.BlockSpec(memory_space=pl.ANY),
                  pl.BlockSpec((tm, LANES), lambda i: (i, 0)),
                  pl.BlockSpec((tm, d), lambda i: (i, 0)),
                  pl.BlockSpec((d, f), lambda i: (0, 0)),
                  pl.BlockSpec((d, f), lambda i: (0, 0)),
                  pl.BlockSpec((f, d), lambda i: (0, 0)),
                  pl.BlockSpec((tm, d), lambda i: (i, 0)),
                  pl.BlockSpec((1, d), lambda i: (0, 0))],
        out_specs=pl.BlockSpec((tm, d), lambda i: (i, 0)),
        out_shape=jax.ShapeDtypeStruct((s, d), F32),
        scratch_shapes=[pltpu.SMEM((n,), jnp.int32),
                        pltpu.SMEM((n,), jnp.int32),
                        pltpu.VMEM((tm * TOKEN_PITCH, LANES), jnp.uint32),
                        pltpu.VMEM((tm * TOKEN_PITCH, LANES), jnp.uint32),
                        pltpu.SemaphoreType.DMA,
                        pltpu.SemaphoreType.DMA((2,))],
        compiler_params=_cparams(("arbitrary",)),
        name="moe_combine",
    )(dest_tm, y_sorted, wsel, h2, wsg, wsu, wsd, x1, gate)


def _expert_tables(counts):
    i32 = jnp.int32
    padded = ((counts + MOE_BLOCK - 1) // MOE_BLOCK * MOE_BLOCK).astype(i32)
    pad_end = jnp.cumsum(padded).astype(i32)
    pad_start = pad_end - padded
    ids = jnp.arange(N_EXPERTS, dtype=i32)
    busy = padded > 0
    later_busy = busy[None, :] & (ids[None, :] > ids[:, None])
    nxt = jnp.min(jnp.where(later_busy, ids[None, :], N_EXPERTS), axis=1)
    nxt = jnp.where(nxt < N_EXPERTS, nxt, -1).astype(i32)
    slot = (jnp.maximum(jnp.cumsum(busy.astype(i32)) - 1, 0) % 2).astype(i32)
    first = jnp.min(jnp.where(busy, ids, N_EXPERTS)).astype(i32).reshape(1)
    n_used = (pad_end[-1] // MOE_BLOCK).reshape(1)
    return pad_start, pad_end, padded // MOE_BLOCK, pad_start // MOE_BLOCK, nxt, slot, first, n_used


def kernel(x, c, ctx, c_ctx, w_ada, b_ada, norm_mix, norm_ffn, w_in, q_norm_a, k_norm_a, q_norm_b, k_norm_b, lambda_q1, lambda_k1, lambda_q2, lambda_k2, subln_b, w_branch_a, w_branch_b, w_out, w_router, router_bias, w_exp_gate, w_exp_up, w_exp_down, w_sh_gate, w_sh_up, w_sh_down):
    depth = w_ada.shape[0]
    assert depth == 1 and x.shape[0] == 1 and ctx.shape[0] == 1
    s, d = x.shape[1], x.shape[2]
    n_ctx = ctx.shape[1]
    i = 0
    lam_init = 0.8 - 0.6 * math.exp(-0.3 * i)
    xs = x[0]

    mod = _adaln(jnp.concatenate([c, c_ctx[None, :]], axis=0), w_ada[i], b_ada[i]).reshape(2, N_MOD, d)

    h = _prenorm(xs, ctx[0], norm_mix[i], mod)
    tc, tsa, tsb = _rope_tables(s, n_ctx)
    gains = _head_gains(q_norm_a[i], k_norm_a[i], q_norm_b[i], k_norm_b[i])
    proj = _inproj(h, w_in[i], gains, tc, tsa, tsb)
    oa = _gqa(proj, s)
    lam_vecs = jnp.stack([lambda_q1[i], lambda_k1[i], lambda_q2[i], lambda_k2[i]]).astype(F32)
    ob = _diff(proj, s, lam_vecs, subln_b[i], lam_init)
    x1 = _merge(oa, ob, proj, w_branch_a[i].astype(BF16), w_branch_b[i].astype(BF16),
                w_out[i].astype(BF16), xs, mod[0, 2:3, :])

    h2, h2p, idx, wsel, rank, cnt = _router(x1, norm_ffn[i], mod[0:1], w_router[i], router_bias[i])
    counts = cnt[0, :N_EXPERTS].astype(jnp.int32)
    n_blocks = -(-(s * TOP_K) // MOE_BLOCK) + N_EXPERTS
    pad_start, pad_end, nblk, blk0, nxt, slot, first, n_used = _expert_tables(counts)
    tm_r = 256
    e_ids = jnp.arange(N_EXPERTS, dtype=jnp.int32)
    starts = jnp.sum(jnp.where(idx[:, :TOP_K, None] == e_ids, pad_start, 0), axis=-1)
    dest_tm = (starts + rank[:, :TOP_K]).astype(jnp.int32).reshape(s // tm_r, tm_r * TOP_K)
    xs = _dispatch(pad_start, pad_end, n_used, dest_tm, h2p, n_blocks)
    y_sorted = _moe(nblk, blk0, nxt, slot, first, n_used, xs,
                    w_exp_gate[i], w_exp_up[i], w_exp_down[i], n_blocks)
    out = _combine(dest_tm, y_sorted, wsel, h2,
                   w_sh_gate[i].astype(BF16), w_sh_up[i].astype(BF16), w_sh_down[i].astype(BF16),
                   x1, mod[0, 5:6, :])
    return out[None]
```

```python
import functools
import math

import jax
import jax.numpy as jnp
from jax import lax
from jax.experimental import pallas as pl
from jax.experimental.pallas import tpu as pltpu

F32 = jnp.float32
BF16 = jnp.bfloat16

D_MODEL = 2048
GRID_W = 64
HEAD_DIM = 128
ROPE_PAIRS = HEAD_DIM // 4
ROPE_THETA = 10000.0
A_HEADS = 8
A_KV_HEADS = 2
A_GROUP = A_HEADS // A_KV_HEADS
B_HEADS = 4
B_V_DIM = 2 * HEAD_DIM
N_EXPERTS = 64
TOP_K = 8
N_GROUPS = 8
TOPK_GROUPS = 4
EXPERT_DIM = 512
SHARED_DIM = 512
ROUTED_SCALE = 2.5
N_MOD = 6
EPS = 1e-6

A_Q_W = A_HEADS * HEAD_DIM
A_KV_W = A_KV_HEADS * HEAD_DIM
B_QK_W = B_HEADS * 2 * HEAD_DIM
B_V_W = B_HEADS * B_V_DIM
IN_W = A_Q_W + 2 * A_KV_W + 2 * B_QK_W + B_V_W + 2 * D_MODEL

COL_AQ = 0
COL_AK = A_Q_W // HEAD_DIM
COL_AV = COL_AK + A_KV_HEADS
COL_BQ = COL_AV + A_KV_HEADS
COL_BK = COL_BQ + 2 * B_HEADS
COL_BV = COL_BK + 2 * B_HEADS
COL_GA = COL_BV + B_V_W // HEAD_DIM
COL_GB = COL_GA + D_MODEL // HEAD_DIM

LANES = 128
SUBLANES = 8
VMEM_LIMIT = 56 * 1024 * 1024

PROJ_TN = 512
MOE_BLOCK = 256
ONES_ROWS = 16
PACK_TILES = D_MODEL // 2 // LANES
YBUF_PITCH = PACK_TILES + 4
TOKEN_PITCH = TOP_K * YBUF_PITCH + 4
LOG2E = 1.4426950408889634


def _cparams(sem, vmem=VMEM_LIMIT):
    return pltpu.CompilerParams(dimension_semantics=sem, vmem_limit_bytes=vmem)


def _adaln_kernel(cb_ref, w_ref, b_ref, o_ref):
    tn = w_ref.shape[1]
    nl = tn // LANES
    rows = 32

    def body(g, accs):
        accs = list(accs)
        r0 = pl.multiple_of(g * rows, rows)
        for u in range(rows // SUBLANES):
            r = r0 + u * SUBLANES
            w = w_ref[pl.ds(r, SUBLANES), :]
            for v in range(2):
                c = cb_ref[v, pl.ds(r, SUBLANES), :]
                s = c * jax.nn.sigmoid(c)
                for j in range(nl):
                    accs[v * nl + j] = accs[v * nl + j] + w[:, j * LANES:(j + 1) * LANES] * s
        return tuple(accs)

    init = tuple(jnp.zeros((SUBLANES, LANES), F32) for _ in range(2 * nl))
    accs = lax.fori_loop(0, w_ref.shape[0] // rows, body, init)
    for v in range(2):
        row = jnp.concatenate(
            [jnp.sum(accs[v * nl + j], axis=0, keepdims=True) for j in range(nl)], axis=1)
        o_ref[v:v + 1, :] = row + b_ref[...]


def _adaln(cvecs, w, b):
    d, n = w.shape
    tn = 1536
    cb = jnp.broadcast_to(cvecs[:, :, None], (2, d, LANES))
    return pl.pallas_call(
        _adaln_kernel,
        grid=(n // tn,),
        in_specs=[pl.BlockSpec((2, d, LANES), lambda j: (0, 0, 0)),
                  pl.BlockSpec((d, tn), lambda j: (0, j)),
                  pl.BlockSpec((1, tn), lambda j: (0, j))],
        out_specs=pl.BlockSpec((2, tn), lambda j: (0, j)),
        out_shape=jax.ShapeDtypeStruct((2, n), F32),
        compiler_params=_cparams(("arbitrary",)),
        name="adaln",
    )(cb, w, b.reshape(1, n))


def _rms_mod(x, g, shift, scale):
    y = x * lax.rsqrt(jnp.mean(x * x, axis=-1, keepdims=True) + EPS) * g
    return y * (1.0 + scale) + shift


def _prenorm_kernel(x_ref, c_ref, g_ref, mod_ref, o_ref, *, n_lat_tiles):
    is_ctx = pl.program_id(0) >= n_lat_tiles
    x = jnp.where(is_ctx, c_ref[...], x_ref[...])
    o_ref[...] = _rms_mod(x, g_ref[...], mod_ref[0, 0:1, :], mod_ref[0, 1:2, :]).astype(o_ref.dtype)


def _prenorm(x, ctx, g, mod):
    s, d = x.shape
    c = ctx.shape[0]
    tm = 256
    nl, nc = s // tm, c // tm
    return pl.pallas_call(
        functools.partial(_prenorm_kernel, n_lat_tiles=nl),
        grid=(nl + nc,),
        in_specs=[pl.BlockSpec((tm, d), lambda i: (jnp.minimum(i, nl - 1), 0)),
                  pl.BlockSpec((tm, d), lambda i: (jnp.maximum(i - nl, 0), 0)),
                  pl.BlockSpec((1, d), lambda i: (0, 0)),
                  pl.BlockSpec((1, N_MOD, d), lambda i: (i // nl, 0, 0))],
        out_specs=pl.BlockSpec((tm, d), lambda i: (i, 0)),
        out_shape=jax.ShapeDtypeStruct((s + c, d), BF16),
        compiler_params=_cparams(("arbitrary",)),
        name="prenorm_mix",
    )(x, ctx, g.reshape(1, d), mod)


def _inproj_kernel(h_ref, w_ref, gain_ref, c_ref, sa_ref, sb_ref, o_ref):
    j = pl.program_id(1)
    acc = jnp.dot(h_ref[...], w_ref[...].astype(h_ref.dtype), preferred_element_type=F32)
    nh = acc.shape[1] // HEAD_DIM

    def norm_rope(a, gain):
        y = a * lax.rsqrt(jnp.mean(a * a, axis=-1, keepdims=True) + EPS) * gain
        return (y * c_ref[...] + pltpu.roll(y, ROPE_PAIRS, 1) * sa_ref[...]
                + pltpu.roll(y, HEAD_DIM - ROPE_PAIRS, 1) * sb_ref[...])

    def store(n_normed):
        for hd in range(nh):
            sl = slice(hd * HEAD_DIM, (hd + 1) * HEAD_DIM)
            a = acc[:, sl]
            if hd < n_normed:
                a = norm_rope(a, gain_ref[0, :, sl])
            o_ref[:, sl] = a.astype(o_ref.dtype)

    all_normed = (j < 2) | ((j >= 3) & (j < 7))
    pl.when(all_normed)(lambda: store(nh))
    pl.when(j == 2)(lambda: store(A_KV_HEADS))
    pl.when(j >= 7)(lambda: store(0))


def _inproj(h, w, gains, rope_c, rope_sa, rope_sb):
    t, d = h.shape
    n = w.shape[1]
    tm = t // 8
    tn = PROJ_TN
    return pl.pallas_call(
        _inproj_kernel,
        grid=(t // tm, n // tn),
        in_specs=[pl.BlockSpec((tm, d), lambda i, j: (i, 0)),
                  pl.BlockSpec((d, tn), lambda i, j: (0, j)),
                  pl.BlockSpec((1, 1, tn), lambda i, j: (j, 0, 0)),
                  pl.BlockSpec((tm, HEAD_DIM), lambda i, j: (i, 0)),
                  pl.BlockSpec((tm, HEAD_DIM), lambda i, j: (i, 0)),
                  pl.BlockSpec((tm, HEAD_DIM), lambda i, j: (i, 0))],
        out_specs=pl.BlockSpec((tm, tn), lambda i, j: (i, j)),
        out_shape=jax.ShapeDtypeStruct((t, n), BF16),
        compiler_params=_cparams(("arbitrary", "arbitrary")),
        name="inproj",
    )(h, w, gains, rope_c, rope_sa, rope_sb)


def _rope_tables(s, c):
    rows_n = s // GRID_W
    inv = ROPE_THETA ** (-jnp.arange(ROPE_PAIRS, dtype=F32) / ROPE_PAIRS)
    ang_r = jnp.arange(rows_n, dtype=F32)[:, None] * inv
    ang_c = jnp.arange(GRID_W, dtype=F32)[:, None] * inv
    cr, sr, cc, sc = jnp.cos(ang_r), jnp.sin(ang_r), jnp.cos(ang_c), jnp.sin(ang_c)
    zr, zc = jnp.zeros_like(sr), jnp.zeros_like(sc)

    def table(row_parts, col_parts, ctx_value):
        by_row = jnp.concatenate(row_parts + [zr, zr], axis=1)
        by_col = jnp.concatenate([zc, zc] + col_parts, axis=1)
        lat = (by_row[:, None, :] + by_col[None, :, :]).reshape(s, HEAD_DIM)
        return jnp.concatenate([lat, jnp.full((c, HEAD_DIM), ctx_value, F32)], axis=0)

    return (table([cr, cr], [cc, cc], 1.0), table([zr, sr], [zc, sc], 0.0), table([-sr, zr], [-sc, zc], 0.0))


def _head_gains(qn_a, kn_a, qn_b, kn_b):
    qs = HEAD_DIM ** -0.5 * LOG2E
    one = jnp.ones((HEAD_DIM,), F32)
    heads = ([qn_a * qs] * A_HEADS + [kn_a] * A_KV_HEADS + [one] * A_KV_HEADS
             + [qn_b * qs] * (2 * B_HEADS) + [kn_b] * (2 * B_HEADS))
    heads = heads + [one] * (IN_W // HEAD_DIM - len(heads))
    return jnp.concatenate(heads).reshape(IN_W // PROJ_TN, 1, PROJ_TN)


def _build_vt(v_ref, vt_ref, tk):
    n_chunks, rows, _ = vt_ref.shape
    dv = rows - ONES_ROWS
    tail = (lax.broadcasted_iota(jnp.int32, (ONES_ROWS, tk), 0) == 0).astype(vt_ref.dtype)
    for c in range(n_chunks):
        vt_ref[c, 0:dv, :] = v_ref[c * tk:(c + 1) * tk, :].astype(F32).T.astype(vt_ref.dtype)
        vt_ref[c, dv:rows, :] = tail


def _attend(qs, k_refs, vt_ref, s_ref, m_ref, acc_ref):
    n_chunks, _, tk = vt_ref.shape
    n_st = len(qs)
    m_ref[...] = jnp.full(m_ref.shape, -jnp.inf, F32)
    acc_ref[...] = jnp.zeros(acc_ref.shape, F32)

    def scores(i, c, slot):
        off = c * tk if isinstance(c, int) else pl.multiple_of(c * tk, tk)
        s_ref[i, slot] = lax.dot_general(k_refs[i][pl.ds(off, tk), :], qs[i], (((1,), (1,)), ((), ())),
                                         preferred_element_type=F32)

    def update(i, c, slot):
        s = s_ref[i, slot]
        m_old = m_ref[i]
        m_new = jnp.maximum(m_old, jnp.max(s, axis=0, keepdims=True))
        p = jnp.exp2(s - m_new).astype(vt_ref.dtype)
        acc_ref[i] = (acc_ref[i] * jnp.exp2(m_old - m_new)
                      + jnp.dot(vt_ref[c], p, preferred_element_type=F32))
        m_ref[i] = m_new

    for i in range(n_st):
        scores(i, 0, 0)

    def pair(j, carry):
        c = 2 * j
        for i in range(n_st):
            scores(i, c + 1, 1)
        for i in range(n_st):
            update(i, c, 0)
        for i in range(n_st):
            scores(i, c + 2, 0)
        for i in range(n_st):
            update(i, c + 1, 1)
        return carry

    n_pairs = (n_chunks - 1) // 2
    lax.fori_loop(0, n_pairs, pair, 0)
    done = 2 * n_pairs
    if n_chunks - done == 2:
        for i in range(n_st):
            scores(i, done + 1, 1)
    for i in range(n_st):
        update(i, done, 0)
    if n_chunks - done == 2:
        for i in range(n_st):
            update(i, done + 1, 1)


def _gqa_kernel(q_ref, k_ref, v_ref, o_ref, vt_ref, s_ref, m_ref, acc_ref):
    tk = vt_ref.shape[2]
    n_st = s_ref.shape[0]
    pl.when((pl.program_id(1) == 0) & (pl.program_id(2) == 0))(lambda: _build_vt(v_ref, vt_ref, tk))
    qs = [q_ref[:, i * HEAD_DIM:(i + 1) * HEAD_DIM] for i in range(n_st)]
    _attend(qs, [k_ref] * n_st, vt_ref, s_ref, m_ref, acc_ref)
    for i in range(n_st):
        o_t = acc_ref[i, 0:HEAD_DIM, :] / acc_ref[i, HEAD_DIM:HEAD_DIM + 1, :]
        o_ref[:, i * HEAD_DIM:(i + 1) * HEAD_DIM] = o_t.T.astype(o_ref.dtype)


def _key_chunk(t):
    for tk in (1408, 768, 1024, 512, 640, 384, 256, 128):
        if t % tk == 0:
            return tk
    raise ValueError(f"unsupported key count {t}")


def _gqa(proj, s):
    t = proj.shape[0]
    tq = 512
    tk = _key_chunk(t)
    n_st = 4
    per_g = A_GROUP // n_st
    return pl.pallas_call(
        _gqa_kernel,
        grid=(A_KV_HEADS, per_g, s // tq),
        in_specs=[pl.BlockSpec((tq, n_st * HEAD_DIM), lambda g, hh, i: (i, COL_AQ // n_st + g * per_g + hh)),
                  pl.BlockSpec((t, HEAD_DIM), lambda g, hh, i: (0, COL_AK + g)),
                  pl.BlockSpec((t, HEAD_DIM), lambda g, hh, i: (0, COL_AV + g))],
        out_specs=pl.BlockSpec((tq, n_st * HEAD_DIM), lambda g, hh, i: (i, g * per_g + hh)),
        out_shape=jax.ShapeDtypeStruct((s, A_Q_W), BF16),
        scratch_shapes=[pltpu.VMEM((t // tk, HEAD_DIM + ONES_ROWS, tk), BF16),
                        pltpu.VMEM((n_st, 2, tk, tq), F32),
                        pltpu.VMEM((n_st, 1, tq), F32),
                        pltpu.VMEM((n_st, HEAD_DIM + ONES_ROWS, tq), F32)],
        compiler_params=_cparams(("arbitrary", "arbitrary", "arbitrary")),
        name="gqa_attn",
    )(proj, proj, proj)


def _diff_kernel(lam_ref, q0_ref, q1_ref, k0_ref, k1_ref, v_ref, g_ref, o_ref,
                 vt_ref, s_ref, m_ref, acc_ref, *, lam_init):
    tk = vt_ref.shape[2]
    pl.when(pl.program_id(1) == 0)(lambda: _build_vt(v_ref, vt_ref, tk))
    lv = lam_ref[...]
    lam = (jnp.exp(jnp.sum(lv[0:1, :] * lv[1:2, :], axis=-1, keepdims=True))
           - jnp.exp(jnp.sum(lv[2:3, :] * lv[3:4, :], axis=-1, keepdims=True)) + lam_init)
    _attend([q0_ref[...], q1_ref[...]], [k0_ref, k1_ref], vt_ref, s_ref, m_ref, acc_ref)
    o_t = (acc_ref[0, 0:B_V_DIM, :] / acc_ref[0, B_V_DIM:B_V_DIM + 1, :]
           - lam * (acc_ref[1, 0:B_V_DIM, :] / acc_ref[1, B_V_DIM:B_V_DIM + 1, :]))
    o = o_t.T
    y = o * lax.rsqrt(jnp.mean(o * o, axis=-1, keepdims=True) + EPS) * g_ref[...]
    o_ref[...] = (y * (1.0 - lam_init)).astype(o_ref.dtype)


def _diff(proj, s, lam_vecs, subln_g, lam_init):
    t = proj.shape[0]
    tq = 512
    tk = _key_chunk(t)
    vb = B_V_DIM // HEAD_DIM
    return pl.pallas_call(
        functools.partial(_diff_kernel, lam_init=lam_init),
        grid=(B_HEADS, s // tq),
        in_specs=[pl.BlockSpec((4, HEAD_DIM), lambda h, i: (0, 0)),
                  pl.BlockSpec((tq, HEAD_DIM), lambda h, i: (i, COL_BQ + 2 * h)),
                  pl.BlockSpec((tq, HEAD_DIM), lambda h, i: (i, COL_BQ + 2 * h + 1)),
                  pl.BlockSpec((t, HEAD_DIM), lambda h, i: (0, COL_BK + 2 * h)),
                  pl.BlockSpec((t, HEAD_DIM), lambda h, i: (0, COL_BK + 2 * h + 1)),
                  pl.BlockSpec((t, B_V_DIM), lambda h, i: (0, COL_BV // vb + h)),
                  pl.BlockSpec((1, B_V_DIM), lambda h, i: (0, 0))],
        out_specs=pl.BlockSpec((tq, B_V_DIM), lambda h, i: (i, h)),
        out_shape=jax.ShapeDtypeStruct((s, B_V_W), BF16),
        scratch_shapes=[pltpu.VMEM((t // tk, B_V_DIM + ONES_ROWS, tk), BF16),
                        pltpu.VMEM((2, 2, tk, tq), F32),
                        pltpu.VMEM((2, 1, tq), F32),
                        pltpu.VMEM((2, B_V_DIM + ONES_ROWS, tq), F32)],
        compiler_params=_cparams(("arbitrary", "arbitrary")),
        name="diff_attn",
    )(lam_vecs, proj, proj, proj, proj, proj, subln_g.reshape(1, B_V_DIM))


def _merge_kernel(oa_ref, ob_ref, ga_ref, gb_ref, wa_ref, wb_ref, wo_ref, x_ref, gate_ref, o_ref, t_ref):
    c = pl.program_id(1)
    nc, _, tn = t_ref.shape
    ya = jnp.dot(oa_ref[...], wa_ref[...], preferred_element_type=F32)
    yb = jnp.dot(ob_ref[...], wb_ref[...], preferred_element_type=F32)
    t = (jax.nn.sigmoid(ga_ref[...].astype(F32)) * ya + jax.nn.sigmoid(gb_ref[...].astype(F32)) * yb)
    t_ref[c] = t.astype(t_ref.dtype)

    @pl.when(c == nc - 1)
    def _():
        y = jnp.dot(t_ref[0], wo_ref[0:tn, :], preferred_element_type=F32)
        for cc in range(1, nc):
            y = y + jnp.dot(t_ref[cc], wo_ref[cc * tn:(cc + 1) * tn, :], preferred_element_type=F32)
        o_ref[...] = x_ref[...] + gate_ref[...] * y


def _merge(oa, ob, proj, wa, wb, wo, x, gate):
    s, d = x.shape
    tm = 512
    tn = PROJ_TN
    nc = d // tn
    ga0 = COL_GA * HEAD_DIM // tn
    gb0 = COL_GB * HEAD_DIM // tn
    return pl.pallas_call(
        _merge_kernel,
        grid=(s // tm, nc),
        in_specs=[pl.BlockSpec((tm, A_Q_W), lambda i, c: (i, 0)),
                  pl.BlockSpec((tm, B_V_W), lambda i, c: (i, 0)),
                  pl.BlockSpec((tm, tn), lambda i, c: (i, ga0 + c)),
                  pl.BlockSpec((tm, tn), lambda i, c: (i, gb0 + c)),
                  pl.BlockSpec((A_Q_W, tn), lambda i, c: (0, c)),
                  pl.BlockSpec((B_V_W, tn), lambda i, c: (0, c)),
                  pl.BlockSpec((d, d), lambda i, c: (0, 0)),
                  pl.BlockSpec((tm, d), lambda i, c: (i, 0)),
                  pl.BlockSpec((1, d), lambda i, c: (0, 0))],
        out_specs=pl.BlockSpec((tm, d), lambda i, c: (i, 0)),
        out_shape=jax.ShapeDtypeStruct((s, d), F32),
        scratch_shapes=[pltpu.VMEM((nc, tm, tn), BF16)],
        compiler_params=_cparams(("arbitrary", "arbitrary")),
        name="merge_out",
    )(oa, ob, proj, proj, wa, wb, wo, x, gate)


def _store_packed_tiles(ref, x):
    rows = x.shape[0]
    bits = lax.bitcast_convert_type(x.astype(BF16).astype(F32), jnp.uint32)
    half = bits.shape[1] // 2
    words = (bits[:, :half] >> 16) | (bits[:, half:] & jnp.uint32(0xFFFF0000))
    for c in range(PACK_TILES):
        ref[pl.ds(c, rows, stride=PACK_TILES), :] = words[:, c * LANES:(c + 1) * LANES]


def _lane_max(x):
    return jnp.max(x, axis=-1, keepdims=True)


def _lane_min(x):
    return jnp.min(x, axis=-1, keepdims=True)


def _group_allreduce(x, lane, op):
    for sft in (1, 2, 4):
        up = pltpu.roll(x, sft, 1)
        dn = pltpu.roll(x, LANES - sft, 1)
        x = op(x, jnp.where((lane & sft) != 0, up, dn))
    return x


def _router_kernel(x_ref, g_ref, mod_ref, wr_ref, rb_ref,
                   h_ref, hp_ref, idx_ref, wsel_ref, rank_ref, cnt_ref, carry_ref):
    @pl.when(pl.program_id(0) == 0)
    def _():
        carry_ref[...] = jnp.zeros_like(carry_ref)

    h = _rms_mod(x_ref[...], g_ref[...], mod_ref[0, 3:4, :], mod_ref[0, 4:5, :])
    hb = h.astype(BF16)
    h_ref[...] = hb
    _store_packed_tiles(hp_ref, h)
    tm = h.shape[0]

    h_lo = (h - hb.astype(F32)).astype(BF16)
    logits = (jnp.dot(hb, wr_ref[0], preferred_element_type=F32)
              + jnp.dot(hb, wr_ref[1], preferred_element_type=F32)
              + jnp.dot(h_lo, wr_ref[0], preferred_element_type=F32))
    scores = jax.nn.sigmoid(logits)
    lane = lax.broadcasted_iota(jnp.int32, (tm, LANES), 1)
    lane_f = lane.astype(F32)
    gid_f = (lane >> 3).astype(F32)
    valid = lane < N_EXPERTS
    neg = jnp.float32(-jnp.inf)
    big = jnp.float32(LANES)
    biased = jnp.where(valid, scores + rb_ref[...], neg)

    m1 = _group_allreduce(biased, lane, jnp.maximum)
    a1 = _group_allreduce(jnp.where(biased == m1, lane_f, big), lane, jnp.minimum)
    m2 = _group_allreduce(jnp.where(lane_f == a1, neg, biased), lane, jnp.maximum)
    gscore = jnp.where(valid, m1 + m2, neg)
    keep = jnp.zeros((tm, LANES), jnp.bool_)
    for _ in range(TOPK_GROUPS):
        best = _lane_max(gscore)
        gsel = _lane_min(jnp.where(gscore == best, gid_f, big))
        hit = gid_f == gsel
        keep = keep | hit
        gscore = jnp.where(hit, neg, gscore)
    cand = jnp.where(keep & valid, biased, neg)

    onehot = jnp.zeros((tm, LANES), F32)
    idx_out = jnp.zeros((tm, LANES), F32)
    w_out = jnp.zeros((tm, LANES), F32)
    sels = []
    for k in range(TOP_K):
        best = _lane_max(cand)
        sel = _lane_min(jnp.where(cand == best, lane_f, big))
        hit = lane_f == sel
        wk = jnp.sum(jnp.where(hit, scores, 0.0), axis=-1, keepdims=True)
        cand = jnp.where(hit, neg, cand)
        onehot = jnp.where(hit, 1.0, onehot)
        idx_out = jnp.where(lane == k, sel, idx_out)
        w_out = jnp.where(lane == k, wk, w_out)
        sels.append(hit)
    wsum = jnp.sum(w_out, axis=-1, keepdims=True)
    wsel_ref[...] = w_out / wsum * ROUTED_SCALE
    idx_ref[...] = idx_out.astype(jnp.int32)

    row = lax.broadcasted_iota(jnp.int32, (tm, tm), 0)
    col = lax.broadcasted_iota(jnp.int32, (tm, tm), 1)
    lower = (col < row).astype(BF16)
    before = jnp.dot(lower, onehot.astype(BF16), preferred_element_type=F32) + carry_ref[...]
    rank_out = jnp.zeros((tm, LANES), F32)
    for k in range(TOP_K):
        rk = jnp.sum(jnp.where(sels[k], before, 0.0), axis=-1, keepdims=True)
        rank_out = jnp.where(lane == k, rk, rank_out)
    rank_ref[...] = rank_out.astype(jnp.int32)
    carry_ref[...] = carry_ref[...] + jnp.sum(onehot, axis=0, keepdims=True)
    cnt_ref[...] = carry_ref[...]


def _router(x1, g, mod, w_router, router_bias):
    s, d = x1.shape
    tm = 256
    e = w_router.shape[1]
    wr = jnp.pad(w_router, ((0, 0), (0, LANES - e)))
    wr_hi = wr.astype(BF16)
    wr = jnp.stack([wr_hi, (wr - wr_hi.astype(F32)).astype(BF16)])
    rb = jnp.pad(router_bias, (0, LANES - e)).reshape(1, LANES)
    row_spec = lambda w: pl.BlockSpec((tm, w), lambda i: (i, 0))
    return pl.pallas_call(
        _router_kernel,
        grid=(s // tm,),
        in_specs=[row_spec(d),
                  pl.BlockSpec((1, d), lambda i: (0, 0)),
                  pl.BlockSpec((1, N_MOD, d), lambda i: (0, 0, 0)),
                  pl.BlockSpec((2, d, LANES), lambda i: (0, 0, 0)),
                  pl.BlockSpec((1, LANES), lambda i: (0, 0))],
        out_specs=[row_spec(d), pl.BlockSpec((tm * PACK_TILES, LANES), lambda i: (i, 0)),
                   row_spec(LANES), row_spec(LANES), row_spec(LANES),
                   pl.BlockSpec((1, LANES), lambda i: (0, 0))],
        out_shape=[jax.ShapeDtypeStruct((s, d), BF16),
                   jax.ShapeDtypeStruct((s * PACK_TILES, LANES), jnp.uint32),
                   jax.ShapeDtypeStruct((s, LANES), jnp.int32),
                   jax.ShapeDtypeStruct((s, LANES), F32),
                   jax.ShapeDtypeStruct((s, LANES), jnp.int32),
                   jax.ShapeDtypeStruct((1, LANES), F32)],
        scratch_shapes=[pltpu.VMEM((1, LANES), F32)],
        compiler_params=_cparams(("arbitrary",)),
        name="ffn_router",
    )(x1, g.reshape(1, d), mod, wr, rb)


def _unpack_halves(words):
    lo = lax.bitcast_convert_type(words << 16, F32)
    hi = lax.bitcast_convert_type(words & jnp.uint32(0xFFFF0000), F32)
    return lo, hi


def _unpack_rows(words):
    lo, hi = _unpack_halves(words)
    return jnp.concatenate([lo.astype(BF16), hi.astype(BF16)], axis=1)


def _token_rows(row, tiles):
    return pl.ds(pl.multiple_of(row * tiles, tiles), tiles)


def _dispatch_kernel(ps_ref, pe_ref, nused_ref, dest_hbm, hp_ref, xs_hbm,
                     dest_smem, zbuf, sem_i, sem_z, sem_x, *, n_blocks):
    i = pl.program_id(0)
    n = dest_smem.shape[0]
    tm = n // TOP_K
    copy_dest = pltpu.make_async_copy(dest_hbm.at[i], dest_smem, sem_i)
    copy_dest.start()

    @pl.when(i == 0)
    def _():
        zbuf[...] = jnp.zeros_like(zbuf)

        def zero_block(row0):
            span = MOE_BLOCK * PACK_TILES
            return pltpu.make_async_copy(
                zbuf, xs_hbm.at[pl.ds(pl.multiple_of(row0 * PACK_TILES, span), span), :], sem_z)

        def per_expert(action):
            def body(e, carry):
                @pl.when(pe_ref[e] > ps_ref[e])
                def _():
                    action(zero_block(pe_ref[e] - MOE_BLOCK))
                return carry
            lax.fori_loop(0, N_EXPERTS, body, 0)

        def per_tail(action):
            def body(b, carry):
                action(zero_block(b * MOE_BLOCK))
                return carry
            lax.fori_loop(nused_ref[0], n_blocks, body, 0)

        per_expert(lambda cp: cp.start())
        per_tail(lambda cp: cp.start())
        per_expert(lambda cp: cp.wait())
        per_tail(lambda cp: cp.wait())

    copy_dest.wait()

    def issue(t, carry):
        for k in range(TOP_K):
            dest = dest_smem[t * TOP_K + k]
            pltpu.make_async_copy(hp_ref.at[_token_rows(t, PACK_TILES), :],
                                  xs_hbm.at[_token_rows(dest, PACK_TILES), :], sem_x).start(priority=k % 2)
        return carry

    lax.fori_loop(0, tm, issue, 0)
    for _ in range(TOP_K):
        pltpu.make_async_copy(hp_ref, xs_hbm.at[pl.ds(0, tm * PACK_TILES), :], sem_x).wait()


def _dispatch(pad_start, pad_end, n_used, dest_tm, hp, n_blocks):
    n_tiles, n = dest_tm.shape
    tm = n // TOP_K
    grid_spec = pltpu.PrefetchScalarGridSpec(
        num_scalar_prefetch=3,
        grid=(n_tiles,),
        in_specs=[pl.BlockSpec(memory_space=pl.ANY),
                  pl.BlockSpec((tm * PACK_TILES, LANES), lambda i, ps, pe, nu: (i, 0))],
        out_specs=pl.BlockSpec(memory_space=pl.ANY),
        scratch_shapes=[pltpu.SMEM((n,), jnp.int32),
                        pltpu.VMEM((MOE_BLOCK * PACK_TILES, LANES), jnp.uint32),
                        pltpu.SemaphoreType.DMA,
                        pltpu.SemaphoreType.DMA,
                        pltpu.SemaphoreType.DMA])
    return pl.pallas_call(
        functools.partial(_dispatch_kernel, n_blocks=n_blocks),
        grid_spec=grid_spec,
        out_shape=jax.ShapeDtypeStruct((n_blocks * MOE_BLOCK * PACK_TILES, LANES), jnp.uint32),
        compiler_params=_cparams(("arbitrary",)),
        name="moe_dispatch",
    )(pad_start, pad_end, n_used, dest_tm, hp)


def _moe_kernel(nblk_ref, blk0_ref, nxt_ref, slot_ref, first_ref, nused_ref,
                xs_hbm, wg_hbm, wu_hbm, wd_hbm, y_hbm,
                xbuf, ybuf, wg32, wu32, wd32, wgb, wub, wdb, sem_w, sem_x, sem_y, *, n_blocks):
    e = pl.program_id(0)
    nb = nblk_ref[e]
    span = MOE_BLOCK * PACK_TILES

    def block_rows(b):
        return pl.ds(pl.multiple_of(b * span, span), span)

    def fetch(ex, slot):
        return (pltpu.make_async_copy(wg_hbm.at[ex], wg32.at[slot], sem_w.at[slot, 0]),
                pltpu.make_async_copy(wu_hbm.at[ex], wu32.at[slot], sem_w.at[slot, 1]),
                pltpu.make_async_copy(wd_hbm.at[ex], wd32.at[slot], sem_w.at[slot, 2]))

    def x_copy(b, s):
        return pltpu.make_async_copy(xs_hbm.at[block_rows(b), :], xbuf.at[s], sem_x.at[s])

    def y_copy(b, s):
        return pltpu.make_async_copy(ybuf.at[s], y_hbm.at[block_rows(b), :], sem_y.at[s])

    n_used = nused_ref[0]
    n_xbuf = xbuf.shape[0]

    @pl.when(nb > 0)
    def _():
        b0 = blk0_ref[e]
        slot = slot_ref[e]

        @pl.when(e == first_ref[0])
        def _():
            for b in range(n_xbuf - 1):
                pl.when(b < n_used)(lambda b=b: x_copy(b, b).start())
            for cp in fetch(e, 0):
                cp.start(priority=1)

        for cp in fetch(e, slot):
            cp.wait()

        @pl.when(nxt_ref[e] >= 0)
        def _():
            for cp in fetch(nxt_ref[e], 1 - slot):
                cp.start(priority=1)

        wgb[...] = wg32[slot].astype(BF16)
        wub[...] = wu32[slot].astype(BF16)
        wdb[...] = wd32[slot].astype(BF16)

        def body(j, carry):
            b = b0 + j
            s = b % 2
            sx = b % n_xbuf
            ahead = b + n_xbuf - 1
            x_copy(b, sx).wait()
            pl.when(ahead < n_used)(lambda: x_copy(ahead, ahead % n_xbuf).start())
            pl.when(b >= 2)(lambda: y_copy(b - 2, s).wait())
            xb = xbuf.at[sx]
            words = jnp.concatenate(
                [xb[pl.ds(c, MOE_BLOCK, stride=PACK_TILES), :] for c in range(PACK_TILES)], axis=1)
            x = _unpack_rows(words)
            gate = jnp.dot(x, wgb[...], preferred_element_type=F32)
            up = jnp.dot(x, wub[...], preferred_element_type=F32)
            act = (gate * jax.nn.sigmoid(gate) * up).astype(BF16)
            _store_packed_tiles(ybuf.at[s], jnp.dot(act, wdb[...], preferred_element_type=F32))
            y_copy(b, s).start()
            return carry

        lax.fori_loop(0, nb, body, 0)

    @pl.when(e == pl.num_programs(0) - 1)
    def _():
        pl.when(n_used >= 2)(lambda: y_copy(n_used - 2, n_used % 2).wait())
        pl.when(n_used >= 1)(lambda: y_copy(n_used - 1, (n_used - 1) % 2).wait())
        ybuf[0] = jnp.zeros(ybuf.shape[1:], ybuf.dtype)

        def tail(action):
            def step(b, carry):
                action(y_copy(b, 0))
                return carry
            lax.fori_loop(nused_ref[0], n_blocks, step, 0)

        tail(lambda cp: cp.start())
        tail(lambda cp: cp.wait())


def _moe(nblk, blk0, nxt, slot, first, n_used, xs, weg, weu, wed, n_blocks):
    n_exp, d, f = weg.shape
    span = MOE_BLOCK * PACK_TILES
    grid_spec = pltpu.PrefetchScalarGridSpec(
        num_scalar_prefetch=6,
        grid=(n_exp,),
        in_specs=[pl.BlockSpec(memory_space=pl.ANY)] * 4,
        out_specs=pl.BlockSpec(memory_space=pl.ANY),
        scratch_shapes=[pltpu.VMEM((3, span, LANES), jnp.uint32), pltpu.VMEM((2, span, LANES), jnp.uint32),
                        pltpu.VMEM((2, d, f), F32), pltpu.VMEM((2, d, f), F32), pltpu.VMEM((2, f, d), F32),
                        pltpu.VMEM((d, f), BF16), pltpu.VMEM((d, f), BF16), pltpu.VMEM((f, d), BF16),
                        pltpu.SemaphoreType.DMA((2, 3)), pltpu.SemaphoreType.DMA((3,)),
                        pltpu.SemaphoreType.DMA((2,))])
    return pl.pallas_call(
        functools.partial(_moe_kernel, n_blocks=n_blocks),
        grid_spec=grid_spec,
        out_shape=jax.ShapeDtypeStruct((n_blocks * span, LANES), jnp.uint32),
        compiler_params=_cparams(("arbitrary",)),
        name="moe_experts",
    )(nblk, blk0, nxt, slot, first, n_used, xs, weg, weu, wed)


def _combine_kernel(dest_hbm, y_hbm, w_ref, h_ref, wg_ref, wu_ref, wd_ref, x_ref, gate_ref, o_ref,
                    dest0, dest1, ybuf0, ybuf1, sem_i, sem_y):
    i = pl.program_id(0)
    tm = h_ref.shape[0]
    n = tm * TOP_K
    span = n * PACK_TILES
    dests = (dest0, dest1)
    ybufs = (ybuf0, ybuf1)

    def start_tile(tile, slot):
        copy_dest = pltpu.make_async_copy(dest_hbm.at[tile], dests[slot], sem_i)
        copy_dest.start()
        copy_dest.wait()

        def issue(t, carry):
            base = t * TOKEN_PITCH
            for k in range(TOP_K):
                src = dests[slot][t * TOP_K + k]
                pltpu.make_async_copy(
                    y_hbm.at[_token_rows(src, PACK_TILES), :],
                    ybufs[slot].at[pl.ds(pl.multiple_of(base + k * YBUF_PITCH, 4), PACK_TILES), :],
                    sem_y.at[slot]).start(priority=k % 2)
            return carry

        lax.fori_loop(0, tm, issue, 0)

    def step(cur):
        nxt = 1 - cur
        pl.when(i == 0)(lambda: start_tile(0, cur))
        pl.when(i + 1 < pl.num_programs(0))(lambda: start_tile(i + 1, nxt))

        h = h_ref[...]
        gate = jnp.dot(h, wg_ref[...], preferred_element_type=F32)
        up = jnp.dot(h, wu_ref[...], preferred_element_type=F32)
        act = (gate * jax.nn.sigmoid(gate) * up).astype(BF16)
        shared = jnp.dot(act, wd_ref[...], preferred_element_type=F32)

        yb = ybufs[cur]
        pltpu.make_async_copy(y_hbm.at[pl.ds(0, span), :], yb.at[pl.ds(0, span), :], sem_y.at[cur]).wait()
        wks = [jnp.broadcast_to(w_ref[:, k:k + 1], (tm, LANES)) for k in range(TOP_K)]
        lows, highs = [], []
        for c in range(PACK_TILES):
            acc_lo = acc_hi = None
            for k in range(TOP_K):
                lo, hi = _unpack_halves(yb[pl.ds(k * YBUF_PITCH + c, tm, stride=TOKEN_PITCH), :])
                acc_lo = lo * wks[k] if acc_lo is None else acc_lo + lo * wks[k]
                acc_hi = hi * wks[k] if acc_hi is None else acc_hi + hi * wks[k]
            lows.append(acc_lo)
            highs.append(acc_hi)
        routed = jnp.concatenate(lows + highs, axis=1)
        o_ref[...] = x_ref[...] + gate_ref[...] * (routed + shared)

    pl.when(i % 2 == 0)(lambda: step(0))
    pl.when(i % 2 == 1)(lambda: step(1))


def _combine(dest_tm, y_sorted, wsel, h2, wsg, wsu, wsd, x1, gate):
    s, d = x1.shape
    n_tiles, n = dest_tm.shape
    tm = n // TOP_K
    f = wsg.shape[1]
    return pl.pallas_call(
        _combine_kernel,
        grid=(n_tiles,),
        in_specs=[pl.BlockSpec(memory_space=pl.ANY),
                  pl.BlockSpec(memory_space=pl.ANY),
                  pl.BlockSpec((tm, LANES), lambda i: (i, 0)),
                  pl.BlockSpec((tm, d), lambda i: (i, 0)),
                  pl.BlockSpec((d, f), lambda i: (0, 0)),
                  pl.BlockSpec((d, f), lambda i: (0, 0)),
                  pl.BlockSpec((f, d), lambda i: (0, 0)),
                  pl.BlockSpec((tm, d), lambda i: (i, 0)),
                  pl.BlockSpec((1, d), lambda i: (0, 0))],
        out_specs=pl.BlockSpec((tm, d), lambda i: (i, 0)),
        out_shape=jax.ShapeDtypeStruct((s, d), F32),
        scratch_shapes=[pltpu.SMEM((n,), jnp.int32),
                        pltpu.SMEM((n,), jnp.int32),
                        pltpu.VMEM((tm * TOKEN_PITCH, LANES), jnp.uint32),
                        pltpu.VMEM((tm * TOKEN_PITCH, LANES), jnp.uint32),
                        pltpu.SemaphoreType.DMA,
                        pltpu.SemaphoreType.DMA((2,))],
        compiler_params=_cparams(("arbitrary",)),
        name="moe_combine",
    )(dest_tm, y_sorted, wsel, h2, wsg, wsu, wsd, x1, gate)


def _expert_tables(counts):
    i32 = jnp.int32
    padded = ((counts + MOE_BLOCK - 1) // MOE_BLOCK * MOE_BLOCK).astype(i32)
    pad_end = jnp.cumsum(padded).astype(i32)
    pad_start = pad_end - padded
    ids = jnp.arange(N_EXPERTS, dtype=i32)
    busy = padded > 0
    later_busy = busy[None, :] & (ids[None, :] > ids[:, None])
    nxt = jnp.min(jnp.where(later_busy, ids[None, :], N_EXPERTS), axis=1)
    nxt = jnp.where(nxt < N_EXPERTS, nxt, -1).astype(i32)
    slot = (jnp.maximum(jnp.cumsum(busy.astype(i32)) - 1, 0) % 2).astype(i32)
    first = jnp.min(jnp.where(busy, ids, N_EXPERTS)).astype(i32).reshape(1)
    n_used = (pad_end[-1] // MOE_BLOCK).reshape(1)
    return pad_start, pad_end, padded // MOE_BLOCK, pad_start // MOE_BLOCK, nxt, slot, first, n_used


def kernel(x, c, ctx, c_ctx, w_ada, b_ada, norm_mix, norm_ffn, w_in, q_norm_a, k_norm_a, q_norm_b, k_norm_b, lambda_q1, lambda_k1, lambda_q2, lambda_k2, subln_b, w_branch_a, w_branch_b, w_out, w_router, router_bias, w_exp_gate, w_exp_up, w_exp_down, w_sh_gate, w_sh_up, w_sh_down):
    depth = w_ada.shape[0]
    assert depth == 1 and x.shape[0] == 1 and ctx.shape[0] == 1
    s, d = x.shape[1], x.shape[2]
    n_ctx = ctx.shape[1]
    i = 0
    lam_init = 0.8 - 0.6 * math.exp(-0.3 * i)
    xs = x[0]

    mod = _adaln(jnp.concatenate([c, c_ctx[None, :]], axis=0), w_ada[i], b_ada[i]).reshape(2, N_MOD, d)

    h = _prenorm(xs, ctx[0], norm_mix[i], mod)
    tc, tsa, tsb = _rope_tables(s, n_ctx)
    gains = _head_gains(q_norm_a[i], k_norm_a[i], q_norm_b[i], k_norm_b[i])
    proj = _inproj(h, w_in[i], gains, tc, tsa, tsb)
    oa = _gqa(proj, s)
    lam_vecs = jnp.stack([lambda_q1[i], lambda_k1[i], lambda_q2[i], lambda_k2[i]]).astype(F32)
    ob = _diff(proj, s, lam_vecs, subln_b[i], lam_init)
    x1 = _merge(oa, ob, proj, w_branch_a[i].astype(BF16), w_branch_b[i].astype(BF16),
                w_out[i].astype(BF16), xs, mod[0, 2:3, :])

    h2, h2p, idx, wsel, rank, cnt = _router(x1, norm_ffn[i], mod[0:1], w_router[i], router_bias[i])
    counts = cnt[0, :N_EXPERTS].astype(jnp.int32)
    n_blocks = -(-(s * TOP_K) // MOE_BLOCK) + N_EXPERTS
    pad_start, pad_end, nblk, blk0, nxt, slot, first, n_used = _expert_tables(counts)
    tm_dispatch = math.gcd(s, 1024)
    tm_combine = math.gcd(s, 256)
    e_ids = jnp.arange(N_EXPERTS, dtype=jnp.int32)
    starts = jnp.sum(jnp.where(idx[:, :TOP_K, None] == e_ids, pad_start, 0), axis=-1)
    dest = (starts + rank[:, :TOP_K]).astype(jnp.int32)
    tiles = lambda tm: dest.reshape(s // tm, tm * TOP_K)
    xs = _dispatch(pad_start, pad_end, n_used, tiles(tm_dispatch), h2p, n_blocks)
    y_sorted = _moe(nblk, blk0, nxt, slot, first, n_used, xs,
                    w_exp_gate[i], w_exp_up[i], w_exp_down[i], n_blocks)
    out = _combine(tiles(tm_combine), y_sorted, wsel, h2,
                   w_sh_gate[i].astype(BF16), w_sh_up[i].astype(BF16), w_sh_down[i].astype(BF16),
                   x1, mod[0, 5:6, :])
    return out[None]
```

```python
import functools
import math

import jax
import jax.numpy as jnp
from jax import lax
from jax.experimental import pallas as pl
from jax.experimental.pallas import tpu as pltpu

F32 = jnp.float32
BF16 = jnp.bfloat16

D_MODEL = 2048
GRID_W = 64
HEAD_DIM = 128
ROPE_PAIRS = HEAD_DIM // 4
ROPE_THETA = 10000.0
A_HEADS = 8
A_KV_HEADS = 2
A_GROUP = A_HEADS // A_KV_HEADS
B_HEADS = 4
B_V_DIM = 2 * HEAD_DIM
N_EXPERTS = 64
TOP_K = 8
N_GROUPS = 8
TOPK_GROUPS = 4
EXPERT_DIM = 512
SHARED_DIM = 512
ROUTED_SCALE = 2.5
N_MOD = 6
EPS = 1e-6

A_Q_W = A_HEADS * HEAD_DIM
A_KV_W = A_KV_HEADS * HEAD_DIM
B_QK_W = B_HEADS * 2 * HEAD_DIM
B_V_W = B_HEADS * B_V_DIM
IN_W = A_Q_W + 2 * A_KV_W + 2 * B_QK_W + B_V_W + 2 * D_MODEL

COL_AQ = 0
COL_AK = A_Q_W // HEAD_DIM
COL_AV = COL_AK + A_KV_HEADS
COL_BQ = COL_AV + A_KV_HEADS
COL_BK = COL_BQ + 2 * B_HEADS
COL_BV = COL_BK + 2 * B_HEADS
COL_GA = COL_BV + B_V_W // HEAD_DIM
COL_GB = COL_GA + D_MODEL // HEAD_DIM

LANES = 128
SUBLANES = 8
VMEM_LIMIT = 56 * 1024 * 1024

PROJ_TN = 512
MOE_BLOCK = 256
ONES_ROWS = 16
PACK_TILES = D_MODEL // 2 // LANES
YBUF_PITCH = PACK_TILES + 4
TOKEN_PITCH = TOP_K * YBUF_PITCH + 4
LOG2E = 1.4426950408889634


def _cparams(sem, vmem=VMEM_LIMIT):
    return pltpu.CompilerParams(dimension_semantics=sem, vmem_limit_bytes=vmem)


def _adaln_kernel(cb_ref, w_ref, b_ref, o_ref):
    tn = w_ref.shape[1]
    nl = tn // LANES
    rows = 32

    def body(g, accs):
        accs = list(accs)
        r0 = pl.multiple_of(g * rows, rows)
        for u in range(rows // SUBLANES):
            r = r0 + u * SUBLANES
            w = w_ref[pl.ds(r, SUBLANES), :]
            for v in range(2):
                c = cb_ref[v, pl.ds(r, SUBLANES), :]
                s = c * jax.nn.sigmoid(c)
                for j in range(nl):
                    accs[v * nl + j] = accs[v * nl + j] + w[:, j * LANES:(j + 1) * LANES] * s
        return tuple(accs)

    init = tuple(jnp.zeros((SUBLANES, LANES), F32) for _ in range(2 * nl))
    accs = lax.fori_loop(0, w_ref.shape[0] // rows, body, init)
    for v in range(2):
        row = jnp.concatenate(
            [jnp.sum(accs[v * nl + j], axis=0, keepdims=True) for j in range(nl)], axis=1)
        o_ref[v:v + 1, :] = row + b_ref[...]


def _adaln(cvecs, w, b):
    d, n = w.shape
    tn = 1536
    cb = jnp.broadcast_to(cvecs[:, :, None], (2, d, LANES))
    return pl.pallas_call(
        _adaln_kernel,
        grid=(n // tn,),
        in_specs=[pl.BlockSpec((2, d, LANES), lambda j: (0, 0, 0)),
                  pl.BlockSpec((d, tn), lambda j: (0, j)),
                  pl.BlockSpec((1, tn), lambda j: (0, j))],
        out_specs=pl.BlockSpec((2, tn), lambda j: (0, j)),
        out_shape=jax.ShapeDtypeStruct((2, n), F32),
        compiler_params=_cparams(("arbitrary",)),
        name="adaln",
    )(cb, w, b.reshape(1, n))


def _rms_mod(x, g, shift, scale):
    y = x * lax.rsqrt(jnp.mean(x * x, axis=-1, keepdims=True) + EPS) * g
    return y * (1.0 + scale) + shift


def _prenorm_kernel(x_ref, c_ref, g_ref, mod_ref, o_ref, *, n_lat_tiles):
    is_ctx = pl.program_id(0) >= n_lat_tiles
    x = jnp.where(is_ctx, c_ref[...], x_ref[...])
    o_ref[...] = _rms_mod(x, g_ref[...], mod_ref[0, 0:1, :], mod_ref[0, 1:2, :]).astype(o_ref.dtype)


def _prenorm(x, ctx, g, mod):
    s, d = x.shape
    c = ctx.shape[0]
    tm = 256
    nl, nc = s // tm, c // tm
    return pl.pallas_call(
        functools.partial(_prenorm_kernel, n_lat_tiles=nl),
        grid=(nl + nc,),
        in_specs=[pl.BlockSpec((tm, d), lambda i: (jnp.minimum(i, nl - 1), 0)),
                  pl.BlockSpec((tm, d), lambda i: (jnp.maximum(i - nl, 0), 0)),
                  pl.BlockSpec((1, d), lambda i: (0, 0)),
                  pl.BlockSpec((1, N_MOD, d), lambda i: (i // nl, 0, 0))],
        out_specs=pl.BlockSpec((tm, d), lambda i: (i, 0)),
        out_shape=jax.ShapeDtypeStruct((s + c, d), BF16),
        compiler_params=_cparams(("arbitrary",)),
        name="prenorm_mix",
    )(x, ctx, g.reshape(1, d), mod)


def _inproj_kernel(h_ref, w_ref, gain_ref, c_ref, sa_ref, sb_ref, o_ref):
    j = pl.program_id(1)
    tm = h_ref.shape[0]
    nh = o_ref.shape[1] // HEAD_DIM
    n_chunks = next(n for n in (3, 2, 1) if tm % (16 * n) == 0)

    def tile(n_normed, chunks):
        w = w_ref[...].astype(h_ref.dtype)
        rows = tm // chunks
        for r in range(chunks):
            rs = slice(r * rows, (r + 1) * rows)
            acc = jnp.dot(h_ref[rs, :], w, preferred_element_type=F32)
            for hd in range(nh):
                sl = slice(hd * HEAD_DIM, (hd + 1) * HEAD_DIM)
                a = acc[:, sl]
                if hd < n_normed:
                    y = a * lax.rsqrt(jnp.mean(a * a, axis=-1, keepdims=True) + EPS) * gain_ref[0, :, sl]
                    a = (y * c_ref[rs, :] + pltpu.roll(y, ROPE_PAIRS, 1) * sa_ref[rs, :]
                         + pltpu.roll(y, HEAD_DIM - ROPE_PAIRS, 1) * sb_ref[rs, :])
                o_ref[rs, sl] = a.astype(o_ref.dtype)

    all_normed = (j < 2) | ((j >= 3) & (j < 7))
    pl.when(all_normed)(lambda: tile(nh, n_chunks))
    pl.when(j == 2)(lambda: tile(A_KV_HEADS, n_chunks))
    pl.when(j >= 7)(lambda: tile(0, 1))


def _inproj(h, w, gains, rope_c, rope_sa, rope_sb):
    t, d = h.shape
    n = w.shape[1]
    tm = t // 8
    tn = PROJ_TN
    return pl.pallas_call(
        _inproj_kernel,
        grid=(t // tm, n // tn),
        in_specs=[pl.BlockSpec((tm, d), lambda i, j: (i, 0)),
                  pl.BlockSpec((d, tn), lambda i, j: (0, j)),
                  pl.BlockSpec((1, 1, tn), lambda i, j: (j, 0, 0)),
                  pl.BlockSpec((tm, HEAD_DIM), lambda i, j: (i, 0)),
                  pl.BlockSpec((tm, HEAD_DIM), lambda i, j: (i, 0)),
                  pl.BlockSpec((tm, HEAD_DIM), lambda i, j: (i, 0))],
        out_specs=pl.BlockSpec((tm, tn), lambda i, j: (i, j)),
        out_shape=jax.ShapeDtypeStruct((t, n), BF16),
        compiler_params=_cparams(("arbitrary", "arbitrary")),
        name="inproj",
    )(h, w, gains, rope_c, rope_sa, rope_sb)


def _rope_tables(s, c):
    rows_n = s // GRID_W
    inv = ROPE_THETA ** (-jnp.arange(ROPE_PAIRS, dtype=F32) / ROPE_PAIRS)
    ang_r = jnp.arange(rows_n, dtype=F32)[:, None] * inv
    ang_c = jnp.arange(GRID_W, dtype=F32)[:, None] * inv
    cr, sr, cc, sc = jnp.cos(ang_r), jnp.sin(ang_r), jnp.cos(ang_c), jnp.sin(ang_c)
    zr, zc = jnp.zeros_like(sr), jnp.zeros_like(sc)

    def table(row_parts, col_parts, ctx_value):
        by_row = jnp.concatenate(row_parts + [zr, zr], axis=1)
        by_col = jnp.concatenate([zc, zc] + col_parts, axis=1)
        lat = (by_row[:, None, :] + by_col[None, :, :]).reshape(s, HEAD_DIM)
        return jnp.concatenate([lat, jnp.full((c, HEAD_DIM), ctx_value, F32)], axis=0)

    return (table([cr, cr], [cc, cc], 1.0), table([zr, sr], [zc, sc], 0.0), table([-sr, zr], [-sc, zc], 0.0))


def _head_gains(qn_a, kn_a, qn_b, kn_b):
    qs = HEAD_DIM ** -0.5 * LOG2E
    one = jnp.ones((HEAD_DIM,), F32)
    heads = ([qn_a * qs] * A_HEADS + [kn_a] * A_KV_HEADS + [one] * A_KV_HEADS
             + [qn_b * qs] * (2 * B_HEADS) + [kn_b] * (2 * B_HEADS))
    heads = heads + [one] * (IN_W // HEAD_DIM - len(heads))
    return jnp.concatenate(heads).reshape(IN_W // PROJ_TN, 1, PROJ_TN)


def _build_vt(v_ref, vt_ref, tk):
    n_chunks, rows, _ = vt_ref.shape
    dv = rows - ONES_ROWS
    tail = (lax.broadcasted_iota(jnp.int32, (ONES_ROWS, tk), 0) == 0).astype(vt_ref.dtype)
    for c in range(n_chunks):
        vt_ref[c, 0:dv, :] = v_ref[c * tk:(c + 1) * tk, :].astype(F32).T.astype(vt_ref.dtype)
        vt_ref[c, dv:rows, :] = tail


def _attend(qs, k_refs, vt_ref, s_ref, m_ref, acc_ref):
    n_chunks, _, tk = vt_ref.shape
    n_st = len(qs)
    m_ref[...] = jnp.full(m_ref.shape, -jnp.inf, F32)
    acc_ref[...] = jnp.zeros(acc_ref.shape, F32)

    def scores(i, c, slot):
        off = c * tk if isinstance(c, int) else pl.multiple_of(c * tk, tk)
        s_ref[i, slot] = lax.dot_general(k_refs[i][pl.ds(off, tk), :], qs[i], (((1,), (1,)), ((), ())),
                                         preferred_element_type=F32)

    def update(i, c, slot):
        s = s_ref[i, slot]
        m_old = m_ref[i]
        m_new = jnp.maximum(m_old, jnp.max(s, axis=0, keepdims=True))
        p = jnp.exp2(s - m_new).astype(vt_ref.dtype)
        acc_ref[i] = (acc_ref[i] * jnp.exp2(m_old - m_new)
                      + jnp.dot(vt_ref[c], p, preferred_element_type=F32))
        m_ref[i] = m_new

    for i in range(n_st):
        scores(i, 0, 0)

    def pair(j, carry):
        c = 2 * j
        for i in range(n_st):
            scores(i, c + 1, 1)
        for i in range(n_st):
            update(i, c, 0)
        for i in range(n_st):
            scores(i, c + 2, 0)
        for i in range(n_st):
            update(i, c + 1, 1)
        return carry

    n_pairs = (n_chunks - 1) // 2
    lax.fori_loop(0, n_pairs, pair, 0)
    done = 2 * n_pairs
    if n_chunks - done == 2:
        for i in range(n_st):
            scores(i, done + 1, 1)
    for i in range(n_st):
        update(i, done, 0)
    if n_chunks - done == 2:
        for i in range(n_st):
            update(i, done + 1, 1)


def _gqa_kernel(q_ref, k_ref, v_ref, o_ref, vt_ref, s_ref, m_ref, acc_ref):
    tk = vt_ref.shape[2]
    n_st = s_ref.shape[0]
    pl.when((pl.program_id(1) == 0) & (pl.program_id(2) == 0))(lambda: _build_vt(v_ref, vt_ref, tk))
    qs = [q_ref[:, i * HEAD_DIM:(i + 1) * HEAD_DIM] for i in range(n_st)]
    _attend(qs, [k_ref] * n_st, vt_ref, s_ref, m_ref, acc_ref)
    for i in range(n_st):
        o_t = acc_ref[i, 0:HEAD_DIM, :] / acc_ref[i, HEAD_DIM:HEAD_DIM + 1, :]
        o_ref[:, i * HEAD_DIM:(i + 1) * HEAD_DIM] = o_t.T.astype(o_ref.dtype)


def _key_chunk(t):
    for tk in (1408, 768, 1024, 512, 640, 384, 256, 128):
        if t % tk == 0:
            return tk
    raise ValueError(f"unsupported key count {t}")


def _gqa(proj, s):
    t = proj.shape[0]
    tq = 512
    tk = _key_chunk(t)
    n_st = 4
    per_g = A_GROUP // n_st
    return pl.pallas_call(
        _gqa_kernel,
        grid=(A_KV_HEADS, per_g, s // tq),
        in_specs=[pl.BlockSpec((tq, n_st * HEAD_DIM), lambda g, hh, i: (i, COL_AQ // n_st + g * per_g + hh)),
                  pl.BlockSpec((t, HEAD_DIM), lambda g, hh, i: (0, COL_AK + g)),
                  pl.BlockSpec((t, HEAD_DIM), lambda g, hh, i: (0, COL_AV + g))],
        out_specs=pl.BlockSpec((tq, n_st * HEAD_DIM), lambda g, hh, i: (i, g * per_g + hh)),
        out_shape=jax.ShapeDtypeStruct((s, A_Q_W), BF16),
        scratch_shapes=[pltpu.VMEM((t // tk, HEAD_DIM + ONES_ROWS, tk), BF16),
                        pltpu.VMEM((n_st, 2, tk, tq), F32),
                        pltpu.VMEM((n_st, 1, tq), F32),
                        pltpu.VMEM((n_st, HEAD_DIM + ONES_ROWS, tq), F32)],
        compiler_params=_cparams(("arbitrary", "arbitrary", "arbitrary")),
        name="gqa_attn",
    )(proj, proj, proj)


def _diff_kernel(lam_ref, q0_ref, q1_ref, k0_ref, k1_ref, v_ref, g_ref, o_ref,
                 vt_ref, s_ref, m_ref, acc_ref, *, lam_init):
    tk = vt_ref.shape[2]
    pl.when(pl.program_id(1) == 0)(lambda: _build_vt(v_ref, vt_ref, tk))
    lv = lam_ref[...]
    lam = (jnp.exp(jnp.sum(lv[0:1, :] * lv[1:2, :], axis=-1, keepdims=True))
           - jnp.exp(jnp.sum(lv[2:3, :] * lv[3:4, :], axis=-1, keepdims=True)) + lam_init)
    _attend([q0_ref[...], q1_ref[...]], [k0_ref, k1_ref], vt_ref, s_ref, m_ref, acc_ref)
    o_t = (acc_ref[0, 0:B_V_DIM, :] / acc_ref[0, B_V_DIM:B_V_DIM + 1, :]
           - lam * (acc_ref[1, 0:B_V_DIM, :] / acc_ref[1, B_V_DIM:B_V_DIM + 1, :]))
    o = o_t.T
    y = o * lax.rsqrt(jnp.mean(o * o, axis=-1, keepdims=True) + EPS) * g_ref[...]
    o_ref[...] = (y * (1.0 - lam_init)).astype(o_ref.dtype)


def _diff(proj, s, lam_vecs, subln_g, lam_init):
    t = proj.shape[0]
    tq = 512
    tk = _key_chunk(t)
    vb = B_V_DIM // HEAD_DIM
    return pl.pallas_call(
        functools.partial(_diff_kernel, lam_init=lam_init),
        grid=(B_HEADS, s // tq),
        in_specs=[pl.BlockSpec((4, HEAD_DIM), lambda h, i: (0, 0)),
                  pl.BlockSpec((tq, HEAD_DIM), lambda h, i: (i, COL_BQ + 2 * h)),
                  pl.BlockSpec((tq, HEAD_DIM), lambda h, i: (i, COL_BQ + 2 * h + 1)),
                  pl.BlockSpec((t, HEAD_DIM), lambda h, i: (0, COL_BK + 2 * h)),
                  pl.BlockSpec((t, HEAD_DIM), lambda h, i: (0, COL_BK + 2 * h + 1)),
                  pl.BlockSpec((t, B_V_DIM), lambda h, i: (0, COL_BV // vb + h)),
                  pl.BlockSpec((1, B_V_DIM), lambda h, i: (0, 0))],
        out_specs=pl.BlockSpec((tq, B_V_DIM), lambda h, i: (i, h)),
        out_shape=jax.ShapeDtypeStruct((s, B_V_W), BF16),
        scratch_shapes=[pltpu.VMEM((t // tk, B_V_DIM + ONES_ROWS, tk), BF16),
                        pltpu.VMEM((2, 2, tk, tq), F32),
                        pltpu.VMEM((2, 1, tq), F32),
                        pltpu.VMEM((2, B_V_DIM + ONES_ROWS, tq), F32)],
        compiler_params=_cparams(("arbitrary", "arbitrary")),
        name="diff_attn",
    )(lam_vecs, proj, proj, proj, proj, proj, subln_g.reshape(1, B_V_DIM))


def _merge_kernel(oa_ref, ob_ref, ga_ref, gb_ref, wa_ref, wb_ref, wo_ref, x_ref, gate_ref, o_ref, t_ref):
    c = pl.program_id(1)
    nc, _, tn = t_ref.shape
    ya = jnp.dot(oa_ref[...], wa_ref[...].astype(oa_ref.dtype), preferred_element_type=F32)
    yb = jnp.dot(ob_ref[...], wb_ref[...].astype(ob_ref.dtype), preferred_element_type=F32)
    t = (jax.nn.sigmoid(ga_ref[...].astype(F32)) * ya + jax.nn.sigmoid(gb_ref[...].astype(F32)) * yb)
    t_ref[c] = t.astype(t_ref.dtype)

    @pl.when(c == nc - 1)
    def _():
        y = jnp.dot(t_ref[0], wo_ref[0:tn, :], preferred_element_type=F32)
        for cc in range(1, nc):
            y = y + jnp.dot(t_ref[cc], wo_ref[cc * tn:(cc + 1) * tn, :], preferred_element_type=F32)
        o_ref[...] = x_ref[...] + gate_ref[...] * y


def _merge(oa, ob, proj, wa, wb, wo, x, gate):
    s, d = x.shape
    tm = 512
    tn = PROJ_TN
    nc = d // tn
    ga0 = COL_GA * HEAD_DIM // tn
    gb0 = COL_GB * HEAD_DIM // tn
    return pl.pallas_call(
        _merge_kernel,
        grid=(s // tm, nc),
        in_specs=[pl.BlockSpec((tm, A_Q_W), lambda i, c: (i, 0)),
                  pl.BlockSpec((tm, B_V_W), lambda i, c: (i, 0)),
                  pl.BlockSpec((tm, tn), lambda i, c: (i, ga0 + c)),
                  pl.BlockSpec((tm, tn), lambda i, c: (i, gb0 + c)),
                  pl.BlockSpec((A_Q_W, tn), lambda i, c: (0, c)),
                  pl.BlockSpec((B_V_W, tn), lambda i, c: (0, c)),
                  pl.BlockSpec((d, d), lambda i, c: (0, 0)),
                  pl.BlockSpec((tm, d), lambda i, c: (i, 0)),
                  pl.BlockSpec((1, d), lambda i, c: (0, 0))],
        out_specs=pl.BlockSpec((tm, d), lambda i, c: (i, 0)),
        out_shape=jax.ShapeDtypeStruct((s, d), F32),
        scratch_shapes=[pltpu.VMEM((nc, tm, tn), BF16)],
        compiler_params=_cparams(("arbitrary", "arbitrary")),
        name="merge_out",
    )(oa, ob, proj, proj, wa, wb, wo, x, gate)


def _store_packed_tiles(ref, x):
    rows = x.shape[0]
    bits = lax.bitcast_convert_type(x.astype(BF16).astype(F32), jnp.uint32)
    half = bits.shape[1] // 2
    words = (bits[:, :half] >> 16) | (bits[:, half:] & jnp.uint32(0xFFFF0000))
    for c in range(PACK_TILES):
        ref[pl.ds(c, rows, stride=PACK_TILES), :] = words[:, c * LANES:(c + 1) * LANES]


def _lane_max(x):
    return jnp.max(x, axis=-1, keepdims=True)


def _lane_min(x):
    return jnp.min(x, axis=-1, keepdims=True)


def _group_allreduce(x, lane, op):
    for sft in (1, 2, 4):
        up = pltpu.roll(x, sft, 1)
        dn = pltpu.roll(x, LANES - sft, 1)
        x = op(x, jnp.where((lane & sft) != 0, up, dn))
    return x


def _router_kernel(x_ref, g_ref, mod_ref, wr_ref, rb_ref,
                   h_ref, hp_ref, idx_ref, wsel_ref, rank_ref, cnt_ref, carry_ref):
    @pl.when(pl.program_id(0) == 0)
    def _():
        carry_ref[...] = jnp.zeros_like(carry_ref)

    h = _rms_mod(x_ref[...], g_ref[...], mod_ref[0, 3:4, :], mod_ref[0, 4:5, :])
    hb = h.astype(BF16)
    h_ref[...] = hb
    _store_packed_tiles(hp_ref, h)
    tm = h.shape[0]

    h_lo = (h - hb.astype(F32)).astype(BF16)
    logits = (jnp.dot(hb, wr_ref[0], preferred_element_type=F32)
              + jnp.dot(hb, wr_ref[1], preferred_element_type=F32)
              + jnp.dot(h_lo, wr_ref[0], preferred_element_type=F32))
    scores = jax.nn.sigmoid(logits)
    lane = lax.broadcasted_iota(jnp.int32, (tm, LANES), 1)
    lane_f = lane.astype(F32)
    gid_f = (lane >> 3).astype(F32)
    valid = lane < N_EXPERTS
    neg = jnp.float32(-jnp.inf)
    big = jnp.float32(LANES)
    biased = jnp.where(valid, scores + rb_ref[...], neg)

    m1 = _group_allreduce(biased, lane, jnp.maximum)
    a1 = _group_allreduce(jnp.where(biased == m1, lane_f, big), lane, jnp.minimum)
    m2 = _group_allreduce(jnp.where(lane_f == a1, neg, biased), lane, jnp.maximum)
    gscore = jnp.where(valid, m1 + m2, neg)
    keep = jnp.zeros((tm, LANES), jnp.bool_)
    for _ in range(TOPK_GROUPS):
        best = _lane_max(gscore)
        gsel = _lane_min(jnp.where(gscore == best, gid_f, big))
        hit = gid_f == gsel
        keep = keep | hit
        gscore = jnp.where(hit, neg, gscore)
    cand = jnp.where(keep & valid, biased, neg)

    onehot = jnp.zeros((tm, LANES), F32)
    idx_out = jnp.zeros((tm, LANES), F32)
    w_out = jnp.zeros((tm, LANES), F32)
    sels = []
    for k in range(TOP_K):
        best = _lane_max(cand)
        sel = _lane_min(jnp.where(cand == best, lane_f, big))
        hit = lane_f == sel
        wk = jnp.sum(jnp.where(hit, scores, 0.0), axis=-1, keepdims=True)
        cand = jnp.where(hit, neg, cand)
        onehot = jnp.where(hit, 1.0, onehot)
        idx_out = jnp.where(lane == k, sel, idx_out)
        w_out = jnp.where(lane == k, wk, w_out)
        sels.append(hit)
    wsum = jnp.sum(w_out, axis=-1, keepdims=True)
    wsel_ref[...] = w_out / wsum * ROUTED_SCALE
    idx_ref[...] = idx_out.astype(jnp.int32)

    row = lax.broadcasted_iota(jnp.int32, (tm, tm), 0)
    col = lax.broadcasted_iota(jnp.int32, (tm, tm), 1)
    lower = (col < row).astype(BF16)
    before = jnp.dot(lower, onehot.astype(BF16), preferred_element_type=F32) + carry_ref[...]
    rank_out = jnp.zeros((tm, LANES), F32)
    for k in range(TOP_K):
        rk = jnp.sum(jnp.where(sels[k], before, 0.0), axis=-1, keepdims=True)
        rank_out = jnp.where(lane == k, rk, rank_out)
    rank_ref[...] = rank_out.astype(jnp.int32)
    carry_ref[...] = carry_ref[...] + jnp.sum(onehot, axis=0, keepdims=True)
    cnt_ref[...] = carry_ref[...]


def _router(x1, g, mod, w_router, router_bias):
    s, d = x1.shape
    tm = 256
    e = w_router.shape[1]
    wr = jnp.pad(w_router, ((0, 0), (0, LANES - e)))
    wr_hi = wr.astype(BF16)
    wr = jnp.stack([wr_hi, (wr - wr_hi.astype(F32)).astype(BF16)])
    rb = jnp.pad(router_bias, (0, LANES - e)).reshape(1, LANES)
    row_spec = lambda w: pl.BlockSpec((tm, w), lambda i: (i, 0))
    return pl.pallas_call(
        _router_kernel,
        grid=(s // tm,),
        in_specs=[row_spec(d),
                  pl.BlockSpec((1, d), lambda i: (0, 0)),
                  pl.BlockSpec((1, N_MOD, d), lambda i: (0, 0, 0)),
                  pl.BlockSpec((2, d, LANES), lambda i: (0, 0, 0)),
                  pl.BlockSpec((1, LANES), lambda i: (0, 0))],
        out_specs=[row_spec(d), pl.BlockSpec((tm * PACK_TILES, LANES), lambda i: (i, 0)),
                   row_spec(LANES), row_spec(LANES), row_spec(LANES),
                   pl.BlockSpec((1, LANES), lambda i: (0, 0))],
        out_shape=[jax.ShapeDtypeStruct((s, d), BF16),
                   jax.ShapeDtypeStruct((s * PACK_TILES, LANES), jnp.uint32),
                   jax.ShapeDtypeStruct((s, LANES), jnp.int32),
                   jax.ShapeDtypeStruct((s, LANES), F32),
                   jax.ShapeDtypeStruct((s, LANES), jnp.int32),
                   jax.ShapeDtypeStruct((1, LANES), F32)],
        scratch_shapes=[pltpu.VMEM((1, LANES), F32)],
        compiler_params=_cparams(("arbitrary",)),
        name="ffn_router",
    )(x1, g.reshape(1, d), mod, wr, rb)


def _unpack_halves(words):
    lo = lax.bitcast_convert_type(words << 16, F32)
    hi = lax.bitcast_convert_type(words & jnp.uint32(0xFFFF0000), F32)
    return lo, hi


def _unpack_rows(words):
    lo, hi = _unpack_halves(words)
    return jnp.concatenate([lo.astype(BF16), hi.astype(BF16)], axis=1)


def _token_rows(row, tiles):
    return pl.ds(pl.multiple_of(row * tiles, tiles), tiles)


def _dispatch_kernel(ps_ref, pe_ref, nused_ref, dest_hbm, hp_ref, xs_hbm,
                     dest_smem, zbuf, sem_i, sem_z, sem_x, *, n_blocks):
    i = pl.program_id(0)
    n = dest_smem.shape[0]
    tm = n // TOP_K
    copy_dest = pltpu.make_async_copy(dest_hbm.at[i], dest_smem, sem_i)
    copy_dest.start()

    @pl.when(i == 0)
    def _():
        zbuf[...] = jnp.zeros_like(zbuf)

        def zero_block(row0):
            span = MOE_BLOCK * PACK_TILES
            return pltpu.make_async_copy(
                zbuf, xs_hbm.at[pl.ds(pl.multiple_of(row0 * PACK_TILES, span), span), :], sem_z)

        def per_expert(action):
            def body(e, carry):
                @pl.when(pe_ref[e] > ps_ref[e])
                def _():
                    action(zero_block(pe_ref[e] - MOE_BLOCK))
                return carry
            lax.fori_loop(0, N_EXPERTS, body, 0)

        def per_tail(action):
            def body(b, carry):
                action(zero_block(b * MOE_BLOCK))
                return carry
            lax.fori_loop(nused_ref[0], n_blocks, body, 0)

        per_expert(lambda cp: cp.start())
        per_tail(lambda cp: cp.start())
        per_expert(lambda cp: cp.wait())
        per_tail(lambda cp: cp.wait())

    copy_dest.wait()

    def issue(t, carry):
        for k in range(TOP_K):
            dest = dest_smem[t * TOP_K + k]
            pltpu.make_async_copy(hp_ref.at[_token_rows(t, PACK_TILES), :],
                                  xs_hbm.at[_token_rows(dest, PACK_TILES), :], sem_x).start(priority=k % 2)
        return carry

    lax.fori_loop(0, tm, issue, 0)
    for _ in range(TOP_K):
        pltpu.make_async_copy(hp_ref, xs_hbm.at[pl.ds(0, tm * PACK_TILES), :], sem_x).wait()


def _dispatch(pad_start, pad_end, n_used, dest_tm, hp, n_blocks):
    n_tiles, n = dest_tm.shape
    tm = n // TOP_K
    grid_spec = pltpu.PrefetchScalarGridSpec(
        num_scalar_prefetch=3,
        grid=(n_tiles,),
        in_specs=[pl.BlockSpec(memory_space=pl.ANY),
                  pl.BlockSpec((tm * PACK_TILES, LANES), lambda i, ps, pe, nu: (i, 0))],
        out_specs=pl.BlockSpec(memory_space=pl.ANY),
        scratch_shapes=[pltpu.SMEM((n,), jnp.int32),
                        pltpu.VMEM((MOE_BLOCK * PACK_TILES, LANES), jnp.uint32),
                        pltpu.SemaphoreType.DMA,
                        pltpu.SemaphoreType.DMA,
                        pltpu.SemaphoreType.DMA])
    return pl.pallas_call(
        functools.partial(_dispatch_kernel, n_blocks=n_blocks),
        grid_spec=grid_spec,
        out_shape=jax.ShapeDtypeStruct((n_blocks * MOE_BLOCK * PACK_TILES, LANES), jnp.uint32),
        compiler_params=_cparams(("arbitrary",)),
        name="moe_dispatch",
    )(pad_start, pad_end, n_used, dest_tm, hp)


def _moe_kernel(nblk_ref, blk0_ref, nxt_ref, slot_ref, first_ref, nused_ref,
                xs_hbm, wg_hbm, wu_hbm, wd_hbm, y_hbm,
                xbuf, ybuf, wg32, wu32, wd32, wgb, wub, wdb, sem_w, sem_x, sem_y, *, n_blocks):
    e = pl.program_id(0)
    nb = nblk_ref[e]
    span = MOE_BLOCK * PACK_TILES

    def block_rows(b):
        return pl.ds(pl.multiple_of(b * span, span), span)

    def fetch(ex, slot):
        return (pltpu.make_async_copy(wg_hbm.at[ex], wg32.at[slot], sem_w.at[slot, 0]),
                pltpu.make_async_copy(wu_hbm.at[ex], wu32.at[slot], sem_w.at[slot, 1]),
                pltpu.make_async_copy(wd_hbm.at[ex], wd32.at[slot], sem_w.at[slot, 2]))

    def x_copy(b, s):
        return pltpu.make_async_copy(xs_hbm.at[block_rows(b), :], xbuf.at[s], sem_x.at[s])

    def y_copy(b, s):
        return pltpu.make_async_copy(ybuf.at[s], y_hbm.at[block_rows(b), :], sem_y.at[s])

    n_used = nused_ref[0]
    n_xbuf = xbuf.shape[0]

    @pl.when(nb > 0)
    def _():
        b0 = blk0_ref[e]
        slot = slot_ref[e]

        @pl.when(e == first_ref[0])
        def _():
            for b in range(n_xbuf - 1):
                pl.when(b < n_used)(lambda b=b: x_copy(b, b).start())
            for cp in fetch(e, 0):
                cp.start(priority=1)

        for cp in fetch(e, slot):
            cp.wait()

        @pl.when(nxt_ref[e] >= 0)
        def _():
            for cp in fetch(nxt_ref[e], 1 - slot):
                cp.start(priority=1)

        wgb[...] = wg32[slot].astype(BF16)
        wub[...] = wu32[slot].astype(BF16)
        wdb[...] = wd32[slot].astype(BF16)

        def body(j, carry):
            b = b0 + j
            s = b % 2
            sx = b % n_xbuf
            ahead = b + n_xbuf - 1
            x_copy(b, sx).wait()
            pl.when(ahead < n_used)(lambda: x_copy(ahead, ahead % n_xbuf).start())
            pl.when(b >= 2)(lambda: y_copy(b - 2, s).wait())
            xb = xbuf.at[sx]
            words = jnp.concatenate(
                [xb[pl.ds(c, MOE_BLOCK, stride=PACK_TILES), :] for c in range(PACK_TILES)], axis=1)
            x = _unpack_rows(words)
            gate = jnp.dot(x, wgb[...], preferred_element_type=F32)
            up = jnp.dot(x, wub[...], preferred_element_type=F32)
            act = (gate * jax.nn.sigmoid(gate) * up).astype(BF16)
            _store_packed_tiles(ybuf.at[s], jnp.dot(act, wdb[...], preferred_element_type=F32))
            y_copy(b, s).start()
            return carry

        lax.fori_loop(0, nb, body, 0)

    @pl.when(e == pl.num_programs(0) - 1)
    def _():
        pl.when(n_used >= 2)(lambda: y_copy(n_used - 2, n_used % 2).wait())
        pl.when(n_used >= 1)(lambda: y_copy(n_used - 1, (n_used - 1) % 2).wait())
        ybuf[0] = jnp.zeros(ybuf.shape[1:], ybuf.dtype)

        def tail(action):
            def step(b, carry):
                action(y_copy(b, 0))
                return carry
            lax.fori_loop(nused_ref[0], n_blocks, step, 0)

        tail(lambda cp: cp.start())
        tail(lambda cp: cp.wait())


def _moe(nblk, blk0, nxt, slot, first, n_used, xs, weg, weu, wed, n_blocks):
    n_exp, d, f = weg.shape
    span = MOE_BLOCK * PACK_TILES
    grid_spec = pltpu.PrefetchScalarGridSpec(
        num_scalar_prefetch=6,
        grid=(n_exp,),
        in_specs=[pl.BlockSpec(memory_space=pl.ANY)] * 4,
        out_specs=pl.BlockSpec(memory_space=pl.ANY),
        scratch_shapes=[pltpu.VMEM((3, span, LANES), jnp.uint32), pltpu.VMEM((2, span, LANES), jnp.uint32),
                        pltpu.VMEM((2, d, f), F32), pltpu.VMEM((2, d, f), F32), pltpu.VMEM((2, f, d), F32),
                        pltpu.VMEM((d, f), BF16), pltpu.VMEM((d, f), BF16), pltpu.VMEM((f, d), BF16),
                        pltpu.SemaphoreType.DMA((2, 3)), pltpu.SemaphoreType.DMA((3,)),
                        pltpu.SemaphoreType.DMA((2,))])
    return pl.pallas_call(
        functools.partial(_moe_kernel, n_blocks=n_blocks),
        grid_spec=grid_spec,
        out_shape=jax.ShapeDtypeStruct((n_blocks * span, LANES), jnp.uint32),
        compiler_params=_cparams(("arbitrary",)),
        name="moe_experts",
    )(nblk, blk0, nxt, slot, first, n_used, xs, weg, weu, wed)


def _combine_kernel(dest_hbm, y_hbm, w_ref, h_ref, wg_ref, wu_ref, wd_ref, x_ref, gate_ref, o_ref,
                    dest0, dest1, ybuf0, ybuf1, sem_i, sem_y):
    i = pl.program_id(0)
    tm = h_ref.shape[0]
    n = tm * TOP_K
    span = n * PACK_TILES
    dests = (dest0, dest1)
    ybufs = (ybuf0, ybuf1)

    def start_tile(tile, slot):
        copy_dest = pltpu.make_async_copy(dest_hbm.at[tile], dests[slot], sem_i)
        copy_dest.start()
        copy_dest.wait()

        def issue(t, carry):
            base = t * TOKEN_PITCH
            for k in range(TOP_K):
                src = dests[slot][t * TOP_K + k]
                pltpu.make_async_copy(
                    y_hbm.at[_token_rows(src, PACK_TILES), :],
                    ybufs[slot].at[pl.ds(pl.multiple_of(base + k * YBUF_PITCH, 4), PACK_TILES), :],
                    sem_y.at[slot]).start(priority=k % 2)
            return carry

        lax.fori_loop(0, tm, issue, 0)

    def step(cur):
        nxt = 1 - cur
        pl.when(i == 0)(lambda: start_tile(0, cur))
        pl.when(i + 1 < pl.num_programs(0))(lambda: start_tile(i + 1, nxt))

        h = h_ref[...]
        gate = jnp.dot(h, wg_ref[...], preferred_element_type=F32)
        up = jnp.dot(h, wu_ref[...], preferred_element_type=F32)
        act = (gate * jax.nn.sigmoid(gate) * up).astype(BF16)
        shared = jnp.dot(act, wd_ref[...], preferred_element_type=F32)

        yb = ybufs[cur]
        pltpu.make_async_copy(y_hbm.at[pl.ds(0, span), :], yb.at[pl.ds(0, span), :], sem_y.at[cur]).wait()
        wks = [jnp.broadcast_to(w_ref[:, k:k + 1], (tm, LANES)) for k in range(TOP_K)]
        lows, highs = [], []
        for c in range(PACK_TILES):
            acc_lo = acc_hi = None
            for k in range(TOP_K):
                lo, hi = _unpack_halves(yb[pl.ds(k * YBUF_PITCH + c, tm, stride=TOKEN_PITCH), :])
                acc_lo = lo * wks[k] if acc_lo is None else acc_lo + lo * wks[k]
                acc_hi = hi * wks[k] if acc_hi is None else acc_hi + hi * wks[k]
            lows.append(acc_lo)
            highs.append(acc_hi)
        routed = jnp.concatenate(lows + highs, axis=1)
        o_ref[...] = x_ref[...] + gate_ref[...] * (routed + shared)

    pl.when(i % 2 == 0)(lambda: step(0))
    pl.when(i % 2 == 1)(lambda: step(1))


def _combine(dest_tm, y_sorted, wsel, h2, wsg, wsu, wsd, x1, gate):
    s, d = x1.shape
    n_tiles, n = dest_tm.shape
    tm = n // TOP_K
    f = wsg.shape[1]
    return pl.pallas_call(
        _combine_kernel,
        grid=(n_tiles,),
        in_specs=[pl.BlockSpec(memory_space=pl.ANY),
                  pl.BlockSpec(memory_space=pl.ANY),
                  pl.BlockSpec((tm, LANES), lambda i: (i, 0)),
                  pl.BlockSpec((tm, d), lambda i: (i, 0)),
                  pl.BlockSpec((d, f), lambda i: (0, 0)),
                  pl.BlockSpec((d, f), lambda i: (0, 0)),
                  pl.BlockSpec((f, d), lambda i: (0, 0)),
                  pl.BlockSpec((tm, d), lambda i: (i, 0)),
                  pl.BlockSpec((1, d), lambda i: (0, 0))],
        out_specs=pl.BlockSpec((tm, d), lambda i: (i, 0)),
        out_shape=jax.ShapeDtypeStruct((s, d), F32),
        scratch_shapes=[pltpu.SMEM((n,), jnp.int32),
                        pltpu.SMEM((n,), jnp.int32),
                        pltpu.VMEM((tm * TOKEN_PITCH, LANES), jnp.uint32),
                        pltpu.VMEM((tm * TOKEN_PITCH, LANES), jnp.uint32),
                        pltpu.SemaphoreType.DMA,
                        pltpu.SemaphoreType.DMA((2,))],
        compiler_params=_cparams(("arbitrary",)),
        name="moe_combine",
    )(dest_tm, y_sorted, wsel, h2, wsg, wsu, wsd, x1, gate)


def _expert_tables(counts):
    i32 = jnp.int32
    padded = ((counts + MOE_BLOCK - 1) // MOE_BLOCK * MOE_BLOCK).astype(i32)
    pad_end = jnp.cumsum(padded).astype(i32)
    pad_start = pad_end - padded
    ids = jnp.arange(N_EXPERTS, dtype=i32)
    busy = padded > 0
    later_busy = busy[None, :] & (ids[None, :] > ids[:, None])
    nxt = jnp.min(jnp.where(later_busy, ids[None, :], N_EXPERTS), axis=1)
    nxt = jnp.where(nxt < N_EXPERTS, nxt, -1).astype(i32)
    slot = (jnp.maximum(jnp.cumsum(busy.astype(i32)) - 1, 0) % 2).astype(i32)
    first = jnp.min(jnp.where(busy, ids, N_EXPERTS)).astype(i32).reshape(1)
    n_used = (pad_end[-1] // MOE_BLOCK).reshape(1)
    return pad_start, pad_end, padded // MOE_BLOCK, pad_start // MOE_BLOCK, nxt, slot, first, n_used


def kernel(x, c, ctx, c_ctx, w_ada, b_ada, norm_mix, norm_ffn, w_in, q_norm_a, k_norm_a, q_norm_b, k_norm_b, lambda_q1, lambda_k1, lambda_q2, lambda_k2, subln_b, w_branch_a, w_branch_b, w_out, w_router, router_bias, w_exp_gate, w_exp_up, w_exp_down, w_sh_gate, w_sh_up, w_sh_down):
    depth = w_ada.shape[0]
    assert depth == 1 and x.shape[0] == 1 and ctx.shape[0] == 1
    s, d = x.shape[1], x.shape[2]
    n_ctx = ctx.shape[1]
    i = 0
    lam_init = 0.8 - 0.6 * math.exp(-0.3 * i)
    xs = x[0]

    mod = _adaln(jnp.concatenate([c, c_ctx[None, :]], axis=0), w_ada[i], b_ada[i]).reshape(2, N_MOD, d)

    h = _prenorm(xs, ctx[0], norm_mix[i], mod)
    tc, tsa, tsb = _rope_tables(s, n_ctx)
    gains = _head_gains(q_norm_a[i], k_norm_a[i], q_norm_b[i], k_norm_b[i])
    proj = _inproj(h, w_in[i], gains, tc, tsa, tsb)
    oa = _gqa(proj, s)
    lam_vecs = jnp.stack([lambda_q1[i], lambda_k1[i], lambda_q2[i], lambda_k2[i]]).astype(F32)
    ob = _diff(proj, s, lam_vecs, subln_b[i], lam_init)
    x1 = _merge(oa, ob, proj, w_branch_a[i], w_branch_b[i],
                w_out[i].astype(BF16), xs, mod[0, 2:3, :])

    h2, h2p, idx, wsel, rank, cnt = _router(x1, norm_ffn[i], mod[0:1], w_router[i], router_bias[i])
    counts = cnt[0, :N_EXPERTS].astype(jnp.int32)
    n_blocks = -(-(s * TOP_K) // MOE_BLOCK) + N_EXPERTS
    pad_start, pad_end, nblk, blk0, nxt, slot, first, n_used = _expert_tables(counts)
    tm_dispatch = math.gcd(s, 1024)
    tm_combine = math.gcd(s, 256)
    e_ids = jnp.arange(N_EXPERTS, dtype=jnp.int32)
    starts = jnp.sum(jnp.where(idx[:, :TOP_K, None] == e_ids, pad_start, 0), axis=-1)
    dest = (starts + rank[:, :TOP_K]).astype(jnp.int32)
    tiles = lambda tm: dest.reshape(s // tm, tm * TOP_K)
    xs = _dispatch(pad_start, pad_end, n_used, tiles(tm_dispatch), h2p, n_blocks)
    y_sorted = _moe(nblk, blk0, nxt, slot, first, n_used, xs,
                    w_exp_gate[i], w_exp_up[i], w_exp_down[i], n_blocks)
    out = _combine(tiles(tm_combine), y_sorted, wsel, h2,
                   w_sh_gate[i].astype(BF16), w_sh_up[i].astype(BF16), w_sh_down[i].astype(BF16),
                   x1, mod[0, 5:6, :])
    return out[None]
```

```python
import functools
import math

import jax
import jax.numpy as jnp
from jax import lax
from jax.experimental import pallas as pl
from jax.experimental.pallas import tpu as pltpu

F32 = jnp.float32
BF16 = jnp.bfloat16

D_MODEL = 2048
GRID_W = 64
HEAD_DIM = 128
ROPE_PAIRS = HEAD_DIM // 4
ROPE_THETA = 10000.0
A_HEADS = 8
A_KV_HEADS = 2
A_GROUP = A_HEADS // A_KV_HEADS
B_HEADS = 4
B_V_DIM = 2 * HEAD_DIM
N_EXPERTS = 64
TOP_K = 8
N_GROUPS = 8
TOPK_GROUPS = 4
EXPERT_DIM = 512
SHARED_DIM = 512
ROUTED_SCALE = 2.5
N_MOD = 6
EPS = 1e-6

A_Q_W = A_HEADS * HEAD_DIM
A_KV_W = A_KV_HEADS * HEAD_DIM
B_QK_W = B_HEADS * 2 * HEAD_DIM
B_V_W = B_HEADS * B_V_DIM
IN_W = A_Q_W + 2 * A_KV_W + 2 * B_QK_W + B_V_W + 2 * D_MODEL

COL_AQ = 0
COL_AK = A_Q_W // HEAD_DIM
COL_AV = COL_AK + A_KV_HEADS
COL_BQ = COL_AV + A_KV_HEADS
COL_BK = COL_BQ + 2 * B_HEADS
COL_BV = COL_BK + 2 * B_HEADS
COL_GA = COL_BV + B_V_W // HEAD_DIM
COL_GB = COL_GA + D_MODEL // HEAD_DIM

LANES = 128
SUBLANES = 8
VMEM_LIMIT = 56 * 1024 * 1024

PROJ_TN = 512
MOE_BLOCK = 256
ONES_ROWS = 16
PACK_TILES = D_MODEL // 2 // LANES
YBUF_PITCH = PACK_TILES + 4
TOKEN_PITCH = TOP_K * YBUF_PITCH + 4
LOG2E = 1.4426950408889634


def _cparams(sem, vmem=VMEM_LIMIT):
    return pltpu.CompilerParams(dimension_semantics=sem, vmem_limit_bytes=vmem)


def _adaln_kernel(cb_ref, w_ref, b_ref, o_ref):
    tn = w_ref.shape[1]
    nl = tn // LANES
    rows = 32

    def body(g, accs):
        accs = list(accs)
        r0 = pl.multiple_of(g * rows, rows)
        for u in range(rows // SUBLANES):
            r = r0 + u * SUBLANES
            w = w_ref[pl.ds(r, SUBLANES), :]
            for v in range(2):
                c = cb_ref[v, pl.ds(r, SUBLANES), :]
                s = c * jax.nn.sigmoid(c)
                for j in range(nl):
                    accs[v * nl + j] = accs[v * nl + j] + w[:, j * LANES:(j + 1) * LANES] * s
        return tuple(accs)

    init = tuple(jnp.zeros((SUBLANES, LANES), F32) for _ in range(2 * nl))
    accs = lax.fori_loop(0, w_ref.shape[0] // rows, body, init)
    for v in range(2):
        row = jnp.concatenate(
            [jnp.sum(accs[v * nl + j], axis=0, keepdims=True) for j in range(nl)], axis=1)
        o_ref[v:v + 1, :] = row + b_ref[...]


def _adaln(cvecs, w, b):
    d, n = w.shape
    tn = 1536
    cb = jnp.broadcast_to(cvecs[:, :, None], (2, d, LANES))
    return pl.pallas_call(
        _adaln_kernel,
        grid=(n // tn,),
        in_specs=[pl.BlockSpec((2, d, LANES), lambda j: (0, 0, 0)),
                  pl.BlockSpec((d, tn), lambda j: (0, j)),
                  pl.BlockSpec((1, tn), lambda j: (0, j))],
        out_specs=pl.BlockSpec((2, tn), lambda j: (0, j)),
        out_shape=jax.ShapeDtypeStruct((2, n), F32),
        compiler_params=_cparams(("arbitrary",)),
        name="adaln",
    )(cb, w, b.reshape(1, n))


def _rms_mod(x, g, shift, scale):
    y = x * lax.rsqrt(jnp.mean(x * x, axis=-1, keepdims=True) + EPS) * g
    return y * (1.0 + scale) + shift


def _prenorm_kernel(x_ref, c_ref, g_ref, mod_ref, o_ref, *, n_lat_tiles):
    is_ctx = pl.program_id(0) >= n_lat_tiles
    x = jnp.where(is_ctx, c_ref[...], x_ref[...])
    o_ref[...] = _rms_mod(x, g_ref[...], mod_ref[0, 0:1, :], mod_ref[0, 1:2, :]).astype(o_ref.dtype)


def _prenorm(x, ctx, g, mod):
    s, d = x.shape
    c = ctx.shape[0]
    tm = 256
    nl, nc = s // tm, c // tm
    return pl.pallas_call(
        functools.partial(_prenorm_kernel, n_lat_tiles=nl),
        grid=(nl + nc,),
        in_specs=[pl.BlockSpec((tm, d), lambda i: (jnp.minimum(i, nl - 1), 0)),
                  pl.BlockSpec((tm, d), lambda i: (jnp.maximum(i - nl, 0), 0)),
                  pl.BlockSpec((1, d), lambda i: (0, 0)),
                  pl.BlockSpec((1, N_MOD, d), lambda i: (i // nl, 0, 0))],
        out_specs=pl.BlockSpec((tm, d), lambda i: (i, 0)),
        out_shape=jax.ShapeDtypeStruct((s + c, d), BF16),
        compiler_params=_cparams(("arbitrary",)),
        name="prenorm_mix",
    )(x, ctx, g.reshape(1, d), mod)


def _inproj_kernel(h_ref, w_ref, gain_ref, c_ref, sa_ref, sb_ref, o_ref):
    j = pl.program_id(1)
    tm = h_ref.shape[0]
    nh = o_ref.shape[1] // HEAD_DIM
    n_chunks = next(n for n in (6, 3, 2, 1) if tm % (16 * n) == 0)

    def tile(n_normed, chunks):
        w = w_ref[...].astype(h_ref.dtype)
        rows = tm // chunks
        for r in range(chunks):
            rs = slice(r * rows, (r + 1) * rows)
            acc = jnp.dot(h_ref[rs, :], w, preferred_element_type=F32)
            for hd in range(nh):
                sl = slice(hd * HEAD_DIM, (hd + 1) * HEAD_DIM)
                a = acc[:, sl]
                if hd < n_normed:
                    y = a * lax.rsqrt(jnp.mean(a * a, axis=-1, keepdims=True) + EPS) * gain_ref[0, :, sl]
                    a = (y * c_ref[rs, :] + pltpu.roll(y, ROPE_PAIRS, 1) * sa_ref[rs, :]
                         + pltpu.roll(y, HEAD_DIM - ROPE_PAIRS, 1) * sb_ref[rs, :])
                o_ref[rs, sl] = a.astype(o_ref.dtype)

    all_normed = (j < 2) | ((j >= 3) & (j < 7))
    pl.when(all_normed)(lambda: tile(nh, n_chunks))
    pl.when(j == 2)(lambda: tile(A_KV_HEADS, n_chunks))
    pl.when(j >= 7)(lambda: tile(0, n_chunks))


def _inproj(h, w, gains, rope_c, rope_sa, rope_sb):
    t, d = h.shape
    n = w.shape[1]
    tm = t // 4
    tn = PROJ_TN
    return pl.pallas_call(
        _inproj_kernel,
        grid=(t // tm, n // tn),
        in_specs=[pl.BlockSpec((tm, d), lambda i, j: (i, 0)),
                  pl.BlockSpec((d, tn), lambda i, j: (0, j)),
                  pl.BlockSpec((1, 1, tn), lambda i, j: (j, 0, 0)),
                  pl.BlockSpec((tm, HEAD_DIM), lambda i, j: (i, 0)),
                  pl.BlockSpec((tm, HEAD_DIM), lambda i, j: (i, 0)),
                  pl.BlockSpec((tm, HEAD_DIM), lambda i, j: (i, 0))],
        out_specs=pl.BlockSpec((tm, tn), lambda i, j: (i, j)),
        out_shape=jax.ShapeDtypeStruct((t, n), BF16),
        compiler_params=_cparams(("arbitrary", "arbitrary")),
        name="inproj",
    )(h, w, gains, rope_c, rope_sa, rope_sb)


def _rope_tables(s, c):
    rows_n = s // GRID_W
    inv = ROPE_THETA ** (-jnp.arange(ROPE_PAIRS, dtype=F32) / ROPE_PAIRS)
    ang_r = jnp.arange(rows_n, dtype=F32)[:, None] * inv
    ang_c = jnp.arange(GRID_W, dtype=F32)[:, None] * inv
    cr, sr, cc, sc = jnp.cos(ang_r), jnp.sin(ang_r), jnp.cos(ang_c), jnp.sin(ang_c)
    zr, zc = jnp.zeros_like(sr), jnp.zeros_like(sc)

    def table(row_parts, col_parts, ctx_value):
        by_row = jnp.concatenate(row_parts + [zr, zr], axis=1)
        by_col = jnp.concatenate([zc, zc] + col_parts, axis=1)
        lat = (by_row[:, None, :] + by_col[None, :, :]).reshape(s, HEAD_DIM)
        return jnp.concatenate([lat, jnp.full((c, HEAD_DIM), ctx_value, F32)], axis=0)

    return (table([cr, cr], [cc, cc], 1.0), table([zr, sr], [zc, sc], 0.0), table([-sr, zr], [-sc, zc], 0.0))


def _head_gains(qn_a, kn_a, qn_b, kn_b):
    qs = HEAD_DIM ** -0.5 * LOG2E
    one = jnp.ones((HEAD_DIM,), F32)
    heads = ([qn_a * qs] * A_HEADS + [kn_a] * A_KV_HEADS + [one] * A_KV_HEADS
             + [qn_b * qs] * (2 * B_HEADS) + [kn_b] * (2 * B_HEADS))
    heads = heads + [one] * (IN_W // HEAD_DIM - len(heads))
    return jnp.concatenate(heads).reshape(IN_W // PROJ_TN, 1, PROJ_TN)


def _build_vt(v_ref, vt_ref, tk):
    n_chunks, rows, _ = vt_ref.shape
    dv = rows - ONES_ROWS
    tail = (lax.broadcasted_iota(jnp.int32, (ONES_ROWS, tk), 0) == 0).astype(vt_ref.dtype)
    for c in range(n_chunks):
        vt_ref[c, 0:dv, :] = v_ref[c * tk:(c + 1) * tk, :].astype(F32).T.astype(vt_ref.dtype)
        vt_ref[c, dv:rows, :] = tail


def _attend(qs, k_refs, vt_ref, s_ref, m_ref, acc_ref):
    n_chunks, _, tk = vt_ref.shape
    n_st = len(qs)
    m_ref[...] = jnp.full(m_ref.shape, -jnp.inf, F32)
    acc_ref[...] = jnp.zeros(acc_ref.shape, F32)

    def scores(i, c, slot):
        off = c * tk if isinstance(c, int) else pl.multiple_of(c * tk, tk)
        s_ref[i, slot] = lax.dot_general(k_refs[i][pl.ds(off, tk), :], qs[i], (((1,), (1,)), ((), ())),
                                         preferred_element_type=F32)

    def update(i, c, slot):
        s = s_ref[i, slot]
        m_old = m_ref[i]
        m_new = jnp.maximum(m_old, jnp.max(s, axis=0, keepdims=True))
        p = jnp.exp2(s - m_new).astype(vt_ref.dtype)
        acc_ref[i] = (acc_ref[i] * jnp.exp2(m_old - m_new)
                      + jnp.dot(vt_ref[c], p, preferred_element_type=F32))
        m_ref[i] = m_new

    for i in range(n_st):
        scores(i, 0, 0)

    def pair(j, carry):
        c = 2 * j
        for i in range(n_st):
            scores(i, c + 1, 1)
        for i in range(n_st):
            update(i, c, 0)
        for i in range(n_st):
            scores(i, c + 2, 0)
        for i in range(n_st):
            update(i, c + 1, 1)
        return carry

    n_pairs = (n_chunks - 1) // 2
    lax.fori_loop(0, n_pairs, pair, 0)
    done = 2 * n_pairs
    if n_chunks - done == 2:
        for i in range(n_st):
            scores(i, done + 1, 1)
    for i in range(n_st):
        update(i, done, 0)
    if n_chunks - done == 2:
        for i in range(n_st):
            update(i, done + 1, 1)


def _gqa_kernel(q_ref, k_ref, v_ref, o_ref, vt_ref, s_ref, m_ref, acc_ref):
    tk = vt_ref.shape[2]
    n_st = s_ref.shape[0]
    pl.when((pl.program_id(1) == 0) & (pl.program_id(2) == 0))(lambda: _build_vt(v_ref, vt_ref, tk))
    qs = [q_ref[:, i * HEAD_DIM:(i + 1) * HEAD_DIM] for i in range(n_st)]
    _attend(qs, [k_ref] * n_st, vt_ref, s_ref, m_ref, acc_ref)
    for i in range(n_st):
        o_t = acc_ref[i, 0:HEAD_DIM, :] / acc_ref[i, HEAD_DIM:HEAD_DIM + 1, :]
        o_ref[:, i * HEAD_DIM:(i + 1) * HEAD_DIM] = o_t.T.astype(o_ref.dtype)


def _key_chunk(t):
    for tk in (1408, 768, 1024, 512, 640, 384, 256, 128):
        if t % tk == 0:
            return tk
    raise ValueError(f"unsupported key count {t}")


def _gqa(proj, s):
    t = proj.shape[0]
    tq = 512
    tk = _key_chunk(t)
    n_st = 4
    per_g = A_GROUP // n_st
    return pl.pallas_call(
        _gqa_kernel,
        grid=(A_KV_HEADS, per_g, s // tq),
        in_specs=[pl.BlockSpec((tq, n_st * HEAD_DIM), lambda g, hh, i: (i, COL_AQ // n_st + g * per_g + hh)),
                  pl.BlockSpec((t, HEAD_DIM), lambda g, hh, i: (0, COL_AK + g)),
                  pl.BlockSpec((t, HEAD_DIM), lambda g, hh, i: (0, COL_AV + g))],
        out_specs=pl.BlockSpec((tq, n_st * HEAD_DIM), lambda g, hh, i: (i, g * per_g + hh)),
        out_shape=jax.ShapeDtypeStruct((s, A_Q_W), BF16),
        scratch_shapes=[pltpu.VMEM((t // tk, HEAD_DIM + ONES_ROWS, tk), BF16),
                        pltpu.VMEM((n_st, 2, tk, tq), F32),
                        pltpu.VMEM((n_st, 1, tq), F32),
                        pltpu.VMEM((n_st, HEAD_DIM + ONES_ROWS, tq), F32)],
        compiler_params=_cparams(("arbitrary", "arbitrary", "arbitrary")),
        name="gqa_attn",
    )(proj, proj, proj)


def _diff_kernel(lam_ref, q0_ref, q1_ref, k0_ref, k1_ref, v_ref, g_ref, o_ref,
                 vt_ref, s_ref, m_ref, acc_ref, *, lam_init):
    tk = vt_ref.shape[2]
    pl.when(pl.program_id(1) == 0)(lambda: _build_vt(v_ref, vt_ref, tk))
    lv = lam_ref[...]
    lam = (jnp.exp(jnp.sum(lv[0:1, :] * lv[1:2, :], axis=-1, keepdims=True))
           - jnp.exp(jnp.sum(lv[2:3, :] * lv[3:4, :], axis=-1, keepdims=True)) + lam_init)
    _attend([q0_ref[...], q1_ref[...]], [k0_ref, k1_ref], vt_ref, s_ref, m_ref, acc_ref)
    o_t = (acc_ref[0, 0:B_V_DIM, :] / acc_ref[0, B_V_DIM:B_V_DIM + 1, :]
           - lam * (acc_ref[1, 0:B_V_DIM, :] / acc_ref[1, B_V_DIM:B_V_DIM + 1, :]))
    o = o_t.T
    y = o * lax.rsqrt(jnp.mean(o * o, axis=-1, keepdims=True) + EPS) * g_ref[...]
    o_ref[...] = (y * (1.0 - lam_init)).astype(o_ref.dtype)


def _diff(proj, s, lam_vecs, subln_g, lam_init):
    t = proj.shape[0]
    tq = 512
    tk = _key_chunk(t)
    vb = B_V_DIM // HEAD_DIM
    return pl.pallas_call(
        functools.partial(_diff_kernel, lam_init=lam_init),
        grid=(B_HEADS, s // tq),
        in_specs=[pl.BlockSpec((4, HEAD_DIM), lambda h, i: (0, 0)),
                  pl.BlockSpec((tq, HEAD_DIM), lambda h, i: (i, COL_BQ + 2 * h)),
                  pl.BlockSpec((tq, HEAD_DIM), lambda h, i: (i, COL_BQ + 2 * h + 1)),
                  pl.BlockSpec((t, HEAD_DIM), lambda h, i: (0, COL_BK + 2 * h)),
                  pl.BlockSpec((t, HEAD_DIM), lambda h, i: (0, COL_BK + 2 * h + 1)),
                  pl.BlockSpec((t, B_V_DIM), lambda h, i: (0, COL_BV // vb + h)),
                  pl.BlockSpec((1, B_V_DIM), lambda h, i: (0, 0))],
        out_specs=pl.BlockSpec((tq, B_V_DIM), lambda h, i: (i, h)),
        out_shape=jax.ShapeDtypeStruct((s, B_V_W), BF16),
        scratch_shapes=[pltpu.VMEM((t // tk, B_V_DIM + ONES_ROWS, tk), BF16),
                        pltpu.VMEM((2, 2, tk, tq), F32),
                        pltpu.VMEM((2, 1, tq), F32),
                        pltpu.VMEM((2, B_V_DIM + ONES_ROWS, tq), F32)],
        compiler_params=_cparams(("arbitrary", "arbitrary")),
        name="diff_attn",
    )(lam_vecs, proj, proj, proj, proj, proj, subln_g.reshape(1, B_V_DIM))


def _merge_kernel(oa_ref, ob_ref, ga_ref, gb_ref, wa_ref, wb_ref, wo_ref, x_ref, gate_ref, o_ref, t_ref):
    c = pl.program_id(1)
    nc, _, tn = t_ref.shape
    ya = jnp.dot(oa_ref[...], wa_ref[...].astype(oa_ref.dtype), preferred_element_type=F32)
    yb = jnp.dot(ob_ref[...], wb_ref[...].astype(ob_ref.dtype), preferred_element_type=F32)
    t = (jax.nn.sigmoid(ga_ref[...].astype(F32)) * ya + jax.nn.sigmoid(gb_ref[...].astype(F32)) * yb)
    t_ref[c] = t.astype(t_ref.dtype)

    @pl.when(c == nc - 1)
    def _():
        y = jnp.dot(t_ref[0], wo_ref[0:tn, :], preferred_element_type=F32)
        for cc in range(1, nc):
            y = y + jnp.dot(t_ref[cc], wo_ref[cc * tn:(cc + 1) * tn, :], preferred_element_type=F32)
        o_ref[...] = x_ref[...] + gate_ref[...] * y


def _merge(oa, ob, proj, wa, wb, wo, x, gate):
    s, d = x.shape
    tm = 512
    tn = PROJ_TN
    nc = d // tn
    ga0 = COL_GA * HEAD_DIM // tn
    gb0 = COL_GB * HEAD_DIM // tn
    return pl.pallas_call(
        _merge_kernel,
        grid=(s // tm, nc),
        in_specs=[pl.BlockSpec((tm, A_Q_W), lambda i, c: (i, 0)),
                  pl.BlockSpec((tm, B_V_W), lambda i, c: (i, 0)),
                  pl.BlockSpec((tm, tn), lambda i, c: (i, ga0 + c)),
                  pl.BlockSpec((tm, tn), lambda i, c: (i, gb0 + c)),
                  pl.BlockSpec((A_Q_W, tn), lambda i, c: (0, c)),
                  pl.BlockSpec((B_V_W, tn), lambda i, c: (0, c)),
                  pl.BlockSpec((d, d), lambda i, c: (0, 0)),
                  pl.BlockSpec((tm, d), lambda i, c: (i, 0)),
                  pl.BlockSpec((1, d), lambda i, c: (0, 0))],
        out_specs=pl.BlockSpec((tm, d), lambda i, c: (i, 0)),
        out_shape=jax.ShapeDtypeStruct((s, d), F32),
        scratch_shapes=[pltpu.VMEM((nc, tm, tn), BF16)],
        compiler_params=_cparams(("arbitrary", "arbitrary")),
        name="merge_out",
    )(oa, ob, proj, proj, wa, wb, wo, x, gate)


def _store_packed_tiles(ref, x):
    rows = x.shape[0]
    bits = lax.bitcast_convert_type(x.astype(BF16).astype(F32), jnp.uint32)
    half = bits.shape[1] // 2
    words = (bits[:, :half] >> 16) | (bits[:, half:] & jnp.uint32(0xFFFF0000))
    for c in range(PACK_TILES):
        ref[pl.ds(c, rows, stride=PACK_TILES), :] = words[:, c * LANES:(c + 1) * LANES]


def _lane_max(x):
    return jnp.max(x, axis=-1, keepdims=True)


def _lane_min(x):
    return jnp.min(x, axis=-1, keepdims=True)


def _group_allreduce(x, lane, op):
    for sft in (1, 2, 4):
        up = pltpu.roll(x, sft, 1)
        dn = pltpu.roll(x, LANES - sft, 1)
        x = op(x, jnp.where((lane & sft) != 0, up, dn))
    return x


def _router_kernel(x_ref, g_ref, mod_ref, wr_ref, rb_ref,
                   h_ref, hp_ref, idx_ref, wsel_ref, rank_ref, cnt_ref, carry_ref):
    @pl.when(pl.program_id(0) == 0)
    def _():
        carry_ref[...] = jnp.zeros_like(carry_ref)

    h = _rms_mod(x_ref[...], g_ref[...], mod_ref[0, 3:4, :], mod_ref[0, 4:5, :])
    hb = h.astype(BF16)
    h_ref[...] = hb
    _store_packed_tiles(hp_ref, h)
    tm = h.shape[0]

    h_lo = (h - hb.astype(F32)).astype(BF16)
    logits = (jnp.dot(hb, wr_ref[0], preferred_element_type=F32)
              + jnp.dot(hb, wr_ref[1], preferred_element_type=F32)
              + jnp.dot(h_lo, wr_ref[0], preferred_element_type=F32))
    scores = jax.nn.sigmoid(logits)
    lane = lax.broadcasted_iota(jnp.int32, (tm, LANES), 1)
    lane_f = lane.astype(F32)
    gid_f = (lane >> 3).astype(F32)
    valid = lane < N_EXPERTS
    neg = jnp.float32(-jnp.inf)
    big = jnp.float32(LANES)
    biased = jnp.where(valid, scores + rb_ref[...], neg)

    m1 = _group_allreduce(biased, lane, jnp.maximum)
    a1 = _group_allreduce(jnp.where(biased == m1, lane_f, big), lane, jnp.minimum)
    m2 = _group_allreduce(jnp.where(lane_f == a1, neg, biased), lane, jnp.maximum)
    gscore = jnp.where(valid, m1 + m2, neg)
    keep = jnp.zeros((tm, LANES), jnp.bool_)
    for _ in range(TOPK_GROUPS):
        best = _lane_max(gscore)
        gsel = _lane_min(jnp.where(gscore == best, gid_f, big))
        hit = gid_f == gsel
        keep = keep | hit
        gscore = jnp.where(hit, neg, gscore)
    cand = jnp.where(keep & valid, biased, neg)

    onehot = jnp.zeros((tm, LANES), F32)
    idx_out = jnp.zeros((tm, LANES), F32)
    w_out = jnp.zeros((tm, LANES), F32)
    sels = []
    for k in range(TOP_K):
        best = _lane_max(cand)
        sel = _lane_min(jnp.where(cand == best, lane_f, big))
        hit = lane_f == sel
        wk = jnp.sum(jnp.where(hit, scores, 0.0), axis=-1, keepdims=True)
        cand = jnp.where(hit, neg, cand)
        onehot = jnp.where(hit, 1.0, onehot)
        idx_out = jnp.where(lane == k, sel, idx_out)
        w_out = jnp.where(lane == k, wk, w_out)
        sels.append(hit)
    wsum = jnp.sum(w_out, axis=-1, keepdims=True)
    wsel_ref[...] = w_out / wsum * ROUTED_SCALE
    idx_ref[...] = idx_out.astype(jnp.int32)

    row = lax.broadcasted_iota(jnp.int32, (tm, tm), 0)
    col = lax.broadcasted_iota(jnp.int32, (tm, tm), 1)
    lower = (col < row).astype(BF16)
    before = jnp.dot(lower, onehot.astype(BF16), preferred_element_type=F32) + carry_ref[...]
    rank_out = jnp.zeros((tm, LANES), F32)
    for k in range(TOP_K):
        rk = jnp.sum(jnp.where(sels[k], before, 0.0), axis=-1, keepdims=True)
        rank_out = jnp.where(lane == k, rk, rank_out)
    rank_ref[...] = rank_out.astype(jnp.int32)
    carry_ref[...] = carry_ref[...] + jnp.sum(onehot, axis=0, keepdims=True)
    cnt_ref[...] = carry_ref[...]


def _router(x1, g, mod, w_router, router_bias):
    s, d = x1.shape
    tm = 256
    e = w_router.shape[1]
    wr = jnp.pad(w_router, ((0, 0), (0, LANES - e)))
    wr_hi = wr.astype(BF16)
    wr = jnp.stack([wr_hi, (wr - wr_hi.astype(F32)).astype(BF16)])
    rb = jnp.pad(router_bias, (0, LANES - e)).reshape(1, LANES)
    row_spec = lambda w: pl.BlockSpec((tm, w), lambda i: (i, 0))
    return pl.pallas_call(
        _router_kernel,
        grid=(s // tm,),
        in_specs=[row_spec(d),
                  pl.BlockSpec((1, d), lambda i: (0, 0)),
                  pl.BlockSpec((1, N_MOD, d), lambda i: (0, 0, 0)),
                  pl.BlockSpec((2, d, LANES), lambda i: (0, 0, 0)),
                  pl.BlockSpec((1, LANES), lambda i: (0, 0))],
        out_specs=[row_spec(d), pl.BlockSpec((tm * PACK_TILES, LANES), lambda i: (i, 0)),
                   row_spec(LANES), row_spec(LANES), row_spec(LANES),
                   pl.BlockSpec((1, LANES), lambda i: (0, 0))],
        out_shape=[jax.ShapeDtypeStruct((s, d), BF16),
                   jax.ShapeDtypeStruct((s * PACK_TILES, LANES), jnp.uint32),
                   jax.ShapeDtypeStruct((s, LANES), jnp.int32),
                   jax.ShapeDtypeStruct((s, LANES), F32),
                   jax.ShapeDtypeStruct((s, LANES), jnp.int32),
                   jax.ShapeDtypeStruct((1, LANES), F32)],
        scratch_shapes=[pltpu.VMEM((1, LANES), F32)],
        compiler_params=_cparams(("arbitrary",)),
        name="ffn_router",
    )(x1, g.reshape(1, d), mod, wr, rb)


def _unpack_halves(words):
    lo = lax.bitcast_convert_type(words << 16, F32)
    hi = lax.bitcast_convert_type(words & jnp.uint32(0xFFFF0000), F32)
    return lo, hi


def _unpack_rows(words):
    lo, hi = _unpack_halves(words)
    return jnp.concatenate([lo.astype(BF16), hi.astype(BF16)], axis=1)


def _token_rows(row, tiles):
    return pl.ds(pl.multiple_of(row * tiles, tiles), tiles)


def _dispatch_kernel(ps_ref, pe_ref, nused_ref, dest_hbm, hp_ref, xs_hbm,
                     dest_smem, zbuf, sem_i, sem_z, sem_x, *, n_blocks):
    i = pl.program_id(0)
    n = dest_smem.shape[0]
    tm = n // TOP_K
    copy_dest = pltpu.make_async_copy(dest_hbm.at[i], dest_smem, sem_i)
    copy_dest.start()

    @pl.when(i == 0)
    def _():
        zbuf[...] = jnp.zeros_like(zbuf)

        def zero_block(row0):
            span = MOE_BLOCK * PACK_TILES
            return pltpu.make_async_copy(
                zbuf, xs_hbm.at[pl.ds(pl.multiple_of(row0 * PACK_TILES, span), span), :], sem_z)

        def per_expert(action):
            def body(e, carry):
                @pl.when(pe_ref[e] > ps_ref[e])
                def _():
                    action(zero_block(pe_ref[e] - MOE_BLOCK))
                return carry
            lax.fori_loop(0, N_EXPERTS, body, 0)

        def per_tail(action):
            def body(b, carry):
                action(zero_block(b * MOE_BLOCK))
                return carry
            lax.fori_loop(nused_ref[0], n_blocks, body, 0)

        per_expert(lambda cp: cp.start())
        per_tail(lambda cp: cp.start())
        per_expert(lambda cp: cp.wait())
        per_tail(lambda cp: cp.wait())

    copy_dest.wait()

    def issue(t, carry):
        for k in range(TOP_K):
            dest = dest_smem[t * TOP_K + k]
            pltpu.make_async_copy(hp_ref.at[_token_rows(t, PACK_TILES), :],
                                  xs_hbm.at[_token_rows(dest, PACK_TILES), :], sem_x).start(priority=k % 2)
        return carry

    lax.fori_loop(0, tm, issue, 0)
    for _ in range(TOP_K):
        pltpu.make_async_copy(hp_ref, xs_hbm.at[pl.ds(0, tm * PACK_TILES), :], sem_x).wait()


def _dispatch(pad_start, pad_end, n_used, dest_tm, hp, n_blocks):
    n_tiles, n = dest_tm.shape
    tm = n // TOP_K
    grid_spec = pltpu.PrefetchScalarGridSpec(
        num_scalar_prefetch=3,
        grid=(n_tiles,),
        in_specs=[pl.BlockSpec(memory_space=pl.ANY),
                  pl.BlockSpec((tm * PACK_TILES, LANES), lambda i, ps, pe, nu: (i, 0))],
        out_specs=pl.BlockSpec(memory_space=pl.ANY),
        scratch_shapes=[pltpu.SMEM((n,), jnp.int32),
                        pltpu.VMEM((MOE_BLOCK * PACK_TILES, LANES), jnp.uint32),
                        pltpu.SemaphoreType.DMA,
                        pltpu.SemaphoreType.DMA,
                        pltpu.SemaphoreType.DMA])
    return pl.pallas_call(
        functools.partial(_dispatch_kernel, n_blocks=n_blocks),
        grid_spec=grid_spec,
        out_shape=jax.ShapeDtypeStruct((n_blocks * MOE_BLOCK * PACK_TILES, LANES), jnp.uint32),
        compiler_params=_cparams(("arbitrary",)),
        name="moe_dispatch",
    )(pad_start, pad_end, n_used, dest_tm, hp)


def _moe_kernel(nblk_ref, blk0_ref, nxt_ref, slot_ref, first_ref, nused_ref,
                xs_hbm, wg_hbm, wu_hbm, wd_hbm, y_hbm,
                xbuf, ybuf, wg32, wu32, wd32, wgb, wub, wdb, sem_w, sem_x, sem_y, *, n_blocks):
    e = pl.program_id(0)
    nb = nblk_ref[e]
    span = MOE_BLOCK * PACK_TILES

    def block_rows(b):
        return pl.ds(pl.multiple_of(b * span, span), span)

    def fetch(ex, slot):
        return (pltpu.make_async_copy(wg_hbm.at[ex], wg32.at[slot], sem_w.at[slot, 0]),
                pltpu.make_async_copy(wu_hbm.at[ex], wu32.at[slot], sem_w.at[slot, 1]),
                pltpu.make_async_copy(wd_hbm.at[ex], wd32.at[slot], sem_w.at[slot, 2]))

    def x_copy(b, s):
        return pltpu.make_async_copy(xs_hbm.at[block_rows(b), :], xbuf.at[s], sem_x.at[s])

    def y_copy(b, s):
        return pltpu.make_async_copy(ybuf.at[s], y_hbm.at[block_rows(b), :], sem_y.at[s])

    n_used = nused_ref[0]
    n_xbuf = xbuf.shape[0]

    @pl.when(nb > 0)
    def _():
        b0 = blk0_ref[e]
        slot = slot_ref[e]

        @pl.when(e == first_ref[0])
        def _():
            for b in range(n_xbuf - 1):
                pl.when(b < n_used)(lambda b=b: x_copy(b, b).start())
            for cp in fetch(e, 0):
                cp.start(priority=1)

        for cp in fetch(e, slot):
            cp.wait()

        @pl.when(nxt_ref[e] >= 0)
        def _():
            for cp in fetch(nxt_ref[e], 1 - slot):
                cp.start(priority=1)

        wgb[...] = wg32[slot].astype(BF16)
        wub[...] = wu32[slot].astype(BF16)
        wdb[...] = wd32[slot].astype(BF16)

        def body(j, carry):
            b = b0 + j
            s = b % 2
            sx = b % n_xbuf
            ahead = b + n_xbuf - 1
            x_copy(b, sx).wait()
            pl.when(ahead < n_used)(lambda: x_copy(ahead, ahead % n_xbuf).start())
            pl.when(b >= 2)(lambda: y_copy(b - 2, s).wait())
            xb = xbuf.at[sx]
            words = jnp.concatenate(
                [xb[pl.ds(c, MOE_BLOCK, stride=PACK_TILES), :] for c in range(PACK_TILES)], axis=1)
            x = _unpack_rows(words)
            gate = jnp.dot(x, wgb[...], preferred_element_type=F32)
            up = jnp.dot(x, wub[...], preferred_element_type=F32)
            act = (gate * jax.nn.sigmoid(gate) * up).astype(BF16)
            _store_packed_tiles(ybuf.at[s], jnp.dot(act, wdb[...], preferred_element_type=F32))
            y_copy(b, s).start()
            return carry

        lax.fori_loop(0, nb, body, 0)

    @pl.when(e == pl.num_programs(0) - 1)
    def _():
        pl.when(n_used >= 2)(lambda: y_copy(n_used - 2, n_used % 2).wait())
        pl.when(n_used >= 1)(lambda: y_copy(n_used - 1, (n_used - 1) % 2).wait())
        ybuf[0] = jnp.zeros(ybuf.shape[1:], ybuf.dtype)

        def tail(action):
            def step(b, carry):
                action(y_copy(b, 0))
                return carry
            lax.fori_loop(nused_ref[0], n_blocks, step, 0)

        tail(lambda cp: cp.start())
        tail(lambda cp: cp.wait())


def _moe(nblk, blk0, nxt, slot, first, n_used, xs, weg, weu, wed, n_blocks):
    n_exp, d, f = weg.shape
    span = MOE_BLOCK * PACK_TILES
    grid_spec = pltpu.PrefetchScalarGridSpec(
        num_scalar_prefetch=6,
        grid=(n_exp,),
        in_specs=[pl.BlockSpec(memory_space=pl.ANY)] * 4,
        out_specs=pl.BlockSpec(memory_space=pl.ANY),
        scratch_shapes=[pltpu.VMEM((3, span, LANES), jnp.uint32), pltpu.VMEM((2, span, LANES), jnp.uint32),
                        pltpu.VMEM((2, d, f), F32), pltpu.VMEM((2, d, f), F32), pltpu.VMEM((2, f, d), F32),
                        pltpu.VMEM((d, f), BF16), pltpu.VMEM((d, f), BF16), pltpu.VMEM((f, d), BF16),
                        pltpu.SemaphoreType.DMA((2, 3)), pltpu.SemaphoreType.DMA((3,)),
                        pltpu.SemaphoreType.DMA((2,))])
    return pl.pallas_call(
        functools.partial(_moe_kernel, n_blocks=n_blocks),
        grid_spec=grid_spec,
        out_shape=jax.ShapeDtypeStruct((n_blocks * span, LANES), jnp.uint32),
        compiler_params=_cparams(("arbitrary",)),
        name="moe_experts",
    )(nblk, blk0, nxt, slot, first, n_used, xs, weg, weu, wed)


def _combine_kernel(dest_hbm, y_hbm, w_ref, h_ref, wg_ref, wu_ref, wd_ref, x_ref, gate_ref, o_ref,
                    dest0, dest1, ybuf0, ybuf1, sem_i, sem_y):
    i = pl.program_id(0)
    tm = h_ref.shape[0]
    n = tm * TOP_K
    span = n * PACK_TILES
    dests = (dest0, dest1)
    ybufs = (ybuf0, ybuf1)

    def start_tile(tile, slot):
        copy_dest = pltpu.make_async_copy(dest_hbm.at[tile], dests[slot], sem_i)
        copy_dest.start()
        copy_dest.wait()

        def issue(t, carry):
            base = t * TOKEN_PITCH
            for k in range(TOP_K):
                src = dests[slot][t * TOP_K + k]
                pltpu.make_async_copy(
                    y_hbm.at[_token_rows(src, PACK_TILES), :],
                    ybufs[slot].at[pl.ds(pl.multiple_of(base + k * YBUF_PITCH, 4), PACK_TILES), :],
                    sem_y.at[slot]).start(priority=k % 2)
            return carry

        lax.fori_loop(0, tm, issue, 0)

    def step(cur):
        nxt = 1 - cur
        pl.when(i == 0)(lambda: start_tile(0, cur))
        pl.when(i + 1 < pl.num_programs(0))(lambda: start_tile(i + 1, nxt))

        h = h_ref[...]
        gate = jnp.dot(h, wg_ref[...], preferred_element_type=F32)
        up = jnp.dot(h, wu_ref[...], preferred_element_type=F32)
        act = (gate * jax.nn.sigmoid(gate) * up).astype(BF16)
        shared = jnp.dot(act, wd_ref[...], preferred_element_type=F32)

        yb = ybufs[cur]
        pltpu.make_async_copy(y_hbm.at[pl.ds(0, span), :], yb.at[pl.ds(0, span), :], sem_y.at[cur]).wait()
        wks = [jnp.broadcast_to(w_ref[:, k:k + 1], (tm, LANES)) for k in range(TOP_K)]
        lows, highs = [], []
        for c in range(PACK_TILES):
            acc_lo = acc_hi = None
            for k in range(TOP_K):
                lo, hi = _unpack_halves(yb[pl.ds(k * YBUF_PITCH + c, tm, stride=TOKEN_PITCH), :])
                acc_lo = lo * wks[k] if acc_lo is None else acc_lo + lo * wks[k]
                acc_hi = hi * wks[k] if acc_hi is None else acc_hi + hi * wks[k]
            lows.append(acc_lo)
            highs.append(acc_hi)
        routed = jnp.concatenate(lows + highs, axis=1)
        o_ref[...] = x_ref[...] + gate_ref[...] * (routed + shared)

    pl.when(i % 2 == 0)(lambda: step(0))
    pl.when(i % 2 == 1)(lambda: step(1))


def _combine(dest_tm, y_sorted, wsel, h2, wsg, wsu, wsd, x1, gate):
    s, d = x1.shape
    n_tiles, n = dest_tm.shape
    tm = n // TOP_K
    f = wsg.shape[1]
    return pl.pallas_call(
        _combine_kernel,
        grid=(n_tiles,),
        in_specs=[pl.BlockSpec(memory_space=pl.ANY),
                  pl.BlockSpec(memory_space=pl.ANY),
                  pl.BlockSpec((tm, LANES), lambda i: (i, 0)),
                  pl.BlockSpec((tm, d), lambda i: (i, 0)),
                  pl.BlockSpec((d, f), lambda i: (0, 0)),
                  pl.BlockSpec((d, f), lambda i: (0, 0)),
                  pl.BlockSpec((f, d), lambda i: (0, 0)),
                  pl.BlockSpec((tm, d), lambda i: (i, 0)),
                  pl.BlockSpec((1, d), lambda i: (0, 0))],
        out_specs=pl.BlockSpec((tm, d), lambda i: (i, 0)),
        out_shape=jax.ShapeDtypeStruct((s, d), F32),
        scratch_shapes=[pltpu.SMEM((n,), jnp.int32),
                        pltpu.SMEM((n,), jnp.int32),
                        pltpu.VMEM((tm * TOKEN_PITCH, LANES), jnp.uint32),
                        pltpu.VMEM((tm * TOKEN_PITCH, LANES), jnp.uint32),
                        pltpu.SemaphoreType.DMA,
                        pltpu.SemaphoreType.DMA((2,))],
        compiler_params=_cparams(("arbitrary",)),
        name="moe_combine",
    )(dest_tm, y_sorted, wsel, h2, wsg, wsu, wsd, x1, gate)


def _expert_tables(counts):
    i32 = jnp.int32
    padded = ((counts + MOE_BLOCK - 1) // MOE_BLOCK * MOE_BLOCK).astype(i32)
    pad_end = jnp.cumsum(padded).astype(i32)
    pad_start = pad_end - padded
    ids = jnp.arange(N_EXPERTS, dtype=i32)
    busy = padded > 0
    later_busy = busy[None, :] & (ids[None, :] > ids[:, None])
    nxt = jnp.min(jnp.where(later_busy, ids[None, :], N_EXPERTS), axis=1)
    nxt = jnp.where(nxt < N_EXPERTS, nxt, -1).astype(i32)
    slot = (jnp.maximum(jnp.cumsum(busy.astype(i32)) - 1, 0) % 2).astype(i32)
    first = jnp.min(jnp.where(busy, ids, N_EXPERTS)).astype(i32).reshape(1)
    n_used = (pad_end[-1] // MOE_BLOCK).reshape(1)
    return pad_start, pad_end, padded // MOE_BLOCK, pad_start // MOE_BLOCK, nxt, slot, first, n_used


def kernel(x, c, ctx, c_ctx, w_ada, b_ada, norm_mix, norm_ffn, w_in, q_norm_a, k_norm_a, q_norm_b, k_norm_b, lambda_q1, lambda_k1, lambda_q2, lambda_k2, subln_b, w_branch_a, w_branch_b, w_out, w_router, router_bias, w_exp_gate, w_exp_up, w_exp_down, w_sh_gate, w_sh_up, w_sh_down):
    depth = w_ada.shape[0]
    assert depth == 1 and x.shape[0] == 1 and ctx.shape[0] == 1
    s, d = x.shape[1], x.shape[2]
    n_ctx = ctx.shape[1]
    i = 0
    lam_init = 0.8 - 0.6 * math.exp(-0.3 * i)
    xs = x[0]

    mod = _adaln(jnp.concatenate([c, c_ctx[None, :]], axis=0), w_ada[i], b_ada[i]).reshape(2, N_MOD, d)

    h = _prenorm(xs, ctx[0], norm_mix[i], mod)
    tc, tsa, tsb = _rope_tables(s, n_ctx)
    gains = _head_gains(q_norm_a[i], k_norm_a[i], q_norm_b[i], k_norm_b[i])
    proj = _inproj(h, w_in[i], gains, tc, tsa, tsb)
    oa = _gqa(proj, s)
    lam_vecs = jnp.stack([lambda_q1[i], lambda_k1[i], lambda_q2[i], lambda_k2[i]]).astype(F32)
    ob = _diff(proj, s, lam_vecs, subln_b[i], lam_init)
    x1 = _merge(oa, ob, proj, w_branch_a[i], w_branch_b[i],
                w_out[i].astype(BF16), xs, mod[0, 2:3, :])

    h2, h2p, idx, wsel, rank, cnt = _router(x1, norm_ffn[i], mod[0:1], w_router[i], router_bias[i])
    counts = cnt[0, :N_EXPERTS].astype(jnp.int32)
    n_blocks = -(-(s * TOP_K) // MOE_BLOCK) + N_EXPERTS
    pad_start, pad_end, nblk, blk0, nxt, slot, first, n_used = _expert_tables(counts)
    tm_dispatch = math.gcd(s, 1024)
    tm_combine = math.gcd(s, 256)
    e_ids = jnp.arange(N_EXPERTS, dtype=jnp.int32)
    starts = jnp.sum(jnp.where(idx[:, :TOP_K, None] == e_ids, pad_start, 0), axis=-1)
    dest = (starts + rank[:, :TOP_K]).astype(jnp.int32)
    tiles = lambda tm: dest.reshape(s // tm, tm * TOP_K)
    xs = _dispatch(pad_start, pad_end, n_used, tiles(tm_dispatch), h2p, n_blocks)
    y_sorted = _moe(nblk, blk0, nxt, slot, first, n_used, xs,
                    w_exp_gate[i], w_exp_up[i], w_exp_down[i], n_blocks)
    out = _combine(tiles(tm_combine), y_sorted, wsel, h2,
                   w_sh_gate[i].astype(BF16), w_sh_up[i].astype(BF16), w_sh_down[i].astype(BF16),
                   x1, mod[0, 5:6, :])
    return out[None]
```

```python
import functools
import math

import jax
import jax.numpy as jnp
from jax import lax
from jax.experimental import pallas as pl
from jax.experimental.pallas import tpu as pltpu

F32 = jnp.float32
BF16 = jnp.bfloat16

D_MODEL = 2048
GRID_W = 64
HEAD_DIM = 128
ROPE_PAIRS = HEAD_DIM // 4
ROPE_THETA = 10000.0
A_HEADS = 8
A_KV_HEADS = 2
A_GROUP = A_HEADS // A_KV_HEADS
B_HEADS = 4
B_V_DIM = 2 * HEAD_DIM
N_EXPERTS = 64
TOP_K = 8
N_GROUPS = 8
TOPK_GROUPS = 4
EXPERT_DIM = 512
SHARED_DIM = 512
ROUTED_SCALE = 2.5
N_MOD = 6
EPS = 1e-6

A_Q_W = A_HEADS * HEAD_DIM
A_KV_W = A_KV_HEADS * HEAD_DIM
B_QK_W = B_HEADS * 2 * HEAD_DIM
B_V_W = B_HEADS * B_V_DIM
IN_W = A_Q_W + 2 * A_KV_W + 2 * B_QK_W + B_V_W + 2 * D_MODEL

COL_AQ = 0
COL_AK = A_Q_W // HEAD_DIM
COL_AV = COL_AK + A_KV_HEADS
COL_BQ = COL_AV + A_KV_HEADS
COL_BK = COL_BQ + 2 * B_HEADS
COL_BV = COL_BK + 2 * B_HEADS
COL_GA = COL_BV + B_V_W // HEAD_DIM
COL_GB = COL_GA + D_MODEL // HEAD_DIM

LANES = 128
SUBLANES = 8
VMEM_LIMIT = 56 * 1024 * 1024

PROJ_TN = 512
MOE_BLOCK = 256
ONES_ROWS = 16
PACK_TILES = D_MODEL // 2 // LANES
YBUF_PITCH = PACK_TILES + 4
TOKEN_PITCH = TOP_K * YBUF_PITCH + 4
LOG2E = 1.4426950408889634


def _cparams(sem, vmem=VMEM_LIMIT):
    return pltpu.CompilerParams(dimension_semantics=sem, vmem_limit_bytes=vmem)


def _adaln_kernel(cb_ref, w_ref, b_ref, o_ref):
    tn = w_ref.shape[1]
    nl = tn // LANES
    rows = 32

    def body(g, accs):
        accs = list(accs)
        r0 = pl.multiple_of(g * rows, rows)
        for u in range(rows // SUBLANES):
            r = r0 + u * SUBLANES
            w = w_ref[pl.ds(r, SUBLANES), :]
            for v in range(2):
                c = cb_ref[v, pl.ds(r, SUBLANES), :]
                s = c * jax.nn.sigmoid(c)
                for j in range(nl):
                    accs[v * nl + j] = accs[v * nl + j] + w[:, j * LANES:(j + 1) * LANES] * s
        return tuple(accs)

    init = tuple(jnp.zeros((SUBLANES, LANES), F32) for _ in range(2 * nl))
    accs = lax.fori_loop(0, w_ref.shape[0] // rows, body, init)
    for v in range(2):
        row = jnp.concatenate(
            [jnp.sum(accs[v * nl + j], axis=0, keepdims=True) for j in range(nl)], axis=1)
        o_ref[v:v + 1, :] = row + b_ref[...]


def _adaln(cvecs, w, b):
    d, n = w.shape
    tn = 1536
    cb = jnp.broadcast_to(cvecs[:, :, None], (2, d, LANES))
    return pl.pallas_call(
        _adaln_kernel,
        grid=(n // tn,),
        in_specs=[pl.BlockSpec((2, d, LANES), lambda j: (0, 0, 0)),
                  pl.BlockSpec((d, tn), lambda j: (0, j)),
                  pl.BlockSpec((1, tn), lambda j: (0, j))],
        out_specs=pl.BlockSpec((2, tn), lambda j: (0, j)),
        out_shape=jax.ShapeDtypeStruct((2, n), F32),
        compiler_params=_cparams(("arbitrary",)),
        name="adaln",
    )(cb, w, b.reshape(1, n))


def _rms_mod(x, g, shift, scale):
    y = x * lax.rsqrt(jnp.mean(x * x, axis=-1, keepdims=True) + EPS) * g
    return y * (1.0 + scale) + shift


def _prenorm_kernel(x_ref, c_ref, g_ref, mod_ref, o_ref, *, n_lat_tiles):
    is_ctx = pl.program_id(0) >= n_lat_tiles
    x = jnp.where(is_ctx, c_ref[...], x_ref[...])
    o_ref[...] = _rms_mod(x, g_ref[...], mod_ref[0, 0:1, :], mod_ref[0, 1:2, :]).astype(o_ref.dtype)


def _prenorm(x, ctx, g, mod):
    s, d = x.shape
    c = ctx.shape[0]
    tm = 256
    nl, nc = s // tm, c // tm
    return pl.pallas_call(
        functools.partial(_prenorm_kernel, n_lat_tiles=nl),
        grid=(nl + nc,),
        in_specs=[pl.BlockSpec((tm, d), lambda i: (jnp.minimum(i, nl - 1), 0)),
                  pl.BlockSpec((tm, d), lambda i: (jnp.maximum(i - nl, 0), 0)),
                  pl.BlockSpec((1, d), lambda i: (0, 0)),
                  pl.BlockSpec((1, N_MOD, d), lambda i: (i // nl, 0, 0))],
        out_specs=pl.BlockSpec((tm, d), lambda i: (i, 0)),
        out_shape=jax.ShapeDtypeStruct((s + c, d), BF16),
        compiler_params=_cparams(("arbitrary",)),
        name="prenorm_mix",
    )(x, ctx, g.reshape(1, d), mod)


def _inproj_kernel(h_ref, w_ref, gain_ref, c_ref, sa_ref, sb_ref, o_ref):
    j = pl.program_id(1)
    tm = h_ref.shape[0]
    nh = o_ref.shape[1] // HEAD_DIM
    n_chunks = next(n for n in (6, 3, 2, 1) if tm % (16 * n) == 0)

    def tile(n_normed, chunks):
        w = w_ref[...].astype(h_ref.dtype)
        rows = tm // chunks
        for r in range(chunks):
            rs = slice(r * rows, (r + 1) * rows)
            acc = jnp.dot(h_ref[rs, :], w, preferred_element_type=F32)
            for hd in range(nh):
                sl = slice(hd * HEAD_DIM, (hd + 1) * HEAD_DIM)
                a = acc[:, sl]
                if hd < n_normed:
                    y = a * lax.rsqrt(jnp.mean(a * a, axis=-1, keepdims=True) + EPS) * gain_ref[0, :, sl]
                    a = (y * c_ref[rs, :] + pltpu.roll(y, ROPE_PAIRS, 1) * sa_ref[rs, :]
                         + pltpu.roll(y, HEAD_DIM - ROPE_PAIRS, 1) * sb_ref[rs, :])
                o_ref[rs, sl] = a.astype(o_ref.dtype)

    all_normed = (j < 2) | ((j >= 3) & (j < 7))
    pl.when(all_normed)(lambda: tile(nh, n_chunks))
    pl.when(j == 2)(lambda: tile(A_KV_HEADS, n_chunks))
    pl.when(j >= 7)(lambda: tile(0, n_chunks))


def _inproj(h, w, gains, rope_c, rope_sa, rope_sb):
    t, d = h.shape
    n = w.shape[1]
    tm = t // 4
    tn = PROJ_TN
    return pl.pallas_call(
        _inproj_kernel,
        grid=(t // tm, n // tn),
        in_specs=[pl.BlockSpec((tm, d), lambda i, j: (i, 0)),
                  pl.BlockSpec((d, tn), lambda i, j: (0, j)),
                  pl.BlockSpec((1, 1, tn), lambda i, j: (j, 0, 0)),
                  pl.BlockSpec((tm, HEAD_DIM), lambda i, j: (i, 0)),
                  pl.BlockSpec((tm, HEAD_DIM), lambda i, j: (i, 0)),
                  pl.BlockSpec((tm, HEAD_DIM), lambda i, j: (i, 0))],
        out_specs=pl.BlockSpec((tm, tn), lambda i, j: (i, j)),
        out_shape=jax.ShapeDtypeStruct((t, n), BF16),
        compiler_params=_cparams(("arbitrary", "arbitrary")),
        name="inproj",
    )(h, w, gains, rope_c, rope_sa, rope_sb)


def _rope_tables(s, c):
    rows_n = s // GRID_W
    inv = ROPE_THETA ** (-jnp.arange(ROPE_PAIRS, dtype=F32) / ROPE_PAIRS)
    ang_r = jnp.arange(rows_n, dtype=F32)[:, None] * inv
    ang_c = jnp.arange(GRID_W, dtype=F32)[:, None] * inv
    cr, sr, cc, sc = jnp.cos(ang_r), jnp.sin(ang_r), jnp.cos(ang_c), jnp.sin(ang_c)
    zr, zc = jnp.zeros_like(sr), jnp.zeros_like(sc)

    def table(row_parts, col_parts, ctx_value):
        by_row = jnp.concatenate(row_parts + [zr, zr], axis=1)
        by_col = jnp.concatenate([zc, zc] + col_parts, axis=1)
        lat = (by_row[:, None, :] + by_col[None, :, :]).reshape(s, HEAD_DIM)
        return jnp.concatenate([lat, jnp.full((c, HEAD_DIM), ctx_value, F32)], axis=0)

    return (table([cr, cr], [cc, cc], 1.0), table([zr, sr], [zc, sc], 0.0), table([-sr, zr], [-sc, zc], 0.0))


def _head_gains(qn_a, kn_a, qn_b, kn_b):
    qs = HEAD_DIM ** -0.5 * LOG2E
    one = jnp.ones((HEAD_DIM,), F32)
    heads = ([qn_a * qs] * A_HEADS + [kn_a] * A_KV_HEADS + [one] * A_KV_HEADS
             + [qn_b * qs] * (2 * B_HEADS) + [kn_b] * (2 * B_HEADS))
    heads = heads + [one] * (IN_W // HEAD_DIM - len(heads))
    return jnp.concatenate(heads).reshape(IN_W // PROJ_TN, 1, PROJ_TN)


def _build_vt(v_ref, vt_ref, tk):
    n_chunks, rows, _ = vt_ref.shape
    dv = rows - ONES_ROWS
    tail = (lax.broadcasted_iota(jnp.int32, (ONES_ROWS, tk), 0) == 0).astype(vt_ref.dtype)
    for c in range(n_chunks):
        vt_ref[c, 0:dv, :] = v_ref[c * tk:(c + 1) * tk, :].astype(F32).T.astype(vt_ref.dtype)
        vt_ref[c, dv:rows, :] = tail


def _attend(qs, k_refs, vt_ref, s_ref, m_ref, acc_ref):
    n_chunks, _, tk = vt_ref.shape
    n_st = len(qs)
    m_ref[...] = jnp.full(m_ref.shape, -jnp.inf, F32)
    acc_ref[...] = jnp.zeros(acc_ref.shape, F32)

    def scores(i, c, slot):
        off = c * tk if isinstance(c, int) else pl.multiple_of(c * tk, tk)
        s_ref[i, slot] = lax.dot_general(k_refs[i][pl.ds(off, tk), :], qs[i], (((1,), (1,)), ((), ())),
                                         preferred_element_type=F32)

    def update(i, c, slot):
        s = s_ref[i, slot]
        m_old = m_ref[i]
        m_new = jnp.maximum(m_old, jnp.max(s, axis=0, keepdims=True))
        p = jnp.exp2(s - m_new).astype(vt_ref.dtype)
        acc_ref[i] = (acc_ref[i] * jnp.exp2(m_old - m_new)
                      + jnp.dot(vt_ref[c], p, preferred_element_type=F32))
        m_ref[i] = m_new

    for i in range(n_st):
        scores(i, 0, 0)

    def pair(j, carry):
        c = 2 * j
        for i in range(n_st):
            scores(i, c + 1, 1)
        for i in range(n_st):
            update(i, c, 0)
        for i in range(n_st):
            scores(i, c + 2, 0)
        for i in range(n_st):
            update(i, c + 1, 1)
        return carry

    n_pairs = (n_chunks - 1) // 2
    lax.fori_loop(0, n_pairs, pair, 0)
    done = 2 * n_pairs
    if n_chunks - done == 2:
        for i in range(n_st):
            scores(i, done + 1, 1)
    for i in range(n_st):
        update(i, done, 0)
    if n_chunks - done == 2:
        for i in range(n_st):
            update(i, done + 1, 1)


def _gqa_kernel(q_ref, k_ref, v_ref, o_ref, vt_ref, s_ref, m_ref, acc_ref):
    tk = vt_ref.shape[2]
    n_st = s_ref.shape[0]
    pl.when((pl.program_id(1) == 0) & (pl.program_id(2) == 0))(lambda: _build_vt(v_ref, vt_ref, tk))
    qs = [q_ref[:, i * HEAD_DIM:(i + 1) * HEAD_DIM] for i in range(n_st)]
    _attend(qs, [k_ref] * n_st, vt_ref, s_ref, m_ref, acc_ref)
    for i in range(n_st):
        o_t = acc_ref[i, 0:HEAD_DIM, :] / acc_ref[i, HEAD_DIM:HEAD_DIM + 1, :]
        o_ref[:, i * HEAD_DIM:(i + 1) * HEAD_DIM] = o_t.T.astype(o_ref.dtype)


def _key_chunk(t):
    for tk in (1408, 768, 1024, 512, 640, 384, 256, 128):
        if t % tk == 0:
            return tk
    raise ValueError(f"unsupported key count {t}")


def _gqa(proj, s):
    t = proj.shape[0]
    tq = 512
    tk = _key_chunk(t)
    n_st = 4
    per_g = A_GROUP // n_st
    return pl.pallas_call(
        _gqa_kernel,
        grid=(A_KV_HEADS, per_g, s // tq),
        in_specs=[pl.BlockSpec((tq, n_st * HEAD_DIM), lambda g, hh, i: (i, COL_AQ // n_st + g * per_g + hh)),
                  pl.BlockSpec((t, HEAD_DIM), lambda g, hh, i: (0, COL_AK + g)),
                  pl.BlockSpec((t, HEAD_DIM), lambda g, hh, i: (0, COL_AV + g))],
        out_specs=pl.BlockSpec((tq, n_st * HEAD_DIM), lambda g, hh, i: (i, g * per_g + hh)),
        out_shape=jax.ShapeDtypeStruct((s, A_Q_W), BF16),
        scratch_shapes=[pltpu.VMEM((t // tk, HEAD_DIM + ONES_ROWS, tk), BF16),
                        pltpu.VMEM((n_st, 2, tk, tq), F32),
                        pltpu.VMEM((n_st, 1, tq), F32),
                        pltpu.VMEM((n_st, HEAD_DIM + ONES_ROWS, tq), F32)],
        compiler_params=_cparams(("arbitrary", "arbitrary", "arbitrary")),
        name="gqa_attn",
    )(proj, proj, proj)


def _diff_kernel(lam_ref, q0_ref, q1_ref, k0_ref, k1_ref, v_ref, g_ref, o_ref,
                 vt_ref, s_ref, m_ref, acc_ref, *, lam_init):
    tk = vt_ref.shape[2]
    pl.when(pl.program_id(1) == 0)(lambda: _build_vt(v_ref, vt_ref, tk))
    lv = lam_ref[...]
    lam = (jnp.exp(jnp.sum(lv[0:1, :] * lv[1:2, :], axis=-1, keepdims=True))
           - jnp.exp(jnp.sum(lv[2:3, :] * lv[3:4, :], axis=-1, keepdims=True)) + lam_init)
    _attend([q0_ref[...], q1_ref[...]], [k0_ref, k1_ref], vt_ref, s_ref, m_ref, acc_ref)
    o_t = (acc_ref[0, 0:B_V_DIM, :] / acc_ref[0, B_V_DIM:B_V_DIM + 1, :]
           - lam * (acc_ref[1, 0:B_V_DIM, :] / acc_ref[1, B_V_DIM:B_V_DIM + 1, :]))
    o = o_t.T
    y = o * lax.rsqrt(jnp.mean(o * o, axis=-1, keepdims=True) + EPS) * g_ref[...]
    o_ref[...] = (y * (1.0 - lam_init)).astype(o_ref.dtype)


def _diff(proj, s, lam_vecs, subln_g, lam_init):
    t = proj.shape[0]
    tq = 512
    tk = _key_chunk(t)
    vb = B_V_DIM // HEAD_DIM
    return pl.pallas_call(
        functools.partial(_diff_kernel, lam_init=lam_init),
        grid=(B_HEADS, s // tq),
        in_specs=[pl.BlockSpec((4, HEAD_DIM), lambda h, i: (0, 0)),
                  pl.BlockSpec((tq, HEAD_DIM), lambda h, i: (i, COL_BQ + 2 * h)),
                  pl.BlockSpec((tq, HEAD_DIM), lambda h, i: (i, COL_BQ + 2 * h + 1)),
                  pl.BlockSpec((t, HEAD_DIM), lambda h, i: (0, COL_BK + 2 * h)),
                  pl.BlockSpec((t, HEAD_DIM), lambda h, i: (0, COL_BK + 2 * h + 1)),
                  pl.BlockSpec((t, B_V_DIM), lambda h, i: (0, COL_BV // vb + h)),
                  pl.BlockSpec((1, B_V_DIM), lambda h, i: (0, 0))],
        out_specs=pl.BlockSpec((tq, B_V_DIM), lambda h, i: (i, h)),
        out_shape=jax.ShapeDtypeStruct((s, B_V_W), BF16),
        scratch_shapes=[pltpu.VMEM((t // tk, B_V_DIM + ONES_ROWS, tk), BF16),
                        pltpu.VMEM((2, 2, tk, tq), F32),
                        pltpu.VMEM((2, 1, tq), F32),
                        pltpu.VMEM((2, B_V_DIM + ONES_ROWS, tq), F32)],
        compiler_params=_cparams(("arbitrary", "arbitrary")),
        name="diff_attn",
    )(lam_vecs, proj, proj, proj, proj, proj, subln_g.reshape(1, B_V_DIM))


def _merge_kernel(oa_ref, ob_ref, *refs):
    nc = (len(refs) - 6) // 2
    ga_refs, gb_refs = refs[:nc], refs[nc:2 * nc]
    wa_ref, wb_ref, wo_ref, x_ref, gate_ref, o_ref = refs[2 * nc:]
    tn = ga_refs[0].shape[1]
    ya = jnp.dot(oa_ref[...], wa_ref[...], preferred_element_type=F32)
    yb = jnp.dot(ob_ref[...], wb_ref[...], preferred_element_type=F32)
    parts = []
    for c in range(nc):
        sl = slice(c * tn, (c + 1) * tn)
        t = (jax.nn.sigmoid(ga_refs[c][...].astype(F32)) * ya[:, sl]
             + jax.nn.sigmoid(gb_refs[c][...].astype(F32)) * yb[:, sl])
        parts.append(t.astype(wo_ref.dtype))
    y = jnp.dot(jnp.concatenate(parts, axis=1), wo_ref[...], preferred_element_type=F32)
    o_ref[...] = x_ref[...] + gate_ref[...] * y


def _merge(oa, ob, proj, wa, wb, wo, x, gate):
    s, d = x.shape
    tm = 256
    tn = PROJ_TN
    nc = d // tn
    ga0 = COL_GA * HEAD_DIM // tn
    gb0 = COL_GB * HEAD_DIM // tn
    resident = lambda shape: pl.BlockSpec(shape, lambda i: (0, 0), pipeline_mode=pl.Buffered(1))
    gate_specs = [pl.BlockSpec((tm, tn), lambda i, c=c0 + c: (i, c)) for c0 in (ga0, gb0) for c in range(nc)]
    return pl.pallas_call(
        _merge_kernel,
        grid=(s // tm,),
        in_specs=[pl.BlockSpec((tm, A_Q_W), lambda i: (i, 0)),
                  pl.BlockSpec((tm, B_V_W), lambda i: (i, 0)),
                  *gate_specs,
                  resident((A_Q_W, d)), resident((B_V_W, d)), resident((d, d)),
                  pl.BlockSpec((tm, d), lambda i: (i, 0)),
                  pl.BlockSpec((1, d), lambda i: (0, 0))],
        out_specs=pl.BlockSpec((tm, d), lambda i: (i, 0)),
        out_shape=jax.ShapeDtypeStruct((s, d), F32),
        compiler_params=_cparams(("arbitrary",)),
        name="merge_out",
    )(oa, ob, *([proj] * (2 * nc)), wa, wb, wo, x, gate)


def _store_packed_tiles(ref, x):
    rows = x.shape[0]
    bits = lax.bitcast_convert_type(x.astype(BF16).astype(F32), jnp.uint32)
    half = bits.shape[1] // 2
    words = (bits[:, :half] >> 16) | (bits[:, half:] & jnp.uint32(0xFFFF0000))
    for c in range(PACK_TILES):
        ref[pl.ds(c, rows, stride=PACK_TILES), :] = words[:, c * LANES:(c + 1) * LANES]


def _lane_max(x):
    return jnp.max(x, axis=-1, keepdims=True)


def _lane_min(x):
    return jnp.min(x, axis=-1, keepdims=True)


def _group_allreduce(x, lane, op):
    for sft in (1, 2, 4):
        up = pltpu.roll(x, sft, 1)
        dn = pltpu.roll(x, LANES - sft, 1)
        x = op(x, jnp.where((lane & sft) != 0, up, dn))
    return x


def _router_kernel(x_ref, g_ref, mod_ref, wr_ref, rb_ref,
                   h_ref, hp_ref, idx_ref, wsel_ref, rank_ref, cnt_ref, carry_ref):
    @pl.when(pl.program_id(0) == 0)
    def _():
        carry_ref[...] = jnp.zeros_like(carry_ref)

    h = _rms_mod(x_ref[...], g_ref[...], mod_ref[0, 3:4, :], mod_ref[0, 4:5, :])
    hb = h.astype(BF16)
    h_ref[...] = hb
    _store_packed_tiles(hp_ref, h)
    tm = h.shape[0]

    h_lo = (h - hb.astype(F32)).astype(BF16)
    logits = (jnp.dot(hb, wr_ref[0], preferred_element_type=F32)
              + jnp.dot(hb, wr_ref[1], preferred_element_type=F32)
              + jnp.dot(h_lo, wr_ref[0], preferred_element_type=F32))
    scores = jax.nn.sigmoid(logits)
    lane = lax.broadcasted_iota(jnp.int32, (tm, LANES), 1)
    lane_f = lane.astype(F32)
    gid_f = (lane >> 3).astype(F32)
    valid = lane < N_EXPERTS
    neg = jnp.float32(-jnp.inf)
    big = jnp.float32(LANES)
    biased = jnp.where(valid, scores + rb_ref[...], neg)

    m1 = _group_allreduce(biased, lane, jnp.maximum)
    a1 = _group_allreduce(jnp.where(biased == m1, lane_f, big), lane, jnp.minimum)
    m2 = _group_allreduce(jnp.where(lane_f == a1, neg, biased), lane, jnp.maximum)
    gscore = jnp.where(valid, m1 + m2, neg)
    keep = jnp.zeros((tm, LANES), jnp.bool_)
    for _ in range(TOPK_GROUPS):
        best = _lane_max(gscore)
        gsel = _lane_min(jnp.where(gscore == best, gid_f, big))
        hit = gid_f == gsel
        keep = keep | hit
        gscore = jnp.where(hit, neg, gscore)
    cand = jnp.where(keep & valid, biased, neg)

    onehot = jnp.zeros((tm, LANES), F32)
    idx_out = jnp.zeros((tm, LANES), F32)
    w_out = jnp.zeros((tm, LANES), F32)
    sels = []
    for k in range(TOP_K):
        best = _lane_max(cand)
        sel = _lane_min(jnp.where(cand == best, lane_f, big))
        hit = lane_f == sel
        wk = jnp.sum(jnp.where(hit, scores, 0.0), axis=-1, keepdims=True)
        cand = jnp.where(hit, neg, cand)
        onehot = jnp.where(hit, 1.0, onehot)
        idx_out = jnp.where(lane == k, sel, idx_out)
        w_out = jnp.where(lane == k, wk, w_out)
        sels.append(hit)
    wsum = jnp.sum(w_out, axis=-1, keepdims=True)
    wsel_ref[...] = w_out / wsum * ROUTED_SCALE
    idx_ref[...] = idx_out.astype(jnp.int32)

    row = lax.broadcasted_iota(jnp.int32, (tm, tm), 0)
    col = lax.broadcasted_iota(jnp.int32, (tm, tm), 1)
    lower = (col < row).astype(BF16)
    before = jnp.dot(lower, onehot.astype(BF16), preferred_element_type=F32) + carry_ref[...]
    rank_out = jnp.zeros((tm, LANES), F32)
    for k in range(TOP_K):
        rk = jnp.sum(jnp.where(sels[k], before, 0.0), axis=-1, keepdims=True)
        rank_out = jnp.where(lane == k, rk, rank_out)
    rank_ref[...] = rank_out.astype(jnp.int32)
    carry_ref[...] = carry_ref[...] + jnp.sum(onehot, axis=0, keepdims=True)
    cnt_ref[...] = carry_ref[...]


def _router(x1, g, mod, w_router, router_bias):
    s, d = x1.shape
    tm = 256
    e = w_router.shape[1]
    wr = jnp.pad(w_router, ((0, 0), (0, LANES - e)))
    wr_hi = wr.astype(BF16)
    wr = jnp.stack([wr_hi, (wr - wr_hi.astype(F32)).astype(BF16)])
    rb = jnp.pad(router_bias, (0, LANES - e)).reshape(1, LANES)
    row_spec = lambda w: pl.BlockSpec((tm, w), lambda i: (i, 0))
    return pl.pallas_call(
        _router_kernel,
        grid=(s // tm,),
        in_specs=[row_spec(d),
                  pl.BlockSpec((1, d), lambda i: (0, 0)),
                  pl.BlockSpec((1, N_MOD, d), lambda i: (0, 0, 0)),
                  pl.BlockSpec((2, d, LANES), lambda i: (0, 0, 0)),
                  pl.BlockSpec((1, LANES), lambda i: (0, 0))],
        out_specs=[row_spec(d), pl.BlockSpec((tm * PACK_TILES, LANES), lambda i: (i, 0)),
                   row_spec(LANES), row_spec(LANES), row_spec(LANES),
                   pl.BlockSpec((1, LANES), lambda i: (0, 0))],
        out_shape=[jax.ShapeDtypeStruct((s, d), BF16),
                   jax.ShapeDtypeStruct((s * PACK_TILES, LANES), jnp.uint32),
                   jax.ShapeDtypeStruct((s, LANES), jnp.int32),
                   jax.ShapeDtypeStruct((s, LANES), F32),
                   jax.ShapeDtypeStruct((s, LANES), jnp.int32),
                   jax.ShapeDtypeStruct((1, LANES), F32)],
        scratch_shapes=[pltpu.VMEM((1, LANES), F32)],
        compiler_params=_cparams(("arbitrary",)),
        name="ffn_router",
    )(x1, g.reshape(1, d), mod, wr, rb)


def _unpack_halves(words):
    lo = lax.bitcast_convert_type(words << 16, F32)
    hi = lax.bitcast_convert_type(words & jnp.uint32(0xFFFF0000), F32)
    return lo, hi


def _unpack_rows(words):
    lo, hi = _unpack_halves(words)
    return jnp.concatenate([lo.astype(BF16), hi.astype(BF16)], axis=1)


def _token_rows(row, tiles):
    return pl.ds(pl.multiple_of(row * tiles, tiles), tiles)


def _dispatch_kernel(ps_ref, pe_ref, nused_ref, dest_hbm, hp_ref, xs_hbm,
                     dest_smem, zbuf, sem_i, sem_z, sem_x, *, n_blocks):
    i = pl.program_id(0)
    n = dest_smem.shape[0]
    tm = n // TOP_K
    copy_dest = pltpu.make_async_copy(dest_hbm.at[i], dest_smem, sem_i)
    copy_dest.start()

    @pl.when(i == 0)
    def _():
        zbuf[...] = jnp.zeros_like(zbuf)

        def zero_block(row0):
            span = MOE_BLOCK * PACK_TILES
            return pltpu.make_async_copy(
                zbuf, xs_hbm.at[pl.ds(pl.multiple_of(row0 * PACK_TILES, span), span), :], sem_z)

        def per_expert(action):
            def body(e, carry):
                @pl.when(pe_ref[e] > ps_ref[e])
                def _():
                    action(zero_block(pe_ref[e] - MOE_BLOCK))
                return carry
            lax.fori_loop(0, N_EXPERTS, body, 0)

        def per_tail(action):
            def body(b, carry):
                action(zero_block(b * MOE_BLOCK))
                return carry
            lax.fori_loop(nused_ref[0], n_blocks, body, 0)

        per_expert(lambda cp: cp.start())
        per_tail(lambda cp: cp.start())
        per_expert(lambda cp: cp.wait())
        per_tail(lambda cp: cp.wait())

    copy_dest.wait()

    def issue(t, carry):
        for k in range(TOP_K):
            dest = dest_smem[t * TOP_K + k]
            pltpu.make_async_copy(hp_ref.at[_token_rows(t, PACK_TILES), :],
                                  xs_hbm.at[_token_rows(dest, PACK_TILES), :], sem_x).start(priority=k % 2)
        return carry

    lax.fori_loop(0, tm, issue, 0)
    for _ in range(TOP_K):
        pltpu.make_async_copy(hp_ref, xs_hbm.at[pl.ds(0, tm * PACK_TILES), :], sem_x).wait()


def _dispatch(pad_start, pad_end, n_used, dest_tm, hp, n_blocks):
    n_tiles, n = dest_tm.shape
    tm = n // TOP_K
    grid_spec = pltpu.PrefetchScalarGridSpec(
        num_scalar_prefetch=3,
        grid=(n_tiles,),
        in_specs=[pl.BlockSpec(memory_space=pl.ANY),
                  pl.BlockSpec((tm * PACK_TILES, LANES), lambda i, ps, pe, nu: (i, 0))],
        out_specs=pl.BlockSpec(memory_space=pl.ANY),
        scratch_shapes=[pltpu.SMEM((n,), jnp.int32),
                        pltpu.VMEM((MOE_BLOCK * PACK_TILES, LANES), jnp.uint32),
                        pltpu.SemaphoreType.DMA,
                        pltpu.SemaphoreType.DMA,
                        pltpu.SemaphoreType.DMA])
    return pl.pallas_call(
        functools.partial(_dispatch_kernel, n_blocks=n_blocks),
        grid_spec=grid_spec,
        out_shape=jax.ShapeDtypeStruct((n_blocks * MOE_BLOCK * PACK_TILES, LANES), jnp.uint32),
        compiler_params=_cparams(("arbitrary",)),
        name="moe_dispatch",
    )(pad_start, pad_end, n_used, dest_tm, hp)


def _moe_kernel(nblk_ref, blk0_ref, nxt_ref, slot_ref, first_ref, nused_ref,
                xs_hbm, wg_hbm, wu_hbm, wd_hbm, y_hbm,
                xbuf, ybuf, wg32, wu32, wd32, wgb, wub, wdb, sem_w, sem_x, sem_y, *, n_blocks):
    e = pl.program_id(0)
    nb = nblk_ref[e]
    span = MOE_BLOCK * PACK_TILES

    def block_rows(b):
        return pl.ds(pl.multiple_of(b * span, span), span)

    def fetch(ex, slot):
        return (pltpu.make_async_copy(wg_hbm.at[ex], wg32.at[slot], sem_w.at[slot, 0]),
                pltpu.make_async_copy(wu_hbm.at[ex], wu32.at[slot], sem_w.at[slot, 1]),
                pltpu.make_async_copy(wd_hbm.at[ex], wd32.at[slot], sem_w.at[slot, 2]))

    def x_copy(b, s):
        return pltpu.make_async_copy(xs_hbm.at[block_rows(b), :], xbuf.at[s], sem_x.at[s])

    def y_copy(b, s):
        return pltpu.make_async_copy(ybuf.at[s], y_hbm.at[block_rows(b), :], sem_y.at[s])

    n_used = nused_ref[0]
    n_xbuf = xbuf.shape[0]

    @pl.when(nb > 0)
    def _():
        b0 = blk0_ref[e]
        slot = slot_ref[e]

        @pl.when(e == first_ref[0])
        def _():
            for b in range(n_xbuf - 1):
                pl.when(b < n_used)(lambda b=b: x_copy(b, b).start())
            for cp in fetch(e, 0):
                cp.start(priority=1)

        for cp in fetch(e, slot):
            cp.wait()

        @pl.when(nxt_ref[e] >= 0)
        def _():
            for cp in fetch(nxt_ref[e], 1 - slot):
                cp.start(priority=1)

        wgb[...] = wg32[slot].astype(BF16)
        wub[...] = wu32[slot].astype(BF16)
        wdb[...] = wd32[slot].astype(BF16)

        def body(j, carry):
            b = b0 + j
            s = b % 2
            sx = b % n_xbuf
            ahead = b + n_xbuf - 1
            x_copy(b, sx).wait()
            pl.when(ahead < n_used)(lambda: x_copy(ahead, ahead % n_xbuf).start())
            pl.when(b >= 2)(lambda: y_copy(b - 2, s).wait())
            xb = xbuf.at[sx]
            words = jnp.concatenate(
                [xb[pl.ds(c, MOE_BLOCK, stride=PACK_TILES), :] for c in range(PACK_TILES)], axis=1)
            x = _unpack_rows(words)
            gate = jnp.dot(x, wgb[...], preferred_element_type=F32)
            up = jnp.dot(x, wub[...], preferred_element_type=F32)
            act = (gate * jax.nn.sigmoid(gate) * up).astype(BF16)
            _store_packed_tiles(ybuf.at[s], jnp.dot(act, wdb[...], preferred_element_type=F32))
            y_copy(b, s).start()
            return carry

        lax.fori_loop(0, nb, body, 0)

    @pl.when(e == pl.num_programs(0) - 1)
    def _():
        pl.when(n_used >= 2)(lambda: y_copy(n_used - 2, n_used % 2).wait())
        pl.when(n_used >= 1)(lambda: y_copy(n_used - 1, (n_used - 1) % 2).wait())
        ybuf[0] = jnp.zeros(ybuf.shape[1:], ybuf.dtype)

        def tail(action):
            def step(b, carry):
                action(y_copy(b, 0))
                return carry
            lax.fori_loop(nused_ref[0], n_blocks, step, 0)

        tail(lambda cp: cp.start())
        tail(lambda cp: cp.wait())


def _moe(nblk, blk0, nxt, slot, first, n_used, xs, weg, weu, wed, n_blocks):
    n_exp, d, f = weg.shape
    span = MOE_BLOCK * PACK_TILES
    grid_spec = pltpu.PrefetchScalarGridSpec(
        num_scalar_prefetch=6,
        grid=(n_exp,),
        in_specs=[pl.BlockSpec(memory_space=pl.ANY)] * 4,
        out_specs=pl.BlockSpec(memory_space=pl.ANY),
        scratch_shapes=[pltpu.VMEM((3, span, LANES), jnp.uint32), pltpu.VMEM((2, span, LANES), jnp.uint32),
                        pltpu.VMEM((2, d, f), F32), pltpu.VMEM((2, d, f), F32), pltpu.VMEM((2, f, d), F32),
                        pltpu.VMEM((d, f), BF16), pltpu.VMEM((d, f), BF16), pltpu.VMEM((f, d), BF16),
                        pltpu.SemaphoreType.DMA((2, 3)), pltpu.SemaphoreType.DMA((3,)),
                        pltpu.SemaphoreType.DMA((2,))])
    return pl.pallas_call(
        functools.partial(_moe_kernel, n_blocks=n_blocks),
        grid_spec=grid_spec,
        out_shape=jax.ShapeDtypeStruct((n_blocks * span, LANES), jnp.uint32),
        compiler_params=_cparams(("arbitrary",)),
        name="moe_experts",
    )(nblk, blk0, nxt, slot, first, n_used, xs, weg, weu, wed)


def _combine_kernel(dest_hbm, y_hbm, w_ref, h_ref, wg_ref, wu_ref, wd_ref, x_ref, gate_ref, o_ref,
                    dest0, dest1, ybuf0, ybuf1, sem_i, sem_y):
    i = pl.program_id(0)
    tm = h_ref.shape[0]
    n = tm * TOP_K
    span = n * PACK_TILES
    dests = (dest0, dest1)
    ybufs = (ybuf0, ybuf1)

    def start_tile(tile, slot):
        copy_dest = pltpu.make_async_copy(dest_hbm.at[tile], dests[slot], sem_i)
        copy_dest.start()
        copy_dest.wait()

        def issue(t, carry):
            base = t * TOKEN_PITCH
            for k in range(TOP_K):
                src = dests[slot][t * TOP_K + k]
                pltpu.make_async_copy(
                    y_hbm.at[_token_rows(src, PACK_TILES), :],
                    ybufs[slot].at[pl.ds(pl.multiple_of(base + k * YBUF_PITCH, 4), PACK_TILES), :],
                    sem_y.at[slot]).start(priority=k % 2)
            return carry

        lax.fori_loop(0, tm, issue, 0)

    def step(cur):
        nxt = 1 - cur
        pl.when(i == 0)(lambda: start_tile(0, cur))
        pl.when(i + 1 < pl.num_programs(0))(lambda: start_tile(i + 1, nxt))

        h = h_ref[...]
        gate = jnp.dot(h, wg_ref[...], preferred_element_type=F32)
        up = jnp.dot(h, wu_ref[...], preferred_element_type=F32)
        act = (gate * jax.nn.sigmoid(gate) * up).astype(BF16)
        shared = jnp.dot(act, wd_ref[...], preferred_element_type=F32)

        yb = ybufs[cur]
        pltpu.make_async_copy(y_hbm.at[pl.ds(0, span), :], yb.at[pl.ds(0, span), :], sem_y.at[cur]).wait()
        wks = [jnp.broadcast_to(w_ref[:, k:k + 1], (tm, LANES)) for k in range(TOP_K)]
        lows, highs = [], []
        for c in range(PACK_TILES):
            acc_lo = acc_hi = None
            for k in range(TOP_K):
                lo, hi = _unpack_halves(yb[pl.ds(k * YBUF_PITCH + c, tm, stride=TOKEN_PITCH), :])
                acc_lo = lo * wks[k] if acc_lo is None else acc_lo + lo * wks[k]
                acc_hi = hi * wks[k] if acc_hi is None else acc_hi + hi * wks[k]
            lows.append(acc_lo)
            highs.append(acc_hi)
        routed = jnp.concatenate(lows + highs, axis=1)
        o_ref[...] = x_ref[...] + gate_ref[...] * (routed + shared)

    pl.when(i % 2 == 0)(lambda: step(0))
    pl.when(i % 2 == 1)(lambda: step(1))


def _combine(dest_tm, y_sorted, wsel, h2, wsg, wsu, wsd, x1, gate):
    s, d = x1.shape
    n_tiles, n = dest_tm.shape
    tm = n // TOP_K
    f = wsg.shape[1]
    return pl.pallas_call(
        _combine_kernel,
        grid=(n_tiles,),
        in_specs=[pl.BlockSpec(memory_space=pl.ANY),
                  pl.BlockSpec(memory_space=pl.ANY),
                  pl.BlockSpec((tm, LANES), lambda i: (i, 0)),
                  pl.BlockSpec((tm, d), lambda i: (i, 0)),
                  pl.BlockSpec((d, f), lambda i: (0, 0)),
                  pl.BlockSpec((d, f), lambda i: (0, 0)),
                  pl.BlockSpec((f, d), lambda i: (0, 0)),
                  pl.BlockSpec((tm, d), lambda i: (i, 0)),
                  pl.BlockSpec((1, d), lambda i: (0, 0))],
        out_specs=pl.BlockSpec((tm, d), lambda i: (i, 0)),
        out_shape=jax.ShapeDtypeStruct((s, d), F32),
        scratch_shapes=[pltpu.SMEM((n,), jnp.int32),
                        pltpu.SMEM((n,), jnp.int32),
                        pltpu.VMEM((tm * TOKEN_PITCH, LANES), jnp.uint32),
                        pltpu.VMEM((tm * TOKEN_PITCH, LANES), jnp.uint32),
                        pltpu.SemaphoreType.DMA,
                        pltpu.SemaphoreType.DMA((2,))],
        compiler_params=_cparams(("arbitrary",)),
        name="moe_combine",
    )(dest_tm, y_sorted, wsel, h2, wsg, wsu, wsd, x1, gate)


def _expert_tables(counts):
    i32 = jnp.int32
    padded = ((counts + MOE_BLOCK - 1) // MOE_BLOCK * MOE_BLOCK).astype(i32)
    pad_end = jnp.cumsum(padded).astype(i32)
    pad_start = pad_end - padded
    ids = jnp.arange(N_EXPERTS, dtype=i32)
    busy = padded > 0
    later_busy = busy[None, :] & (ids[None, :] > ids[:, None])
    nxt = jnp.min(jnp.where(later_busy, ids[None, :], N_EXPERTS), axis=1)
    nxt = jnp.where(nxt < N_EXPERTS, nxt, -1).astype(i32)
    slot = (jnp.maximum(jnp.cumsum(busy.astype(i32)) - 1, 0) % 2).astype(i32)
    first = jnp.min(jnp.where(busy, ids, N_EXPERTS)).astype(i32).reshape(1)
    n_used = (pad_end[-1] // MOE_BLOCK).reshape(1)
    return pad_start, pad_end, padded // MOE_BLOCK, pad_start // MOE_BLOCK, nxt, slot, first, n_used


def kernel(x, c, ctx, c_ctx, w_ada, b_ada, norm_mix, norm_ffn, w_in, q_norm_a, k_norm_a, q_norm_b, k_norm_b, lambda_q1, lambda_k1, lambda_q2, lambda_k2, subln_b, w_branch_a, w_branch_b, w_out, w_router, router_bias, w_exp_gate, w_exp_up, w_exp_down, w_sh_gate, w_sh_up, w_sh_down):
    depth = w_ada.shape[0]
    assert depth == 1 and x.shape[0] == 1 and ctx.shape[0] == 1
    s, d = x.shape[1], x.shape[2]
    n_ctx = ctx.shape[1]
    i = 0
    lam_init = 0.8 - 0.6 * math.exp(-0.3 * i)
    xs = x[0]

    mod = _adaln(jnp.concatenate([c, c_ctx[None, :]], axis=0), w_ada[i], b_ada[i]).reshape(2, N_MOD, d)

    h = _prenorm(xs, ctx[0], norm_mix[i], mod)
    tc, tsa, tsb = _rope_tables(s, n_ctx)
    gains = _head_gains(q_norm_a[i], k_norm_a[i], q_norm_b[i], k_norm_b[i])
    proj = _inproj(h, w_in[i], gains, tc, tsa, tsb)
    oa = _gqa(proj, s)
    lam_vecs = jnp.stack([lambda_q1[i], lambda_k1[i], lambda_q2[i], lambda_k2[i]]).astype(F32)
    ob = _diff(proj, s, lam_vecs, subln_b[i], lam_init)
    x1 = _merge(oa, ob, proj, w_branch_a[i].astype(BF16), w_branch_b[i].astype(BF16),
                w_out[i].astype(BF16), xs, mod[0, 2:3, :])

    h2, h2p, idx, wsel, rank, cnt = _router(x1, norm_ffn[i], mod[0:1], w_router[i], router_bias[i])
    counts = cnt[0, :N_EXPERTS].astype(jnp.int32)
    n_blocks = -(-(s * TOP_K) // MOE_BLOCK) + N_EXPERTS
    pad_start, pad_end, nblk, blk0, nxt, slot, first, n_used = _expert_tables(counts)
    tm_dispatch = math.gcd(s, 1024)
    tm_combine = math.gcd(s, 256)
    e_ids = jnp.arange(N_EXPERTS, dtype=jnp.int32)
    starts = jnp.sum(jnp.where(idx[:, :TOP_K, None] == e_ids, pad_start, 0), axis=-1)
    dest = (starts + rank[:, :TOP_K]).astype(jnp.int32)
    tiles = lambda tm: dest.reshape(s // tm, tm * TOP_K)
    xs = _dispatch(pad_start, pad_end, n_used, tiles(tm_dispatch), h2p, n_blocks)
    y_sorted = _moe(nblk, blk0, nxt, slot, first, n_used, xs,
                    w_exp_gate[i], w_exp_up[i], w_exp_down[i], n_blocks)
    out = _combine(tiles(tm_combine), y_sorted, wsel, h2,
                   w_sh_gate[i].astype(BF16), w_sh_up[i].astype(BF16), w_sh_down[i].astype(BF16),
                   x1, mod[0, 5:6, :])
    return out[None]
```

```python
import functools
import math

import jax
import jax.numpy as jnp
from jax import lax
from jax.experimental import pallas as pl
from jax.experimental.pallas import tpu as pltpu

F32 = jnp.float32
BF16 = jnp.bfloat16

D_MODEL = 2048
GRID_W = 64
HEAD_DIM = 128
ROPE_PAIRS = HEAD_DIM // 4
ROPE_THETA = 10000.0
A_HEADS = 8
A_KV_HEADS = 2
A_GROUP = A_HEADS // A_KV_HEADS
B_HEADS = 4
B_V_DIM = 2 * HEAD_DIM
N_EXPERTS = 64
TOP_K = 8
N_GROUPS = 8
TOPK_GROUPS = 4
EXPERT_DIM = 512
SHARED_DIM = 512
ROUTED_SCALE = 2.5
N_MOD = 6
EPS = 1e-6

A_Q_W = A_HEADS * HEAD_DIM
A_KV_W = A_KV_HEADS * HEAD_DIM
B_QK_W = B_HEADS * 2 * HEAD_DIM
B_V_W = B_HEADS * B_V_DIM
IN_W = A_Q_W + 2 * A_KV_W + 2 * B_QK_W + B_V_W + 2 * D_MODEL

COL_AQ = 0
COL_AK = A_Q_W // HEAD_DIM
COL_AV = COL_AK + A_KV_HEADS
COL_BQ = COL_AV + A_KV_HEADS
COL_BK = COL_BQ + 2 * B_HEADS
COL_BV = COL_BK + 2 * B_HEADS
COL_GA = COL_BV + B_V_W // HEAD_DIM
COL_GB = COL_GA + D_MODEL // HEAD_DIM

LANES = 128
SUBLANES = 8
VMEM_LIMIT = 56 * 1024 * 1024

ADALN_TN = 1536
PRENORM_TM = 256
INPROJ_ROW_TILES = 4
PROJ_TN = 512
ATTN_TQ = 512
GQA_STREAMS = 4
MERGE_TM = 512
ROUTER_TM = 512
DISPATCH_TM = 1024
COMBINE_TM = 256
MOE_BLOCK = 256
ONES_ROWS = 16
PACK_TILES = D_MODEL // 2 // LANES
YBUF_PITCH = PACK_TILES + 4
TOKEN_PITCH = TOP_K * YBUF_PITCH + 4
LOG2E = 1.4426950408889634


def _cparams(sem, vmem=VMEM_LIMIT):
    return pltpu.CompilerParams(dimension_semantics=sem, vmem_limit_bytes=vmem)


def _adaln_kernel(cb_ref, w_ref, b_ref, o_ref):
    tn = w_ref.shape[1]
    nl = tn // LANES
    rows = 32

    def body(g, accs):
        accs = list(accs)
        r0 = pl.multiple_of(g * rows, rows)
        for u in range(rows // SUBLANES):
            r = r0 + u * SUBLANES
            w = w_ref[pl.ds(r, SUBLANES), :]
            for v in range(2):
                c = cb_ref[v, pl.ds(r, SUBLANES), :]
                s = c * jax.nn.sigmoid(c)
                for j in range(nl):
                    accs[v * nl + j] = accs[v * nl + j] + w[:, j * LANES:(j + 1) * LANES] * s
        return tuple(accs)

    init = tuple(jnp.zeros((SUBLANES, LANES), F32) for _ in range(2 * nl))
    accs = lax.fori_loop(0, w_ref.shape[0] // rows, body, init)
    for v in range(2):
        row = jnp.concatenate(
            [jnp.sum(accs[v * nl + j], axis=0, keepdims=True) for j in range(nl)], axis=1)
        o_ref[v:v + 1, :] = row + b_ref[...]


def _adaln(cvecs, w, b):
    d, n = w.shape
    tn = ADALN_TN
    cb = jnp.broadcast_to(cvecs[:, :, None], (2, d, LANES))
    return pl.pallas_call(
        _adaln_kernel,
        grid=(n // tn,),
        in_specs=[pl.BlockSpec((2, d, LANES), lambda j: (0, 0, 0)),
                  pl.BlockSpec((d, tn), lambda j: (0, j)),
                  pl.BlockSpec((1, tn), lambda j: (0, j))],
        out_specs=pl.BlockSpec((2, tn), lambda j: (0, j)),
        out_shape=jax.ShapeDtypeStruct((2, n), F32),
        compiler_params=_cparams(("arbitrary",)),
        name="adaln",
    )(cb, w, b.reshape(1, n))


def _rms_mod(x, g, shift, scale):
    y = x * lax.rsqrt(jnp.mean(x * x, axis=-1, keepdims=True) + EPS) * g
    return y * (1.0 + scale) + shift


def _prenorm_kernel(x_ref, c_ref, g_ref, mod_ref, o_ref, *, n_lat_tiles):
    is_ctx = pl.program_id(0) >= n_lat_tiles
    x = jnp.where(is_ctx, c_ref[...], x_ref[...])
    o_ref[...] = _rms_mod(x, g_ref[...], mod_ref[0, 0:1, :], mod_ref[0, 1:2, :]).astype(o_ref.dtype)


def _prenorm(x, ctx, g, mod):
    s, d = x.shape
    c = ctx.shape[0]
    tm = PRENORM_TM
    nl, nc = s // tm, c // tm
    return pl.pallas_call(
        functools.partial(_prenorm_kernel, n_lat_tiles=nl),
        grid=(nl + nc,),
        in_specs=[pl.BlockSpec((tm, d), lambda i: (jnp.minimum(i, nl - 1), 0)),
                  pl.BlockSpec((tm, d), lambda i: (jnp.maximum(i - nl, 0), 0)),
                  pl.BlockSpec((1, d), lambda i: (0, 0)),
                  pl.BlockSpec((1, N_MOD, d), lambda i: (i // nl, 0, 0))],
        out_specs=pl.BlockSpec((tm, d), lambda i: (i, 0)),
        out_shape=jax.ShapeDtypeStruct((s + c, d), BF16),
        compiler_params=_cparams(("arbitrary",)),
        name="prenorm_mix",
    )(x, ctx, g.reshape(1, d), mod)


def _inproj_kernel(h_ref, w_ref, gain_ref, c_ref, sa_ref, sb_ref, o_ref):
    j = pl.program_id(1)
    tm = h_ref.shape[0]
    nh = o_ref.shape[1] // HEAD_DIM
    n_chunks = next(n for n in (6, 3, 2, 1) if tm % (16 * n) == 0)

    def tile(n_normed, chunks):
        w = w_ref[...].astype(h_ref.dtype)
        rows = tm // chunks
        for r in range(chunks):
            rs = slice(r * rows, (r + 1) * rows)
            acc = jnp.dot(h_ref[rs, :], w, preferred_element_type=F32)
            for hd in range(nh):
                sl = slice(hd * HEAD_DIM, (hd + 1) * HEAD_DIM)
                a = acc[:, sl]
                if hd < n_normed:
                    y = a * lax.rsqrt(jnp.mean(a * a, axis=-1, keepdims=True) + EPS) * gain_ref[0, :, sl]
                    a = (y * c_ref[rs, :] + pltpu.roll(y, ROPE_PAIRS, 1) * sa_ref[rs, :]
                         + pltpu.roll(y, HEAD_DIM - ROPE_PAIRS, 1) * sb_ref[rs, :])
                o_ref[rs, sl] = a.astype(o_ref.dtype)

    all_normed = (j < 2) | ((j >= 3) & (j < 7))
    pl.when(all_normed)(lambda: tile(nh, n_chunks))
    pl.when(j == 2)(lambda: tile(A_KV_HEADS, n_chunks))
    pl.when(j >= 7)(lambda: tile(0, n_chunks))


def _inproj(h, w, gains, rope_c, rope_sa, rope_sb):
    t, d = h.shape
    n = w.shape[1]
    tm = t // INPROJ_ROW_TILES
    tn = PROJ_TN
    return pl.pallas_call(
        _inproj_kernel,
        grid=(t // tm, n // tn),
        in_specs=[pl.BlockSpec((tm, d), lambda i, j: (i, 0)),
                  pl.BlockSpec((d, tn), lambda i, j: (0, j)),
                  pl.BlockSpec((1, 1, tn), lambda i, j: (j, 0, 0)),
                  pl.BlockSpec((tm, HEAD_DIM), lambda i, j: (i, 0)),
                  pl.BlockSpec((tm, HEAD_DIM), lambda i, j: (i, 0)),
                  pl.BlockSpec((tm, HEAD_DIM), lambda i, j: (i, 0))],
        out_specs=pl.BlockSpec((tm, tn), lambda i, j: (i, j)),
        out_shape=jax.ShapeDtypeStruct((t, n), BF16),
        compiler_params=_cparams(("arbitrary", "arbitrary")),
        name="inproj",
    )(h, w, gains, rope_c, rope_sa, rope_sb)


def _rope_tables(s, c):
    rows_n = s // GRID_W
    inv = ROPE_THETA ** (-jnp.arange(ROPE_PAIRS, dtype=F32) / ROPE_PAIRS)
    ang_r = jnp.arange(rows_n, dtype=F32)[:, None] * inv
    ang_c = jnp.arange(GRID_W, dtype=F32)[:, None] * inv
    cr, sr, cc, sc = jnp.cos(ang_r), jnp.sin(ang_r), jnp.cos(ang_c), jnp.sin(ang_c)
    zr, zc = jnp.zeros_like(sr), jnp.zeros_like(sc)

    def table(row_parts, col_parts, ctx_value):
        by_row = jnp.concatenate(row_parts + [zr, zr], axis=1)
        by_col = jnp.concatenate([zc, zc] + col_parts, axis=1)
        lat = (by_row[:, None, :] + by_col[None, :, :]).reshape(s, HEAD_DIM)
        return jnp.concatenate([lat, jnp.full((c, HEAD_DIM), ctx_value, F32)], axis=0)

    return (table([cr, cr], [cc, cc], 1.0), table([zr, sr], [zc, sc], 0.0), table([-sr, zr], [-sc, zc], 0.0))


def _head_gains(qn_a, kn_a, qn_b, kn_b):
    qs = HEAD_DIM ** -0.5 * LOG2E
    one = jnp.ones((HEAD_DIM,), F32)
    heads = ([qn_a * qs] * A_HEADS + [kn_a] * A_KV_HEADS + [one] * A_KV_HEADS
             + [qn_b * qs] * (2 * B_HEADS) + [kn_b] * (2 * B_HEADS))
    heads = heads + [one] * (IN_W // HEAD_DIM - len(heads))
    return jnp.concatenate(heads).reshape(IN_W // PROJ_TN, 1, PROJ_TN)


def _build_vt(v_ref, vt_ref, tk):
    n_chunks, rows, _ = vt_ref.shape
    dv = rows - ONES_ROWS
    tail = (lax.broadcasted_iota(jnp.int32, (ONES_ROWS, tk), 0) == 0).astype(vt_ref.dtype)
    for c in range(n_chunks):
        vt_ref[c, 0:dv, :] = v_ref[c * tk:(c + 1) * tk, :].astype(F32).T.astype(vt_ref.dtype)
        vt_ref[c, dv:rows, :] = tail


def _attend(qs, k_refs, vt_ref, s_ref, m_ref, acc_ref):
    n_chunks, _, tk = vt_ref.shape
    n_st = len(qs)
    m_ref[...] = jnp.full(m_ref.shape, -jnp.inf, F32)
    acc_ref[...] = jnp.zeros(acc_ref.shape, F32)

    def scores(i, c, slot):
        off = c * tk if isinstance(c, int) else pl.multiple_of(c * tk, tk)
        s_ref[i, slot] = lax.dot_general(k_refs[i][pl.ds(off, tk), :], qs[i], (((1,), (1,)), ((), ())),
                                         preferred_element_type=F32)

    def update(i, c, slot):
        s = s_ref[i, slot]
        m_old = m_ref[i]
        m_new = jnp.maximum(m_old, jnp.max(s, axis=0, keepdims=True))
        p = jnp.exp2(s - m_new).astype(vt_ref.dtype)
        acc_ref[i] = (acc_ref[i] * jnp.exp2(m_old - m_new)
                      + jnp.dot(vt_ref[c], p, preferred_element_type=F32))
        m_ref[i] = m_new

    for i in range(n_st):
        scores(i, 0, 0)

    def pair(j, carry):
        c = 2 * j
        for i in range(n_st):
            scores(i, c + 1, 1)
        for i in range(n_st):
            update(i, c, 0)
        for i in range(n_st):
            scores(i, c + 2, 0)
        for i in range(n_st):
            update(i, c + 1, 1)
        return carry

    n_pairs = (n_chunks - 1) // 2
    lax.fori_loop(0, n_pairs, pair, 0)
    done = 2 * n_pairs
    if n_chunks - done == 2:
        for i in range(n_st):
            scores(i, done + 1, 1)
    for i in range(n_st):
        update(i, done, 0)
    if n_chunks - done == 2:
        for i in range(n_st):
            update(i, done + 1, 1)


def _gqa_kernel(q_ref, k_ref, v_ref, o_ref, vt_ref, s_ref, m_ref, acc_ref):
    tk = vt_ref.shape[2]
    n_st = s_ref.shape[0]
    pl.when((pl.program_id(1) == 0) & (pl.program_id(2) == 0))(lambda: _build_vt(v_ref, vt_ref, tk))
    qs = [q_ref[:, i * HEAD_DIM:(i + 1) * HEAD_DIM] for i in range(n_st)]
    _attend(qs, [k_ref] * n_st, vt_ref, s_ref, m_ref, acc_ref)
    for i in range(n_st):
        o_t = acc_ref[i, 0:HEAD_DIM, :] / acc_ref[i, HEAD_DIM:HEAD_DIM + 1, :]
        o_ref[:, i * HEAD_DIM:(i + 1) * HEAD_DIM] = o_t.T.astype(o_ref.dtype)


def _key_chunk(t):
    for tk in (1408, 768, 1024, 512, 640, 384, 256, 128):
        if t % tk == 0:
            return tk
    raise ValueError(f"unsupported key count {t}")


def _gqa(proj, s):
    t = proj.shape[0]
    tq = ATTN_TQ
    tk = _key_chunk(t)
    n_st = GQA_STREAMS
    per_g = A_GROUP // n_st
    return pl.pallas_call(
        _gqa_kernel,
        grid=(A_KV_HEADS, per_g, s // tq),
        in_specs=[pl.BlockSpec((tq, n_st * HEAD_DIM), lambda g, hh, i: (i, COL_AQ // n_st + g * per_g + hh)),
                  pl.BlockSpec((t, HEAD_DIM), lambda g, hh, i: (0, COL_AK + g)),
                  pl.BlockSpec((t, HEAD_DIM), lambda g, hh, i: (0, COL_AV + g))],
        out_specs=pl.BlockSpec((tq, n_st * HEAD_DIM), lambda g, hh, i: (i, g * per_g + hh)),
        out_shape=jax.ShapeDtypeStruct((s, A_Q_W), BF16),
        scratch_shapes=[pltpu.VMEM((t // tk, HEAD_DIM + ONES_ROWS, tk), BF16),
                        pltpu.VMEM((n_st, 2, tk, tq), F32),
                        pltpu.VMEM((n_st, 1, tq), F32),
                        pltpu.VMEM((n_st, HEAD_DIM + ONES_ROWS, tq), F32)],
        compiler_params=_cparams(("arbitrary", "arbitrary", "arbitrary")),
        name="gqa_attn",
    )(proj, proj, proj)


def _diff_kernel(lam_ref, q0_ref, q1_ref, k0_ref, k1_ref, v_ref, g_ref, o_ref,
                 vt_ref, s_ref, m_ref, acc_ref, *, lam_init):
    tk = vt_ref.shape[2]
    pl.when(pl.program_id(1) == 0)(lambda: _build_vt(v_ref, vt_ref, tk))
    lv = lam_ref[...]
    lam = (jnp.exp(jnp.sum(lv[0:1, :] * lv[1:2, :], axis=-1, keepdims=True))
           - jnp.exp(jnp.sum(lv[2:3, :] * lv[3:4, :], axis=-1, keepdims=True)) + lam_init)
    _attend([q0_ref[...], q1_ref[...]], [k0_ref, k1_ref], vt_ref, s_ref, m_ref, acc_ref)
    o_t = (acc_ref[0, 0:B_V_DIM, :] / acc_ref[0, B_V_DIM:B_V_DIM + 1, :]
           - lam * (acc_ref[1, 0:B_V_DIM, :] / acc_ref[1, B_V_DIM:B_V_DIM + 1, :]))
    o = o_t.T
    y = o * lax.rsqrt(jnp.mean(o * o, axis=-1, keepdims=True) + EPS) * g_ref[...]
    o_ref[...] = (y * (1.0 - lam_init)).astype(o_ref.dtype)


def _diff(proj, s, lam_vecs, subln_g, lam_init):
    t = proj.shape[0]
    tq = ATTN_TQ
    tk = _key_chunk(t)
    vb = B_V_DIM // HEAD_DIM
    return pl.pallas_call(
        functools.partial(_diff_kernel, lam_init=lam_init),
        grid=(B_HEADS, s // tq),
        in_specs=[pl.BlockSpec((4, HEAD_DIM), lambda h, i: (0, 0)),
                  pl.BlockSpec((tq, HEAD_DIM), lambda h, i: (i, COL_BQ + 2 * h)),
                  pl.BlockSpec((tq, HEAD_DIM), lambda h, i: (i, COL_BQ + 2 * h + 1)),
                  pl.BlockSpec((t, HEAD_DIM), lambda h, i: (0, COL_BK + 2 * h)),
                  pl.BlockSpec((t, HEAD_DIM), lambda h, i: (0, COL_BK + 2 * h + 1)),
                  pl.BlockSpec((t, B_V_DIM), lambda h, i: (0, COL_BV // vb + h)),
                  pl.BlockSpec((1, B_V_DIM), lambda h, i: (0, 0))],
        out_specs=pl.BlockSpec((tq, B_V_DIM), lambda h, i: (i, h)),
        out_shape=jax.ShapeDtypeStruct((s, B_V_W), BF16),
        scratch_shapes=[pltpu.VMEM((t // tk, B_V_DIM + ONES_ROWS, tk), BF16),
                        pltpu.VMEM((2, 2, tk, tq), F32),
                        pltpu.VMEM((2, 1, tq), F32),
                        pltpu.VMEM((2, B_V_DIM + ONES_ROWS, tq), F32)],
        compiler_params=_cparams(("arbitrary", "arbitrary")),
        name="diff_attn",
    )(lam_vecs, proj, proj, proj, proj, proj, subln_g.reshape(1, B_V_DIM))


def _merge_kernel(oa_ref, ob_ref, *refs):
    nc = (len(refs) - 6) // 2
    ga_refs, gb_refs = refs[:nc], refs[nc:2 * nc]
    wa_ref, wb_ref, wo_ref, x_ref, gate_ref, o_ref = refs[2 * nc:]
    tn = ga_refs[0].shape[1]
    ya = jnp.dot(oa_ref[...], wa_ref[...], preferred_element_type=F32)
    yb = jnp.dot(ob_ref[...], wb_ref[...], preferred_element_type=F32)
    parts = []
    for c in range(nc):
        sl = slice(c * tn, (c + 1) * tn)
        t = (jax.nn.sigmoid(ga_refs[c][...].astype(F32)) * ya[:, sl]
             + jax.nn.sigmoid(gb_refs[c][...].astype(F32)) * yb[:, sl])
        parts.append(t.astype(wo_ref.dtype))
    y = jnp.dot(jnp.concatenate(parts, axis=1), wo_ref[...], preferred_element_type=F32)
    o_ref[...] = x_ref[...] + gate_ref[...] * y


def _merge(oa, ob, proj, wa, wb, wo, x, gate):
    s, d = x.shape
    tm = MERGE_TM
    tn = PROJ_TN
    nc = d // tn
    ga0 = COL_GA * HEAD_DIM // tn
    gb0 = COL_GB * HEAD_DIM // tn
    resident = lambda shape: pl.BlockSpec(shape, lambda i: (0, 0), pipeline_mode=pl.Buffered(1))
    gate_specs = [pl.BlockSpec((tm, tn), lambda i, c=c0 + c: (i, c)) for c0 in (ga0, gb0) for c in range(nc)]
    return pl.pallas_call(
        _merge_kernel,
        grid=(s // tm,),
        in_specs=[pl.BlockSpec((tm, A_Q_W), lambda i: (i, 0)),
                  pl.BlockSpec((tm, B_V_W), lambda i: (i, 0)),
                  *gate_specs,
                  resident((A_Q_W, d)), resident((B_V_W, d)), resident((d, d)),
                  pl.BlockSpec((tm, d), lambda i: (i, 0)),
                  pl.BlockSpec((1, d), lambda i: (0, 0))],
        out_specs=pl.BlockSpec((tm, d), lambda i: (i, 0)),
        out_shape=jax.ShapeDtypeStruct((s, d), F32),
        compiler_params=_cparams(("arbitrary",)),
        name="merge_out",
    )(oa, ob, *([proj] * (2 * nc)), wa, wb, wo, x, gate)


def _store_packed_tiles(ref, x):
    rows = x.shape[0]
    bits = lax.bitcast_convert_type(x.astype(BF16).astype(F32), jnp.uint32)
    half = bits.shape[1] // 2
    words = (bits[:, :half] >> 16) | (bits[:, half:] & jnp.uint32(0xFFFF0000))
    for c in range(PACK_TILES):
        ref[pl.ds(c, rows, stride=PACK_TILES), :] = words[:, c * LANES:(c + 1) * LANES]


def _lane_max(x):
    return jnp.max(x, axis=-1, keepdims=True)


def _lane_min(x):
    return jnp.min(x, axis=-1, keepdims=True)


def _group_allreduce(x, lane, op):
    for sft in (1, 2, 4):
        up = pltpu.roll(x, sft, 1)
        dn = pltpu.roll(x, LANES - sft, 1)
        x = op(x, jnp.where((lane & sft) != 0, up, dn))
    return x


def _router_kernel(x_ref, g_ref, mod_ref, wr_ref, rb_ref,
                   h_ref, hp_ref, idx_ref, wsel_ref, rank_ref, cnt_ref, carry_ref):
    @pl.when(pl.program_id(0) == 0)
    def _():
        carry_ref[...] = jnp.zeros_like(carry_ref)

    h = _rms_mod(x_ref[...], g_ref[...], mod_ref[0, 3:4, :], mod_ref[0, 4:5, :])
    hb = h.astype(BF16)
    h_ref[...] = hb
    _store_packed_tiles(hp_ref, h)
    tm = h.shape[0]

    h_lo = (h - hb.astype(F32)).astype(BF16)
    logits = (jnp.dot(hb, wr_ref[0], preferred_element_type=F32)
              + jnp.dot(hb, wr_ref[1], preferred_element_type=F32)
              + jnp.dot(h_lo, wr_ref[0], preferred_element_type=F32))
    scores = jax.nn.sigmoid(logits)
    lane = lax.broadcasted_iota(jnp.int32, (tm, LANES), 1)
    lane_f = lane.astype(F32)
    gid_f = (lane >> 3).astype(F32)
    valid = lane < N_EXPERTS
    neg = jnp.float32(-jnp.inf)
    big = jnp.float32(LANES)
    biased = jnp.where(valid, scores + rb_ref[...], neg)

    m1 = _group_allreduce(biased, lane, jnp.maximum)
    a1 = _group_allreduce(jnp.where(biased == m1, lane_f, big), lane, jnp.minimum)
    m2 = _group_allreduce(jnp.where(lane_f == a1, neg, biased), lane, jnp.maximum)
    gscore = jnp.where(valid, m1 + m2, neg)
    keep = jnp.zeros((tm, LANES), jnp.bool_)
    for _ in range(TOPK_GROUPS):
        best = _lane_max(gscore)
        gsel = _lane_min(jnp.where(gscore == best, gid_f, big))
        hit = gid_f == gsel
        keep = keep | hit
        gscore = jnp.where(hit, neg, gscore)
    cand = jnp.where(keep & valid, biased, neg)

    onehot = jnp.zeros((tm, LANES), F32)
    idx_out = jnp.zeros((tm, LANES), F32)
    w_out = jnp.zeros((tm, LANES), F32)
    sels = []
    for k in range(TOP_K):
        best = _lane_max(cand)
        sel = _lane_min(jnp.where(cand == best, lane_f, big))
        hit = lane_f == sel
        wk = jnp.sum(jnp.where(hit, scores, 0.0), axis=-1, keepdims=True)
        cand = jnp.where(hit, neg, cand)
        onehot = jnp.where(hit, 1.0, onehot)
        idx_out = jnp.where(lane == k, sel, idx_out)
        w_out = jnp.where(lane == k, wk, w_out)
        sels.append(hit)
    wsum = jnp.sum(w_out, axis=-1, keepdims=True)
    wsel_ref[...] = w_out / wsum * ROUTED_SCALE
    idx_ref[...] = idx_out.astype(jnp.int32)

    row = lax.broadcasted_iota(jnp.int32, (tm, tm), 0)
    col = lax.broadcasted_iota(jnp.int32, (tm, tm), 1)
    lower = (col < row).astype(BF16)
    before = jnp.dot(lower, onehot.astype(BF16), preferred_element_type=F32) + carry_ref[...]
    rank_out = jnp.zeros((tm, LANES), F32)
    for k in range(TOP_K):
        rk = jnp.sum(jnp.where(sels[k], before, 0.0), axis=-1, keepdims=True)
        rank_out = jnp.where(lane == k, rk, rank_out)
    rank_ref[...] = rank_out.astype(jnp.int32)
    carry_ref[...] = carry_ref[...] + jnp.sum(onehot, axis=0, keepdims=True)
    cnt_ref[...] = carry_ref[...]


def _router(x1, g, mod, w_router, router_bias):
    s, d = x1.shape
    tm = ROUTER_TM
    e = w_router.shape[1]
    wr = jnp.pad(w_router, ((0, 0), (0, LANES - e)))
    wr_hi = wr.astype(BF16)
    wr = jnp.stack([wr_hi, (wr - wr_hi.astype(F32)).astype(BF16)])
    rb = jnp.pad(router_bias, (0, LANES - e)).reshape(1, LANES)
    row_spec = lambda w: pl.BlockSpec((tm, w), lambda i: (i, 0))
    return pl.pallas_call(
        _router_kernel,
        grid=(s // tm,),
        in_specs=[row_spec(d),
                  pl.BlockSpec((1, d), lambda i: (0, 0)),
                  pl.BlockSpec((1, N_MOD, d), lambda i: (0, 0, 0)),
                  pl.BlockSpec((2, d, LANES), lambda i: (0, 0, 0)),
                  pl.BlockSpec((1, LANES), lambda i: (0, 0))],
        out_specs=[row_spec(d), pl.BlockSpec((tm * PACK_TILES, LANES), lambda i: (i, 0)),
                   row_spec(LANES), row_spec(LANES), row_spec(LANES),
                   pl.BlockSpec((1, LANES), lambda i: (0, 0))],
        out_shape=[jax.ShapeDtypeStruct((s, d), BF16),
                   jax.ShapeDtypeStruct((s * PACK_TILES, LANES), jnp.uint32),
                   jax.ShapeDtypeStruct((s, LANES), jnp.int32),
                   jax.ShapeDtypeStruct((s, LANES), F32),
                   jax.ShapeDtypeStruct((s, LANES), jnp.int32),
                   jax.ShapeDtypeStruct((1, LANES), F32)],
        scratch_shapes=[pltpu.VMEM((1, LANES), F32)],
        compiler_params=_cparams(("arbitrary",)),
        name="ffn_router",
    )(x1, g.reshape(1, d), mod, wr, rb)


def _unpack_halves(words):
    lo = lax.bitcast_convert_type(words << 16, F32)
    hi = lax.bitcast_convert_type(words & jnp.uint32(0xFFFF0000), F32)
    return lo, hi


def _unpack_rows(words):
    lo, hi = _unpack_halves(words)
    return jnp.concatenate([lo.astype(BF16), hi.astype(BF16)], axis=1)


def _token_rows(row, tiles):
    return pl.ds(pl.multiple_of(row * tiles, tiles), tiles)


def _dispatch_kernel(ps_ref, pe_ref, nused_ref, dest_hbm, hp_ref, xs_hbm,
                     dest_smem, zbuf, sem_i, sem_z, sem_x, *, n_blocks):
    i = pl.program_id(0)
    n = dest_smem.shape[0]
    tm = n // TOP_K
    copy_dest = pltpu.make_async_copy(dest_hbm.at[i], dest_smem, sem_i)
    copy_dest.start()

    @pl.when(i == 0)
    def _():
        zbuf[...] = jnp.zeros_like(zbuf)

        def zero_block(row0):
            span = MOE_BLOCK * PACK_TILES
            return pltpu.make_async_copy(
                zbuf, xs_hbm.at[pl.ds(pl.multiple_of(row0 * PACK_TILES, span), span), :], sem_z)

        def per_expert(action):
            def body(e, carry):
                @pl.when(pe_ref[e] > ps_ref[e])
                def _():
                    action(zero_block(pe_ref[e] - MOE_BLOCK))
                return carry
            lax.fori_loop(0, N_EXPERTS, body, 0)

        def per_tail(action):
            def body(b, carry):
                action(zero_block(b * MOE_BLOCK))
                return carry
            lax.fori_loop(nused_ref[0], n_blocks, body, 0)

        per_expert(lambda cp: cp.start())
        per_tail(lambda cp: cp.start())
        per_expert(lambda cp: cp.wait())
        per_tail(lambda cp: cp.wait())

    copy_dest.wait()

    def issue(t, carry):
        for k in range(TOP_K):
            dest = dest_smem[t * TOP_K + k]
            pltpu.make_async_copy(hp_ref.at[_token_rows(t, PACK_TILES), :],
                                  xs_hbm.at[_token_rows(dest, PACK_TILES), :], sem_x).start(priority=k % 2)
        return carry

    lax.fori_loop(0, tm, issue, 0)
    for _ in range(TOP_K):
        pltpu.make_async_copy(hp_ref, xs_hbm.at[pl.ds(0, tm * PACK_TILES), :], sem_x).wait()


def _dispatch(pad_start, pad_end, n_used, dest_tm, hp, n_blocks):
    n_tiles, n = dest_tm.shape
    tm = n // TOP_K
    grid_spec = pltpu.PrefetchScalarGridSpec(
        num_scalar_prefetch=3,
        grid=(n_tiles,),
        in_specs=[pl.BlockSpec(memory_space=pl.ANY),
                  pl.BlockSpec((tm * PACK_TILES, LANES), lambda i, ps, pe, nu: (i, 0))],
        out_specs=pl.BlockSpec(memory_space=pl.ANY),
        scratch_shapes=[pltpu.SMEM((n,), jnp.int32),
                        pltpu.VMEM((MOE_BLOCK * PACK_TILES, LANES), jnp.uint32),
                        pltpu.SemaphoreType.DMA,
                        pltpu.SemaphoreType.DMA,
                        pltpu.SemaphoreType.DMA])
    return pl.pallas_call(
        functools.partial(_dispatch_kernel, n_blocks=n_blocks),
        grid_spec=grid_spec,
        out_shape=jax.ShapeDtypeStruct((n_blocks * MOE_BLOCK * PACK_TILES, LANES), jnp.uint32),
        compiler_params=_cparams(("arbitrary",)),
        name="moe_dispatch",
    )(pad_start, pad_end, n_used, dest_tm, hp)


def _moe_kernel(nblk_ref, blk0_ref, nxt_ref, slot_ref, first_ref, nused_ref,
                xs_hbm, wg_hbm, wu_hbm, wd_hbm, y_hbm,
                xbuf, ybuf, wg32, wu32, wd32, wgb, wub, wdb, sem_w, sem_x, sem_y, *, n_blocks):
    e = pl.program_id(0)
    nb = nblk_ref[e]
    span = MOE_BLOCK * PACK_TILES

    def block_rows(b):
        return pl.ds(pl.multiple_of(b * span, span), span)

    def fetch(ex, slot):
        return (pltpu.make_async_copy(wg_hbm.at[ex], wg32.at[slot], sem_w.at[slot, 0]),
                pltpu.make_async_copy(wu_hbm.at[ex], wu32.at[slot], sem_w.at[slot, 1]),
                pltpu.make_async_copy(wd_hbm.at[ex], wd32.at[slot], sem_w.at[slot, 2]))

    def x_copy(b, s):
        return pltpu.make_async_copy(xs_hbm.at[block_rows(b), :], xbuf.at[s], sem_x.at[s])

    def y_copy(b, s):
        return pltpu.make_async_copy(ybuf.at[s], y_hbm.at[block_rows(b), :], sem_y.at[s])

    n_used = nused_ref[0]
    n_xbuf = xbuf.shape[0]

    @pl.when(nb > 0)
    def _():
        b0 = blk0_ref[e]
        slot = slot_ref[e]

        @pl.when(e == first_ref[0])
        def _():
            for b in range(n_xbuf - 1):
                pl.when(b < n_used)(lambda b=b: x_copy(b, b).start())
            for cp in fetch(e, 0):
                cp.start(priority=1)

        for cp in fetch(e, slot):
            cp.wait()

        @pl.when(nxt_ref[e] >= 0)
        def _():
            for cp in fetch(nxt_ref[e], 1 - slot):
                cp.start(priority=1)

        wgb[...] = wg32[slot].astype(BF16)
        wub[...] = wu32[slot].astype(BF16)
        wdb[...] = wd32[slot].astype(BF16)

        def body(j, carry):
            b = b0 + j
            s = b % 2
            sx = b % n_xbuf
            ahead = b + n_xbuf - 1
            x_copy(b, sx).wait()
            pl.when(ahead < n_used)(lambda: x_copy(ahead, ahead % n_xbuf).start())
            pl.when(b >= 2)(lambda: y_copy(b - 2, s).wait())
            xb = xbuf.at[sx]
            words = jnp.concatenate(
                [xb[pl.ds(c, MOE_BLOCK, stride=PACK_TILES), :] for c in range(PACK_TILES)], axis=1)
            x = _unpack_rows(words)
            gate = jnp.dot(x, wgb[...], preferred_element_type=F32)
            up = jnp.dot(x, wub[...], preferred_element_type=F32)
            act = (gate * jax.nn.sigmoid(gate) * up).astype(BF16)
            _store_packed_tiles(ybuf.at[s], jnp.dot(act, wdb[...], preferred_element_type=F32))
            y_copy(b, s).start()
            return carry

        lax.fori_loop(0, nb, body, 0)

    @pl.when(e == pl.num_programs(0) - 1)
    def _():
        pl.when(n_used >= 2)(lambda: y_copy(n_used - 2, n_used % 2).wait())
        pl.when(n_used >= 1)(lambda: y_copy(n_used - 1, (n_used - 1) % 2).wait())
        ybuf[0] = jnp.zeros(ybuf.shape[1:], ybuf.dtype)

        def tail(action):
            def step(b, carry):
                action(y_copy(b, 0))
                return carry
            lax.fori_loop(nused_ref[0], n_blocks, step, 0)

        tail(lambda cp: cp.start())
        tail(lambda cp: cp.wait())


def _moe(nblk, blk0, nxt, slot, first, n_used, xs, weg, weu, wed, n_blocks):
    n_exp, d, f = weg.shape
    span = MOE_BLOCK * PACK_TILES
    grid_spec = pltpu.PrefetchScalarGridSpec(
        num_scalar_prefetch=6,
        grid=(n_exp,),
        in_specs=[pl.BlockSpec(memory_space=pl.ANY)] * 4,
        out_specs=pl.BlockSpec(memory_space=pl.ANY),
        scratch_shapes=[pltpu.VMEM((3, span, LANES), jnp.uint32), pltpu.VMEM((2, span, LANES), jnp.uint32),
                        pltpu.VMEM((2, d, f), F32), pltpu.VMEM((2, d, f), F32), pltpu.VMEM((2, f, d), F32),
                        pltpu.VMEM((d, f), BF16), pltpu.VMEM((d, f), BF16), pltpu.VMEM((f, d), BF16),
                        pltpu.SemaphoreType.DMA((2, 3)), pltpu.SemaphoreType.DMA((3,)),
                        pltpu.SemaphoreType.DMA((2,))])
    return pl.pallas_call(
        functools.partial(_moe_kernel, n_blocks=n_blocks),
        grid_spec=grid_spec,
        out_shape=jax.ShapeDtypeStruct((n_blocks * span, LANES), jnp.uint32),
        compiler_params=_cparams(("arbitrary",)),
        name="moe_experts",
    )(nblk, blk0, nxt, slot, first, n_used, xs, weg, weu, wed)


def _combine_kernel(dest_hbm, y_hbm, w_ref, h_ref, wg_ref, wu_ref, wd_ref, x_ref, gate_ref, o_ref,
                    dest0, dest1, ybuf0, ybuf1, sem_i, sem_y):
    i = pl.program_id(0)
    tm = h_ref.shape[0]
    n = tm * TOP_K
    span = n * PACK_TILES
    dests = (dest0, dest1)
    ybufs = (ybuf0, ybuf1)

    def start_tile(tile, slot):
        copy_dest = pltpu.make_async_copy(dest_hbm.at[tile], dests[slot], sem_i)
        copy_dest.start()
        copy_dest.wait()

        def issue(t, carry):
            base = t * TOKEN_PITCH
            for k in range(TOP_K):
                src = dests[slot][t * TOP_K + k]
                pltpu.make_async_copy(
                    y_hbm.at[_token_rows(src, PACK_TILES), :],
                    ybufs[slot].at[pl.ds(pl.multiple_of(base + k * YBUF_PITCH, 4), PACK_TILES), :],
                    sem_y.at[slot]).start(priority=k % 2)
            return carry

        lax.fori_loop(0, tm, issue, 0)

    def step(cur):
        nxt = 1 - cur
        pl.when(i == 0)(lambda: start_tile(0, cur))
        pl.when(i + 1 < pl.num_programs(0))(lambda: start_tile(i + 1, nxt))

        h = h_ref[...]
        gate = jnp.dot(h, wg_ref[...], preferred_element_type=F32)
        up = jnp.dot(h, wu_ref[...], preferred_element_type=F32)
        act = (gate * jax.nn.sigmoid(gate) * up).astype(BF16)
        shared = jnp.dot(act, wd_ref[...], preferred_element_type=F32)

        yb = ybufs[cur]
        pltpu.make_async_copy(y_hbm.at[pl.ds(0, span), :], yb.at[pl.ds(0, span), :], sem_y.at[cur]).wait()
        wks = [jnp.broadcast_to(w_ref[:, k:k + 1], (tm, LANES)) for k in range(TOP_K)]
        lows, highs = [], []
        for c in range(PACK_TILES):
            acc_lo = acc_hi = None
            for k in range(TOP_K):
                lo, hi = _unpack_halves(yb[pl.ds(k * YBUF_PITCH + c, tm, stride=TOKEN_PITCH), :])
                acc_lo = lo * wks[k] if acc_lo is None else acc_lo + lo * wks[k]
                acc_hi = hi * wks[k] if acc_hi is None else acc_hi + hi * wks[k]
            lows.append(acc_lo)
            highs.append(acc_hi)
        routed = jnp.concatenate(lows + highs, axis=1)
        o_ref[...] = x_ref[...] + gate_ref[...] * (routed + shared)

    pl.when(i % 2 == 0)(lambda: step(0))
    pl.when(i % 2 == 1)(lambda: step(1))


def _combine(dest_tm, y_sorted, wsel, h2, wsg, wsu, wsd, x1, gate):
    s, d = x1.shape
    n_tiles, n = dest_tm.shape
    tm = n // TOP_K
    f = wsg.shape[1]
    return pl.pallas_call(
        _combine_kernel,
        grid=(n_tiles,),
        in_specs=[pl.BlockSpec(memory_space=pl.ANY),
                  pl.BlockSpec(memory_space=pl.ANY),
                  pl.BlockSpec((tm, LANES), lambda i: (i, 0)),
                  pl.BlockSpec((tm, d), lambda i: (i, 0)),
                  pl.BlockSpec((d, f), lambda i: (0, 0)),
                  pl.BlockSpec((d, f), lambda i: (0, 0)),
                  pl.BlockSpec((f, d), lambda i: (0, 0)),
                  pl.BlockSpec((tm, d), lambda i: (i, 0)),
                  pl.BlockSpec((1, d), lambda i: (0, 0))],
        out_specs=pl.BlockSpec((tm, d), lambda i: (i, 0)),
        out_shape=jax.ShapeDtypeStruct((s, d), F32),
        scratch_shapes=[pltpu.SMEM((n,), jnp.int32),
                        pltpu.SMEM((n,), jnp.int32),
                        pltpu.VMEM((tm * TOKEN_PITCH, LANES), jnp.uint32),
                        pltpu.VMEM((tm * TOKEN_PITCH, LANES), jnp.uint32),
                        pltpu.SemaphoreType.DMA,
                        pltpu.SemaphoreType.DMA((2,))],
        compiler_params=_cparams(("arbitrary",)),
        name="moe_combine",
    )(dest_tm, y_sorted, wsel, h2, wsg, wsu, wsd, x1, gate)


def _expert_tables(counts):
    i32 = jnp.int32
    padded = ((counts + MOE_BLOCK - 1) // MOE_BLOCK * MOE_BLOCK).astype(i32)
    pad_end = jnp.cumsum(padded).astype(i32)
    pad_start = pad_end - padded
    ids = jnp.arange(N_EXPERTS, dtype=i32)
    busy = padded > 0
    later_busy = busy[None, :] & (ids[None, :] > ids[:, None])
    nxt = jnp.min(jnp.where(later_busy, ids[None, :], N_EXPERTS), axis=1)
    nxt = jnp.where(nxt < N_EXPERTS, nxt, -1).astype(i32)
    slot = (jnp.maximum(jnp.cumsum(busy.astype(i32)) - 1, 0) % 2).astype(i32)
    first = jnp.min(jnp.where(busy, ids, N_EXPERTS)).astype(i32).reshape(1)
    n_used = (pad_end[-1] // MOE_BLOCK).reshape(1)
    return pad_start, pad_end, padded // MOE_BLOCK, pad_start // MOE_BLOCK, nxt, slot, first, n_used


def kernel(x, c, ctx, c_ctx, w_ada, b_ada, norm_mix, norm_ffn, w_in, q_norm_a, k_norm_a, q_norm_b, k_norm_b, lambda_q1, lambda_k1, lambda_q2, lambda_k2, subln_b, w_branch_a, w_branch_b, w_out, w_router, router_bias, w_exp_gate, w_exp_up, w_exp_down, w_sh_gate, w_sh_up, w_sh_down):
    depth = w_ada.shape[0]
    assert depth == 1 and x.shape[0] == 1 and ctx.shape[0] == 1
    s, d = x.shape[1], x.shape[2]
    n_ctx = ctx.shape[1]
    assert d == D_MODEL and s % GRID_W == 0
    assert s % PRENORM_TM == 0 and n_ctx % PRENORM_TM == 0 and s % ATTN_TQ == 0
    assert s % MERGE_TM == 0 and s % ROUTER_TM == 0 and (s + n_ctx) % (INPROJ_ROW_TILES * 16) == 0
    i = 0
    lam_init = 0.8 - 0.6 * math.exp(-0.3 * i)
    xs = x[0]

    mod = _adaln(jnp.concatenate([c, c_ctx[None, :]], axis=0), w_ada[i], b_ada[i]).reshape(2, N_MOD, d)

    h = _prenorm(xs, ctx[0], norm_mix[i], mod)
    tc, tsa, tsb = _rope_tables(s, n_ctx)
    gains = _head_gains(q_norm_a[i], k_norm_a[i], q_norm_b[i], k_norm_b[i])
    proj = _inproj(h, w_in[i], gains, tc, tsa, tsb)
    oa = _gqa(proj, s)
    lam_vecs = jnp.stack([lambda_q1[i], lambda_k1[i], lambda_q2[i], lambda_k2[i]]).astype(F32)
    ob = _diff(proj, s, lam_vecs, subln_b[i], lam_init)
    x1 = _merge(oa, ob, proj, w_branch_a[i].astype(BF16), w_branch_b[i].astype(BF16),
                w_out[i].astype(BF16), xs, mod[0, 2:3, :])

    h2, h2p, idx, wsel, rank, cnt = _router(x1, norm_ffn[i], mod[0:1], w_router[i], router_bias[i])
    counts = cnt[0, :N_EXPERTS].astype(jnp.int32)
    n_blocks = -(-(s * TOP_K) // MOE_BLOCK) + N_EXPERTS
    pad_start, pad_end, nblk, blk0, nxt, slot, first, n_used = _expert_tables(counts)
    tm_dispatch = math.gcd(s, DISPATCH_TM)
    tm_combine = math.gcd(s, COMBINE_TM)
    e_ids = jnp.arange(N_EXPERTS, dtype=jnp.int32)
    starts = jnp.sum(jnp.where(idx[:, :TOP_K, None] == e_ids, pad_start, 0), axis=-1)
    dest = (starts + rank[:, :TOP_K]).astype(jnp.int32)
    tiles = lambda tm: dest.reshape(s // tm, tm * TOP_K)
    xs = _dispatch(pad_start, pad_end, n_used, tiles(tm_dispatch), h2p, n_blocks)
    y_sorted = _moe(nblk, blk0, nxt, slot, first, n_used, xs,
                    w_exp_gate[i], w_exp_up[i], w_exp_down[i], n_blocks)
    out = _combine(tiles(tm_combine), y_sorted, wsel, h2,
                   w_sh_gate[i].astype(BF16), w_sh_up[i].astype(BF16), w_sh_down[i].astype(BF16),
                   x1, mod[0, 5:6, :])
    return out[None]
```

```python
import functools
import math

import jax
import jax.numpy as jnp
from jax import lax
from jax.experimental import pallas as pl
from jax.experimental.pallas import tpu as pltpu

F32 = jnp.float32
BF16 = jnp.bfloat16

D_MODEL = 2048
GRID_W = 64
HEAD_DIM = 128
ROPE_PAIRS = HEAD_DIM // 4
ROPE_THETA = 10000.0
A_HEADS = 8
A_KV_HEADS = 2
A_GROUP = A_HEADS // A_KV_HEADS
B_HEADS = 4
B_V_DIM = 2 * HEAD_DIM
N_EXPERTS = 64
TOP_K = 8
N_GROUPS = 8
TOPK_GROUPS = 4
EXPERT_DIM = 512
SHARED_DIM = 512
ROUTED_SCALE = 2.5
N_MOD = 6
EPS = 1e-6

A_Q_W = A_HEADS * HEAD_DIM
A_KV_W = A_KV_HEADS * HEAD_DIM
B_QK_W = B_HEADS * 2 * HEAD_DIM
B_V_W = B_HEADS * B_V_DIM
IN_W = A_Q_W + 2 * A_KV_W + 2 * B_QK_W + B_V_W + 2 * D_MODEL

COL_AQ = 0
COL_AK = A_Q_W // HEAD_DIM
COL_AV = COL_AK + A_KV_HEADS
COL_BQ = COL_AV + A_KV_HEADS
COL_BK = COL_BQ + 2 * B_HEADS
COL_BV = COL_BK + 2 * B_HEADS
COL_GA = COL_BV + B_V_W // HEAD_DIM
COL_GB = COL_GA + D_MODEL // HEAD_DIM

LANES = 128
SUBLANES = 8
VMEM_LIMIT = 56 * 1024 * 1024

ADALN_TN = 1536
PRENORM_TM = 256
INPROJ_ROW_TILES = 4
PROJ_TN = 512
ATTN_TQ = 512
GQA_STREAMS = 4
DIFF_HEADS = 2
MERGE_TM = 512
ROUTER_TM = 512
DISPATCH_TM = 1024
COMBINE_TM = 256
MOE_BLOCK = 256
ONES_ROWS = 16
PACK_TILES = D_MODEL // 2 // LANES
YBUF_PITCH = PACK_TILES + 4
TOKEN_PITCH = TOP_K * YBUF_PITCH + 4
LOG2E = 1.4426950408889634


def _cparams(sem, vmem=VMEM_LIMIT):
    return pltpu.CompilerParams(dimension_semantics=sem, vmem_limit_bytes=vmem)


def _adaln_kernel(cb_ref, w_ref, b_ref, o_ref):
    tn = w_ref.shape[1]
    nl = tn // LANES
    rows = 32

    def body(g, accs):
        accs = list(accs)
        r0 = pl.multiple_of(g * rows, rows)
        for u in range(rows // SUBLANES):
            r = r0 + u * SUBLANES
            w = w_ref[pl.ds(r, SUBLANES), :]
            for v in range(2):
                c = cb_ref[v, pl.ds(r, SUBLANES), :]
                s = c * jax.nn.sigmoid(c)
                for j in range(nl):
                    accs[v * nl + j] = accs[v * nl + j] + w[:, j * LANES:(j + 1) * LANES] * s
        return tuple(accs)

    init = tuple(jnp.zeros((SUBLANES, LANES), F32) for _ in range(2 * nl))
    accs = lax.fori_loop(0, w_ref.shape[0] // rows, body, init)
    for v in range(2):
        row = jnp.concatenate(
            [jnp.sum(accs[v * nl + j], axis=0, keepdims=True) for j in range(nl)], axis=1)
        o_ref[v:v + 1, :] = row + b_ref[...]


def _adaln(cvecs, w, b):
    d, n = w.shape
    tn = ADALN_TN
    cb = jnp.broadcast_to(cvecs[:, :, None], (2, d, LANES))
    return pl.pallas_call(
        _adaln_kernel,
        grid=(n // tn,),
        in_specs=[pl.BlockSpec((2, d, LANES), lambda j: (0, 0, 0)),
                  pl.BlockSpec((d, tn), lambda j: (0, j)),
                  pl.BlockSpec((1, tn), lambda j: (0, j))],
        out_specs=pl.BlockSpec((2, tn), lambda j: (0, j)),
        out_shape=jax.ShapeDtypeStruct((2, n), F32),
        compiler_params=_cparams(("arbitrary",)),
        name="adaln",
    )(cb, w, b.reshape(1, n))


def _rms_mod(x, g, shift, scale):
    y = x * lax.rsqrt(jnp.mean(x * x, axis=-1, keepdims=True) + EPS) * g
    return y * (1.0 + scale) + shift


def _prenorm_kernel(x_ref, c_ref, g_ref, mod_ref, o_ref, *, n_lat_tiles):
    is_ctx = pl.program_id(0) >= n_lat_tiles
    x = jnp.where(is_ctx, c_ref[...], x_ref[...])
    o_ref[...] = _rms_mod(x, g_ref[...], mod_ref[0, 0:1, :], mod_ref[0, 1:2, :]).astype(o_ref.dtype)


def _prenorm(x, ctx, g, mod):
    s, d = x.shape
    c = ctx.shape[0]
    tm = PRENORM_TM
    nl, nc = s // tm, c // tm
    return pl.pallas_call(
        functools.partial(_prenorm_kernel, n_lat_tiles=nl),
        grid=(nl + nc,),
        in_specs=[pl.BlockSpec((tm, d), lambda i: (jnp.minimum(i, nl - 1), 0)),
                  pl.BlockSpec((tm, d), lambda i: (jnp.maximum(i - nl, 0), 0)),
                  pl.BlockSpec((1, d), lambda i: (0, 0)),
                  pl.BlockSpec((1, N_MOD, d), lambda i: (i // nl, 0, 0))],
        out_specs=pl.BlockSpec((tm, d), lambda i: (i, 0)),
        out_shape=jax.ShapeDtypeStruct((s + c, d), BF16),
        compiler_params=_cparams(("arbitrary",)),
        name="prenorm_mix",
    )(x, ctx, g.reshape(1, d), mod)


def _inproj_kernel(h_ref, w_ref, gain_ref, c_ref, sa_ref, sb_ref, o_ref):
    j = pl.program_id(1)
    tm = h_ref.shape[0]
    nh = o_ref.shape[1] // HEAD_DIM
    n_chunks = next(n for n in (6, 3, 2, 1) if tm % (16 * n) == 0)

    def tile(n_normed, chunks):
        w = w_ref[...].astype(h_ref.dtype)
        rows = tm // chunks
        for r in range(chunks):
            rs = slice(r * rows, (r + 1) * rows)
            acc = jnp.dot(h_ref[rs, :], w, preferred_element_type=F32)
            for hd in range(nh):
                sl = slice(hd * HEAD_DIM, (hd + 1) * HEAD_DIM)
                a = acc[:, sl]
                if hd < n_normed:
                    y = a * lax.rsqrt(jnp.mean(a * a, axis=-1, keepdims=True) + EPS) * gain_ref[0, :, sl]
                    a = (y * c_ref[rs, :] + pltpu.roll(y, ROPE_PAIRS, 1) * sa_ref[rs, :]
                         + pltpu.roll(y, HEAD_DIM - ROPE_PAIRS, 1) * sb_ref[rs, :])
                o_ref[rs, sl] = a.astype(o_ref.dtype)

    all_normed = (j < 2) | ((j >= 3) & (j < 7))
    pl.when(all_normed)(lambda: tile(nh, n_chunks))
    pl.when(j == 2)(lambda: tile(A_KV_HEADS, n_chunks))
    pl.when(j >= 7)(lambda: tile(0, n_chunks))


def _inproj(h, w, gains, rope_c, rope_sa, rope_sb):
    t, d = h.shape
    n = w.shape[1]
    tm = t // INPROJ_ROW_TILES
    tn = PROJ_TN
    return pl.pallas_call(
        _inproj_kernel,
        grid=(t // tm, n // tn),
        in_specs=[pl.BlockSpec((tm, d), lambda i, j: (i, 0)),
                  pl.BlockSpec((d, tn), lambda i, j: (0, j)),
                  pl.BlockSpec((1, 1, tn), lambda i, j: (j, 0, 0)),
                  pl.BlockSpec((tm, HEAD_DIM), lambda i, j: (i, 0)),
                  pl.BlockSpec((tm, HEAD_DIM), lambda i, j: (i, 0)),
                  pl.BlockSpec((tm, HEAD_DIM), lambda i, j: (i, 0))],
        out_specs=pl.BlockSpec((tm, tn), lambda i, j: (i, j)),
        out_shape=jax.ShapeDtypeStruct((t, n), BF16),
        compiler_params=_cparams(("arbitrary", "arbitrary")),
        name="inproj",
    )(h, w, gains, rope_c, rope_sa, rope_sb)


def _rope_tables(s, c):
    rows_n = s // GRID_W
    inv = ROPE_THETA ** (-jnp.arange(ROPE_PAIRS, dtype=F32) / ROPE_PAIRS)
    ang_r = jnp.arange(rows_n, dtype=F32)[:, None] * inv
    ang_c = jnp.arange(GRID_W, dtype=F32)[:, None] * inv
    cr, sr, cc, sc = jnp.cos(ang_r), jnp.sin(ang_r), jnp.cos(ang_c), jnp.sin(ang_c)
    zr, zc = jnp.zeros_like(sr), jnp.zeros_like(sc)

    def table(row_parts, col_parts, ctx_value):
        by_row = jnp.concatenate(row_parts + [zr, zr], axis=1)
        by_col = jnp.concatenate([zc, zc] + col_parts, axis=1)
        lat = (by_row[:, None, :] + by_col[None, :, :]).reshape(s, HEAD_DIM)
        return jnp.concatenate([lat, jnp.full((c, HEAD_DIM), ctx_value, F32)], axis=0)

    return (table([cr, cr], [cc, cc], 1.0), table([zr, sr], [zc, sc], 0.0), table([-sr, zr], [-sc, zc], 0.0))


def _head_gains(qn_a, kn_a, qn_b, kn_b):
    qs = HEAD_DIM ** -0.5 * LOG2E
    one = jnp.ones((HEAD_DIM,), F32)
    heads = ([qn_a * qs] * A_HEADS + [kn_a] * A_KV_HEADS + [one] * A_KV_HEADS
             + [qn_b * qs] * (2 * B_HEADS) + [kn_b] * (2 * B_HEADS))
    heads = heads + [one] * (IN_W // HEAD_DIM - len(heads))
    return jnp.concatenate(heads).reshape(IN_W // PROJ_TN, 1, PROJ_TN)


def _build_vt(v_ref, cols, vt_ref, tk):
    n_chunks, rows, _ = vt_ref.shape
    dv = rows - ONES_ROWS
    tail = (lax.broadcasted_iota(jnp.int32, (ONES_ROWS, tk), 0) == 0).astype(vt_ref.dtype)
    for c in range(n_chunks):
        vt_ref[c, 0:dv, :] = v_ref[c * tk:(c + 1) * tk, cols].astype(F32).T.astype(vt_ref.dtype)
        vt_ref[c, dv:rows, :] = tail


def _attend(qs, k_ref, k_cols, vts, s_ref, m_ref, acc_ref):
    n_chunks, _, tk = vts[0].shape
    n_st = len(qs)
    m_ref[...] = jnp.full(m_ref.shape, -jnp.inf, F32)
    acc_ref[...] = jnp.zeros(acc_ref.shape, F32)

    def scores(i, c, slot):
        off = c * tk if isinstance(c, int) else pl.multiple_of(c * tk, tk)
        s_ref[i, slot] = lax.dot_general(k_ref[pl.ds(off, tk), k_cols[i]], qs[i], (((1,), (1,)), ((), ())),
                                         preferred_element_type=F32)

    def update(i, c, slot):
        s = s_ref[i, slot]
        m_old = m_ref[i]
        m_new = jnp.maximum(m_old, jnp.max(s, axis=0, keepdims=True))
        p = jnp.exp2(s - m_new).astype(vts[i].dtype)
        acc_ref[i] = (acc_ref[i] * jnp.exp2(m_old - m_new)
                      + jnp.dot(vts[i][c], p, preferred_element_type=F32))
        m_ref[i] = m_new

    for i in range(n_st):
        scores(i, 0, 0)

    def pair(j, carry):
        c = 2 * j
        for i in range(n_st):
            scores(i, c + 1, 1)
        for i in range(n_st):
            update(i, c, 0)
        for i in range(n_st):
            scores(i, c + 2, 0)
        for i in range(n_st):
            update(i, c + 1, 1)
        return carry

    n_pairs = (n_chunks - 1) // 2
    lax.fori_loop(0, n_pairs, pair, 0)
    done = 2 * n_pairs
    if n_chunks - done == 2:
        for i in range(n_st):
            scores(i, done + 1, 1)
    for i in range(n_st):
        update(i, done, 0)
    if n_chunks - done == 2:
        for i in range(n_st):
            update(i, done + 1, 1)


def _gqa_kernel(q_ref, k_ref, v_ref, o_ref, vt_ref, s_ref, m_ref, acc_ref):
    tk = vt_ref.shape[2]
    n_st = s_ref.shape[0]
    pl.when((pl.program_id(1) == 0) & (pl.program_id(2) == 0))(
        lambda: _build_vt(v_ref, slice(None), vt_ref, tk))
    qs = [q_ref[:, i * HEAD_DIM:(i + 1) * HEAD_DIM] for i in range(n_st)]
    _attend(qs, k_ref, [slice(None)] * n_st, [vt_ref] * n_st, s_ref, m_ref, acc_ref)
    for i in range(n_st):
        o_t = acc_ref[i, 0:HEAD_DIM, :] / acc_ref[i, HEAD_DIM:HEAD_DIM + 1, :]
        o_ref[:, i * HEAD_DIM:(i + 1) * HEAD_DIM] = o_t.T.astype(o_ref.dtype)


def _key_chunk(t, largest=1408):
    for tk in (1408, 768, 1024, 512, 640, 384, 256, 128):
        if tk <= largest and t % tk == 0:
            return tk
    raise ValueError(f"unsupported key count {t}")


def _gqa(proj, s):
    t = proj.shape[0]
    tq = ATTN_TQ
    tk = _key_chunk(t)
    n_st = GQA_STREAMS
    per_g = A_GROUP // n_st
    return pl.pallas_call(
        _gqa_kernel,
        grid=(A_KV_HEADS, per_g, s // tq),
        in_specs=[pl.BlockSpec((tq, n_st * HEAD_DIM), lambda g, hh, i: (i, COL_AQ // n_st + g * per_g + hh)),
                  pl.BlockSpec((t, HEAD_DIM), lambda g, hh, i: (0, COL_AK + g)),
                  pl.BlockSpec((t, HEAD_DIM), lambda g, hh, i: (0, COL_AV + g))],
        out_specs=pl.BlockSpec((tq, n_st * HEAD_DIM), lambda g, hh, i: (i, g * per_g + hh)),
        out_shape=jax.ShapeDtypeStruct((s, A_Q_W), BF16),
        scratch_shapes=[pltpu.VMEM((t // tk, HEAD_DIM + ONES_ROWS, tk), BF16),
                        pltpu.VMEM((n_st, 2, tk, tq), F32),
                        pltpu.VMEM((n_st, 1, tq), F32),
                        pltpu.VMEM((n_st, HEAD_DIM + ONES_ROWS, tq), F32)],
        compiler_params=_cparams(("arbitrary", "arbitrary", "arbitrary")),
        name="gqa_attn",
    )(proj, proj, proj)


def _diff_kernel(lam_ref, q_ref, k_ref, v_ref, g_ref, o_ref, vt_ref, s_ref, m_ref, acc_ref, *, lam_init):
    n_heads, _, _, tk = vt_ref.shape
    n_st = 2 * n_heads
    head_cols = lambda i: slice(i * HEAD_DIM, (i + 1) * HEAD_DIM)
    v_cols = lambda hd: slice(hd * B_V_DIM, (hd + 1) * B_V_DIM)

    @pl.when(pl.program_id(1) == 0)
    def _():
        for hd in range(n_heads):
            _build_vt(v_ref, v_cols(hd), vt_ref.at[hd], tk)

    lv = lam_ref[...]
    lam = (jnp.exp(jnp.sum(lv[0:1, :] * lv[1:2, :], axis=-1, keepdims=True))
           - jnp.exp(jnp.sum(lv[2:3, :] * lv[3:4, :], axis=-1, keepdims=True)) + lam_init)
    qs = [q_ref[:, head_cols(i)] for i in range(n_st)]
    _attend(qs, k_ref, [head_cols(i) for i in range(n_st)], [vt_ref.at[i // 2] for i in range(n_st)],
            s_ref, m_ref, acc_ref)
    for hd in range(n_heads):
        a0, a1 = acc_ref.at[2 * hd], acc_ref.at[2 * hd + 1]
        o_t = (a0[0:B_V_DIM, :] / a0[B_V_DIM:B_V_DIM + 1, :]
               - lam * (a1[0:B_V_DIM, :] / a1[B_V_DIM:B_V_DIM + 1, :]))
        o = o_t.T
        y = o * lax.rsqrt(jnp.mean(o * o, axis=-1, keepdims=True) + EPS) * g_ref[...]
        o_ref[:, v_cols(hd)] = (y * (1.0 - lam_init)).astype(o_ref.dtype)


def _diff(proj, s, lam_vecs, subln_g, lam_init):
    t = proj.shape[0]
    tq = ATTN_TQ
    nh = DIFF_HEADS
    tk = _key_chunk(t, largest=768)
    n_st = 2 * nh
    qk_w = n_st * HEAD_DIM
    v_w = nh * B_V_DIM
    resident = lambda w, col0: pl.BlockSpec((t, w), lambda h, i: (0, col0 * HEAD_DIM // w + h),
                                            pipeline_mode=pl.Buffered(1))
    return pl.pallas_call(
        functools.partial(_diff_kernel, lam_init=lam_init),
        grid=(B_HEADS // nh, s // tq),
        in_specs=[pl.BlockSpec((4, HEAD_DIM), lambda h, i: (0, 0)),
                  pl.BlockSpec((tq, qk_w), lambda h, i: (i, COL_BQ * HEAD_DIM // qk_w + h)),
                  resident(qk_w, COL_BK),
                  resident(v_w, COL_BV),
                  pl.BlockSpec((1, B_V_DIM), lambda h, i: (0, 0))],
        out_specs=pl.BlockSpec((tq, v_w), lambda h, i: (i, h)),
        out_shape=jax.ShapeDtypeStruct((s, B_V_W), BF16),
        scratch_shapes=[pltpu.VMEM((nh, t // tk, B_V_DIM + ONES_ROWS, tk), BF16),
                        pltpu.VMEM((n_st, 2, tk, tq), F32),
                        pltpu.VMEM((n_st, 1, tq), F32),
                        pltpu.VMEM((n_st, B_V_DIM + ONES_ROWS, tq), F32)],
        compiler_params=_cparams(("arbitrary", "arbitrary")),
        name="diff_attn",
    )(lam_vecs, proj, proj, proj, subln_g.reshape(1, B_V_DIM))


def _merge_kernel(oa_ref, ob_ref, *refs):
    nc = (len(refs) - 6) // 2
    ga_refs, gb_refs = refs[:nc], refs[nc:2 * nc]
    wa_ref, wb_ref, wo_ref, x_ref, gate_ref, o_ref = refs[2 * nc:]
    tn = ga_refs[0].shape[1]
    ya = jnp.dot(oa_ref[...], wa_ref[...], preferred_element_type=F32)
    yb = jnp.dot(ob_ref[...], wb_ref[...], preferred_element_type=F32)
    parts = []
    for c in range(nc):
        sl = slice(c * tn, (c + 1) * tn)
        t = (jax.nn.sigmoid(ga_refs[c][...].astype(F32)) * ya[:, sl]
             + jax.nn.sigmoid(gb_refs[c][...].astype(F32)) * yb[:, sl])
        parts.append(t.astype(wo_ref.dtype))
    y = jnp.dot(jnp.concatenate(parts, axis=1), wo_ref[...], preferred_element_type=F32)
    o_ref[...] = x_ref[...] + gate_ref[...] * y


def _merge(oa, ob, proj, wa, wb, wo, x, gate):
    s, d = x.shape
    tm = MERGE_TM
    tn = PROJ_TN
    nc = d // tn
    ga0 = COL_GA * HEAD_DIM // tn
    gb0 = COL_GB * HEAD_DIM // tn
    resident = lambda shape: pl.BlockSpec(shape, lambda i: (0, 0), pipeline_mode=pl.Buffered(1))
    gate_specs = [pl.BlockSpec((tm, tn), lambda i, c=c0 + c: (i, c)) for c0 in (ga0, gb0) for c in range(nc)]
    return pl.pallas_call(
        _merge_kernel,
        grid=(s // tm,),
        in_specs=[pl.BlockSpec((tm, A_Q_W), lambda i: (i, 0)),
                  pl.BlockSpec((tm, B_V_W), lambda i: (i, 0)),
                  *gate_specs,
                  resident((A_Q_W, d)), resident((B_V_W, d)), resident((d, d)),
                  pl.BlockSpec((tm, d), lambda i: (i, 0)),
                  pl.BlockSpec((1, d), lambda i: (0, 0))],
        out_specs=pl.BlockSpec((tm, d), lambda i: (i, 0)),
        out_shape=jax.ShapeDtypeStruct((s, d), F32),
        compiler_params=_cparams(("arbitrary",)),
        name="merge_out",
    )(oa, ob, *([proj] * (2 * nc)), wa, wb, wo, x, gate)


def _store_packed_tiles(ref, x):
    rows = x.shape[0]
    bits = lax.bitcast_convert_type(x.astype(BF16).astype(F32), jnp.uint32)
    half = bits.shape[1] // 2
    words = (bits[:, :half] >> 16) | (bits[:, half:] & jnp.uint32(0xFFFF0000))
    for c in range(PACK_TILES):
        ref[pl.ds(c, rows, stride=PACK_TILES), :] = words[:, c * LANES:(c + 1) * LANES]


def _lane_max(x):
    return jnp.max(x, axis=-1, keepdims=True)


def _lane_min(x):
    return jnp.min(x, axis=-1, keepdims=True)


def _group_allreduce(x, lane, op):
    for sft in (1, 2, 4):
        up = pltpu.roll(x, sft, 1)
        dn = pltpu.roll(x, LANES - sft, 1)
        x = op(x, jnp.where((lane & sft) != 0, up, dn))
    return x


def _router_kernel(x_ref, g_ref, mod_ref, wr_ref, rb_ref,
                   h_ref, hp_ref, idx_ref, wsel_ref, rank_ref, cnt_ref, carry_ref):
    @pl.when(pl.program_id(0) == 0)
    def _():
        carry_ref[...] = jnp.zeros_like(carry_ref)

    h = _rms_mod(x_ref[...], g_ref[...], mod_ref[0, 3:4, :], mod_ref[0, 4:5, :])
    hb = h.astype(BF16)
    h_ref[...] = hb
    _store_packed_tiles(hp_ref, h)
    tm = h.shape[0]

    h_lo = (h - hb.astype(F32)).astype(BF16)
    logits = (jnp.dot(hb, wr_ref[0], preferred_element_type=F32)
              + jnp.dot(hb, wr_ref[1], preferred_element_type=F32)
              + jnp.dot(h_lo, wr_ref[0], preferred_element_type=F32))
    scores = jax.nn.sigmoid(logits)
    lane = lax.broadcasted_iota(jnp.int32, (tm, LANES), 1)
    lane_f = lane.astype(F32)
    gid_f = (lane >> 3).astype(F32)
    valid = lane < N_EXPERTS
    neg = jnp.float32(-jnp.inf)
    big = jnp.float32(LANES)
    biased = jnp.where(valid, scores + rb_ref[...], neg)

    m1 = _group_allreduce(biased, lane, jnp.maximum)
    a1 = _group_allreduce(jnp.where(biased == m1, lane_f, big), lane, jnp.minimum)
    m2 = _group_allreduce(jnp.where(lane_f == a1, neg, biased), lane, jnp.maximum)
    gscore = jnp.where(valid, m1 + m2, neg)
    keep = jnp.zeros((tm, LANES), jnp.bool_)
    for _ in range(TOPK_GROUPS):
        best = _lane_max(gscore)
        gsel = _lane_min(jnp.where(gscore == best, gid_f, big))
        hit = gid_f == gsel
        keep = keep | hit
        gscore = jnp.where(hit, neg, gscore)
    cand = jnp.where(keep & valid, biased, neg)

    onehot = jnp.zeros((tm, LANES), F32)
    idx_out = jnp.zeros((tm, LANES), F32)
    w_out = jnp.zeros((tm, LANES), F32)
    sels = []
    for k in range(TOP_K):
        best = _lane_max(cand)
        sel = _lane_min(jnp.where(cand == best, lane_f, big))
        hit = lane_f == sel
        wk = jnp.sum(jnp.where(hit, scores, 0.0), axis=-1, keepdims=True)
        cand = jnp.where(hit, neg, cand)
        onehot = jnp.where(hit, 1.0, onehot)
        idx_out = jnp.where(lane == k, sel, idx_out)
        w_out = jnp.where(lane == k, wk, w_out)
        sels.append(hit)
    wsum = jnp.sum(w_out, axis=-1, keepdims=True)
    wsel_ref[...] = w_out / wsum * ROUTED_SCALE
    idx_ref[...] = idx_out.astype(jnp.int32)

    row = lax.broadcasted_iota(jnp.int32, (tm, tm), 0)
    col = lax.broadcasted_iota(jnp.int32, (tm, tm), 1)
    lower = (col < row).astype(BF16)
    before = jnp.dot(lower, onehot.astype(BF16), preferred_element_type=F32) + carry_ref[...]
    rank_out = jnp.zeros((tm, LANES), F32)
    for k in range(TOP_K):
        rk = jnp.sum(jnp.where(sels[k], before, 0.0), axis=-1, keepdims=True)
        rank_out = jnp.where(lane == k, rk, rank_out)
    rank_ref[...] = rank_out.astype(jnp.int32)
    carry_ref[...] = carry_ref[...] + jnp.sum(onehot, axis=0, keepdims=True)
    cnt_ref[...] = carry_ref[...]


def _router(x1, g, mod, w_router, router_bias):
    s, d = x1.shape
    tm = ROUTER_TM
    e = w_router.shape[1]
    wr = jnp.pad(w_router, ((0, 0), (0, LANES - e)))
    wr_hi = wr.astype(BF16)
    wr = jnp.stack([wr_hi, (wr - wr_hi.astype(F32)).astype(BF16)])
    rb = jnp.pad(router_bias, (0, LANES - e)).reshape(1, LANES)
    row_spec = lambda w: pl.BlockSpec((tm, w), lambda i: (i, 0))
    return pl.pallas_call(
        _router_kernel,
        grid=(s // tm,),
        in_specs=[row_spec(d),
                  pl.BlockSpec((1, d), lambda i: (0, 0)),
                  pl.BlockSpec((1, N_MOD, d), lambda i: (0, 0, 0)),
                  pl.BlockSpec((2, d, LANES), lambda i: (0, 0, 0)),
                  pl.BlockSpec((1, LANES), lambda i: (0, 0))],
        out_specs=[row_spec(d), pl.BlockSpec((tm * PACK_TILES, LANES), lambda i: (i, 0)),
                   row_spec(LANES), row_spec(LANES), row_spec(LANES),
                   pl.BlockSpec((1, LANES), lambda i: (0, 0))],
        out_shape=[jax.ShapeDtypeStruct((s, d), BF16),
                   jax.ShapeDtypeStruct((s * PACK_TILES, LANES), jnp.uint32),
                   jax.ShapeDtypeStruct((s, LANES), jnp.int32),
                   jax.ShapeDtypeStruct((s, LANES), F32),
                   jax.ShapeDtypeStruct((s, LANES), jnp.int32),
                   jax.ShapeDtypeStruct((1, LANES), F32)],
        scratch_shapes=[pltpu.VMEM((1, LANES), F32)],
        compiler_params=_cparams(("arbitrary",)),
        name="ffn_router",
    )(x1, g.reshape(1, d), mod, wr, rb)


def _unpack_halves(words):
    lo = lax.bitcast_convert_type(words << 16, F32)
    hi = lax.bitcast_convert_type(words & jnp.uint32(0xFFFF0000), F32)
    return lo, hi


def _unpack_rows(words):
    lo, hi = _unpack_halves(words)
    return jnp.concatenate([lo.astype(BF16), hi.astype(BF16)], axis=1)


def _token_rows(row, tiles):
    return pl.ds(pl.multiple_of(row * tiles, tiles), tiles)


def _dispatch_kernel(ps_ref, pe_ref, nused_ref, dest_hbm, hp_ref, xs_hbm,
                     dest_smem, zbuf, sem_i, sem_z, sem_x, *, n_blocks):
    i = pl.program_id(0)
    n = dest_smem.shape[0]
    tm = n // TOP_K
    copy_dest = pltpu.make_async_copy(dest_hbm.at[i], dest_smem, sem_i)
    copy_dest.start()

    @pl.when(i == 0)
    def _():
        zbuf[...] = jnp.zeros_like(zbuf)

        def zero_block(row0):
            span = MOE_BLOCK * PACK_TILES
            return pltpu.make_async_copy(
                zbuf, xs_hbm.at[pl.ds(pl.multiple_of(row0 * PACK_TILES, span), span), :], sem_z)

        def per_expert(action):
            def body(e, carry):
                @pl.when(pe_ref[e] > ps_ref[e])
                def _():
                    action(zero_block(pe_ref[e] - MOE_BLOCK))
                return carry
            lax.fori_loop(0, N_EXPERTS, body, 0)

        def per_tail(action):
            def body(b, carry):
                action(zero_block(b * MOE_BLOCK))
                return carry
            lax.fori_loop(nused_ref[0], n_blocks, body, 0)

        per_expert(lambda cp: cp.start())
        per_tail(lambda cp: cp.start())
        per_expert(lambda cp: cp.wait())
        per_tail(lambda cp: cp.wait())

    copy_dest.wait()

    def issue(t, carry):
        for k in range(TOP_K):
            dest = dest_smem[t * TOP_K + k]
            pltpu.make_async_copy(hp_ref.at[_token_rows(t, PACK_TILES), :],
                                  xs_hbm.at[_token_rows(dest, PACK_TILES), :], sem_x).start(priority=k % 2)
        return carry

    lax.fori_loop(0, tm, issue, 0)
    for _ in range(TOP_K):
        pltpu.make_async_copy(hp_ref, xs_hbm.at[pl.ds(0, tm * PACK_TILES), :], sem_x).wait()


def _dispatch(pad_start, pad_end, n_used, dest_tm, hp, n_blocks):
    n_tiles, n = dest_tm.shape
    tm = n // TOP_K
    grid_spec = pltpu.PrefetchScalarGridSpec(
        num_scalar_prefetch=3,
        grid=(n_tiles,),
        in_specs=[pl.BlockSpec(memory_space=pl.ANY),
                  pl.BlockSpec((tm * PACK_TILES, LANES), lambda i, ps, pe, nu: (i, 0))],
        out_specs=pl.BlockSpec(memory_space=pl.ANY),
        scratch_shapes=[pltpu.SMEM((n,), jnp.int32),
                        pltpu.VMEM((MOE_BLOCK * PACK_TILES, LANES), jnp.uint32),
                        pltpu.SemaphoreType.DMA,
                        pltpu.SemaphoreType.DMA,
                        pltpu.SemaphoreType.DMA])
    return pl.pallas_call(
        functools.partial(_dispatch_kernel, n_blocks=n_blocks),
        grid_spec=grid_spec,
        out_shape=jax.ShapeDtypeStruct((n_blocks * MOE_BLOCK * PACK_TILES, LANES), jnp.uint32),
        compiler_params=_cparams(("arbitrary",)),
        name="moe_dispatch",
    )(pad_start, pad_end, n_used, dest_tm, hp)


def _moe_kernel(nblk_ref, blk0_ref, nxt_ref, slot_ref, first_ref, nused_ref,
                xs_hbm, wg_hbm, wu_hbm, wd_hbm, y_hbm,
                xbuf, ybuf, wg32, wu32, wd32, wgb, wub, wdb, sem_w, sem_x, sem_y, *, n_blocks):
    e = pl.program_id(0)
    nb = nblk_ref[e]
    span = MOE_BLOCK * PACK_TILES

    def block_rows(b):
        return pl.ds(pl.multiple_of(b * span, span), span)

    def fetch(ex, slot):
        return (pltpu.make_async_copy(wg_hbm.at[ex], wg32.at[slot], sem_w.at[slot, 0]),
                pltpu.make_async_copy(wu_hbm.at[ex], wu32.at[slot], sem_w.at[slot, 1]),
                pltpu.make_async_copy(wd_hbm.at[ex], wd32.at[slot], sem_w.at[slot, 2]))

    def x_copy(b, s):
        return pltpu.make_async_copy(xs_hbm.at[block_rows(b), :], xbuf.at[s], sem_x.at[s])

    def y_copy(b, s):
        return pltpu.make_async_copy(ybuf.at[s], y_hbm.at[block_rows(b), :], sem_y.at[s])

    n_used = nused_ref[0]
    n_xbuf = xbuf.shape[0]

    @pl.when(nb > 0)
    def _():
        b0 = blk0_ref[e]
        slot = slot_ref[e]

        @pl.when(e == first_ref[0])
        def _():
            for b in range(n_xbuf - 1):
                pl.when(b < n_used)(lambda b=b: x_copy(b, b).start())
            for cp in fetch(e, 0):
                cp.start(priority=1)

        for cp in fetch(e, slot):
            cp.wait()

        @pl.when(nxt_ref[e] >= 0)
        def _():
            for cp in fetch(nxt_ref[e], 1 - slot):
                cp.start(priority=1)

        wgb[...] = wg32[slot].astype(BF16)
        wub[...] = wu32[slot].astype(BF16)
        wdb[...] = wd32[slot].astype(BF16)

        def body(j, carry):
            b = b0 + j
            s = b % 2
            sx = b % n_xbuf
            ahead = b + n_xbuf - 1
            x_copy(b, sx).wait()
            pl.when(ahead < n_used)(lambda: x_copy(ahead, ahead % n_xbuf).start())
            pl.when(b >= 2)(lambda: y_copy(b - 2, s).wait())
            xb = xbuf.at[sx]
            words = jnp.concatenate(
                [xb[pl.ds(c, MOE_BLOCK, stride=PACK_TILES), :] for c in range(PACK_TILES)], axis=1)
            x = _unpack_rows(words)
            gate = jnp.dot(x, wgb[...], preferred_element_type=F32)
            up = jnp.dot(x, wub[...], preferred_element_type=F32)
            act = (gate * jax.nn.sigmoid(gate) * up).astype(BF16)
            _store_packed_tiles(ybuf.at[s], jnp.dot(act, wdb[...], preferred_element_type=F32))
            y_copy(b, s).start()
            return carry

        lax.fori_loop(0, nb, body, 0)

    @pl.when(e == pl.num_programs(0) - 1)
    def _():
        pl.when(n_used >= 2)(lambda: y_copy(n_used - 2, n_used % 2).wait())
        pl.when(n_used >= 1)(lambda: y_copy(n_used - 1, (n_used - 1) % 2).wait())
        ybuf[0] = jnp.zeros(ybuf.shape[1:], ybuf.dtype)

        def tail(action):
            def step(b, carry):
                action(y_copy(b, 0))
                return carry
            lax.fori_loop(nused_ref[0], n_blocks, step, 0)

        tail(lambda cp: cp.start())
        tail(lambda cp: cp.wait())


def _moe(nblk, blk0, nxt, slot, first, n_used, xs, weg, weu, wed, n_blocks):
    n_exp, d, f = weg.shape
    span = MOE_BLOCK * PACK_TILES
    grid_spec = pltpu.PrefetchScalarGridSpec(
        num_scalar_prefetch=6,
        grid=(n_exp,),
        in_specs=[pl.BlockSpec(memory_space=pl.ANY)] * 4,
        out_specs=pl.BlockSpec(memory_space=pl.ANY),
        scratch_shapes=[pltpu.VMEM((3, span, LANES), jnp.uint32), pltpu.VMEM((2, span, LANES), jnp.uint32),
                        pltpu.VMEM((2, d, f), F32), pltpu.VMEM((2, d, f), F32), pltpu.VMEM((2, f, d), F32),
                        pltpu.VMEM((d, f), BF16), pltpu.VMEM((d, f), BF16), pltpu.VMEM((f, d), BF16),
                        pltpu.SemaphoreType.DMA((2, 3)), pltpu.SemaphoreType.DMA((3,)),
                        pltpu.SemaphoreType.DMA((2,))])
    return pl.pallas_call(
        functools.partial(_moe_kernel, n_blocks=n_blocks),
        grid_spec=grid_spec,
        out_shape=jax.ShapeDtypeStruct((n_blocks * span, LANES), jnp.uint32),
        compiler_params=_cparams(("arbitrary",)),
        name="moe_experts",
    )(nblk, blk0, nxt, slot, first, n_used, xs, weg, weu, wed)


def _combine_kernel(dest_hbm, y_hbm, w_ref, h_ref, wg_ref, wu_ref, wd_ref, x_ref, gate_ref, o_ref,
                    dest0, dest1, ybuf0, ybuf1, sem_i, sem_y):
    i = pl.program_id(0)
    tm = h_ref.shape[0]
    n = tm * TOP_K
    span = n * PACK_TILES
    dests = (dest0, dest1)
    ybufs = (ybuf0, ybuf1)

    def start_tile(tile, slot):
        copy_dest = pltpu.make_async_copy(dest_hbm.at[tile], dests[slot], sem_i)
        copy_dest.start()
        copy_dest.wait()

        def issue(t, carry):
            base = t * TOKEN_PITCH
            for k in range(TOP_K):
                src = dests[slot][t * TOP_K + k]
                pltpu.make_async_copy(
                    y_hbm.at[_token_rows(src, PACK_TILES), :],
                    ybufs[slot].at[pl.ds(pl.multiple_of(base + k * YBUF_PITCH, 4), PACK_TILES), :],
                    sem_y.at[slot]).start(priority=k % 2)
            return carry

        lax.fori_loop(0, tm, issue, 0)

    def step(cur):
        nxt = 1 - cur
        pl.when(i == 0)(lambda: start_tile(0, cur))
        pl.when(i + 1 < pl.num_programs(0))(lambda: start_tile(i + 1, nxt))

        h = h_ref[...]
        gate = jnp.dot(h, wg_ref[...], preferred_element_type=F32)
        up = jnp.dot(h, wu_ref[...], preferred_element_type=F32)
        act = (gate * jax.nn.sigmoid(gate) * up).astype(BF16)
        shared = jnp.dot(act, wd_ref[...], preferred_element_type=F32)

        yb = ybufs[cur]
        pltpu.make_async_copy(y_hbm.at[pl.ds(0, span), :], yb.at[pl.ds(0, span), :], sem_y.at[cur]).wait()
        wks = [jnp.broadcast_to(w_ref[:, k:k + 1], (tm, LANES)) for k in range(TOP_K)]
        lows, highs = [], []
        for c in range(PACK_TILES):
            acc_lo = acc_hi = None
            for k in range(TOP_K):
                lo, hi = _unpack_halves(yb[pl.ds(k * YBUF_PITCH + c, tm, stride=TOKEN_PITCH), :])
                acc_lo = lo * wks[k] if acc_lo is None else acc_lo + lo * wks[k]
                acc_hi = hi * wks[k] if acc_hi is None else acc_hi + hi * wks[k]
            lows.append(acc_lo)
            highs.append(acc_hi)
        routed = jnp.concatenate(lows + highs, axis=1)
        o_ref[...] = x_ref[...] + gate_ref[...] * (routed + shared)

    pl.when(i % 2 == 0)(lambda: step(0))
    pl.when(i % 2 == 1)(lambda: step(1))


def _combine(dest_tm, y_sorted, wsel, h2, wsg, wsu, wsd, x1, gate):
    s, d = x1.shape
    n_tiles, n = dest_tm.shape
    tm = n // TOP_K
    f = wsg.shape[1]
    return pl.pallas_call(
        _combine_kernel,
        grid=(n_tiles,),
        in_specs=[pl.BlockSpec(memory_space=pl.ANY),
                  pl.BlockSpec(memory_space=pl.ANY),
                  pl.BlockSpec((tm, LANES), lambda i: (i, 0)),
                  pl.BlockSpec((tm, d), lambda i: (i, 0)),
                  pl.BlockSpec((d, f), lambda i: (0, 0)),
                  pl.BlockSpec((d, f), lambda i: (0, 0)),
                  pl.BlockSpec((f, d), lambda i: (0, 0)),
                  pl.BlockSpec((tm, d), lambda i: (i, 0)),
                  pl.BlockSpec((1, d), lambda i: (0, 0))],
        out_specs=pl.BlockSpec((tm, d), lambda i: (i, 0)),
        out_shape=jax.ShapeDtypeStruct((s, d), F32),
        scratch_shapes=[pltpu.SMEM((n,), jnp.int32),
                        pltpu.SMEM((n,), jnp.int32),
                        pltpu.VMEM((tm * TOKEN_PITCH, LANES), jnp.uint32),
                        pltpu.VMEM((tm * TOKEN_PITCH, LANES), jnp.uint32),
                        pltpu.SemaphoreType.DMA,
                        pltpu.SemaphoreType.DMA((2,))],
        compiler_params=_cparams(("arbitrary",)),
        name="moe_combine",
    )(dest_tm, y_sorted, wsel, h2, wsg, wsu, wsd, x1, gate)


def _expert_tables(counts):
    i32 = jnp.int32
    padded = ((counts + MOE_BLOCK - 1) // MOE_BLOCK * MOE_BLOCK).astype(i32)
    pad_end = jnp.cumsum(padded).astype(i32)
    pad_start = pad_end - padded
    ids = jnp.arange(N_EXPERTS, dtype=i32)
    busy = padded > 0
    later_busy = busy[None, :] & (ids[None, :] > ids[:, None])
    nxt = jnp.min(jnp.where(later_busy, ids[None, :], N_EXPERTS), axis=1)
    nxt = jnp.where(nxt < N_EXPERTS, nxt, -1).astype(i32)
    slot = (jnp.maximum(jnp.cumsum(busy.astype(i32)) - 1, 0) % 2).astype(i32)
    first = jnp.min(jnp.where(busy, ids, N_EXPERTS)).astype(i32).reshape(1)
    n_used = (pad_end[-1] // MOE_BLOCK).reshape(1)
    return pad_start, pad_end, padded // MOE_BLOCK, pad_start // MOE_BLOCK, nxt, slot, first, n_used


def kernel(x, c, ctx, c_ctx, w_ada, b_ada, norm_mix, norm_ffn, w_in, q_norm_a, k_norm_a, q_norm_b, k_norm_b, lambda_q1, lambda_k1, lambda_q2, lambda_k2, subln_b, w_branch_a, w_branch_b, w_out, w_router, router_bias, w_exp_gate, w_exp_up, w_exp_down, w_sh_gate, w_sh_up, w_sh_down):
    depth = w_ada.shape[0]
    assert depth == 1 and x.shape[0] == 1 and ctx.shape[0] == 1
    s, d = x.shape[1], x.shape[2]
    n_ctx = ctx.shape[1]
    assert d == D_MODEL and s % GRID_W == 0
    assert s % PRENORM_TM == 0 and n_ctx % PRENORM_TM == 0 and s % ATTN_TQ == 0
    assert s % MERGE_TM == 0 and s % ROUTER_TM == 0 and (s + n_ctx) % (INPROJ_ROW_TILES * 16) == 0
    i = 0
    lam_init = 0.8 - 0.6 * math.exp(-0.3 * i)
    xs = x[0]

    mod = _adaln(jnp.concatenate([c, c_ctx[None, :]], axis=0), w_ada[i], b_ada[i]).reshape(2, N_MOD, d)

    h = _prenorm(xs, ctx[0], norm_mix[i], mod)
    tc, tsa, tsb = _rope_tables(s, n_ctx)
    gains = _head_gains(q_norm_a[i], k_norm_a[i], q_norm_b[i], k_norm_b[i])
    proj = _inproj(h, w_in[i], gains, tc, tsa, tsb)
    oa = _gqa(proj, s)
    lam_vecs = jnp.stack([lambda_q1[i], lambda_k1[i], lambda_q2[i], lambda_k2[i]]).astype(F32)
    ob = _diff(proj, s, lam_vecs, subln_b[i], lam_init)
    x1 = _merge(oa, ob, proj, w_branch_a[i].astype(BF16), w_branch_b[i].astype(BF16),
                w_out[i].astype(BF16), xs, mod[0, 2:3, :])

    h2, h2p, idx, wsel, rank, cnt = _router(x1, norm_ffn[i], mod[0:1], w_router[i], router_bias[i])
    counts = cnt[0, :N_EXPERTS].astype(jnp.int32)
    n_blocks = -(-(s * TOP_K) // MOE_BLOCK) + N_EXPERTS
    pad_start, pad_end, nblk, blk0, nxt, slot, first, n_used = _expert_tables(counts)
    tm_dispatch = math.gcd(s, DISPATCH_TM)
    tm_combine = math.gcd(s, COMBINE_TM)
    e_ids = jnp.arange(N_EXPERTS, dtype=jnp.int32)
    starts = jnp.sum(jnp.where(idx[:, :TOP_K, None] == e_ids, pad_start, 0), axis=-1)
    dest = (starts + rank[:, :TOP_K]).astype(jnp.int32)
    tiles = lambda tm: dest.reshape(s // tm, tm * TOP_K)
    xs = _dispatch(pad_start, pad_end, n_used, tiles(tm_dispatch), h2p, n_blocks)
    y_sorted = _moe(nblk, blk0, nxt, slot, first, n_used, xs,
                    w_exp_gate[i], w_exp_up[i], w_exp_down[i], n_blocks)
    out = _combine(tiles(tm_combine), y_sorted, wsel, h2,
                   w_sh_gate[i].astype(BF16), w_sh_up[i].astype(BF16), w_sh_down[i].astype(BF16),
                   x1, mod[0, 5:6, :])
    return out[None]
```

```python
import functools
import math

import jax
import jax.numpy as jnp
from jax import lax
from jax.experimental import pallas as pl
from jax.experimental.pallas import tpu as pltpu

F32 = jnp.float32
BF16 = jnp.bfloat16

D_MODEL = 2048
GRID_W = 64
HEAD_DIM = 128
ROPE_PAIRS = HEAD_DIM // 4
ROPE_THETA = 10000.0
A_HEADS = 8
A_KV_HEADS = 2
A_GROUP = A_HEADS // A_KV_HEADS
B_HEADS = 4
B_V_DIM = 2 * HEAD_DIM
N_EXPERTS = 64
TOP_K = 8
N_GROUPS = 8
TOPK_GROUPS = 4
EXPERT_DIM = 512
SHARED_DIM = 512
ROUTED_SCALE = 2.5
N_MOD = 6
EPS = 1e-6

A_Q_W = A_HEADS * HEAD_DIM
A_KV_W = A_KV_HEADS * HEAD_DIM
B_QK_W = B_HEADS * 2 * HEAD_DIM
B_V_W = B_HEADS * B_V_DIM
IN_W = A_Q_W + 2 * A_KV_W + 2 * B_QK_W + B_V_W + 2 * D_MODEL

COL_AQ = 0
COL_AK = A_Q_W // HEAD_DIM
COL_AV = COL_AK + A_KV_HEADS
COL_BQ = COL_AV + A_KV_HEADS
COL_BK = COL_BQ + 2 * B_HEADS
COL_BV = COL_BK + 2 * B_HEADS
COL_GA = COL_BV + B_V_W // HEAD_DIM
COL_GB = COL_GA + D_MODEL // HEAD_DIM

LANES = 128
SUBLANES = 8
VMEM_LIMIT = 56 * 1024 * 1024

ADALN_TN = 1536
PRENORM_TM = 256
INPROJ_ROW_TILES = 4
PROJ_TN = 512
ATTN_TQ = 512
GQA_STREAMS = 4
DIFF_HEADS = 2
MERGE_TM = 512
ROUTER_TM = 512
DISPATCH_TM = 2048
COMBINE_TM = 256
MOE_BLOCK = 256
ONES_ROWS = 16
PACK_TILES = D_MODEL // 2 // LANES
YBUF_PITCH = PACK_TILES + 4
TOKEN_PITCH = TOP_K * YBUF_PITCH + 4
LOG2E = 1.4426950408889634


def _cparams(sem, vmem=VMEM_LIMIT):
    return pltpu.CompilerParams(dimension_semantics=sem, vmem_limit_bytes=vmem)


def _adaln_kernel(cb_ref, w_ref, b_ref, o_ref):
    tn = w_ref.shape[1]
    nl = tn // LANES
    rows = 32

    def body(g, accs):
        accs = list(accs)
        r0 = pl.multiple_of(g * rows, rows)
        for u in range(rows // SUBLANES):
            r = r0 + u * SUBLANES
            w = w_ref[pl.ds(r, SUBLANES), :]
            for v in range(2):
                c = cb_ref[v, pl.ds(r, SUBLANES), :]
                s = c * jax.nn.sigmoid(c)
                for j in range(nl):
                    accs[v * nl + j] = accs[v * nl + j] + w[:, j * LANES:(j + 1) * LANES] * s
        return tuple(accs)

    init = tuple(jnp.zeros((SUBLANES, LANES), F32) for _ in range(2 * nl))
    accs = lax.fori_loop(0, w_ref.shape[0] // rows, body, init)
    for v in range(2):
        row = jnp.concatenate(
            [jnp.sum(accs[v * nl + j], axis=0, keepdims=True) for j in range(nl)], axis=1)
        o_ref[v:v + 1, :] = row + b_ref[...]


def _adaln(cvecs, w, b):
    d, n = w.shape
    tn = ADALN_TN
    cb = jnp.broadcast_to(cvecs[:, :, None], (2, d, LANES))
    return pl.pallas_call(
        _adaln_kernel,
        grid=(n // tn,),
        in_specs=[pl.BlockSpec((2, d, LANES), lambda j: (0, 0, 0)),
                  pl.BlockSpec((d, tn), lambda j: (0, j)),
                  pl.BlockSpec((1, tn), lambda j: (0, j))],
        out_specs=pl.BlockSpec((2, tn), lambda j: (0, j)),
        out_shape=jax.ShapeDtypeStruct((2, n), F32),
        compiler_params=_cparams(("arbitrary",)),
        name="adaln",
    )(cb, w, b.reshape(1, n))


def _rms_mod(x, g, shift, scale):
    y = x * lax.rsqrt(jnp.mean(x * x, axis=-1, keepdims=True) + EPS) * g
    return y * (1.0 + scale) + shift


def _prenorm_kernel(x_ref, c_ref, g_ref, mod_ref, o_ref, *, n_lat_tiles):
    is_ctx = pl.program_id(0) >= n_lat_tiles
    x = jnp.where(is_ctx, c_ref[...], x_ref[...])
    o_ref[...] = _rms_mod(x, g_ref[...], mod_ref[0, 0:1, :], mod_ref[0, 1:2, :]).astype(o_ref.dtype)


def _prenorm(x, ctx, g, mod):
    s, d = x.shape
    c = ctx.shape[0]
    tm = PRENORM_TM
    nl, nc = s // tm, c // tm
    return pl.pallas_call(
        functools.partial(_prenorm_kernel, n_lat_tiles=nl),
        grid=(nl + nc,),
        in_specs=[pl.BlockSpec((tm, d), lambda i: (jnp.minimum(i, nl - 1), 0)),
                  pl.BlockSpec((tm, d), lambda i: (jnp.maximum(i - nl, 0), 0)),
                  pl.BlockSpec((1, d), lambda i: (0, 0)),
                  pl.BlockSpec((1, N_MOD, d), lambda i: (i // nl, 0, 0))],
        out_specs=pl.BlockSpec((tm, d), lambda i: (i, 0)),
        out_shape=jax.ShapeDtypeStruct((s + c, d), BF16),
        compiler_params=_cparams(("arbitrary",)),
        name="prenorm_mix",
    )(x, ctx, g.reshape(1, d), mod)


def _inproj_kernel(h_ref, w_ref, gain_ref, c_ref, sa_ref, sb_ref, o_ref):
    j = pl.program_id(1)
    tm = h_ref.shape[0]
    nh = o_ref.shape[1] // HEAD_DIM
    n_chunks = next(n for n in (6, 3, 2, 1) if tm % (16 * n) == 0)

    def tile(n_normed, chunks):
        w = w_ref[...].astype(h_ref.dtype)
        rows = tm // chunks
        for r in range(chunks):
            rs = slice(r * rows, (r + 1) * rows)
            acc = jnp.dot(h_ref[rs, :], w, preferred_element_type=F32)
            for hd in range(nh):
                sl = slice(hd * HEAD_DIM, (hd + 1) * HEAD_DIM)
                a = acc[:, sl]
                if hd < n_normed:
                    y = a * lax.rsqrt(jnp.mean(a * a, axis=-1, keepdims=True) + EPS) * gain_ref[0, :, sl]
                    a = (y * c_ref[rs, :] + pltpu.roll(y, ROPE_PAIRS, 1) * sa_ref[rs, :]
                         + pltpu.roll(y, HEAD_DIM - ROPE_PAIRS, 1) * sb_ref[rs, :])
                o_ref[rs, sl] = a.astype(o_ref.dtype)

    all_normed = (j < 2) | ((j >= 3) & (j < 7))
    pl.when(all_normed)(lambda: tile(nh, n_chunks))
    pl.when(j == 2)(lambda: tile(A_KV_HEADS, n_chunks))
    pl.when(j >= 7)(lambda: tile(0, n_chunks))


def _inproj(h, w, gains, rope_c, rope_sa, rope_sb):
    t, d = h.shape
    n = w.shape[1]
    tm = t // INPROJ_ROW_TILES
    tn = PROJ_TN
    return pl.pallas_call(
        _inproj_kernel,
        grid=(t // tm, n // tn),
        in_specs=[pl.BlockSpec((tm, d), lambda i, j: (i, 0)),
                  pl.BlockSpec((d, tn), lambda i, j: (0, j)),
                  pl.BlockSpec((1, 1, tn), lambda i, j: (j, 0, 0)),
                  pl.BlockSpec((tm, HEAD_DIM), lambda i, j: (i, 0)),
                  pl.BlockSpec((tm, HEAD_DIM), lambda i, j: (i, 0)),
                  pl.BlockSpec((tm, HEAD_DIM), lambda i, j: (i, 0))],
        out_specs=pl.BlockSpec((tm, tn), lambda i, j: (i, j)),
        out_shape=jax.ShapeDtypeStruct((t, n), BF16),
        compiler_params=_cparams(("arbitrary", "arbitrary")),
        name="inproj",
    )(h, w, gains, rope_c, rope_sa, rope_sb)


def _rope_tables(s, c):
    rows_n = s // GRID_W
    inv = ROPE_THETA ** (-jnp.arange(ROPE_PAIRS, dtype=F32) / ROPE_PAIRS)
    ang_r = jnp.arange(rows_n, dtype=F32)[:, None] * inv
    ang_c = jnp.arange(GRID_W, dtype=F32)[:, None] * inv
    cr, sr, cc, sc = jnp.cos(ang_r), jnp.sin(ang_r), jnp.cos(ang_c), jnp.sin(ang_c)
    zr, zc = jnp.zeros_like(sr), jnp.zeros_like(sc)

    def table(row_parts, col_parts, ctx_value):
        by_row = jnp.concatenate(row_parts + [zr, zr], axis=1)
        by_col = jnp.concatenate([zc, zc] + col_parts, axis=1)
        lat = (by_row[:, None, :] + by_col[None, :, :]).reshape(s, HEAD_DIM)
        return jnp.concatenate([lat, jnp.full((c, HEAD_DIM), ctx_value, F32)], axis=0)

    return (table([cr, cr], [cc, cc], 1.0), table([zr, sr], [zc, sc], 0.0), table([-sr, zr], [-sc, zc], 0.0))


def _head_gains(qn_a, kn_a, qn_b, kn_b):
    qs = HEAD_DIM ** -0.5 * LOG2E
    one = jnp.ones((HEAD_DIM,), F32)
    heads = ([qn_a * qs] * A_HEADS + [kn_a] * A_KV_HEADS + [one] * A_KV_HEADS
             + [qn_b * qs] * (2 * B_HEADS) + [kn_b] * (2 * B_HEADS))
    heads = heads + [one] * (IN_W // HEAD_DIM - len(heads))
    return jnp.concatenate(heads).reshape(IN_W // PROJ_TN, 1, PROJ_TN)


def _build_vt(v_ref, cols, vt_ref, tk):
    n_chunks, rows, _ = vt_ref.shape
    dv = rows - ONES_ROWS
    tail = (lax.broadcasted_iota(jnp.int32, (ONES_ROWS, tk), 0) == 0).astype(vt_ref.dtype)
    for c in range(n_chunks):
        vt_ref[c, 0:dv, :] = v_ref[c * tk:(c + 1) * tk, cols].astype(F32).T.astype(vt_ref.dtype)
        vt_ref[c, dv:rows, :] = tail


def _attend(qs, k_ref, k_cols, vts, s_ref, m_ref, acc_ref):
    n_chunks, _, tk = vts[0].shape
    n_st = len(qs)
    m_ref[...] = jnp.full(m_ref.shape, -jnp.inf, F32)
    acc_ref[...] = jnp.zeros(acc_ref.shape, F32)

    def scores(i, c, slot):
        off = c * tk if isinstance(c, int) else pl.multiple_of(c * tk, tk)
        s_ref[i, slot] = lax.dot_general(k_ref[pl.ds(off, tk), k_cols[i]], qs[i], (((1,), (1,)), ((), ())),
                                         preferred_element_type=F32)

    def update(i, c, slot):
        s = s_ref[i, slot]
        m_old = m_ref[i]
        m_new = jnp.maximum(m_old, jnp.max(s, axis=0, keepdims=True))
        p = jnp.exp2(s - m_new).astype(vts[i].dtype)
        acc_ref[i] = (acc_ref[i] * jnp.exp2(m_old - m_new)
                      + jnp.dot(vts[i][c], p, preferred_element_type=F32))
        m_ref[i] = m_new

    for i in range(n_st):
        scores(i, 0, 0)

    def pair(j, carry):
        c = 2 * j
        for i in range(n_st):
            scores(i, c + 1, 1)
        for i in range(n_st):
            update(i, c, 0)
        for i in range(n_st):
            scores(i, c + 2, 0)
        for i in range(n_st):
            update(i, c + 1, 1)
        return carry

    n_pairs = (n_chunks - 1) // 2
    lax.fori_loop(0, n_pairs, pair, 0)
    done = 2 * n_pairs
    if n_chunks - done == 2:
        for i in range(n_st):
            scores(i, done + 1, 1)
    for i in range(n_st):
        update(i, done, 0)
    if n_chunks - done == 2:
        for i in range(n_st):
            update(i, done + 1, 1)


def _gqa_kernel(q_ref, k_ref, v_ref, o_ref, vt_ref, s_ref, m_ref, acc_ref):
    tk = vt_ref.shape[2]
    n_st = s_ref.shape[0]
    pl.when((pl.program_id(1) == 0) & (pl.program_id(2) == 0))(
        lambda: _build_vt(v_ref, slice(None), vt_ref, tk))
    qs = [q_ref[:, i * HEAD_DIM:(i + 1) * HEAD_DIM] for i in range(n_st)]
    _attend(qs, k_ref, [slice(None)] * n_st, [vt_ref] * n_st, s_ref, m_ref, acc_ref)
    for i in range(n_st):
        o_t = acc_ref[i, 0:HEAD_DIM, :] / acc_ref[i, HEAD_DIM:HEAD_DIM + 1, :]
        o_ref[:, i * HEAD_DIM:(i + 1) * HEAD_DIM] = o_t.T.astype(o_ref.dtype)


def _key_chunk(t, largest=1408):
    for tk in (1408, 768, 1024, 512, 640, 384, 256, 128):
        if tk <= largest and t % tk == 0:
            return tk
    raise ValueError(f"unsupported key count {t}")


def _gqa(proj, s):
    t = proj.shape[0]
    tq = ATTN_TQ
    tk = _key_chunk(t)
    n_st = GQA_STREAMS
    per_g = A_GROUP // n_st
    return pl.pallas_call(
        _gqa_kernel,
        grid=(A_KV_HEADS, per_g, s // tq),
        in_specs=[pl.BlockSpec((tq, n_st * HEAD_DIM), lambda g, hh, i: (i, COL_AQ // n_st + g * per_g + hh)),
                  pl.BlockSpec((t, HEAD_DIM), lambda g, hh, i: (0, COL_AK + g)),
                  pl.BlockSpec((t, HEAD_DIM), lambda g, hh, i: (0, COL_AV + g))],
        out_specs=pl.BlockSpec((tq, n_st * HEAD_DIM), lambda g, hh, i: (i, g * per_g + hh)),
        out_shape=jax.ShapeDtypeStruct((s, A_Q_W), BF16),
        scratch_shapes=[pltpu.VMEM((t // tk, HEAD_DIM + ONES_ROWS, tk), BF16),
                        pltpu.VMEM((n_st, 2, tk, tq), F32),
                        pltpu.VMEM((n_st, 1, tq), F32),
                        pltpu.VMEM((n_st, HEAD_DIM + ONES_ROWS, tq), F32)],
        compiler_params=_cparams(("arbitrary", "arbitrary", "arbitrary")),
        name="gqa_attn",
    )(proj, proj, proj)


def _diff_kernel(lam_ref, q_ref, k_ref, v_ref, g_ref, o_ref, vt_ref, s_ref, m_ref, acc_ref, *, lam_init):
    n_heads, _, _, tk = vt_ref.shape
    n_st = 2 * n_heads
    head_cols = lambda i: slice(i * HEAD_DIM, (i + 1) * HEAD_DIM)
    v_cols = lambda hd: slice(hd * B_V_DIM, (hd + 1) * B_V_DIM)

    @pl.when(pl.program_id(1) == 0)
    def _():
        for hd in range(n_heads):
            _build_vt(v_ref, v_cols(hd), vt_ref.at[hd], tk)

    lv = lam_ref[...]
    lam = (jnp.exp(jnp.sum(lv[0:1, :] * lv[1:2, :], axis=-1, keepdims=True))
           - jnp.exp(jnp.sum(lv[2:3, :] * lv[3:4, :], axis=-1, keepdims=True)) + lam_init)
    qs = [q_ref[:, head_cols(i)] for i in range(n_st)]
    _attend(qs, k_ref, [head_cols(i) for i in range(n_st)], [vt_ref.at[i // 2] for i in range(n_st)],
            s_ref, m_ref, acc_ref)
    for hd in range(n_heads):
        a0, a1 = acc_ref.at[2 * hd], acc_ref.at[2 * hd + 1]
        o_t = (a0[0:B_V_DIM, :] / a0[B_V_DIM:B_V_DIM + 1, :]
               - lam * (a1[0:B_V_DIM, :] / a1[B_V_DIM:B_V_DIM + 1, :]))
        o = o_t.T
        y = o * lax.rsqrt(jnp.mean(o * o, axis=-1, keepdims=True) + EPS) * g_ref[...]
        o_ref[:, v_cols(hd)] = (y * (1.0 - lam_init)).astype(o_ref.dtype)


def _diff(proj, s, lam_vecs, subln_g, lam_init):
    t = proj.shape[0]
    tq = ATTN_TQ
    nh = DIFF_HEADS
    tk = _key_chunk(t, largest=768)
    n_st = 2 * nh
    qk_w = n_st * HEAD_DIM
    v_w = nh * B_V_DIM
    resident = lambda w, col0: pl.BlockSpec((t, w), lambda h, i: (0, col0 * HEAD_DIM // w + h),
                                            pipeline_mode=pl.Buffered(1))
    return pl.pallas_call(
        functools.partial(_diff_kernel, lam_init=lam_init),
        grid=(B_HEADS // nh, s // tq),
        in_specs=[pl.BlockSpec((4, HEAD_DIM), lambda h, i: (0, 0)),
                  pl.BlockSpec((tq, qk_w), lambda h, i: (i, COL_BQ * HEAD_DIM // qk_w + h)),
                  resident(qk_w, COL_BK),
                  resident(v_w, COL_BV),
                  pl.BlockSpec((1, B_V_DIM), lambda h, i: (0, 0))],
        out_specs=pl.BlockSpec((tq, v_w), lambda h, i: (i, h)),
        out_shape=jax.ShapeDtypeStruct((s, B_V_W), BF16),
        scratch_shapes=[pltpu.VMEM((nh, t // tk, B_V_DIM + ONES_ROWS, tk), BF16),
                        pltpu.VMEM((n_st, 2, tk, tq), F32),
                        pltpu.VMEM((n_st, 1, tq), F32),
                        pltpu.VMEM((n_st, B_V_DIM + ONES_ROWS, tq), F32)],
        compiler_params=_cparams(("arbitrary", "arbitrary")),
        name="diff_attn",
    )(lam_vecs, proj, proj, proj, subln_g.reshape(1, B_V_DIM))


def _merge_kernel(oa_ref, ob_ref, *refs):
    nc = (len(refs) - 6) // 2
    ga_refs, gb_refs = refs[:nc], refs[nc:2 * nc]
    wa_ref, wb_ref, wo_ref, x_ref, gate_ref, o_ref = refs[2 * nc:]
    tn = ga_refs[0].shape[1]
    ya = jnp.dot(oa_ref[...], wa_ref[...], preferred_element_type=F32)
    yb = jnp.dot(ob_ref[...], wb_ref[...], preferred_element_type=F32)
    parts = []
    for c in range(nc):
        sl = slice(c * tn, (c + 1) * tn)
        t = (jax.nn.sigmoid(ga_refs[c][...].astype(F32)) * ya[:, sl]
             + jax.nn.sigmoid(gb_refs[c][...].astype(F32)) * yb[:, sl])
        parts.append(t.astype(wo_ref.dtype))
    y = jnp.dot(jnp.concatenate(parts, axis=1), wo_ref[...], preferred_element_type=F32)
    o_ref[...] = x_ref[...] + gate_ref[...] * y


def _merge(oa, ob, proj, wa, wb, wo, x, gate):
    s, d = x.shape
    tm = MERGE_TM
    tn = PROJ_TN
    nc = d // tn
    ga0 = COL_GA * HEAD_DIM // tn
    gb0 = COL_GB * HEAD_DIM // tn
    resident = lambda shape: pl.BlockSpec(shape, lambda i: (0, 0), pipeline_mode=pl.Buffered(1))
    gate_specs = [pl.BlockSpec((tm, tn), lambda i, c=c0 + c: (i, c)) for c0 in (ga0, gb0) for c in range(nc)]
    return pl.pallas_call(
        _merge_kernel,
        grid=(s // tm,),
        in_specs=[pl.BlockSpec((tm, A_Q_W), lambda i: (i, 0)),
                  pl.BlockSpec((tm, B_V_W), lambda i: (i, 0)),
                  *gate_specs,
                  resident((A_Q_W, d)), resident((B_V_W, d)), resident((d, d)),
                  pl.BlockSpec((tm, d), lambda i: (i, 0)),
                  pl.BlockSpec((1, d), lambda i: (0, 0))],
        out_specs=pl.BlockSpec((tm, d), lambda i: (i, 0)),
        out_shape=jax.ShapeDtypeStruct((s, d), F32),
        compiler_params=_cparams(("arbitrary",)),
        name="merge_out",
    )(oa, ob, *([proj] * (2 * nc)), wa, wb, wo, x, gate)


def _store_packed_tiles(ref, x):
    rows = x.shape[0]
    bits = lax.bitcast_convert_type(x.astype(BF16).astype(F32), jnp.uint32)
    half = bits.shape[1] // 2
    words = (bits[:, :half] >> 16) | (bits[:, half:] & jnp.uint32(0xFFFF0000))
    for c in range(PACK_TILES):
        ref[pl.ds(c, rows, stride=PACK_TILES), :] = words[:, c * LANES:(c + 1) * LANES]


def _lane_max(x):
    return jnp.max(x, axis=-1, keepdims=True)


def _lane_min(x):
    return jnp.min(x, axis=-1, keepdims=True)


def _group_allreduce(x, lane, op):
    for sft in (1, 2, 4):
        up = pltpu.roll(x, sft, 1)
        dn = pltpu.roll(x, LANES - sft, 1)
        x = op(x, jnp.where((lane & sft) != 0, up, dn))
    return x


def _router_kernel(x_ref, g_ref, mod_ref, wr_ref, rb_ref,
                   h_ref, hp_ref, idx_ref, wsel_ref, rank_ref, cnt_ref, carry_ref):
    @pl.when(pl.program_id(0) == 0)
    def _():
        carry_ref[...] = jnp.zeros_like(carry_ref)

    h = _rms_mod(x_ref[...], g_ref[...], mod_ref[0, 3:4, :], mod_ref[0, 4:5, :])
    hb = h.astype(BF16)
    h_ref[...] = hb
    _store_packed_tiles(hp_ref, h)
    tm = h.shape[0]

    h_lo = (h - hb.astype(F32)).astype(BF16)
    logits = (jnp.dot(hb, wr_ref[0], preferred_element_type=F32)
              + jnp.dot(hb, wr_ref[1], preferred_element_type=F32)
              + jnp.dot(h_lo, wr_ref[0], preferred_element_type=F32))
    scores = jax.nn.sigmoid(logits)
    lane = lax.broadcasted_iota(jnp.int32, (tm, LANES), 1)
    lane_f = lane.astype(F32)
    gid_f = (lane >> 3).astype(F32)
    valid = lane < N_EXPERTS
    neg = jnp.float32(-jnp.inf)
    big = jnp.float32(LANES)
    biased = jnp.where(valid, scores + rb_ref[...], neg)

    m1 = _group_allreduce(biased, lane, jnp.maximum)
    a1 = _group_allreduce(jnp.where(biased == m1, lane_f, big), lane, jnp.minimum)
    m2 = _group_allreduce(jnp.where(lane_f == a1, neg, biased), lane, jnp.maximum)
    gscore = jnp.where(valid, m1 + m2, neg)
    keep = jnp.zeros((tm, LANES), jnp.bool_)
    for _ in range(TOPK_GROUPS):
        best = _lane_max(gscore)
        gsel = _lane_min(jnp.where(gscore == best, gid_f, big))
        hit = gid_f == gsel
        keep = keep | hit
        gscore = jnp.where(hit, neg, gscore)
    cand = jnp.where(keep & valid, biased, neg)

    onehot = jnp.zeros((tm, LANES), F32)
    idx_out = jnp.zeros((tm, LANES), F32)
    w_out = jnp.zeros((tm, LANES), F32)
    sels = []
    for k in range(TOP_K):
        best = _lane_max(cand)
        sel = _lane_min(jnp.where(cand == best, lane_f, big))
        hit = lane_f == sel
        wk = jnp.sum(jnp.where(hit, scores, 0.0), axis=-1, keepdims=True)
        cand = jnp.where(hit, neg, cand)
        onehot = jnp.where(hit, 1.0, onehot)
        idx_out = jnp.where(lane == k, sel, idx_out)
        w_out = jnp.where(lane == k, wk, w_out)
        sels.append(hit)
    wsum = jnp.sum(w_out, axis=-1, keepdims=True)
    wsel_ref[...] = w_out / wsum * ROUTED_SCALE
    idx_ref[...] = idx_out.astype(jnp.int32)

    row = lax.broadcasted_iota(jnp.int32, (tm, tm), 0)
    col = lax.broadcasted_iota(jnp.int32, (tm, tm), 1)
    lower = (col < row).astype(BF16)
    before = jnp.dot(lower, onehot.astype(BF16), preferred_element_type=F32) + carry_ref[...]
    rank_out = jnp.zeros((tm, LANES), F32)
    for k in range(TOP_K):
        rk = jnp.sum(jnp.where(sels[k], before, 0.0), axis=-1, keepdims=True)
        rank_out = jnp.where(lane == k, rk, rank_out)
    rank_ref[...] = rank_out.astype(jnp.int32)
    carry_ref[...] = carry_ref[...] + jnp.sum(onehot, axis=0, keepdims=True)
    cnt_ref[...] = carry_ref[...]


def _router(x1, g, mod, w_router, router_bias):
    s, d = x1.shape
    tm = ROUTER_TM
    e = w_router.shape[1]
    wr = jnp.pad(w_router, ((0, 0), (0, LANES - e)))
    wr_hi = wr.astype(BF16)
    wr = jnp.stack([wr_hi, (wr - wr_hi.astype(F32)).astype(BF16)])
    rb = jnp.pad(router_bias, (0, LANES - e)).reshape(1, LANES)
    row_spec = lambda w: pl.BlockSpec((tm, w), lambda i: (i, 0))
    return pl.pallas_call(
        _router_kernel,
        grid=(s // tm,),
        in_specs=[row_spec(d),
                  pl.BlockSpec((1, d), lambda i: (0, 0)),
                  pl.BlockSpec((1, N_MOD, d), lambda i: (0, 0, 0)),
                  pl.BlockSpec((2, d, LANES), lambda i: (0, 0, 0)),
                  pl.BlockSpec((1, LANES), lambda i: (0, 0))],
        out_specs=[row_spec(d), pl.BlockSpec((tm * PACK_TILES, LANES), lambda i: (i, 0)),
                   row_spec(LANES), row_spec(LANES), row_spec(LANES),
                   pl.BlockSpec((1, LANES), lambda i: (0, 0))],
        out_shape=[jax.ShapeDtypeStruct((s, d), BF16),
                   jax.ShapeDtypeStruct((s * PACK_TILES, LANES), jnp.uint32),
                   jax.ShapeDtypeStruct((s, LANES), jnp.int32),
                   jax.ShapeDtypeStruct((s, LANES), F32),
                   jax.ShapeDtypeStruct((s, LANES), jnp.int32),
                   jax.ShapeDtypeStruct((1, LANES), F32)],
        scratch_shapes=[pltpu.VMEM((1, LANES), F32)],
        compiler_params=_cparams(("arbitrary",)),
        name="ffn_router",
    )(x1, g.reshape(1, d), mod, wr, rb)


def _unpack_halves(words):
    lo = lax.bitcast_convert_type(words << 16, F32)
    hi = lax.bitcast_convert_type(words & jnp.uint32(0xFFFF0000), F32)
    return lo, hi


def _unpack_rows(words):
    lo, hi = _unpack_halves(words)
    return jnp.concatenate([lo.astype(BF16), hi.astype(BF16)], axis=1)


def _token_rows(row, tiles):
    return pl.ds(pl.multiple_of(row * tiles, tiles), tiles)


def _dispatch_kernel(ps_ref, pe_ref, nused_ref, dest_hbm, hp_ref, xs_hbm,
                     dest_smem, zbuf, sem_i, sem_z, sem_x, *, n_blocks):
    i = pl.program_id(0)
    n = dest_smem.shape[0]
    tm = n // TOP_K
    copy_dest = pltpu.make_async_copy(dest_hbm.at[i], dest_smem, sem_i)
    copy_dest.start()

    @pl.when(i == 0)
    def _():
        zbuf[...] = jnp.zeros_like(zbuf)

        def zero_block(row0):
            span = MOE_BLOCK * PACK_TILES
            return pltpu.make_async_copy(
                zbuf, xs_hbm.at[pl.ds(pl.multiple_of(row0 * PACK_TILES, span), span), :], sem_z)

        def per_expert(action):
            def body(e, carry):
                @pl.when(pe_ref[e] > ps_ref[e])
                def _():
                    action(zero_block(pe_ref[e] - MOE_BLOCK))
                return carry
            lax.fori_loop(0, N_EXPERTS, body, 0)

        def per_tail(action):
            def body(b, carry):
                action(zero_block(b * MOE_BLOCK))
                return carry
            lax.fori_loop(nused_ref[0], n_blocks, body, 0)

        per_expert(lambda cp: cp.start())
        per_tail(lambda cp: cp.start())
        per_expert(lambda cp: cp.wait())
        per_tail(lambda cp: cp.wait())

    copy_dest.wait()

    def issue(t, carry):
        for k in range(TOP_K):
            dest = dest_smem[t * TOP_K + k]
            pltpu.make_async_copy(hp_ref.at[_token_rows(t, PACK_TILES), :],
                                  xs_hbm.at[_token_rows(dest, PACK_TILES), :], sem_x).start(priority=k % 2)
        return carry

    lax.fori_loop(0, tm, issue, 0)
    for _ in range(TOP_K):
        pltpu.make_async_copy(hp_ref, xs_hbm.at[pl.ds(0, tm * PACK_TILES), :], sem_x).wait()


def _dispatch(pad_start, pad_end, n_used, dest_tm, hp, n_blocks):
    n_tiles, n = dest_tm.shape
    tm = n // TOP_K
    grid_spec = pltpu.PrefetchScalarGridSpec(
        num_scalar_prefetch=3,
        grid=(n_tiles,),
        in_specs=[pl.BlockSpec(memory_space=pl.ANY),
                  pl.BlockSpec((tm * PACK_TILES, LANES), lambda i, ps, pe, nu: (i, 0))],
        out_specs=pl.BlockSpec(memory_space=pl.ANY),
        scratch_shapes=[pltpu.SMEM((n,), jnp.int32),
                        pltpu.VMEM((MOE_BLOCK * PACK_TILES, LANES), jnp.uint32),
                        pltpu.SemaphoreType.DMA,
                        pltpu.SemaphoreType.DMA,
                        pltpu.SemaphoreType.DMA])
    return pl.pallas_call(
        functools.partial(_dispatch_kernel, n_blocks=n_blocks),
        grid_spec=grid_spec,
        out_shape=jax.ShapeDtypeStruct((n_blocks * MOE_BLOCK * PACK_TILES, LANES), jnp.uint32),
        compiler_params=_cparams(("arbitrary",)),
        name="moe_dispatch",
    )(pad_start, pad_end, n_used, dest_tm, hp)


def _moe_kernel(nblk_ref, blk0_ref, nxt_ref, slot_ref, first_ref, nused_ref,
                xs_hbm, wg_hbm, wu_hbm, wd_hbm, y_hbm,
                xbuf, ybuf, wg32, wu32, wd32, wgb, wub, wdb, sem_w, sem_x, sem_y, *, n_blocks):
    e = pl.program_id(0)
    nb = nblk_ref[e]
    span = MOE_BLOCK * PACK_TILES

    def block_rows(b):
        return pl.ds(pl.multiple_of(b * span, span), span)

    def fetch(ex, slot):
        return (pltpu.make_async_copy(wg_hbm.at[ex], wg32.at[slot], sem_w.at[slot, 0]),
                pltpu.make_async_copy(wu_hbm.at[ex], wu32.at[slot], sem_w.at[slot, 1]),
                pltpu.make_async_copy(wd_hbm.at[ex], wd32.at[slot], sem_w.at[slot, 2]))

    def x_copy(b, s):
        return pltpu.make_async_copy(xs_hbm.at[block_rows(b), :], xbuf.at[s], sem_x.at[s])

    def y_copy(b, s):
        return pltpu.make_async_copy(ybuf.at[s], y_hbm.at[block_rows(b), :], sem_y.at[s])

    n_used = nused_ref[0]
    n_xbuf = xbuf.shape[0]

    @pl.when(nb > 0)
    def _():
        b0 = blk0_ref[e]
        slot = slot_ref[e]

        @pl.when(e == first_ref[0])
        def _():
            for b in range(n_xbuf - 1):
                pl.when(b < n_used)(lambda b=b: x_copy(b, b).start())
            for cp in fetch(e, 0):
                cp.start(priority=1)

        for cp in fetch(e, slot):
            cp.wait()

        @pl.when(nxt_ref[e] >= 0)
        def _():
            for cp in fetch(nxt_ref[e], 1 - slot):
                cp.start(priority=1)

        wgb[...] = wg32[slot].astype(BF16)
        wub[...] = wu32[slot].astype(BF16)
        wdb[...] = wd32[slot].astype(BF16)

        def body(j, carry):
            b = b0 + j
            s = b % 2
            sx = b % n_xbuf
            ahead = b + n_xbuf - 1
            x_copy(b, sx).wait()
            pl.when(ahead < n_used)(lambda: x_copy(ahead, ahead % n_xbuf).start())
            pl.when(b >= 2)(lambda: y_copy(b - 2, s).wait())
            xb = xbuf.at[sx]
            words = jnp.concatenate(
                [xb[pl.ds(c, MOE_BLOCK, stride=PACK_TILES), :] for c in range(PACK_TILES)], axis=1)
            x = _unpack_rows(words)
            gate = jnp.dot(x, wgb[...], preferred_element_type=F32)
            up = jnp.dot(x, wub[...], preferred_element_type=F32)
            act = (gate * jax.nn.sigmoid(gate) * up).astype(BF16)
            _store_packed_tiles(ybuf.at[s], jnp.dot(act, wdb[...], preferred_element_type=F32))
            y_copy(b, s).start()
            return carry

        lax.fori_loop(0, nb, body, 0)

    @pl.when(e == pl.num_programs(0) - 1)
    def _():
        pl.when(n_used >= 2)(lambda: y_copy(n_used - 2, n_used % 2).wait())
        pl.when(n_used >= 1)(lambda: y_copy(n_used - 1, (n_used - 1) % 2).wait())
        ybuf[0] = jnp.zeros(ybuf.shape[1:], ybuf.dtype)

        def tail(action):
            def step(b, carry):
                action(y_copy(b, 0))
                return carry
            lax.fori_loop(nused_ref[0], n_blocks, step, 0)

        tail(lambda cp: cp.start())
        tail(lambda cp: cp.wait())


def _moe(nblk, blk0, nxt, slot, first, n_used, xs, weg, weu, wed, n_blocks):
    n_exp, d, f = weg.shape
    span = MOE_BLOCK * PACK_TILES
    grid_spec = pltpu.PrefetchScalarGridSpec(
        num_scalar_prefetch=6,
        grid=(n_exp,),
        in_specs=[pl.BlockSpec(memory_space=pl.ANY)] * 4,
        out_specs=pl.BlockSpec(memory_space=pl.ANY),
        scratch_shapes=[pltpu.VMEM((3, span, LANES), jnp.uint32), pltpu.VMEM((2, span, LANES), jnp.uint32),
                        pltpu.VMEM((2, d, f), F32), pltpu.VMEM((2, d, f), F32), pltpu.VMEM((2, f, d), F32),
                        pltpu.VMEM((d, f), BF16), pltpu.VMEM((d, f), BF16), pltpu.VMEM((f, d), BF16),
                        pltpu.SemaphoreType.DMA((2, 3)), pltpu.SemaphoreType.DMA((3,)),
                        pltpu.SemaphoreType.DMA((2,))])
    return pl.pallas_call(
        functools.partial(_moe_kernel, n_blocks=n_blocks),
        grid_spec=grid_spec,
        out_shape=jax.ShapeDtypeStruct((n_blocks * span, LANES), jnp.uint32),
        compiler_params=_cparams(("arbitrary",)),
        name="moe_experts",
    )(nblk, blk0, nxt, slot, first, n_used, xs, weg, weu, wed)


def _combine_kernel(dest_hbm, y_hbm, w_ref, h_ref, wg_ref, wu_ref, wd_ref, x_ref, gate_ref, o_ref,
                    dest0, dest1, ybuf0, ybuf1, sem_i, sem_y):
    i = pl.program_id(0)
    tm = h_ref.shape[0]
    n = tm * TOP_K
    span = n * PACK_TILES
    dests = (dest0, dest1)
    ybufs = (ybuf0, ybuf1)

    def start_tile(tile, slot):
        copy_dest = pltpu.make_async_copy(dest_hbm.at[tile], dests[slot], sem_i)
        copy_dest.start()
        copy_dest.wait()

        def issue(t, carry):
            base = t * TOKEN_PITCH
            for k in range(TOP_K):
                src = dests[slot][t * TOP_K + k]
                pltpu.make_async_copy(
                    y_hbm.at[_token_rows(src, PACK_TILES), :],
                    ybufs[slot].at[pl.ds(pl.multiple_of(base + k * YBUF_PITCH, 4), PACK_TILES), :],
                    sem_y.at[slot]).start(priority=k % 2)
            return carry

        lax.fori_loop(0, tm, issue, 0)

    def step(cur):
        nxt = 1 - cur
        pl.when(i == 0)(lambda: start_tile(0, cur))
        pl.when(i + 1 < pl.num_programs(0))(lambda: start_tile(i + 1, nxt))

        h = h_ref[...]
        gate = jnp.dot(h, wg_ref[...], preferred_element_type=F32)
        up = jnp.dot(h, wu_ref[...], preferred_element_type=F32)
        act = (gate * jax.nn.sigmoid(gate) * up).astype(BF16)
        shared = jnp.dot(act, wd_ref[...], preferred_element_type=F32)

        yb = ybufs[cur]
        pltpu.make_async_copy(y_hbm.at[pl.ds(0, span), :], yb.at[pl.ds(0, span), :], sem_y.at[cur]).wait()
        wks = [jnp.broadcast_to(w_ref[:, k:k + 1], (tm, LANES)) for k in range(TOP_K)]
        lows, highs = [], []
        for c in range(PACK_TILES):
            acc_lo = acc_hi = None
            for k in range(TOP_K):
                lo, hi = _unpack_halves(yb[pl.ds(k * YBUF_PITCH + c, tm, stride=TOKEN_PITCH), :])
                acc_lo = lo * wks[k] if acc_lo is None else acc_lo + lo * wks[k]
                acc_hi = hi * wks[k] if acc_hi is None else acc_hi + hi * wks[k]
            lows.append(acc_lo)
            highs.append(acc_hi)
        routed = jnp.concatenate(lows + highs, axis=1)
        o_ref[...] = x_ref[...] + gate_ref[...] * (routed + shared)

    pl.when(i % 2 == 0)(lambda: step(0))
    pl.when(i % 2 == 1)(lambda: step(1))


def _combine(dest_tm, y_sorted, wsel, h2, wsg, wsu, wsd, x1, gate):
    s, d = x1.shape
    n_tiles, n = dest_tm.shape
    tm = n // TOP_K
    f = wsg.shape[1]
    return pl.pallas_call(
        _combine_kernel,
        grid=(n_tiles,),
        in_specs=[pl.BlockSpec(memory_space=pl.ANY),
                  pl.BlockSpec(memory_space=pl.ANY),
                  pl.BlockSpec((tm, LANES), lambda i: (i, 0)),
                  pl.BlockSpec((tm, d), lambda i: (i, 0)),
                  pl.BlockSpec((d, f), lambda i: (0, 0)),
                  pl.BlockSpec((d, f), lambda i: (0, 0)),
                  pl.BlockSpec((f, d), lambda i: (0, 0)),
                  pl.BlockSpec((tm, d), lambda i: (i, 0)),
                  pl.BlockSpec((1, d), lambda i: (0, 0))],
        out_specs=pl.BlockSpec((tm, d), lambda i: (i, 0)),
        out_shape=jax.ShapeDtypeStruct((s, d), F32),
        scratch_shapes=[pltpu.SMEM((n,), jnp.int32),
                        pltpu.SMEM((n,), jnp.int32),
                        pltpu.VMEM((tm * TOKEN_PITCH, LANES), jnp.uint32),
                        pltpu.VMEM((tm * TOKEN_PITCH, LANES), jnp.uint32),
                        pltpu.SemaphoreType.DMA,
                        pltpu.SemaphoreType.DMA((2,))],
        compiler_params=_cparams(("arbitrary",)),
        name="moe_combine",
    )(dest_tm, y_sorted, wsel, h2, wsg, wsu, wsd, x1, gate)


def _expert_tables(counts):
    i32 = jnp.int32
    padded = ((counts + MOE_BLOCK - 1) // MOE_BLOCK * MOE_BLOCK).astype(i32)
    pad_end = jnp.cumsum(padded).astype(i32)
    pad_start = pad_end - padded
    ids = jnp.arange(N_EXPERTS, dtype=i32)
    busy = padded > 0
    later_busy = busy[None, :] & (ids[None, :] > ids[:, None])
    nxt = jnp.min(jnp.where(later_busy, ids[None, :], N_EXPERTS), axis=1)
    nxt = jnp.where(nxt < N_EXPERTS, nxt, -1).astype(i32)
    slot = (jnp.maximum(jnp.cumsum(busy.astype(i32)) - 1, 0) % 2).astype(i32)
    first = jnp.min(jnp.where(busy, ids, N_EXPERTS)).astype(i32).reshape(1)
    n_used = (pad_end[-1] // MOE_BLOCK).reshape(1)
    return pad_start, pad_end, padded // MOE_BLOCK, pad_start // MOE_BLOCK, nxt, slot, first, n_used


def kernel(x, c, ctx, c_ctx, w_ada, b_ada, norm_mix, norm_ffn, w_in, q_norm_a, k_norm_a, q_norm_b, k_norm_b, lambda_q1, lambda_k1, lambda_q2, lambda_k2, subln_b, w_branch_a, w_branch_b, w_out, w_router, router_bias, w_exp_gate, w_exp_up, w_exp_down, w_sh_gate, w_sh_up, w_sh_down):
    depth = w_ada.shape[0]
    assert depth == 1 and x.shape[0] == 1 and ctx.shape[0] == 1
    s, d = x.shape[1], x.shape[2]
    n_ctx = ctx.shape[1]
    assert d == D_MODEL and s % GRID_W == 0
    assert s % PRENORM_TM == 0 and n_ctx % PRENORM_TM == 0 and s % ATTN_TQ == 0
    assert s % MERGE_TM == 0 and s % ROUTER_TM == 0 and (s + n_ctx) % (INPROJ_ROW_TILES * 16) == 0
    i = 0
    lam_init = 0.8 - 0.6 * math.exp(-0.3 * i)
    xs = x[0]

    mod = _adaln(jnp.concatenate([c, c_ctx[None, :]], axis=0), w_ada[i], b_ada[i]).reshape(2, N_MOD, d)

    h = _prenorm(xs, ctx[0], norm_mix[i], mod)
    tc, tsa, tsb = _rope_tables(s, n_ctx)
    gains = _head_gains(q_norm_a[i], k_norm_a[i], q_norm_b[i], k_norm_b[i])
    proj = _inproj(h, w_in[i], gains, tc, tsa, tsb)
    oa = _gqa(proj, s)
    lam_vecs = jnp.stack([lambda_q1[i], lambda_k1[i], lambda_q2[i], lambda_k2[i]]).astype(F32)
    ob = _diff(proj, s, lam_vecs, subln_b[i], lam_init)
    x1 = _merge(oa, ob, proj, w_branch_a[i].astype(BF16), w_branch_b[i].astype(BF16),
                w_out[i].astype(BF16), xs, mod[0, 2:3, :])

    h2, h2p, idx, wsel, rank, cnt = _router(x1, norm_ffn[i], mod[0:1], w_router[i], router_bias[i])
    counts = cnt[0, :N_EXPERTS].astype(jnp.int32)
    n_blocks = -(-(s * TOP_K) // MOE_BLOCK) + N_EXPERTS
    pad_start, pad_end, nblk, blk0, nxt, slot, first, n_used = _expert_tables(counts)
    tm_dispatch = math.gcd(s, DISPATCH_TM)
    tm_combine = math.gcd(s, COMBINE_TM)
    e_ids = jnp.arange(N_EXPERTS, dtype=jnp.int32)
    starts = jnp.sum(jnp.where(idx[:, :TOP_K, None] == e_ids, pad_start, 0), axis=-1)
    dest = (starts + rank[:, :TOP_K]).astype(jnp.int32)
    tiles = lambda tm: dest.reshape(s // tm, tm * TOP_K)
    xs = _dispatch(pad_start, pad_end, n_used, tiles(tm_dispatch), h2p, n_blocks)
    y_sorted = _moe(nblk, blk0, nxt, slot, first, n_used, xs,
                    w_exp_gate[i], w_exp_up[i], w_exp_down[i], n_blocks)
    out = _combine(tiles(tm_combine), y_sorted, wsel, h2,
                   w_sh_gate[i].astype(BF16), w_sh_up[i].astype(BF16), w_sh_down[i].astype(BF16),
                   x1, mod[0, 5:6, :])
    return out[None]
```

```python
import functools
import math

import jax
import jax.numpy as jnp
from jax import lax
from jax.experimental import pallas as pl
from jax.experimental.pallas import tpu as pltpu

F32 = jnp.float32
BF16 = jnp.bfloat16

D_MODEL = 2048
GRID_W = 64
HEAD_DIM = 128
ROPE_PAIRS = HEAD_DIM // 4
ROPE_THETA = 10000.0
A_HEADS = 8
A_KV_HEADS = 2
A_GROUP = A_HEADS // A_KV_HEADS
B_HEADS = 4
B_V_DIM = 2 * HEAD_DIM
N_EXPERTS = 64
TOP_K = 8
N_GROUPS = 8
TOPK_GROUPS = 4
EXPERT_DIM = 512
SHARED_DIM = 512
ROUTED_SCALE = 2.5
N_MOD = 6
EPS = 1e-6

A_Q_W = A_HEADS * HEAD_DIM
A_KV_W = A_KV_HEADS * HEAD_DIM
B_QK_W = B_HEADS * 2 * HEAD_DIM
B_V_W = B_HEADS * B_V_DIM
IN_W = A_Q_W + 2 * A_KV_W + 2 * B_QK_W + B_V_W + 2 * D_MODEL

COL_AQ = 0
COL_AK = A_Q_W // HEAD_DIM
COL_AV = COL_AK + A_KV_HEADS
COL_BQ = COL_AV + A_KV_HEADS
COL_BK = COL_BQ + 2 * B_HEADS
COL_BV = COL_BK + 2 * B_HEADS
COL_GA = COL_BV + B_V_W // HEAD_DIM
COL_GB = COL_GA + D_MODEL // HEAD_DIM

LANES = 128
SUBLANES = 8
VMEM_LIMIT = 56 * 1024 * 1024

ADALN_TN = 1536
PRENORM_TM = 256
INPROJ_ROW_TILES = 4
PROJ_TN = 512
ATTN_TQ = 512
GQA_STREAMS = 4
DIFF_HEADS = 2
MERGE_TM = 512
ROUTER_TM = 512
DISPATCH_TM = 2048
COMBINE_TM = 256
MOE_BLOCK = 256
ONES_ROWS = 16
PACK_TILES = D_MODEL // 2 // LANES
YBUF_PITCH = PACK_TILES + 4
TOKEN_PITCH = TOP_K * YBUF_PITCH + 4
LOG2E = 1.4426950408889634


def _cparams(sem, vmem=VMEM_LIMIT):
    return pltpu.CompilerParams(dimension_semantics=sem, vmem_limit_bytes=vmem)


def _adaln_kernel(cb_ref, w_ref, b_ref, o_ref):
    tn = w_ref.shape[1]
    nl = tn // LANES
    rows = 32

    def body(g, accs):
        accs = list(accs)
        r0 = pl.multiple_of(g * rows, rows)
        for u in range(rows // SUBLANES):
            r = r0 + u * SUBLANES
            w = w_ref[pl.ds(r, SUBLANES), :]
            for v in range(2):
                c = cb_ref[v, pl.ds(r, SUBLANES), :]
                s = c * jax.nn.sigmoid(c)
                for j in range(nl):
                    accs[v * nl + j] = accs[v * nl + j] + w[:, j * LANES:(j + 1) * LANES] * s
        return tuple(accs)

    init = tuple(jnp.zeros((SUBLANES, LANES), F32) for _ in range(2 * nl))
    accs = lax.fori_loop(0, w_ref.shape[0] // rows, body, init)
    for v in range(2):
        row = jnp.concatenate(
            [jnp.sum(accs[v * nl + j], axis=0, keepdims=True) for j in range(nl)], axis=1)
        o_ref[v:v + 1, :] = row + b_ref[...]


def _adaln(cvecs, w, b):
    d, n = w.shape
    tn = ADALN_TN
    cb = jnp.broadcast_to(cvecs[:, :, None], (2, d, LANES))
    return pl.pallas_call(
        _adaln_kernel,
        grid=(n // tn,),
        in_specs=[pl.BlockSpec((2, d, LANES), lambda j: (0, 0, 0)),
                  pl.BlockSpec((d, tn), lambda j: (0, j)),
                  pl.BlockSpec((1, tn), lambda j: (0, j))],
        out_specs=pl.BlockSpec((2, tn), lambda j: (0, j)),
        out_shape=jax.ShapeDtypeStruct((2, n), F32),
        compiler_params=_cparams(("arbitrary",)),
        name="adaln",
    )(cb, w, b.reshape(1, n))


def _rms_mod(x, g, shift, scale):
    y = x * lax.rsqrt(jnp.mean(x * x, axis=-1, keepdims=True) + EPS) * g
    return y * (1.0 + scale) + shift


def _prenorm_kernel(x_ref, c_ref, g_ref, mod_ref, o_ref, *, n_lat_tiles):
    is_ctx = pl.program_id(0) >= n_lat_tiles
    x = jnp.where(is_ctx, c_ref[...], x_ref[...])
    o_ref[...] = _rms_mod(x, g_ref[...], mod_ref[0, 0:1, :], mod_ref[0, 1:2, :]).astype(o_ref.dtype)


def _prenorm(x, ctx, g, mod):
    s, d = x.shape
    c = ctx.shape[0]
    tm = PRENORM_TM
    nl, nc = s // tm, c // tm
    return pl.pallas_call(
        functools.partial(_prenorm_kernel, n_lat_tiles=nl),
        grid=(nl + nc,),
        in_specs=[pl.BlockSpec((tm, d), lambda i: (jnp.minimum(i, nl - 1), 0)),
                  pl.BlockSpec((tm, d), lambda i: (jnp.maximum(i - nl, 0), 0)),
                  pl.BlockSpec((1, d), lambda i: (0, 0)),
                  pl.BlockSpec((1, N_MOD, d), lambda i: (i // nl, 0, 0))],
        out_specs=pl.BlockSpec((tm, d), lambda i: (i, 0)),
        out_shape=jax.ShapeDtypeStruct((s + c, d), BF16),
        compiler_params=_cparams(("arbitrary",)),
        name="prenorm_mix",
    )(x, ctx, g.reshape(1, d), mod)


def _inproj_kernel(h_ref, w_ref, gain_ref, c_ref, sa_ref, sb_ref, o_ref):
    j = pl.program_id(1)
    tm = h_ref.shape[0]
    nh = o_ref.shape[1] // HEAD_DIM
    n_chunks = next(n for n in (6, 3, 2, 1) if tm % (16 * n) == 0)

    def tile(n_normed, chunks):
        w = w_ref[...].astype(h_ref.dtype)
        rows = tm // chunks
        for r in range(chunks):
            rs = slice(r * rows, (r + 1) * rows)
            acc = jnp.dot(h_ref[rs, :], w, preferred_element_type=F32)
            for hd in range(nh):
                sl = slice(hd * HEAD_DIM, (hd + 1) * HEAD_DIM)
                a = acc[:, sl]
                if hd < n_normed:
                    y = a * lax.rsqrt(jnp.mean(a * a, axis=-1, keepdims=True) + EPS) * gain_ref[0, :, sl]
                    a = (y * c_ref[rs, :] + pltpu.roll(y, ROPE_PAIRS, 1) * sa_ref[rs, :]
                         + pltpu.roll(y, HEAD_DIM - ROPE_PAIRS, 1) * sb_ref[rs, :])
                o_ref[rs, sl] = a.astype(o_ref.dtype)

    all_normed = (j < 2) | ((j >= 3) & (j < 7))
    pl.when(all_normed)(lambda: tile(nh, n_chunks))
    pl.when(j == 2)(lambda: tile(A_KV_HEADS, n_chunks))
    pl.when(j >= 7)(lambda: tile(0, n_chunks))


def _inproj(h, w, gains, rope_c, rope_sa, rope_sb):
    t, d = h.shape
    n = w.shape[1]
    tm = t // INPROJ_ROW_TILES
    tn = PROJ_TN
    return pl.pallas_call(
        _inproj_kernel,
        grid=(t // tm, n // tn),
        in_specs=[pl.BlockSpec((tm, d), lambda i, j: (i, 0)),
                  pl.BlockSpec((d, tn), lambda i, j: (0, j)),
                  pl.BlockSpec((1, 1, tn), lambda i, j: (j, 0, 0)),
                  pl.BlockSpec((tm, HEAD_DIM), lambda i, j: (i, 0)),
                  pl.BlockSpec((tm, HEAD_DIM), lambda i, j: (i, 0)),
                  pl.BlockSpec((tm, HEAD_DIM), lambda i, j: (i, 0))],
        out_specs=pl.BlockSpec((tm, tn), lambda i, j: (i, j)),
        out_shape=jax.ShapeDtypeStruct((t, n), BF16),
        compiler_params=_cparams(("arbitrary", "arbitrary")),
        name="inproj",
    )(h, w, gains, rope_c, rope_sa, rope_sb)


def _rope_tables(s, c):
    rows_n = s // GRID_W
    inv = ROPE_THETA ** (-jnp.arange(ROPE_PAIRS, dtype=F32) / ROPE_PAIRS)
    ang_r = jnp.arange(rows_n, dtype=F32)[:, None] * inv
    ang_c = jnp.arange(GRID_W, dtype=F32)[:, None] * inv
    cr, sr, cc, sc = jnp.cos(ang_r), jnp.sin(ang_r), jnp.cos(ang_c), jnp.sin(ang_c)
    zr, zc = jnp.zeros_like(sr), jnp.zeros_like(sc)

    def table(row_parts, col_parts, ctx_value):
        by_row = jnp.concatenate(row_parts + [zr, zr], axis=1)
        by_col = jnp.concatenate([zc, zc] + col_parts, axis=1)
        lat = (by_row[:, None, :] + by_col[None, :, :]).reshape(s, HEAD_DIM)
        return jnp.concatenate([lat, jnp.full((c, HEAD_DIM), ctx_value, F32)], axis=0)

    return (table([cr, cr], [cc, cc], 1.0), table([zr, sr], [zc, sc], 0.0), table([-sr, zr], [-sc, zc], 0.0))


def _head_gains(qn_a, kn_a, qn_b, kn_b):
    qs = HEAD_DIM ** -0.5 * LOG2E
    one = jnp.ones((HEAD_DIM,), F32)
    heads = ([qn_a * qs] * A_HEADS + [kn_a] * A_KV_HEADS + [one] * A_KV_HEADS
             + [qn_b * qs] * (2 * B_HEADS) + [kn_b] * (2 * B_HEADS))
    heads = heads + [one] * (IN_W // HEAD_DIM - len(heads))
    return jnp.concatenate(heads).reshape(IN_W // PROJ_TN, 1, PROJ_TN)


def _build_vt(v_ref, cols, vt_ref, tk):
    n_chunks, rows, _ = vt_ref.shape
    dv = rows - ONES_ROWS
    tail = (lax.broadcasted_iota(jnp.int32, (ONES_ROWS, tk), 0) == 0).astype(vt_ref.dtype)
    for c in range(n_chunks):
        vt_ref[c, 0:dv, :] = v_ref[c * tk:(c + 1) * tk, cols].astype(F32).T.astype(vt_ref.dtype)
        vt_ref[c, dv:rows, :] = tail


def _attend(qs, k_ref, k_cols, vts, s_ref, m_ref, acc_ref):
    n_chunks, _, tk = vts[0].shape
    n_st = len(qs)
    m_ref[...] = jnp.full(m_ref.shape, -jnp.inf, F32)
    acc_ref[...] = jnp.zeros(acc_ref.shape, F32)

    def scores(i, c, slot):
        off = c * tk if isinstance(c, int) else pl.multiple_of(c * tk, tk)
        s_ref[i, slot] = lax.dot_general(k_ref[pl.ds(off, tk), k_cols[i]], qs[i], (((1,), (1,)), ((), ())),
                                         preferred_element_type=F32)

    def update(i, c, slot):
        s = s_ref[i, slot]
        m_old = m_ref[i]
        m_new = jnp.maximum(m_old, jnp.max(s, axis=0, keepdims=True))
        p = jnp.exp2(s - m_new).astype(vts[i].dtype)
        acc_ref[i] = (acc_ref[i] * jnp.exp2(m_old - m_new)
                      + jnp.dot(vts[i][c], p, preferred_element_type=F32))
        m_ref[i] = m_new

    for i in range(n_st):
        scores(i, 0, 0)

    def pair(j, carry):
        c = 2 * j
        for i in range(n_st):
            scores(i, c + 1, 1)
        for i in range(n_st):
            update(i, c, 0)
        for i in range(n_st):
            scores(i, c + 2, 0)
        for i in range(n_st):
            update(i, c + 1, 1)
        return carry

    n_pairs = (n_chunks - 1) // 2
    lax.fori_loop(0, n_pairs, pair, 0)
    done = 2 * n_pairs
    if n_chunks - done == 2:
        for i in range(n_st):
            scores(i, done + 1, 1)
    for i in range(n_st):
        update(i, done, 0)
    if n_chunks - done == 2:
        for i in range(n_st):
            update(i, done + 1, 1)


def _gqa_kernel(q_ref, k_ref, v_ref, o_ref, vt_ref, s_ref, m_ref, acc_ref):
    tk = vt_ref.shape[2]
    n_st = s_ref.shape[0]
    pl.when((pl.program_id(1) == 0) & (pl.program_id(2) == 0))(
        lambda: _build_vt(v_ref, slice(None), vt_ref, tk))
    qs = [q_ref[:, i * HEAD_DIM:(i + 1) * HEAD_DIM] for i in range(n_st)]
    _attend(qs, k_ref, [slice(None)] * n_st, [vt_ref] * n_st, s_ref, m_ref, acc_ref)
    for i in range(n_st):
        o_t = acc_ref[i, 0:HEAD_DIM, :] / acc_ref[i, HEAD_DIM:HEAD_DIM + 1, :]
        o_ref[:, i * HEAD_DIM:(i + 1) * HEAD_DIM] = o_t.T.astype(o_ref.dtype)


def _key_chunk(t, largest=1408):
    for tk in (1408, 768, 1024, 512, 640, 384, 256, 128):
        if tk <= largest and t % tk == 0:
            return tk
    raise ValueError(f"unsupported key count {t}")


def _gqa(proj, s):
    t = proj.shape[0]
    tq = ATTN_TQ
    tk = _key_chunk(t)
    n_st = GQA_STREAMS
    per_g = A_GROUP // n_st
    return pl.pallas_call(
        _gqa_kernel,
        grid=(A_KV_HEADS, per_g, s // tq),
        in_specs=[pl.BlockSpec((tq, n_st * HEAD_DIM), lambda g, hh, i: (i, COL_AQ // n_st + g * per_g + hh)),
                  pl.BlockSpec((t, HEAD_DIM), lambda g, hh, i: (0, COL_AK + g)),
                  pl.BlockSpec((t, HEAD_DIM), lambda g, hh, i: (0, COL_AV + g))],
        out_specs=pl.BlockSpec((tq, n_st * HEAD_DIM), lambda g, hh, i: (i, g * per_g + hh)),
        out_shape=jax.ShapeDtypeStruct((s, A_Q_W), BF16),
        scratch_shapes=[pltpu.VMEM((t // tk, HEAD_DIM + ONES_ROWS, tk), BF16),
                        pltpu.VMEM((n_st, 2, tk, tq), F32),
                        pltpu.VMEM((n_st, 1, tq), F32),
                        pltpu.VMEM((n_st, HEAD_DIM + ONES_ROWS, tq), F32)],
        compiler_params=_cparams(("arbitrary", "arbitrary", "arbitrary")),
        name="gqa_attn",
    )(proj, proj, proj)


def _diff_kernel(lam_ref, q_ref, k_ref, v_ref, g_ref, o_ref, vt_ref, s_ref, m_ref, acc_ref, *, lam_init):
    n_heads, _, _, tk = vt_ref.shape
    n_st = 2 * n_heads
    head_cols = lambda i: slice(i * HEAD_DIM, (i + 1) * HEAD_DIM)
    v_cols = lambda hd: slice(hd * B_V_DIM, (hd + 1) * B_V_DIM)

    @pl.when(pl.program_id(1) == 0)
    def _():
        for hd in range(n_heads):
            _build_vt(v_ref, v_cols(hd), vt_ref.at[hd], tk)

    lv = lam_ref[...]
    lam = (jnp.exp(jnp.sum(lv[0:1, :] * lv[1:2, :], axis=-1, keepdims=True))
           - jnp.exp(jnp.sum(lv[2:3, :] * lv[3:4, :], axis=-1, keepdims=True)) + lam_init)
    qs = [q_ref[:, head_cols(i)] for i in range(n_st)]
    _attend(qs, k_ref, [head_cols(i) for i in range(n_st)], [vt_ref.at[i // 2] for i in range(n_st)],
            s_ref, m_ref, acc_ref)
    for hd in range(n_heads):
        a0, a1 = acc_ref.at[2 * hd], acc_ref.at[2 * hd + 1]
        o_t = (a0[0:B_V_DIM, :] / a0[B_V_DIM:B_V_DIM + 1, :]
               - lam * (a1[0:B_V_DIM, :] / a1[B_V_DIM:B_V_DIM + 1, :]))
        o = o_t.T
        y = o * lax.rsqrt(jnp.mean(o * o, axis=-1, keepdims=True) + EPS) * g_ref[...]
        o_ref[:, v_cols(hd)] = (y * (1.0 - lam_init)).astype(o_ref.dtype)


def _diff(proj, s, lam_vecs, subln_g, lam_init):
    t = proj.shape[0]
    tq = ATTN_TQ
    nh = DIFF_HEADS
    tk = _key_chunk(t, largest=768)
    n_st = 2 * nh
    qk_w = n_st * HEAD_DIM
    v_w = nh * B_V_DIM
    resident = lambda w, col0: pl.BlockSpec((t, w), lambda h, i: (0, col0 * HEAD_DIM // w + h),
                                            pipeline_mode=pl.Buffered(1))
    return pl.pallas_call(
        functools.partial(_diff_kernel, lam_init=lam_init),
        grid=(B_HEADS // nh, s // tq),
        in_specs=[pl.BlockSpec((4, HEAD_DIM), lambda h, i: (0, 0)),
                  pl.BlockSpec((tq, qk_w), lambda h, i: (i, COL_BQ * HEAD_DIM // qk_w + h)),
                  resident(qk_w, COL_BK),
                  resident(v_w, COL_BV),
                  pl.BlockSpec((1, B_V_DIM), lambda h, i: (0, 0))],
        out_specs=pl.BlockSpec((tq, v_w), lambda h, i: (i, h)),
        out_shape=jax.ShapeDtypeStruct((s, B_V_W), BF16),
        scratch_shapes=[pltpu.VMEM((nh, t // tk, B_V_DIM + ONES_ROWS, tk), BF16),
                        pltpu.VMEM((n_st, 2, tk, tq), F32),
                        pltpu.VMEM((n_st, 1, tq), F32),
                        pltpu.VMEM((n_st, B_V_DIM + ONES_ROWS, tq), F32)],
        compiler_params=_cparams(("arbitrary", "arbitrary")),
        name="diff_attn",
    )(lam_vecs, proj, proj, proj, subln_g.reshape(1, B_V_DIM))


def _merge_kernel(oa_ref, ob_ref, *refs):
    nc = (len(refs) - 6) // 2
    ga_refs, gb_refs = refs[:nc], refs[nc:2 * nc]
    wa_ref, wb_ref, wo_ref, x_ref, gate_ref, o_ref = refs[2 * nc:]
    tn = ga_refs[0].shape[1]
    ya = jnp.dot(oa_ref[...], wa_ref[...], preferred_element_type=F32)
    yb = jnp.dot(ob_ref[...], wb_ref[...], preferred_element_type=F32)
    parts = []
    for c in range(nc):
        sl = slice(c * tn, (c + 1) * tn)
        t = (jax.nn.sigmoid(ga_refs[c][...].astype(F32)) * ya[:, sl]
             + jax.nn.sigmoid(gb_refs[c][...].astype(F32)) * yb[:, sl])
        parts.append(t.astype(wo_ref.dtype))
    y = jnp.dot(jnp.concatenate(parts, axis=1), wo_ref[...], preferred_element_type=F32)
    o_ref[...] = x_ref[...] + gate_ref[...] * y


def _merge(oa, ob, proj, wa, wb, wo, x, gate):
    s, d = x.shape
    tm = MERGE_TM
    tn = PROJ_TN
    nc = d // tn
    ga0 = COL_GA * HEAD_DIM // tn
    gb0 = COL_GB * HEAD_DIM // tn
    resident = lambda shape: pl.BlockSpec(shape, lambda i: (0, 0), pipeline_mode=pl.Buffered(1))
    gate_specs = [pl.BlockSpec((tm, tn), lambda i, c=c0 + c: (i, c)) for c0 in (ga0, gb0) for c in range(nc)]
    return pl.pallas_call(
        _merge_kernel,
        grid=(s // tm,),
        in_specs=[pl.BlockSpec((tm, A_Q_W), lambda i: (i, 0)),
                  pl.BlockSpec((tm, B_V_W), lambda i: (i, 0)),
                  *gate_specs,
                  resident((A_Q_W, d)), resident((B_V_W, d)), resident((d, d)),
                  pl.BlockSpec((tm, d), lambda i: (i, 0)),
                  pl.BlockSpec((1, d), lambda i: (0, 0))],
        out_specs=pl.BlockSpec((tm, d), lambda i: (i, 0)),
        out_shape=jax.ShapeDtypeStruct((s, d), F32),
        compiler_params=_cparams(("arbitrary",)),
        name="merge_out",
    )(oa, ob, *([proj] * (2 * nc)), wa, wb, wo, x, gate)


def _store_packed_tiles(ref, x):
    rows = x.shape[0]
    bits = lax.bitcast_convert_type(x.astype(BF16).astype(F32), jnp.uint32)
    half = bits.shape[1] // 2
    words = (bits[:, :half] >> 16) | (bits[:, half:] & jnp.uint32(0xFFFF0000))
    for c in range(PACK_TILES):
        ref[pl.ds(c, rows, stride=PACK_TILES), :] = words[:, c * LANES:(c + 1) * LANES]


def _lane_max(x):
    return jnp.max(x, axis=-1, keepdims=True)


def _lane_min(x):
    return jnp.min(x, axis=-1, keepdims=True)


def _group_allreduce(x, lane, op):
    for sft in (1, 2, 4):
        up = pltpu.roll(x, sft, 1)
        dn = pltpu.roll(x, LANES - sft, 1)
        x = op(x, jnp.where((lane & sft) != 0, up, dn))
    return x


def _router_kernel(x_ref, g_ref, mod_ref, wr_ref, rb_ref,
                   h_ref, hp_ref, idx_ref, wsel_ref, rank_ref, cnt_ref, carry_ref):
    @pl.when(pl.program_id(0) == 0)
    def _():
        carry_ref[...] = jnp.zeros_like(carry_ref)

    h = _rms_mod(x_ref[...], g_ref[...], mod_ref[0, 3:4, :], mod_ref[0, 4:5, :])
    hb = h.astype(BF16)
    h_ref[...] = hb
    _store_packed_tiles(hp_ref, h)
    tm = h.shape[0]

    h_lo = (h - hb.astype(F32)).astype(BF16)
    logits = (jnp.dot(hb, wr_ref[0], preferred_element_type=F32)
              + jnp.dot(hb, wr_ref[1], preferred_element_type=F32)
              + jnp.dot(h_lo, wr_ref[0], preferred_element_type=F32))
    scores = jax.nn.sigmoid(logits)
    lane = lax.broadcasted_iota(jnp.int32, (tm, LANES), 1)
    lane_f = lane.astype(F32)
    gid_f = (lane >> 3).astype(F32)
    valid = lane < N_EXPERTS
    neg = jnp.float32(-jnp.inf)
    big = jnp.float32(LANES)
    biased = jnp.where(valid, scores + rb_ref[...], neg)

    m1 = _group_allreduce(biased, lane, jnp.maximum)
    a1 = _group_allreduce(jnp.where(biased == m1, lane_f, big), lane, jnp.minimum)
    m2 = _group_allreduce(jnp.where(lane_f == a1, neg, biased), lane, jnp.maximum)
    gscore = jnp.where(valid, m1 + m2, neg)
    keep = jnp.zeros((tm, LANES), jnp.bool_)
    for _ in range(TOPK_GROUPS):
        best = _lane_max(gscore)
        gsel = _lane_min(jnp.where(gscore == best, gid_f, big))
        hit = gid_f == gsel
        keep = keep | hit
        gscore = jnp.where(hit, neg, gscore)
    cand = jnp.where(keep & valid, biased, neg)

    onehot = jnp.zeros((tm, LANES), F32)
    idx_out = jnp.zeros((tm, LANES), F32)
    w_out = jnp.zeros((tm, LANES), F32)
    sels = []
    for k in range(TOP_K):
        best = _lane_max(cand)
        sel = _lane_min(jnp.where(cand == best, lane_f, big))
        hit = lane_f == sel
        wk = jnp.sum(jnp.where(hit, scores, 0.0), axis=-1, keepdims=True)
        cand = jnp.where(hit, neg, cand)
        onehot = jnp.where(hit, 1.0, onehot)
        idx_out = jnp.where(lane == k, sel, idx_out)
        w_out = jnp.where(lane == k, wk, w_out)
        sels.append(hit)
    wsum = jnp.sum(w_out, axis=-1, keepdims=True)
    wsel_ref[...] = w_out / wsum * ROUTED_SCALE
    idx_ref[...] = idx_out.astype(jnp.int32)

    row = lax.broadcasted_iota(jnp.int32, (tm, tm), 0)
    col = lax.broadcasted_iota(jnp.int32, (tm, tm), 1)
    lower = (col < row).astype(BF16)
    before = jnp.dot(lower, onehot.astype(BF16), preferred_element_type=F32) + carry_ref[...]
    rank_out = jnp.zeros((tm, LANES), F32)
    for k in range(TOP_K):
        rk = jnp.sum(jnp.where(sels[k], before, 0.0), axis=-1, keepdims=True)
        rank_out = jnp.where(lane == k, rk, rank_out)
    rank_ref[...] = rank_out.astype(jnp.int32)
    carry_ref[...] = carry_ref[...] + jnp.sum(onehot, axis=0, keepdims=True)
    cnt_ref[...] = carry_ref[...]


def _router(x1, g, mod, w_router, router_bias):
    s, d = x1.shape
    tm = ROUTER_TM
    e = w_router.shape[1]
    wr = jnp.pad(w_router, ((0, 0), (0, LANES - e)))
    wr_hi = wr.astype(BF16)
    wr = jnp.stack([wr_hi, (wr - wr_hi.astype(F32)).astype(BF16)])
    rb = jnp.pad(router_bias, (0, LANES - e)).reshape(1, LANES)
    row_spec = lambda w: pl.BlockSpec((tm, w), lambda i: (i, 0))
    return pl.pallas_call(
        _router_kernel,
        grid=(s // tm,),
        in_specs=[row_spec(d),
                  pl.BlockSpec((1, d), lambda i: (0, 0)),
                  pl.BlockSpec((1, N_MOD, d), lambda i: (0, 0, 0)),
                  pl.BlockSpec((2, d, LANES), lambda i: (0, 0, 0)),
                  pl.BlockSpec((1, LANES), lambda i: (0, 0))],
        out_specs=[row_spec(d), pl.BlockSpec((tm * PACK_TILES, LANES), lambda i: (i, 0)),
                   row_spec(LANES), row_spec(LANES), row_spec(LANES),
                   pl.BlockSpec((1, LANES), lambda i: (0, 0))],
        out_shape=[jax.ShapeDtypeStruct((s, d), BF16),
                   jax.ShapeDtypeStruct((s * PACK_TILES, LANES), jnp.uint32),
                   jax.ShapeDtypeStruct((s, LANES), jnp.int32),
                   jax.ShapeDtypeStruct((s, LANES), F32),
                   jax.ShapeDtypeStruct((s, LANES), jnp.int32),
                   jax.ShapeDtypeStruct((1, LANES), F32)],
        scratch_shapes=[pltpu.VMEM((1, LANES), F32)],
        compiler_params=_cparams(("arbitrary",)),
        name="ffn_router",
    )(x1, g.reshape(1, d), mod, wr, rb)


def _unpack_halves(words):
    lo = lax.bitcast_convert_type(words << 16, F32)
    hi = lax.bitcast_convert_type(words & jnp.uint32(0xFFFF0000), F32)
    return lo, hi


def _unpack_rows(words):
    lo, hi = _unpack_halves(words)
    return jnp.concatenate([lo.astype(BF16), hi.astype(BF16)], axis=1)


def _token_rows(row, tiles):
    return pl.ds(pl.multiple_of(row * tiles, tiles), tiles)


def _dispatch_kernel(ps_ref, pe_ref, nused_ref, dest_hbm, hp_ref, xs_hbm,
                     dest_smem, zbuf, sem_i, sem_z, sem_x, *, n_blocks):
    i = pl.program_id(0)
    n = dest_smem.shape[0]
    tm = n // TOP_K
    copy_dest = pltpu.make_async_copy(dest_hbm.at[i], dest_smem, sem_i)
    copy_dest.start()

    @pl.when(i == 0)
    def _():
        zbuf[...] = jnp.zeros_like(zbuf)

        def zero_block(row0):
            span = MOE_BLOCK * PACK_TILES
            return pltpu.make_async_copy(
                zbuf, xs_hbm.at[pl.ds(pl.multiple_of(row0 * PACK_TILES, span), span), :], sem_z)

        def per_expert(action):
            def body(e, carry):
                @pl.when(pe_ref[e] > ps_ref[e])
                def _():
                    action(zero_block(pe_ref[e] - MOE_BLOCK))
                return carry
            lax.fori_loop(0, N_EXPERTS, body, 0)

        def per_tail(action):
            def body(b, carry):
                action(zero_block(b * MOE_BLOCK))
                return carry
            lax.fori_loop(nused_ref[0], n_blocks, body, 0)

        per_expert(lambda cp: cp.start())
        per_tail(lambda cp: cp.start())
        per_expert(lambda cp: cp.wait())
        per_tail(lambda cp: cp.wait())

    copy_dest.wait()

    def issue(t, carry):
        for k in range(TOP_K):
            dest = dest_smem[t * TOP_K + k]
            pltpu.make_async_copy(hp_ref.at[_token_rows(t, PACK_TILES), :],
                                  xs_hbm.at[_token_rows(dest, PACK_TILES), :], sem_x).start(priority=k % 2)
        return carry

    lax.fori_loop(0, tm, issue, 0, unroll=4)
    for _ in range(TOP_K):
        pltpu.make_async_copy(hp_ref, xs_hbm.at[pl.ds(0, tm * PACK_TILES), :], sem_x).wait()


def _dispatch(pad_start, pad_end, n_used, dest_tm, hp, n_blocks):
    n_tiles, n = dest_tm.shape
    tm = n // TOP_K
    grid_spec = pltpu.PrefetchScalarGridSpec(
        num_scalar_prefetch=3,
        grid=(n_tiles,),
        in_specs=[pl.BlockSpec(memory_space=pl.ANY),
                  pl.BlockSpec((tm * PACK_TILES, LANES), lambda i, ps, pe, nu: (i, 0))],
        out_specs=pl.BlockSpec(memory_space=pl.ANY),
        scratch_shapes=[pltpu.SMEM((n,), jnp.int32),
                        pltpu.VMEM((MOE_BLOCK * PACK_TILES, LANES), jnp.uint32),
                        pltpu.SemaphoreType.DMA,
                        pltpu.SemaphoreType.DMA,
                        pltpu.SemaphoreType.DMA])
    return pl.pallas_call(
        functools.partial(_dispatch_kernel, n_blocks=n_blocks),
        grid_spec=grid_spec,
        out_shape=jax.ShapeDtypeStruct((n_blocks * MOE_BLOCK * PACK_TILES, LANES), jnp.uint32),
        compiler_params=_cparams(("arbitrary",)),
        name="moe_dispatch",
    )(pad_start, pad_end, n_used, dest_tm, hp)


def _moe_kernel(nblk_ref, blk0_ref, nxt_ref, slot_ref, first_ref, nused_ref,
                xs_hbm, wg_hbm, wu_hbm, wd_hbm, y_hbm,
                xbuf, ybuf, wg32, wu32, wd32, wgb, wub, wdb, sem_w, sem_x, sem_y, *, n_blocks):
    e = pl.program_id(0)
    nb = nblk_ref[e]
    span = MOE_BLOCK * PACK_TILES

    def block_rows(b):
        return pl.ds(pl.multiple_of(b * span, span), span)

    def fetch(ex, slot):
        return (pltpu.make_async_copy(wg_hbm.at[ex], wg32.at[slot], sem_w.at[slot, 0]),
                pltpu.make_async_copy(wu_hbm.at[ex], wu32.at[slot], sem_w.at[slot, 1]),
                pltpu.make_async_copy(wd_hbm.at[ex], wd32.at[slot], sem_w.at[slot, 2]))

    def x_copy(b, s):
        return pltpu.make_async_copy(xs_hbm.at[block_rows(b), :], xbuf.at[s], sem_x.at[s])

    def y_copy(b, s):
        return pltpu.make_async_copy(ybuf.at[s], y_hbm.at[block_rows(b), :], sem_y.at[s])

    n_used = nused_ref[0]
    n_xbuf = xbuf.shape[0]

    @pl.when(nb > 0)
    def _():
        b0 = blk0_ref[e]
        slot = slot_ref[e]

        @pl.when(e == first_ref[0])
        def _():
            for b in range(n_xbuf - 1):
                pl.when(b < n_used)(lambda b=b: x_copy(b, b).start())
            for cp in fetch(e, 0):
                cp.start(priority=1)

        for cp in fetch(e, slot):
            cp.wait()

        @pl.when(nxt_ref[e] >= 0)
        def _():
            for cp in fetch(nxt_ref[e], 1 - slot):
                cp.start(priority=1)

        wgb[...] = wg32[slot].astype(BF16)
        wub[...] = wu32[slot].astype(BF16)
        wdb[...] = wd32[slot].astype(BF16)

        def body(j, carry):
            b = b0 + j
            s = b % 2
            sx = b % n_xbuf
            ahead = b + n_xbuf - 1
            x_copy(b, sx).wait()
            pl.when(ahead < n_used)(lambda: x_copy(ahead, ahead % n_xbuf).start())
            pl.when(b >= 2)(lambda: y_copy(b - 2, s).wait())
            xb = xbuf.at[sx]
            words = jnp.concatenate(
                [xb[pl.ds(c, MOE_BLOCK, stride=PACK_TILES), :] for c in range(PACK_TILES)], axis=1)
            x = _unpack_rows(words)
            gate = jnp.dot(x, wgb[...], preferred_element_type=F32)
            up = jnp.dot(x, wub[...], preferred_element_type=F32)
            act = (gate * jax.nn.sigmoid(gate) * up).astype(BF16)
            _store_packed_tiles(ybuf.at[s], jnp.dot(act, wdb[...], preferred_element_type=F32))
            y_copy(b, s).start()
            return carry

        lax.fori_loop(0, nb, body, 0)

    @pl.when(e == pl.num_programs(0) - 1)
    def _():
        pl.when(n_used >= 2)(lambda: y_copy(n_used - 2, n_used % 2).wait())
        pl.when(n_used >= 1)(lambda: y_copy(n_used - 1, (n_used - 1) % 2).wait())
        ybuf[0] = jnp.zeros(ybuf.shape[1:], ybuf.dtype)

        def tail(action):
            def step(b, carry):
                action(y_copy(b, 0))
                return carry
            lax.fori_loop(nused_ref[0], n_blocks, step, 0)

        tail(lambda cp: cp.start())
        tail(lambda cp: cp.wait())


def _moe(nblk, blk0, nxt, slot, first, n_used, xs, weg, weu, wed, n_blocks):
    n_exp, d, f = weg.shape
    span = MOE_BLOCK * PACK_TILES
    grid_spec = pltpu.PrefetchScalarGridSpec(
        num_scalar_prefetch=6,
        grid=(n_exp,),
        in_specs=[pl.BlockSpec(memory_space=pl.ANY)] * 4,
        out_specs=pl.BlockSpec(memory_space=pl.ANY),
        scratch_shapes=[pltpu.VMEM((3, span, LANES), jnp.uint32), pltpu.VMEM((2, span, LANES), jnp.uint32),
                        pltpu.VMEM((2, d, f), F32), pltpu.VMEM((2, d, f), F32), pltpu.VMEM((2, f, d), F32),
                        pltpu.VMEM((d, f), BF16), pltpu.VMEM((d, f), BF16), pltpu.VMEM((f, d), BF16),
                        pltpu.SemaphoreType.DMA((2, 3)), pltpu.SemaphoreType.DMA((3,)),
                        pltpu.SemaphoreType.DMA((2,))])
    return pl.pallas_call(
        functools.partial(_moe_kernel, n_blocks=n_blocks),
        grid_spec=grid_spec,
        out_shape=jax.ShapeDtypeStruct((n_blocks * span, LANES), jnp.uint32),
        compiler_params=_cparams(("arbitrary",)),
        name="moe_experts",
    )(nblk, blk0, nxt, slot, first, n_used, xs, weg, weu, wed)


def _combine_kernel(dest_hbm, y_hbm, w_ref, h_ref, wg_ref, wu_ref, wd_ref, x_ref, gate_ref, o_ref,
                    dest0, dest1, ybuf0, ybuf1, sem_i, sem_y):
    i = pl.program_id(0)
    tm = h_ref.shape[0]
    n = tm * TOP_K
    span = n * PACK_TILES
    dests = (dest0, dest1)
    ybufs = (ybuf0, ybuf1)

    def start_tile(tile, slot):
        copy_dest = pltpu.make_async_copy(dest_hbm.at[tile], dests[slot], sem_i)
        copy_dest.start()
        copy_dest.wait()

        def issue(t, carry):
            base = t * TOKEN_PITCH
            for k in range(TOP_K):
                src = dests[slot][t * TOP_K + k]
                pltpu.make_async_copy(
                    y_hbm.at[_token_rows(src, PACK_TILES), :],
                    ybufs[slot].at[pl.ds(pl.multiple_of(base + k * YBUF_PITCH, 4), PACK_TILES), :],
                    sem_y.at[slot]).start(priority=k % 2)
            return carry

        lax.fori_loop(0, tm, issue, 0, unroll=4)

    def step(cur):
        nxt = 1 - cur
        pl.when(i == 0)(lambda: start_tile(0, cur))
        pl.when(i + 1 < pl.num_programs(0))(lambda: start_tile(i + 1, nxt))

        h = h_ref[...]
        gate = jnp.dot(h, wg_ref[...], preferred_element_type=F32)
        up = jnp.dot(h, wu_ref[...], preferred_element_type=F32)
        act = (gate * jax.nn.sigmoid(gate) * up).astype(BF16)
        shared = jnp.dot(act, wd_ref[...], preferred_element_type=F32)

        yb = ybufs[cur]
        pltpu.make_async_copy(y_hbm.at[pl.ds(0, span), :], yb.at[pl.ds(0, span), :], sem_y.at[cur]).wait()
        wks = [jnp.broadcast_to(w_ref[:, k:k + 1], (tm, LANES)) for k in range(TOP_K)]
        lows, highs = [], []
        for c in range(PACK_TILES):
            acc_lo = acc_hi = None
            for k in range(TOP_K):
                lo, hi = _unpack_halves(yb[pl.ds(k * YBUF_PITCH + c, tm, stride=TOKEN_PITCH), :])
                acc_lo = lo * wks[k] if acc_lo is None else acc_lo + lo * wks[k]
                acc_hi = hi * wks[k] if acc_hi is None else acc_hi + hi * wks[k]
            lows.append(acc_lo)
            highs.append(acc_hi)
        routed = jnp.concatenate(lows + highs, axis=1)
        o_ref[...] = x_ref[...] + gate_ref[...] * (routed + shared)

    pl.when(i % 2 == 0)(lambda: step(0))
    pl.when(i % 2 == 1)(lambda: step(1))


def _combine(dest_tm, y_sorted, wsel, h2, wsg, wsu, wsd, x1, gate):
    s, d = x1.shape
    n_tiles, n = dest_tm.shape
    tm = n // TOP_K
    f = wsg.shape[1]
    return pl.pallas_call(
        _combine_kernel,
        grid=(n_tiles,),
        in_specs=[pl.BlockSpec(memory_space=pl.ANY),
                  pl.BlockSpec(memory_space=pl.ANY),
                  pl.BlockSpec((tm, LANES), lambda i: (i, 0)),
                  pl.BlockSpec((tm, d), lambda i: (i, 0)),
                  pl.BlockSpec((d, f), lambda i: (0, 0)),
                  pl.BlockSpec((d, f), lambda i: (0, 0)),
                  pl.BlockSpec((f, d), lambda i: (0, 0)),
                  pl.BlockSpec((tm, d), lambda i: (i, 0)),
                  pl.BlockSpec((1, d), lambda i: (0, 0))],
        out_specs=pl.BlockSpec((tm, d), lambda i: (i, 0)),
        out_shape=jax.ShapeDtypeStruct((s, d), F32),
        scratch_shapes=[pltpu.SMEM((n,), jnp.int32),
                        pltpu.SMEM((n,), jnp.int32),
                        pltpu.VMEM((tm * TOKEN_PITCH, LANES), jnp.uint32),
                        pltpu.VMEM((tm * TOKEN_PITCH, LANES), jnp.uint32),
                        pltpu.SemaphoreType.DMA,
                        pltpu.SemaphoreType.DMA((2,))],
        compiler_params=_cparams(("arbitrary",)),
        name="moe_combine",
    )(dest_tm, y_sorted, wsel, h2, wsg, wsu, wsd, x1, gate)


def _expert_tables(counts):
    i32 = jnp.int32
    padded = ((counts + MOE_BLOCK - 1) // MOE_BLOCK * MOE_BLOCK).astype(i32)
    pad_end = jnp.cumsum(padded).astype(i32)
    pad_start = pad_end - padded
    ids = jnp.arange(N_EXPERTS, dtype=i32)
    busy = padded > 0
    later_busy = busy[None, :] & (ids[None, :] > ids[:, None])
    nxt = jnp.min(jnp.where(later_busy, ids[None, :], N_EXPERTS), axis=1)
    nxt = jnp.where(nxt < N_EXPERTS, nxt, -1).astype(i32)
    slot = (jnp.maximum(jnp.cumsum(busy.astype(i32)) - 1, 0) % 2).astype(i32)
    first = jnp.min(jnp.where(busy, ids, N_EXPERTS)).astype(i32).reshape(1)
    n_used = (pad_end[-1] // MOE_BLOCK).reshape(1)
    return pad_start, pad_end, padded // MOE_BLOCK, pad_start // MOE_BLOCK, nxt, slot, first, n_used


def kernel(x, c, ctx, c_ctx, w_ada, b_ada, norm_mix, norm_ffn, w_in, q_norm_a, k_norm_a, q_norm_b, k_norm_b, lambda_q1, lambda_k1, lambda_q2, lambda_k2, subln_b, w_branch_a, w_branch_b, w_out, w_router, router_bias, w_exp_gate, w_exp_up, w_exp_down, w_sh_gate, w_sh_up, w_sh_down):
    depth = w_ada.shape[0]
    assert depth == 1 and x.shape[0] == 1 and ctx.shape[0] == 1
    s, d = x.shape[1], x.shape[2]
    n_ctx = ctx.shape[1]
    assert d == D_MODEL and s % GRID_W == 0
    assert s % PRENORM_TM == 0 and n_ctx % PRENORM_TM == 0 and s % ATTN_TQ == 0
    assert s % MERGE_TM == 0 and s % ROUTER_TM == 0 and (s + n_ctx) % (INPROJ_ROW_TILES * 16) == 0
    i = 0
    lam_init = 0.8 - 0.6 * math.exp(-0.3 * i)
    xs = x[0]

    mod = _adaln(jnp.concatenate([c, c_ctx[None, :]], axis=0), w_ada[i], b_ada[i]).reshape(2, N_MOD, d)

    h = _prenorm(xs, ctx[0], norm_mix[i], mod)
    tc, tsa, tsb = _rope_tables(s, n_ctx)
    gains = _head_gains(q_norm_a[i], k_norm_a[i], q_norm_b[i], k_norm_b[i])
    proj = _inproj(h, w_in[i], gains, tc, tsa, tsb)
    oa = _gqa(proj, s)
    lam_vecs = jnp.stack([lambda_q1[i], lambda_k1[i], lambda_q2[i], lambda_k2[i]]).astype(F32)
    ob = _diff(proj, s, lam_vecs, subln_b[i], lam_init)
    x1 = _merge(oa, ob, proj, w_branch_a[i].astype(BF16), w_branch_b[i].astype(BF16),
                w_out[i].astype(BF16), xs, mod[0, 2:3, :])

    h2, h2p, idx, wsel, rank, cnt = _router(x1, norm_ffn[i], mod[0:1], w_router[i], router_bias[i])
    counts = cnt[0, :N_EXPERTS].astype(jnp.int32)
    n_blocks = -(-(s * TOP_K) // MOE_BLOCK) + N_EXPERTS
    pad_start, pad_end, nblk, blk0, nxt, slot, first, n_used = _expert_tables(counts)
    tm_dispatch = math.gcd(s, DISPATCH_TM)
    tm_combine = math.gcd(s, COMBINE_TM)
    e_ids = jnp.arange(N_EXPERTS, dtype=jnp.int32)
    starts = jnp.sum(jnp.where(idx[:, :TOP_K, None] == e_ids, pad_start, 0), axis=-1)
    dest = (starts + rank[:, :TOP_K]).astype(jnp.int32)
    tiles = lambda tm: dest.reshape(s // tm, tm * TOP_K)
    xs = _dispatch(pad_start, pad_end, n_used, tiles(tm_dispatch), h2p, n_blocks)
    y_sorted = _moe(nblk, blk0, nxt, slot, first, n_used, xs,
                    w_exp_gate[i], w_exp_up[i], w_exp_down[i], n_blocks)
    out = _combine(tiles(tm_combine), y_sorted, wsel, h2,
                   w_sh_gate[i].astype(BF16), w_sh_up[i].astype(BF16), w_sh_down[i].astype(BF16),
                   x1, mod[0, 5:6, :])
    return out[None]
```

```python
import functools
import math

import jax
import jax.numpy as jnp
from jax import lax
from jax.experimental import pallas as pl
from jax.experimental.pallas import tpu as pltpu

F32 = jnp.float32
BF16 = jnp.bfloat16

D_MODEL = 2048
GRID_W = 64
HEAD_DIM = 128
ROPE_PAIRS = HEAD_DIM // 4
ROPE_THETA = 10000.0
A_HEADS = 8
A_KV_HEADS = 2
A_GROUP = A_HEADS // A_KV_HEADS
B_HEADS = 4
B_V_DIM = 2 * HEAD_DIM
N_EXPERTS = 64
TOP_K = 8
N_GROUPS = 8
TOPK_GROUPS = 4
EXPERT_DIM = 512
SHARED_DIM = 512
ROUTED_SCALE = 2.5
N_MOD = 6
EPS = 1e-6

A_Q_W = A_HEADS * HEAD_DIM
A_KV_W = A_KV_HEADS * HEAD_DIM
B_QK_W = B_HEADS * 2 * HEAD_DIM
B_V_W = B_HEADS * B_V_DIM
IN_W = A_Q_W + 2 * A_KV_W + 2 * B_QK_W + B_V_W + 2 * D_MODEL

COL_AQ = 0
COL_AK = A_Q_W // HEAD_DIM
COL_AV = COL_AK + A_KV_HEADS
COL_BQ = COL_AV + A_KV_HEADS
COL_BK = COL_BQ + 2 * B_HEADS
COL_BV = COL_BK + 2 * B_HEADS
COL_GA = COL_BV + B_V_W // HEAD_DIM
COL_GB = COL_GA + D_MODEL // HEAD_DIM

LANES = 128
SUBLANES = 8
VMEM_LIMIT = 56 * 1024 * 1024

ADALN_TN = 1536
PRENORM_TM = 256
INPROJ_ROW_TILES = 4
PROJ_TN = 512
ATTN_TQ = 512
GQA_STREAMS = 4
DIFF_HEADS = 2
MERGE_TM = 512
ROUTER_TM = 512
DISPATCH_TM = 1024
COMBINE_TM = 256
MOE_BLOCK = 256
ONES_ROWS = 16
PACK_TILES = D_MODEL // 2 // LANES
YBUF_PITCH = PACK_TILES + 4
TOKEN_PITCH = TOP_K * YBUF_PITCH + 4
LOG2E = 1.4426950408889634


def _cparams(sem, vmem=VMEM_LIMIT):
    return pltpu.CompilerParams(dimension_semantics=sem, vmem_limit_bytes=vmem)


def _adaln_kernel(cb_ref, w_ref, b_ref, o_ref):
    tn = w_ref.shape[1]
    nl = tn // LANES
    rows = 32

    def body(g, accs):
        accs = list(accs)
        r0 = pl.multiple_of(g * rows, rows)
        for u in range(rows // SUBLANES):
            r = r0 + u * SUBLANES
            w = w_ref[pl.ds(r, SUBLANES), :]
            for v in range(2):
                c = cb_ref[v, pl.ds(r, SUBLANES), :]
                s = c * jax.nn.sigmoid(c)
                for j in range(nl):
                    accs[v * nl + j] = accs[v * nl + j] + w[:, j * LANES:(j + 1) * LANES] * s
        return tuple(accs)

    init = tuple(jnp.zeros((SUBLANES, LANES), F32) for _ in range(2 * nl))
    accs = lax.fori_loop(0, w_ref.shape[0] // rows, body, init)
    for v in range(2):
        row = jnp.concatenate(
            [jnp.sum(accs[v * nl + j], axis=0, keepdims=True) for j in range(nl)], axis=1)
        o_ref[v:v + 1, :] = row + b_ref[...]


def _adaln(cvecs, w, b):
    d, n = w.shape
    tn = ADALN_TN
    cb = jnp.broadcast_to(cvecs[:, :, None], (2, d, LANES))
    return pl.pallas_call(
        _adaln_kernel,
        grid=(n // tn,),
        in_specs=[pl.BlockSpec((2, d, LANES), lambda j: (0, 0, 0)),
                  pl.BlockSpec((d, tn), lambda j: (0, j)),
                  pl.BlockSpec((1, tn), lambda j: (0, j))],
        out_specs=pl.BlockSpec((2, tn), lambda j: (0, j)),
        out_shape=jax.ShapeDtypeStruct((2, n), F32),
        compiler_params=_cparams(("arbitrary",)),
        name="adaln",
    )(cb, w, b.reshape(1, n))


def _rms_mod(x, g, shift, scale):
    y = x * lax.rsqrt(jnp.mean(x * x, axis=-1, keepdims=True) + EPS) * g
    return y * (1.0 + scale) + shift


def _prenorm_kernel(x_ref, c_ref, g_ref, mod_ref, o_ref, *, n_lat_tiles):
    is_ctx = pl.program_id(0) >= n_lat_tiles
    x = jnp.where(is_ctx, c_ref[...], x_ref[...])
    o_ref[...] = _rms_mod(x, g_ref[...], mod_ref[0, 0:1, :], mod_ref[0, 1:2, :]).astype(o_ref.dtype)


def _prenorm(x, ctx, g, mod):
    s, d = x.shape
    c = ctx.shape[0]
    tm = PRENORM_TM
    nl, nc = s // tm, c // tm
    return pl.pallas_call(
        functools.partial(_prenorm_kernel, n_lat_tiles=nl),
        grid=(nl + nc,),
        in_specs=[pl.BlockSpec((tm, d), lambda i: (jnp.minimum(i, nl - 1), 0)),
                  pl.BlockSpec((tm, d), lambda i: (jnp.maximum(i - nl, 0), 0)),
                  pl.BlockSpec((1, d), lambda i: (0, 0)),
                  pl.BlockSpec((1, N_MOD, d), lambda i: (i // nl, 0, 0))],
        out_specs=pl.BlockSpec((tm, d), lambda i: (i, 0)),
        out_shape=jax.ShapeDtypeStruct((s + c, d), BF16),
        compiler_params=_cparams(("arbitrary",)),
        name="prenorm_mix",
    )(x, ctx, g.reshape(1, d), mod)


def _inproj_kernel(h_ref, w_ref, gain_ref, c_ref, sa_ref, sb_ref, o_ref):
    j = pl.program_id(1)
    tm = h_ref.shape[0]
    nh = o_ref.shape[1] // HEAD_DIM
    n_chunks = next(n for n in (6, 3, 2, 1) if tm % (16 * n) == 0)

    def tile(n_normed, chunks):
        w = w_ref[...].astype(h_ref.dtype)
        rows = tm // chunks
        for r in range(chunks):
            rs = slice(r * rows, (r + 1) * rows)
            acc = jnp.dot(h_ref[rs, :], w, preferred_element_type=F32)
            for hd in range(nh):
                sl = slice(hd * HEAD_DIM, (hd + 1) * HEAD_DIM)
                a = acc[:, sl]
                if hd < n_normed:
                    y = a * lax.rsqrt(jnp.mean(a * a, axis=-1, keepdims=True) + EPS) * gain_ref[0, :, sl]
                    a = (y * c_ref[rs, :] + pltpu.roll(y, ROPE_PAIRS, 1) * sa_ref[rs, :]
                         + pltpu.roll(y, HEAD_DIM - ROPE_PAIRS, 1) * sb_ref[rs, :])
                o_ref[rs, sl] = a.astype(o_ref.dtype)

    all_normed = (j < 2) | ((j >= 3) & (j < 7))
    pl.when(all_normed)(lambda: tile(nh, n_chunks))
    pl.when(j == 2)(lambda: tile(A_KV_HEADS, n_chunks))
    pl.when(j >= 7)(lambda: tile(0, n_chunks))


def _inproj(h, w, gains, rope_c, rope_sa, rope_sb):
    t, d = h.shape
    n = w.shape[1]
    tm = t // INPROJ_ROW_TILES
    tn = PROJ_TN
    return pl.pallas_call(
        _inproj_kernel,
        grid=(t // tm, n // tn),
        in_specs=[pl.BlockSpec((tm, d), lambda i, j: (i, 0)),
                  pl.BlockSpec((d, tn), lambda i, j: (0, j)),
                  pl.BlockSpec((1, 1, tn), lambda i, j: (j, 0, 0)),
                  pl.BlockSpec((tm, HEAD_DIM), lambda i, j: (i, 0)),
                  pl.BlockSpec((tm, HEAD_DIM), lambda i, j: (i, 0)),
                  pl.BlockSpec((tm, HEAD_DIM), lambda i, j: (i, 0))],
        out_specs=pl.BlockSpec((tm, tn), lambda i, j: (i, j)),
        out_shape=jax.ShapeDtypeStruct((t, n), BF16),
        compiler_params=_cparams(("arbitrary", "arbitrary")),
        name="inproj",
    )(h, w, gains, rope_c, rope_sa, rope_sb)


def _rope_tables(s, c):
    rows_n = s // GRID_W
    inv = ROPE_THETA ** (-jnp.arange(ROPE_PAIRS, dtype=F32) / ROPE_PAIRS)
    ang_r = jnp.arange(rows_n, dtype=F32)[:, None] * inv
    ang_c = jnp.arange(GRID_W, dtype=F32)[:, None] * inv
    cr, sr, cc, sc = jnp.cos(ang_r), jnp.sin(ang_r), jnp.cos(ang_c), jnp.sin(ang_c)
    zr, zc = jnp.zeros_like(sr), jnp.zeros_like(sc)

    def table(row_parts, col_parts, ctx_value):
        by_row = jnp.concatenate(row_parts + [zr, zr], axis=1)
        by_col = jnp.concatenate([zc, zc] + col_parts, axis=1)
        lat = (by_row[:, None, :] + by_col[None, :, :]).reshape(s, HEAD_DIM)
        return jnp.concatenate([lat, jnp.full((c, HEAD_DIM), ctx_value, F32)], axis=0)

    return (table([cr, cr], [cc, cc], 1.0), table([zr, sr], [zc, sc], 0.0), table([-sr, zr], [-sc, zc], 0.0))


def _head_gains(qn_a, kn_a, qn_b, kn_b):
    qs = HEAD_DIM ** -0.5 * LOG2E
    one = jnp.ones((HEAD_DIM,), F32)
    heads = ([qn_a * qs] * A_HEADS + [kn_a] * A_KV_HEADS + [one] * A_KV_HEADS
             + [qn_b * qs] * (2 * B_HEADS) + [kn_b] * (2 * B_HEADS))
    heads = heads + [one] * (IN_W // HEAD_DIM - len(heads))
    return jnp.concatenate(heads).reshape(IN_W // PROJ_TN, 1, PROJ_TN)


def _build_vt(v_ref, cols, vt_ref, tk):
    n_chunks, rows, _ = vt_ref.shape
    dv = rows - ONES_ROWS
    tail = (lax.broadcasted_iota(jnp.int32, (ONES_ROWS, tk), 0) == 0).astype(vt_ref.dtype)
    for c in range(n_chunks):
        vt_ref[c, 0:dv, :] = v_ref[c * tk:(c + 1) * tk, cols].astype(F32).T.astype(vt_ref.dtype)
        vt_ref[c, dv:rows, :] = tail


def _attend(qs, k_ref, k_cols, vts, s_ref, m_ref, acc_ref):
    n_chunks, _, tk = vts[0].shape
    n_st = len(qs)
    m_ref[...] = jnp.full(m_ref.shape, -jnp.inf, F32)
    acc_ref[...] = jnp.zeros(acc_ref.shape, F32)

    def scores(i, c, slot):
        off = c * tk if isinstance(c, int) else pl.multiple_of(c * tk, tk)
        s_ref[i, slot] = lax.dot_general(k_ref[pl.ds(off, tk), k_cols[i]], qs[i], (((1,), (1,)), ((), ())),
                                         preferred_element_type=F32)

    def update(i, c, slot):
        s = s_ref[i, slot]
        m_old = m_ref[i]
        m_new = jnp.maximum(m_old, jnp.max(s, axis=0, keepdims=True))
        p = jnp.exp2(s - m_new).astype(vts[i].dtype)
        acc_ref[i] = (acc_ref[i] * jnp.exp2(m_old - m_new)
                      + jnp.dot(vts[i][c], p, preferred_element_type=F32))
        m_ref[i] = m_new

    for i in range(n_st):
        scores(i, 0, 0)

    def pair(j, carry):
        c = 2 * j
        for i in range(n_st):
            scores(i, c + 1, 1)
        for i in range(n_st):
            update(i, c, 0)
        for i in range(n_st):
            scores(i, c + 2, 0)
        for i in range(n_st):
            update(i, c + 1, 1)
        return carry

    n_pairs = (n_chunks - 1) // 2
    lax.fori_loop(0, n_pairs, pair, 0)
    done = 2 * n_pairs
    if n_chunks - done == 2:
        for i in range(n_st):
            scores(i, done + 1, 1)
    for i in range(n_st):
        update(i, done, 0)
    if n_chunks - done == 2:
        for i in range(n_st):
            update(i, done + 1, 1)


def _gqa_kernel(q_ref, k_ref, v_ref, o_ref, vt_ref, s_ref, m_ref, acc_ref):
    tk = vt_ref.shape[2]
    n_st = s_ref.shape[0]
    pl.when((pl.program_id(1) == 0) & (pl.program_id(2) == 0))(
        lambda: _build_vt(v_ref, slice(None), vt_ref, tk))
    qs = [q_ref[:, i * HEAD_DIM:(i + 1) * HEAD_DIM] for i in range(n_st)]
    _attend(qs, k_ref, [slice(None)] * n_st, [vt_ref] * n_st, s_ref, m_ref, acc_ref)
    for i in range(n_st):
        o_t = acc_ref[i, 0:HEAD_DIM, :] / acc_ref[i, HEAD_DIM:HEAD_DIM + 1, :]
        o_ref[:, i * HEAD_DIM:(i + 1) * HEAD_DIM] = o_t.T.astype(o_ref.dtype)


def _key_chunk(t, largest=1408):
    for tk in (1408, 768, 1024, 512, 640, 384, 256, 128):
        if tk <= largest and t % tk == 0:
            return tk
    raise ValueError(f"unsupported key count {t}")


def _gqa(proj, s):
    t = proj.shape[0]
    tq = ATTN_TQ
    tk = _key_chunk(t)
    n_st = GQA_STREAMS
    per_g = A_GROUP // n_st
    return pl.pallas_call(
        _gqa_kernel,
        grid=(A_KV_HEADS, per_g, s // tq),
        in_specs=[pl.BlockSpec((tq, n_st * HEAD_DIM), lambda g, hh, i: (i, COL_AQ // n_st + g * per_g + hh)),
                  pl.BlockSpec((t, HEAD_DIM), lambda g, hh, i: (0, COL_AK + g)),
                  pl.BlockSpec((t, HEAD_DIM), lambda g, hh, i: (0, COL_AV + g))],
        out_specs=pl.BlockSpec((tq, n_st * HEAD_DIM), lambda g, hh, i: (i, g * per_g + hh)),
        out_shape=jax.ShapeDtypeStruct((s, A_Q_W), BF16),
        scratch_shapes=[pltpu.VMEM((t // tk, HEAD_DIM + ONES_ROWS, tk), BF16),
                        pltpu.VMEM((n_st, 2, tk, tq), F32),
                        pltpu.VMEM((n_st, 1, tq), F32),
                        pltpu.VMEM((n_st, HEAD_DIM + ONES_ROWS, tq), F32)],
        compiler_params=_cparams(("arbitrary", "arbitrary", "arbitrary")),
        name="gqa_attn",
    )(proj, proj, proj)


def _diff_kernel(lam_ref, q_ref, k_ref, v_ref, g_ref, o_ref, vt_ref, s_ref, m_ref, acc_ref, *, lam_init):
    n_heads, _, _, tk = vt_ref.shape
    n_st = 2 * n_heads
    head_cols = lambda i: slice(i * HEAD_DIM, (i + 1) * HEAD_DIM)
    v_cols = lambda hd: slice(hd * B_V_DIM, (hd + 1) * B_V_DIM)

    @pl.when(pl.program_id(1) == 0)
    def _():
        for hd in range(n_heads):
            _build_vt(v_ref, v_cols(hd), vt_ref.at[hd], tk)

    lv = lam_ref[...]
    lam = (jnp.exp(jnp.sum(lv[0:1, :] * lv[1:2, :], axis=-1, keepdims=True))
           - jnp.exp(jnp.sum(lv[2:3, :] * lv[3:4, :], axis=-1, keepdims=True)) + lam_init)
    qs = [q_ref[:, head_cols(i)] for i in range(n_st)]
    _attend(qs, k_ref, [head_cols(i) for i in range(n_st)], [vt_ref.at[i // 2] for i in range(n_st)],
            s_ref, m_ref, acc_ref)
    for hd in range(n_heads):
        a0, a1 = acc_ref.at[2 * hd], acc_ref.at[2 * hd + 1]
        o_t = (a0[0:B_V_DIM, :] / a0[B_V_DIM:B_V_DIM + 1, :]
               - lam * (a1[0:B_V_DIM, :] / a1[B_V_DIM:B_V_DIM + 1, :]))
        o = o_t.T
        y = o * lax.rsqrt(jnp.mean(o * o, axis=-1, keepdims=True) + EPS) * g_ref[...]
        o_ref[:, v_cols(hd)] = (y * (1.0 - lam_init)).astype(o_ref.dtype)


def _diff(proj, s, lam_vecs, subln_g, lam_init):
    t = proj.shape[0]
    tq = ATTN_TQ
    nh = DIFF_HEADS
    tk = _key_chunk(t, largest=768)
    n_st = 2 * nh
    qk_w = n_st * HEAD_DIM
    v_w = nh * B_V_DIM
    resident = lambda w, col0: pl.BlockSpec((t, w), lambda h, i: (0, col0 * HEAD_DIM // w + h),
                                            pipeline_mode=pl.Buffered(1))
    return pl.pallas_call(
        functools.partial(_diff_kernel, lam_init=lam_init),
        grid=(B_HEADS // nh, s // tq),
        in_specs=[pl.BlockSpec((4, HEAD_DIM), lambda h, i: (0, 0)),
                  pl.BlockSpec((tq, qk_w), lambda h, i: (i, COL_BQ * HEAD_DIM // qk_w + h)),
                  resident(qk_w, COL_BK),
                  resident(v_w, COL_BV),
                  pl.BlockSpec((1, B_V_DIM), lambda h, i: (0, 0))],
        out_specs=pl.BlockSpec((tq, v_w), lambda h, i: (i, h)),
        out_shape=jax.ShapeDtypeStruct((s, B_V_W), BF16),
        scratch_shapes=[pltpu.VMEM((nh, t // tk, B_V_DIM + ONES_ROWS, tk), BF16),
                        pltpu.VMEM((n_st, 2, tk, tq), F32),
                        pltpu.VMEM((n_st, 1, tq), F32),
                        pltpu.VMEM((n_st, B_V_DIM + ONES_ROWS, tq), F32)],
        compiler_params=_cparams(("arbitrary", "arbitrary")),
        name="diff_attn",
    )(lam_vecs, proj, proj, proj, subln_g.reshape(1, B_V_DIM))


def _merge_kernel(oa_ref, ob_ref, *refs):
    nc = (len(refs) - 6) // 2
    ga_refs, gb_refs = refs[:nc], refs[nc:2 * nc]
    wa_ref, wb_ref, wo_ref, x_ref, gate_ref, o_ref = refs[2 * nc:]
    tn = ga_refs[0].shape[1]
    ya = jnp.dot(oa_ref[...], wa_ref[...], preferred_element_type=F32)
    yb = jnp.dot(ob_ref[...], wb_ref[...], preferred_element_type=F32)
    parts = []
    for c in range(nc):
        sl = slice(c * tn, (c + 1) * tn)
        t = (jax.nn.sigmoid(ga_refs[c][...].astype(F32)) * ya[:, sl]
             + jax.nn.sigmoid(gb_refs[c][...].astype(F32)) * yb[:, sl])
        parts.append(t.astype(wo_ref.dtype))
    y = jnp.dot(jnp.concatenate(parts, axis=1), wo_ref[...], preferred_element_type=F32)
    o_ref[...] = x_ref[...] + gate_ref[...] * y


def _merge(oa, ob, proj, wa, wb, wo, x, gate):
    s, d = x.shape
    tm = MERGE_TM
    tn = PROJ_TN
    nc = d // tn
    ga0 = COL_GA * HEAD_DIM // tn
    gb0 = COL_GB * HEAD_DIM // tn
    resident = lambda shape: pl.BlockSpec(shape, lambda i: (0, 0), pipeline_mode=pl.Buffered(1))
    gate_specs = [pl.BlockSpec((tm, tn), lambda i, c=c0 + c: (i, c)) for c0 in (ga0, gb0) for c in range(nc)]
    return pl.pallas_call(
        _merge_kernel,
        grid=(s // tm,),
        in_specs=[pl.BlockSpec((tm, A_Q_W), lambda i: (i, 0)),
                  pl.BlockSpec((tm, B_V_W), lambda i: (i, 0)),
                  *gate_specs,
                  resident((A_Q_W, d)), resident((B_V_W, d)), resident((d, d)),
                  pl.BlockSpec((tm, d), lambda i: (i, 0)),
                  pl.BlockSpec((1, d), lambda i: (0, 0))],
        out_specs=pl.BlockSpec((tm, d), lambda i: (i, 0)),
        out_shape=jax.ShapeDtypeStruct((s, d), F32),
        compiler_params=_cparams(("arbitrary",)),
        name="merge_out",
    )(oa, ob, *([proj] * (2 * nc)), wa, wb, wo, x, gate)


def _store_packed_tiles(ref, x):
    rows = x.shape[0]
    bits = lax.bitcast_convert_type(x.astype(BF16).astype(F32), jnp.uint32)
    half = bits.shape[1] // 2
    words = (bits[:, :half] >> 16) | (bits[:, half:] & jnp.uint32(0xFFFF0000))
    for c in range(PACK_TILES):
        ref[pl.ds(c, rows, stride=PACK_TILES), :] = words[:, c * LANES:(c + 1) * LANES]


def _lane_max(x):
    return jnp.max(x, axis=-1, keepdims=True)


def _lane_min(x):
    return jnp.min(x, axis=-1, keepdims=True)


def _group_allreduce(x, lane, op):
    for sft in (1, 2, 4):
        up = pltpu.roll(x, sft, 1)
        dn = pltpu.roll(x, LANES - sft, 1)
        x = op(x, jnp.where((lane & sft) != 0, up, dn))
    return x


def _router_kernel(x_ref, g_ref, mod_ref, wr_ref, rb_ref,
                   h_ref, hp_ref, idx_ref, wsel_ref, rank_ref, cnt_ref, carry_ref):
    @pl.when(pl.program_id(0) == 0)
    def _():
        carry_ref[...] = jnp.zeros_like(carry_ref)

    h = _rms_mod(x_ref[...], g_ref[...], mod_ref[0, 3:4, :], mod_ref[0, 4:5, :])
    hb = h.astype(BF16)
    h_ref[...] = hb
    _store_packed_tiles(hp_ref, h)
    tm = h.shape[0]

    h_lo = (h - hb.astype(F32)).astype(BF16)
    logits = (jnp.dot(hb, wr_ref[0], preferred_element_type=F32)
              + jnp.dot(hb, wr_ref[1], preferred_element_type=F32)
              + jnp.dot(h_lo, wr_ref[0], preferred_element_type=F32))
    scores = jax.nn.sigmoid(logits)
    lane = lax.broadcasted_iota(jnp.int32, (tm, LANES), 1)
    lane_f = lane.astype(F32)
    gid_f = (lane >> 3).astype(F32)
    valid = lane < N_EXPERTS
    neg = jnp.float32(-jnp.inf)
    big = jnp.float32(LANES)
    biased = jnp.where(valid, scores + rb_ref[...], neg)

    m1 = _group_allreduce(biased, lane, jnp.maximum)
    a1 = _group_allreduce(jnp.where(biased == m1, lane_f, big), lane, jnp.minimum)
    m2 = _group_allreduce(jnp.where(lane_f == a1, neg, biased), lane, jnp.maximum)
    gscore = jnp.where(valid, m1 + m2, neg)
    keep = jnp.zeros((tm, LANES), jnp.bool_)
    for _ in range(TOPK_GROUPS):
        best = _lane_max(gscore)
        gsel = _lane_min(jnp.where(gscore == best, gid_f, big))
        hit = gid_f == gsel
        keep = keep | hit
        gscore = jnp.where(hit, neg, gscore)
    cand = jnp.where(keep & valid, biased, neg)

    onehot = jnp.zeros((tm, LANES), F32)
    idx_out = jnp.zeros((tm, LANES), F32)
    w_out = jnp.zeros((tm, LANES), F32)
    sels = []
    for k in range(TOP_K):
        best = _lane_max(cand)
        sel = _lane_min(jnp.where(cand == best, lane_f, big))
        hit = lane_f == sel
        wk = jnp.sum(jnp.where(hit, scores, 0.0), axis=-1, keepdims=True)
        cand = jnp.where(hit, neg, cand)
        onehot = jnp.where(hit, 1.0, onehot)
        idx_out = jnp.where(lane == k, sel, idx_out)
        w_out = jnp.where(lane == k, wk, w_out)
        sels.append(hit)
    wsum = jnp.sum(w_out, axis=-1, keepdims=True)
    wsel_ref[...] = w_out / wsum * ROUTED_SCALE
    idx_ref[...] = idx_out.astype(jnp.int32)

    row = lax.broadcasted_iota(jnp.int32, (tm, tm), 0)
    col = lax.broadcasted_iota(jnp.int32, (tm, tm), 1)
    lower = (col < row).astype(BF16)
    before = jnp.dot(lower, onehot.astype(BF16), preferred_element_type=F32) + carry_ref[...]
    rank_out = jnp.zeros((tm, LANES), F32)
    for k in range(TOP_K):
        rk = jnp.sum(jnp.where(sels[k], before, 0.0), axis=-1, keepdims=True)
        rank_out = jnp.where(lane == k, rk, rank_out)
    rank_ref[...] = rank_out.astype(jnp.int32)
    carry_ref[...] = carry_ref[...] + jnp.sum(onehot, axis=0, keepdims=True)
    cnt_ref[...] = carry_ref[...]


def _router(x1, g, mod, w_router, router_bias):
    s, d = x1.shape
    tm = ROUTER_TM
    e = w_router.shape[1]
    wr = jnp.pad(w_router, ((0, 0), (0, LANES - e)))
    wr_hi = wr.astype(BF16)
    wr = jnp.stack([wr_hi, (wr - wr_hi.astype(F32)).astype(BF16)])
    rb = jnp.pad(router_bias, (0, LANES - e)).reshape(1, LANES)
    row_spec = lambda w: pl.BlockSpec((tm, w), lambda i: (i, 0))
    return pl.pallas_call(
        _router_kernel,
        grid=(s // tm,),
        in_specs=[row_spec(d),
                  pl.BlockSpec((1, d), lambda i: (0, 0)),
                  pl.BlockSpec((1, N_MOD, d), lambda i: (0, 0, 0)),
                  pl.BlockSpec((2, d, LANES), lambda i: (0, 0, 0)),
                  pl.BlockSpec((1, LANES), lambda i: (0, 0))],
        out_specs=[row_spec(d), pl.BlockSpec((tm * PACK_TILES, LANES), lambda i: (i, 0)),
                   row_spec(LANES), row_spec(LANES), row_spec(LANES),
                   pl.BlockSpec((1, LANES), lambda i: (0, 0))],
        out_shape=[jax.ShapeDtypeStruct((s, d), BF16),
                   jax.ShapeDtypeStruct((s * PACK_TILES, LANES), jnp.uint32),
                   jax.ShapeDtypeStruct((s, LANES), jnp.int32),
                   jax.ShapeDtypeStruct((s, LANES), F32),
                   jax.ShapeDtypeStruct((s, LANES), jnp.int32),
                   jax.ShapeDtypeStruct((1, LANES), F32)],
        scratch_shapes=[pltpu.VMEM((1, LANES), F32)],
        compiler_params=_cparams(("arbitrary",)),
        name="ffn_router",
    )(x1, g.reshape(1, d), mod, wr, rb)


def _unpack_halves(words):
    lo = lax.bitcast_convert_type(words << 16, F32)
    hi = lax.bitcast_convert_type(words & jnp.uint32(0xFFFF0000), F32)
    return lo, hi


def _unpack_rows(words):
    lo, hi = _unpack_halves(words)
    return jnp.concatenate([lo.astype(BF16), hi.astype(BF16)], axis=1)


def _token_rows(row, tiles):
    return pl.ds(pl.multiple_of(row * tiles, tiles), tiles)


def _dispatch_kernel(ps_ref, pe_ref, nused_ref, dest_hbm, hp_ref, h_ref, wg_ref, wu_ref, wd_ref,
                     xs_hbm, sh_ref, dest_smem, zbuf, sem_i, sem_z, sem_x, *, n_blocks):
    i = pl.program_id(0)
    n = dest_smem.shape[0]
    tm = n // TOP_K
    copy_dest = pltpu.make_async_copy(dest_hbm.at[i], dest_smem, sem_i)
    copy_dest.start()

    @pl.when(i == 0)
    def _():
        zbuf[...] = jnp.zeros_like(zbuf)

        def zero_block(row0):
            span = MOE_BLOCK * PACK_TILES
            return pltpu.make_async_copy(
                zbuf, xs_hbm.at[pl.ds(pl.multiple_of(row0 * PACK_TILES, span), span), :], sem_z)

        def per_expert(action):
            def body(e, carry):
                @pl.when(pe_ref[e] > ps_ref[e])
                def _():
                    action(zero_block(pe_ref[e] - MOE_BLOCK))
                return carry
            lax.fori_loop(0, N_EXPERTS, body, 0)

        def per_tail(action):
            def body(b, carry):
                action(zero_block(b * MOE_BLOCK))
                return carry
            lax.fori_loop(nused_ref[0], n_blocks, body, 0)

        per_expert(lambda cp: cp.start())
        per_tail(lambda cp: cp.start())
        per_expert(lambda cp: cp.wait())
        per_tail(lambda cp: cp.wait())

    copy_dest.wait()

    def issue(t, carry):
        for k in range(TOP_K):
            dest = dest_smem[t * TOP_K + k]
            pltpu.make_async_copy(hp_ref.at[_token_rows(t, PACK_TILES), :],
                                  xs_hbm.at[_token_rows(dest, PACK_TILES), :], sem_x).start(priority=k % 2)
        return carry

    lax.fori_loop(0, tm, issue, 0, unroll=4)
    h = h_ref[...]
    gate = jnp.dot(h, wg_ref[...], preferred_element_type=F32)
    up = jnp.dot(h, wu_ref[...], preferred_element_type=F32)
    act = (gate * jax.nn.sigmoid(gate) * up).astype(BF16)
    sh_ref[...] = jnp.dot(act, wd_ref[...], preferred_element_type=F32)
    for _ in range(TOP_K):
        pltpu.make_async_copy(hp_ref, xs_hbm.at[pl.ds(0, tm * PACK_TILES), :], sem_x).wait()


def _dispatch(pad_start, pad_end, n_used, dest_tm, hp, h2, wsg, wsu, wsd, n_blocks):
    n_tiles, n = dest_tm.shape
    tm = n // TOP_K
    d, f = wsg.shape
    resident = lambda shape: pl.BlockSpec(shape, lambda i, ps, pe, nu: (0, 0), pipeline_mode=pl.Buffered(1))
    grid_spec = pltpu.PrefetchScalarGridSpec(
        num_scalar_prefetch=3,
        grid=(n_tiles,),
        in_specs=[pl.BlockSpec(memory_space=pl.ANY),
                  pl.BlockSpec((tm * PACK_TILES, LANES), lambda i, ps, pe, nu: (i, 0)),
                  pl.BlockSpec((tm, d), lambda i, ps, pe, nu: (i, 0)),
                  resident((d, f)), resident((d, f)), resident((f, d))],
        out_specs=[pl.BlockSpec(memory_space=pl.ANY),
                   pl.BlockSpec((tm, d), lambda i, ps, pe, nu: (i, 0))],
        scratch_shapes=[pltpu.SMEM((n,), jnp.int32),
                        pltpu.VMEM((MOE_BLOCK * PACK_TILES, LANES), jnp.uint32),
                        pltpu.SemaphoreType.DMA,
                        pltpu.SemaphoreType.DMA,
                        pltpu.SemaphoreType.DMA])
    return pl.pallas_call(
        functools.partial(_dispatch_kernel, n_blocks=n_blocks),
        grid_spec=grid_spec,
        out_shape=[jax.ShapeDtypeStruct((n_blocks * MOE_BLOCK * PACK_TILES, LANES), jnp.uint32),
                   jax.ShapeDtypeStruct((n_tiles * tm, d), F32)],
        compiler_params=_cparams(("arbitrary",)),
        name="moe_dispatch",
    )(pad_start, pad_end, n_used, dest_tm, hp, h2, wsg, wsu, wsd)


def _moe_kernel(nblk_ref, blk0_ref, nxt_ref, slot_ref, first_ref, nused_ref,
                xs_hbm, wg_hbm, wu_hbm, wd_hbm, y_hbm,
                xbuf, ybuf, wg32, wu32, wd32, wgb, wub, wdb, sem_w, sem_x, sem_y, *, n_blocks):
    e = pl.program_id(0)
    nb = nblk_ref[e]
    span = MOE_BLOCK * PACK_TILES

    def block_rows(b):
        return pl.ds(pl.multiple_of(b * span, span), span)

    def fetch(ex, slot):
        return (pltpu.make_async_copy(wg_hbm.at[ex], wg32.at[slot], sem_w.at[slot, 0]),
                pltpu.make_async_copy(wu_hbm.at[ex], wu32.at[slot], sem_w.at[slot, 1]),
                pltpu.make_async_copy(wd_hbm.at[ex], wd32.at[slot], sem_w.at[slot, 2]))

    def x_copy(b, s):
        return pltpu.make_async_copy(xs_hbm.at[block_rows(b), :], xbuf.at[s], sem_x.at[s])

    def y_copy(b, s):
        return pltpu.make_async_copy(ybuf.at[s], y_hbm.at[block_rows(b), :], sem_y.at[s])

    n_used = nused_ref[0]
    n_xbuf = xbuf.shape[0]

    @pl.when(nb > 0)
    def _():
        b0 = blk0_ref[e]
        slot = slot_ref[e]

        @pl.when(e == first_ref[0])
        def _():
            for b in range(n_xbuf - 1):
                pl.when(b < n_used)(lambda b=b: x_copy(b, b).start())
            for cp in fetch(e, 0):
                cp.start(priority=1)

        for cp in fetch(e, slot):
            cp.wait()

        @pl.when(nxt_ref[e] >= 0)
        def _():
            for cp in fetch(nxt_ref[e], 1 - slot):
                cp.start(priority=1)

        wgb[...] = wg32[slot].astype(BF16)
        wub[...] = wu32[slot].astype(BF16)
        wdb[...] = wd32[slot].astype(BF16)

        def body(j, carry):
            b = b0 + j
            s = b % 2
            sx = b % n_xbuf
            ahead = b + n_xbuf - 1
            x_copy(b, sx).wait()
            pl.when(ahead < n_used)(lambda: x_copy(ahead, ahead % n_xbuf).start())
            pl.when(b >= 2)(lambda: y_copy(b - 2, s).wait())
            xb = xbuf.at[sx]
            words = jnp.concatenate(
                [xb[pl.ds(c, MOE_BLOCK, stride=PACK_TILES), :] for c in range(PACK_TILES)], axis=1)
            x = _unpack_rows(words)
            gate = jnp.dot(x, wgb[...], preferred_element_type=F32)
            up = jnp.dot(x, wub[...], preferred_element_type=F32)
            act = (gate * jax.nn.sigmoid(gate) * up).astype(BF16)
            _store_packed_tiles(ybuf.at[s], jnp.dot(act, wdb[...], preferred_element_type=F32))
            y_copy(b, s).start()
            return carry

        lax.fori_loop(0, nb, body, 0)

    @pl.when(e == pl.num_programs(0) - 1)
    def _():
        pl.when(n_used >= 2)(lambda: y_copy(n_used - 2, n_used % 2).wait())
        pl.when(n_used >= 1)(lambda: y_copy(n_used - 1, (n_used - 1) % 2).wait())
        ybuf[0] = jnp.zeros(ybuf.shape[1:], ybuf.dtype)

        def tail(action):
            def step(b, carry):
                action(y_copy(b, 0))
                return carry
            lax.fori_loop(nused_ref[0], n_blocks, step, 0)

        tail(lambda cp: cp.start())
        tail(lambda cp: cp.wait())


def _moe(nblk, blk0, nxt, slot, first, n_used, xs, weg, weu, wed, n_blocks):
    n_exp, d, f = weg.shape
    span = MOE_BLOCK * PACK_TILES
    grid_spec = pltpu.PrefetchScalarGridSpec(
        num_scalar_prefetch=6,
        grid=(n_exp,),
        in_specs=[pl.BlockSpec(memory_space=pl.ANY)] * 4,
        out_specs=pl.BlockSpec(memory_space=pl.ANY),
        scratch_shapes=[pltpu.VMEM((3, span, LANES), jnp.uint32), pltpu.VMEM((2, span, LANES), jnp.uint32),
                        pltpu.VMEM((2, d, f), F32), pltpu.VMEM((2, d, f), F32), pltpu.VMEM((2, f, d), F32),
                        pltpu.VMEM((d, f), BF16), pltpu.VMEM((d, f), BF16), pltpu.VMEM((f, d), BF16),
                        pltpu.SemaphoreType.DMA((2, 3)), pltpu.SemaphoreType.DMA((3,)),
                        pltpu.SemaphoreType.DMA((2,))])
    return pl.pallas_call(
        functools.partial(_moe_kernel, n_blocks=n_blocks),
        grid_spec=grid_spec,
        out_shape=jax.ShapeDtypeStruct((n_blocks * span, LANES), jnp.uint32),
        compiler_params=_cparams(("arbitrary",)),
        name="moe_experts",
    )(nblk, blk0, nxt, slot, first, n_used, xs, weg, weu, wed)


def _combine_kernel(dest_hbm, y_hbm, w_ref, sh_ref, x_ref, gate_ref, o_ref,
                    dest0, dest1, ybuf0, ybuf1, sem_i, sem_y):
    i = pl.program_id(0)
    tm = sh_ref.shape[0]
    n = tm * TOP_K
    span = n * PACK_TILES
    dests = (dest0, dest1)
    ybufs = (ybuf0, ybuf1)

    def start_tile(tile, slot):
        copy_dest = pltpu.make_async_copy(dest_hbm.at[tile], dests[slot], sem_i)
        copy_dest.start()
        copy_dest.wait()

        def issue(t, carry):
            base = t * TOKEN_PITCH
            for k in range(TOP_K):
                src = dests[slot][t * TOP_K + k]
                pltpu.make_async_copy(
                    y_hbm.at[_token_rows(src, PACK_TILES), :],
                    ybufs[slot].at[pl.ds(pl.multiple_of(base + k * YBUF_PITCH, 4), PACK_TILES), :],
                    sem_y.at[slot]).start(priority=k % 2)
            return carry

        lax.fori_loop(0, tm, issue, 0, unroll=4)

    def step(cur):
        nxt = 1 - cur
        pl.when(i == 0)(lambda: start_tile(0, cur))
        pl.when(i + 1 < pl.num_programs(0))(lambda: start_tile(i + 1, nxt))

        shared = sh_ref[...]
        yb = ybufs[cur]
        pltpu.make_async_copy(y_hbm.at[pl.ds(0, span), :], yb.at[pl.ds(0, span), :], sem_y.at[cur]).wait()
        wks = [jnp.broadcast_to(w_ref[:, k:k + 1], (tm, LANES)) for k in range(TOP_K)]
        lows, highs = [], []
        for c in range(PACK_TILES):
            acc_lo = acc_hi = None
            for k in range(TOP_K):
                lo, hi = _unpack_halves(yb[pl.ds(k * YBUF_PITCH + c, tm, stride=TOKEN_PITCH), :])
                acc_lo = lo * wks[k] if acc_lo is None else acc_lo + lo * wks[k]
                acc_hi = hi * wks[k] if acc_hi is None else acc_hi + hi * wks[k]
            lows.append(acc_lo)
            highs.append(acc_hi)
        routed = jnp.concatenate(lows + highs, axis=1)
        o_ref[...] = x_ref[...] + gate_ref[...] * (routed + shared)

    pl.when(i % 2 == 0)(lambda: step(0))
    pl.when(i % 2 == 1)(lambda: step(1))


def _combine(dest_tm, y_sorted, wsel, shared, x1, gate):
    s, d = x1.shape
    n_tiles, n = dest_tm.shape
    tm = n // TOP_K
    return pl.pallas_call(
        _combine_kernel,
        grid=(n_tiles,),
        in_specs=[pl.BlockSpec(memory_space=pl.ANY),
                  pl.BlockSpec(memory_space=pl.ANY),
                  pl.BlockSpec((tm, LANES), lambda i: (i, 0)),
                  pl.BlockSpec((tm, d), lambda i: (i, 0)),
                  pl.BlockSpec((tm, d), lambda i: (i, 0)),
                  pl.BlockSpec((1, d), lambda i: (0, 0))],
        out_specs=pl.BlockSpec((tm, d), lambda i: (i, 0)),
        out_shape=jax.ShapeDtypeStruct((s, d), F32),
        scratch_shapes=[pltpu.SMEM((n,), jnp.int32),
                        pltpu.SMEM((n,), jnp.int32),
                        pltpu.VMEM((tm * TOKEN_PITCH, LANES), jnp.uint32),
                        pltpu.VMEM((tm * TOKEN_PITCH, LANES), jnp.uint32),
                        pltpu.SemaphoreType.DMA,
                        pltpu.SemaphoreType.DMA((2,))],
        compiler_params=_cparams(("arbitrary",)),
        name="moe_combine",
    )(dest_tm, y_sorted, wsel, shared, x1, gate)


def _expert_tables(counts):
    i32 = jnp.int32
    padded = ((counts + MOE_BLOCK - 1) // MOE_BLOCK * MOE_BLOCK).astype(i32)
    pad_end = jnp.cumsum(padded).astype(i32)
    pad_start = pad_end - padded
    ids = jnp.arange(N_EXPERTS, dtype=i32)
    busy = padded > 0
    later_busy = busy[None, :] & (ids[None, :] > ids[:, None])
    nxt = jnp.min(jnp.where(later_busy, ids[None, :], N_EXPERTS), axis=1)
    nxt = jnp.where(nxt < N_EXPERTS, nxt, -1).astype(i32)
    slot = (jnp.maximum(jnp.cumsum(busy.astype(i32)) - 1, 0) % 2).astype(i32)
    first = jnp.min(jnp.where(busy, ids, N_EXPERTS)).astype(i32).reshape(1)
    n_used = (pad_end[-1] // MOE_BLOCK).reshape(1)
    return pad_start, pad_end, padded // MOE_BLOCK, pad_start // MOE_BLOCK, nxt, slot, first, n_used


def kernel(x, c, ctx, c_ctx, w_ada, b_ada, norm_mix, norm_ffn, w_in, q_norm_a, k_norm_a, q_norm_b, k_norm_b, lambda_q1, lambda_k1, lambda_q2, lambda_k2, subln_b, w_branch_a, w_branch_b, w_out, w_router, router_bias, w_exp_gate, w_exp_up, w_exp_down, w_sh_gate, w_sh_up, w_sh_down):
    depth = w_ada.shape[0]
    assert depth == 1 and x.shape[0] == 1 and ctx.shape[0] == 1
    s, d = x.shape[1], x.shape[2]
    n_ctx = ctx.shape[1]
    assert d == D_MODEL and s % GRID_W == 0
    assert s % PRENORM_TM == 0 and n_ctx % PRENORM_TM == 0 and s % ATTN_TQ == 0
    assert s % MERGE_TM == 0 and s % ROUTER_TM == 0 and (s + n_ctx) % (INPROJ_ROW_TILES * 16) == 0
    i = 0
    lam_init = 0.8 - 0.6 * math.exp(-0.3 * i)
    xs = x[0]

    mod = _adaln(jnp.concatenate([c, c_ctx[None, :]], axis=0), w_ada[i], b_ada[i]).reshape(2, N_MOD, d)

    h = _prenorm(xs, ctx[0], norm_mix[i], mod)
    tc, tsa, tsb = _rope_tables(s, n_ctx)
    gains = _head_gains(q_norm_a[i], k_norm_a[i], q_norm_b[i], k_norm_b[i])
    proj = _inproj(h, w_in[i], gains, tc, tsa, tsb)
    oa = _gqa(proj, s)
    lam_vecs = jnp.stack([lambda_q1[i], lambda_k1[i], lambda_q2[i], lambda_k2[i]]).astype(F32)
    ob = _diff(proj, s, lam_vecs, subln_b[i], lam_init)
    x1 = _merge(oa, ob, proj, w_branch_a[i].astype(BF16), w_branch_b[i].astype(BF16),
                w_out[i].astype(BF16), xs, mod[0, 2:3, :])

    h2, h2p, idx, wsel, rank, cnt = _router(x1, norm_ffn[i], mod[0:1], w_router[i], router_bias[i])
    counts = cnt[0, :N_EXPERTS].astype(jnp.int32)
    n_blocks = -(-(s * TOP_K) // MOE_BLOCK) + N_EXPERTS
    pad_start, pad_end, nblk, blk0, nxt, slot, first, n_used = _expert_tables(counts)
    tm_dispatch = math.gcd(s, DISPATCH_TM)
    tm_combine = math.gcd(s, COMBINE_TM)
    e_ids = jnp.arange(N_EXPERTS, dtype=jnp.int32)
    starts = jnp.sum(jnp.where(idx[:, :TOP_K, None] == e_ids, pad_start, 0), axis=-1)
    dest = (starts + rank[:, :TOP_K]).astype(jnp.int32)
    tiles = lambda tm: dest.reshape(s // tm, tm * TOP_K)
    xs, shared = _dispatch(pad_start, pad_end, n_used, tiles(tm_dispatch), h2p, h2,
                           w_sh_gate[i].astype(BF16), w_sh_up[i].astype(BF16), w_sh_down[i].astype(BF16),
                           n_blocks)
    y_sorted = _moe(nblk, blk0, nxt, slot, first, n_used, xs,
                    w_exp_gate[i], w_exp_up[i], w_exp_down[i], n_blocks)
    out = _combine(tiles(tm_combine), y_sorted, wsel, shared, x1, mod[0, 5:6, :])
    return out[None]
```
